```python
import math
import jax, jax.numpy as jnp
from jax import lax
import numpy as np

D_MODEL = 2048
BATCH = 4
SEQ = 2048
DEPTH = 1
DEC_BATCH = 128
DEC_SEQ = 4
PAST_LEN = 16384
PAGE_SIZE = 128

MIX_A = D_MODEL // 2
MIX_B = D_MODEL - MIX_A
HGRN_HEAD_DIM = 128
HGRN_HEADS = MIX_A // HGRN_HEAD_DIM
SSD_HEAD_DIM = 64
SSD_HEADS = MIX_B // SSD_HEAD_DIM
SSD_GROUPS = 2
SSD_HPG = SSD_HEADS // SSD_GROUPS
SSD_STATE = 128
CONV_WIDTH = 4
CONV_DIM = MIX_B + 2 * SSD_GROUPS * SSD_STATE
IN_DIM = 4 * MIX_A + MIX_B + CONV_DIM + SSD_HEADS
HGRN_CHUNK = 64
SSD_CHUNK = 64
N_EXPERT_GROUPS = 4
EXPERTS_PER_GROUP = 8
N_EXPERTS = N_EXPERT_GROUPS * EXPERTS_PER_GROUP
TOP_K = 2
D_EXPERT = D_MODEL // 2
EPS = 1e-6
F32 = jnp.float32

kernel_name = 'hymba_hgrn2_ssd_hmoe_adaln_step'


def _rms(x):
    return x * lax.rsqrt(jnp.mean(x * x, axis=-1, keepdims=True) + EPS)


def rmsnorm(x, g):
    return (_rms(x.astype(F32)) * g.astype(F32)).astype(x.dtype)


def _pad_time(a, pad):
    return jnp.pad(a, [(0, 0), (0, pad)] + [(0, 0)] * (a.ndim - 2))


def hgrn2_chunked(q, k, v, log_f, s0):
    bsz, t_len, n_h, _ = q.shape
    c = min(HGRN_CHUNK, t_len)
    nc = -(-t_len // c)
    pad = nc * c - t_len

    def to_chunks(a):
        a = _pad_time(a, pad)
        return jnp.moveaxis(a.reshape((bsz, nc, c) + a.shape[2:]), 1, 0)

    tril = jnp.tril(jnp.ones((c, c), dtype=bool))

    def step(s, inp):
        qc, kc, vc, lfc = inp
        b = jnp.cumsum(lfc, axis=1)
        o_inter = jnp.einsum('bthk,bhkv->bthv', qc * jnp.exp(b), s)
        diff = b[:, :, None] - b[:, None, :]
        dec = jnp.where(tril[None, :, :, None, None], jnp.exp(jnp.minimum(diff, 0.0)), 0.0)
        scores = jnp.einsum('bthk,btshk,bshk->bhts', qc, dec, kc)
        o_intra = jnp.einsum('bhts,bshv->bthv', scores, vc)
        b_end = b[:, -1]
        s_new = jnp.exp(b_end)[..., None] * s + jnp.einsum(
            'bshk,bshv->bhkv', kc * jnp.exp(b_end[:, None] - b), vc)
        return s_new, o_inter + o_intra

    s_fin, o = lax.scan(step, s0, (to_chunks(q), to_chunks(k), to_chunks(v), to_chunks(log_f)))
    o = jnp.moveaxis(o, 0, 1).reshape(bsz, nc * c, n_h, v.shape[-1])[:, :t_len]
    return o, s_fin


def ssd_chunked(x, dt, a, bm, cm, h0):
    bsz, t_len = x.shape[:2]
    c = min(SSD_CHUNK, t_len)
    nc = -(-t_len // c)
    pad = nc * c - t_len

    def rs(v):
        v = _pad_time(v, pad)
        return v.reshape((bsz, nc, c) + v.shape[2:])

    x, dt, bm, cm = rs(x), rs(dt), rs(bm), rs(cm)
    la = dt * a
    la_cs = jnp.cumsum(la, axis=2)
    xdt = x * dt[..., None]
    tril = jnp.tril(jnp.ones((c, c), dtype=bool))
    diff = la_cs[:, :, :, None] - la_cs[:, :, None, :]
    lmat = jnp.where(tril[:, :, None, None], jnp.exp(jnp.minimum(diff, 0.0)), 0.0)
    cb = jnp.einsum('bctgn,bcsgn->bctsg', cm, bm)
    y_diag = jnp.einsum('bctsg,bctsgr,bcsgrp->bctgrp', cb, lmat, xdt)
    decay_to_end = jnp.exp(la_cs[:, :, -1:] - la_cs)
    chunk_states = jnp.einsum('bcsgn,bcsgr,bcsgrp->bcgrpn', bm, decay_to_end, xdt)
    chunk_decay = jnp.exp(la_cs[:, :, -1])

    def step(h, inp):
        dec, st = inp
        return dec[..., None, None] * h + st, h

    h_fin, h_start = lax.scan(step, h0, (jnp.moveaxis(chunk_decay, 1, 0), jnp.moveaxis(chunk_states, 1, 0)))
    h_start = jnp.moveaxis(h_start, 0, 1)
    y_off = jnp.einsum('bctgn,bctgr,bcgrpn->bctgrp', cm, jnp.exp(la_cs), h_start)
    y = (y_diag + y_off).reshape((bsz, nc * c) + x.shape[3:])[:, :t_len]
    return y, h_fin


def hier_moe(h, w_grp, b_grp, w_rt, b_rt, w1, w3, w2):
    bsz, t_len, d = h.shape
    ht = h.reshape(-1, d)
    grp_p = jax.nn.softmax((ht @ w_grp).astype(F32) + b_grp.astype(F32), axis=-1)
    gp_top, g_idx = lax.top_k(grp_p, 1)
    el = ((ht @ w_rt).astype(F32) + b_rt.astype(F32)).reshape(-1, N_EXPERT_GROUPS, EXPERTS_PER_GROUP)
    el_sel = jnp.einsum('nge,ng->ne', el, jax.nn.one_hot(g_idx[:, 0], N_EXPERT_GROUPS, dtype=F32))
    ep = jax.nn.softmax(el_sel, axis=-1)
    ep_top, e_idx = lax.top_k(ep, TOP_K)
    wts = gp_top * ep_top / jnp.sum(ep_top, axis=-1, keepdims=True)
    ids = g_idx * EXPERTS_PER_GROUP + e_idx
    gates = jnp.einsum('nk,nke->ne', wts, jax.nn.one_hot(ids, N_EXPERTS, dtype=F32))
    y = jnp.zeros(ht.shape, F32)
    for e in range(N_EXPERTS):
        act = jax.nn.silu(ht @ w1[e]) * (ht @ w3[e])
        y = y + gates[:, e:e + 1] * (act @ w2[e]).astype(F32)
    return y.astype(h.dtype).reshape(bsz, t_len, d)


def setup_inputs(seed: int = 0) -> dict:
    key = jax.random.key(seed)
    ks = jax.random.split(key, 32)

    def nrm(k, shape, s):
        return jax.random.normal(k, shape, F32) * s

    dt_u = jax.random.uniform(ks[14], (DEPTH, SSD_HEADS), F32)
    dt0 = jnp.exp(dt_u * (math.log(0.1) - math.log(1e-3)) + math.log(1e-3))
    dt_bias = dt0 + jnp.log(-jnp.expm1(-dt0))
    a_log = jnp.log(jax.random.uniform(ks[15], (DEPTH, SSD_HEADS), F32, minval=1.0, maxval=16.0))
    return {
        'x_prompt': nrm(ks[0], (BATCH, SEQ, D_MODEL), 1.0),
        'x_sample': nrm(ks[1], (DEC_BATCH, DEC_SEQ, D_MODEL), 1.0),
        'state_hgrn': nrm(ks[2], (DEPTH, DEC_BATCH, HGRN_HEADS, HGRN_HEAD_DIM, HGRN_HEAD_DIM), 1.0),
        'state_ssm': nrm(ks[3], (DEPTH, DEC_BATCH, SSD_HEADS, SSD_HEAD_DIM, SSD_STATE), 1.0),
        'state_conv': nrm(ks[4], (DEPTH, DEC_BATCH, CONV_WIDTH - 1, CONV_DIM), 1.0),
        'c_prompt': nrm(ks[5], (BATCH, D_MODEL), 1.0),
        'c_sample': nrm(ks[6], (DEC_BATCH, D_MODEL), 1.0),
        'ada_w': nrm(ks[7], (DEPTH, D_MODEL, 6 * D_MODEL), D_MODEL ** -0.5),
        'ada_b': nrm(ks[8], (DEPTH, 6 * D_MODEL), 0.02),
        'norm1_g': 1.0 + nrm(ks[9], (DEPTH, D_MODEL), 0.02),
        'norm2_g': 1.0 + nrm(ks[10], (DEPTH, D_MODEL), 0.02),
        'w_in': nrm(ks[11], (DEPTH, D_MODEL, IN_DIM), D_MODEL ** -0.5),
        'hgrn_lb': nrm(ks[12], (DEPTH + 1, MIX_A), 0.1),
        'hgrn_onorm_g': 1.0 + nrm(ks[13], (DEPTH, MIX_A), 0.02),
        'conv_w': nrm(ks[16], (DEPTH, CONV_WIDTH, CONV_DIM), CONV_WIDTH ** -0.5),
        'conv_b': nrm(ks[17], (DEPTH, CONV_DIM), 0.02),
        'dt_bias': dt_bias,
        'a_log': a_log,
        'd_skip': 1.0 + nrm(ks[18], (DEPTH, SSD_HEADS), 0.1),
        'ssm_norm_g': 1.0 + nrm(ks[19], (DEPTH, MIX_B), 0.02),
        'w_out': nrm(ks[20], (DEPTH, D_MODEL, D_MODEL), D_MODEL ** -0.5),
        'w_grp': nrm(ks[21], (DEPTH, D_MODEL, N_EXPERT_GROUPS), D_MODEL ** -0.5),
        'b_grp': nrm(ks[22], (DEPTH, N_EXPERT_GROUPS), 0.01),
        'w_rt': nrm(ks[23], (DEPTH, D_MODEL, N_EXPERTS), D_MODEL ** -0.5),
        'b_rt': nrm(ks[24], (DEPTH, N_EXPERTS), 0.01),
        'w1': nrm(ks[25], (DEPTH, N_EXPERTS, D_MODEL, D_EXPERT), D_MODEL ** -0.5),
        'w3': nrm(ks[26], (DEPTH, N_EXPERTS, D_MODEL, D_EXPERT), D_MODEL ** -0.5),
        'w2': nrm(ks[27], (DEPTH, N_EXPERTS, D_EXPERT, D_MODEL), D_EXPERT ** -0.5),
        'final_g': 1.0 + nrm(ks[28], (D_MODEL,), 0.02),
    }


def reference(x_prompt, x_sample, state_hgrn, state_ssm, state_conv, c_prompt, c_sample,
              ada_w, ada_b, norm1_g, norm2_g, w_in, hgrn_lb, hgrn_onorm_g, conv_w, conv_b,
              dt_bias, a_log, d_skip, ssm_norm_g, w_out, w_grp, b_grp, w_rt, b_rt,
              w1, w3, w2, final_g):
    lb_all = jnp.cumsum(jax.nn.softmax(hgrn_lb.astype(F32), axis=0), axis=0)
    splits = [MIX_A, 2 * MIX_A, 3 * MIX_A, 4 * MIX_A, 4 * MIX_A + MIX_B, 4 * MIX_A + MIX_B + CONV_DIM]
    gw = SSD_HPG * SSD_HEAD_DIM

    def mixer(l, h, s_hgrn, s_ssm, buf):
        bsz, t_len, _ = h.shape
        odt = h.dtype
        proj = h @ w_in[l]
        q, fa, vi, og, z, xbc, dt_raw = jnp.split(proj, splits, axis=-1)
        lb = lb_all[l]
        fgate = lb + (1.0 - lb) * jax.nn.sigmoid(fa.astype(F32))
        hd = (bsz, t_len, HGRN_HEADS, HGRN_HEAD_DIM)
        o, s_hgrn_new = hgrn2_chunked(
            jax.nn.silu(q.astype(F32)).reshape(hd), (1.0 - fgate).reshape(hd),
            vi.astype(F32).reshape(hd), jnp.log(fgate).reshape(hd), s_hgrn.astype(F32))
        o = _rms(o) * hgrn_onorm_g[l].astype(F32).reshape(HGRN_HEADS, HGRN_HEAD_DIM)
        o = o.reshape(bsz, t_len, MIX_A) * jax.nn.silu(og.astype(F32))
        full = jnp.concatenate([buf.astype(xbc.dtype), xbc], axis=1)
        cw = conv_w[l].astype(F32)
        u = conv_b[l].astype(F32) + sum(full[:, j:j + t_len].astype(F32) * cw[j] for j in range(CONV_WIDTH))
        u = jax.nn.silu(u)
        buf_new = full[:, t_len:]
        xs, bm, cm = jnp.split(u, [MIX_B, MIX_B + SSD_GROUPS * SSD_STATE], axis=-1)
        xs = xs.reshape(bsz, t_len, SSD_GROUPS, SSD_HPG, SSD_HEAD_DIM)
        dt = jax.nn.softplus(dt_raw.astype(F32) + dt_bias[l].astype(F32)).reshape(bsz, t_len, SSD_GROUPS, SSD_HPG)
        a_mat = -jnp.exp(a_log[l].astype(F32)).reshape(SSD_GROUPS, SSD_HPG)
        y, s_ssm_new = ssd_chunked(
            xs, dt, a_mat, bm.reshape(bsz, t_len, SSD_GROUPS, SSD_STATE),
            cm.reshape(bsz, t_len, SSD_GROUPS, SSD_STATE),
            s_ssm.astype(F32).reshape(bsz, SSD_GROUPS, SSD_HPG, SSD_HEAD_DIM, SSD_STATE))
        y = y + d_skip[l].astype(F32).reshape(SSD_GROUPS, SSD_HPG)[..., None] * xs
        y = _rms(y.reshape(bsz, t_len, SSD_GROUPS, gw) * jax.nn.silu(z.astype(F32).reshape(bsz, t_len, SSD_GROUPS, gw)))
        y = (y * ssm_norm_g[l].astype(F32).reshape(SSD_GROUPS, gw)).reshape(bsz, t_len, MIX_B)
        out = jnp.concatenate([o, y], axis=-1).astype(odt) @ w_out[l]
        return (out, s_hgrn_new.astype(odt),
                s_ssm_new.reshape(bsz, SSD_HEADS, SSD_HEAD_DIM, SSD_STATE).astype(odt),
                buf_new.astype(odt))

    def trunk(x, c, st_h, st_s, st_c):
        new_h, new_s, new_c = [], [], []
        for l in range(DEPTH):
            mod = (jax.nn.silu(c) @ ada_w[l] + ada_b[l])[:, None, :]
            sh1, sc1, g1, sh2, sc2, g2 = jnp.split(mod, 6, axis=-1)
            hn = rmsnorm(x, norm1_g[l]) * (1.0 + sc1) + sh1
            m, h_new, s_new, c_new = mixer(l, hn, st_h[l], st_s[l], st_c[l])
            x = x + g1 * m
            hn = rmsnorm(x, norm2_g[l]) * (1.0 + sc2) + sh2
            x = x + g2 * hier_moe(hn, w_grp[l], b_grp[l], w_rt[l], b_rt[l], w1[l], w3[l], w2[l])
            new_h.append(h_new)
            new_s.append(s_new)
            new_c.append(c_new)
        return rmsnorm(x, final_g), jnp.stack(new_h), jnp.stack(new_s), jnp.stack(new_c)

    bp = x_prompt.shape[0]
    zh = jnp.zeros((DEPTH, bp, HGRN_HEADS, HGRN_HEAD_DIM, HGRN_HEAD_DIM), x_prompt.dtype)
    zs = jnp.zeros((DEPTH, bp, SSD_HEADS, SSD_HEAD_DIM, SSD_STATE), x_prompt.dtype)
    zc = jnp.zeros((DEPTH, bp, CONV_WIDTH - 1, CONV_DIM), x_prompt.dtype)
    y_prompt, hgrn_p, ssm_p, conv_p = trunk(x_prompt, c_prompt, zh, zs, zc)
    y_sample, hgrn_s, ssm_s, conv_s = trunk(x_sample, c_sample, state_hgrn, state_ssm, state_conv)
    return (y_prompt, y_sample, hgrn_p, ssm_p, conv_p, hgrn_s, ssm_s, conv_s)
```

```python
import functools

import jax
import jax.numpy as jnp
from jax import lax
from jax.experimental import pallas as pl
from jax.experimental.pallas import tpu as pltpu

F32 = jnp.float32
BF16 = jnp.bfloat16
I32 = jnp.int32
EPS = 1e-6

LANES = 128
SUBLANES = 8
VMEM_LIMIT = 56 * 1024 * 1024

HGRN_HEAD_DIM = 128
SSD_HEAD_DIM = 64
SSD_GROUPS = 2
SSD_STATE = 128
CONV_WIDTH = 4
N_EXPERT_GROUPS = 4
EXPERTS_PER_GROUP = 8
N_EXPERTS = N_EXPERT_GROUPS * EXPERTS_PER_GROUP
TOP_K = 2

PROJ_TM = 1024
PROJ_TN = 512
MOE_TM = 256
PROMPT_CHUNK = 64
ROW_DMA_TILE = 256


def _cparams(sem):
    return pltpu.CompilerParams(dimension_semantics=sem, vmem_limit_bytes=VMEM_LIMIT)


def _dot(a, b):
    return jnp.dot(a, b, preferred_element_type=F32)


def _dot_nt(a, b):
    return lax.dot_general(a, b, (((1,), (1,)), ((), ())), preferred_element_type=F32)


def _split2(x):
    hi = x.astype(BF16)
    lo = (x - hi.astype(F32)).astype(BF16)
    return hi, lo


def _split3(x):
    hi = x.astype(BF16)
    r = x - hi.astype(F32)
    mid = r.astype(BF16)
    lo = (r - mid.astype(F32)).astype(BF16)
    return hi, mid, lo


def _dot_exact_lhs(m_bf16, x):
    hi, mid, lo = _split3(x)
    return _dot(m_bf16, hi) + _dot(m_bf16, mid) + _dot(m_bf16, lo)


def _dot_exact_rhs(x, m_bf16):
    hi, mid, lo = _split3(x)
    return _dot(hi, m_bf16) + _dot(mid, m_bf16) + _dot(lo, m_bf16)


def _dot_hp(a, w):
    ah, al = _split2(a)
    wh, wl = _split2(w)
    return _dot(ah, wh) + _dot(al, wh) + _dot(ah, wl)


def _silu(x):
    return x * jax.nn.sigmoid(x)


def _pad_rows(x, rows):
    if x.shape[0] == rows:
        return x
    return jnp.concatenate([x, jnp.zeros((rows - x.shape[0], x.shape[1]), x.dtype)], axis=0)


def _mod_kernel(c_ref, w_ref, b_ref, o_ref):
    a = _silu(c_ref[...])
    o_ref[...] = _dot_hp(a, w_ref[...]) + b_ref[...]


def _modulation(c_all, ada_w, ada_b):
    rows, d = c_all.shape
    n_out = ada_w.shape[1]
    tn = 1024
    return pl.pallas_call(
        _mod_kernel,
        grid=(n_out // tn,),
        in_specs=[
            pl.BlockSpec((rows, d), lambda j: (0, 0)),
            pl.BlockSpec((d, tn), lambda j: (0, j)),
            pl.BlockSpec((1, tn), lambda j: (0, j)),
        ],
        out_specs=pl.BlockSpec((rows, tn), lambda j: (0, j)),
        out_shape=jax.ShapeDtypeStruct((rows, n_out), F32),
        compiler_params=_cparams(("arbitrary",)),
        name="adaln_mod",
    )(c_all, ada_w, ada_b.reshape(1, n_out))


def _proj_kernel(x_ref, sc_ref, sh_ref, g_ref, w_ref, wdt_ref, o_ref, dt_ref, hn_ref):
    j = pl.program_id(1)

    @pl.when(j == 0)
    def _():
        x = x_ref[...]
        ms = jnp.mean(x * x, axis=-1, keepdims=True)
        hn = (x * lax.rsqrt(ms + EPS)) * g_ref[...] * (1.0 + sc_ref[...]) + sh_ref[...]
        hn = hn.reshape(hn_ref.shape)
        hn_ref[...] = hn.astype(BF16)
        dt_ref[...] = _dot_hp(hn, wdt_ref[...])

    o_ref[...] = _dot(hn_ref[...], w_ref[...].astype(BF16))


def _in_projection(x3, mod3, norm_g, w_main, w_dt, n_main, group_rows):
    bsz, t_len, d = x3.shape
    g_blk, r_blk = group_rows
    tm = g_blk * r_blk
    tiles_per_seq = max(t_len // r_blk, 1)
    n_row_tiles = (bsz * t_len) // tm

    def x_map(i, j):
        return (i // tiles_per_seq, i % tiles_per_seq, 0)

    def mod_map(sec):
        return lambda i, j: (i // tiles_per_seq, 0, sec)

    return pl.pallas_call(
        _proj_kernel,
        grid=(n_row_tiles, n_main // PROJ_TN),
        in_specs=[
            pl.BlockSpec((g_blk, r_blk, d), x_map),
            pl.BlockSpec((g_blk, 1, d), mod_map(1)),
            pl.BlockSpec((g_blk, 1, d), mod_map(0)),
            pl.BlockSpec((1, d), lambda i, j: (0, 0)),
            pl.BlockSpec((d, PROJ_TN), lambda i, j: (0, j)),
            pl.BlockSpec((d, LANES), lambda i, j: (0, 0)),
        ],
        out_specs=[
            pl.BlockSpec((tm, PROJ_TN), lambda i, j: (i, j)),
            pl.BlockSpec((tm, LANES), lambda i, j: (i, 0)),
        ],
        out_shape=[
            jax.ShapeDtypeStruct((bsz * t_len, n_main), F32),
            jax.ShapeDtypeStruct((bsz * t_len, LANES), F32),
        ],
        scratch_shapes=[pltpu.VMEM((tm, d), BF16)],
        compiler_params=_cparams(("arbitrary", "arbitrary")),
        name="in_proj",
    )(x3, mod3, mod3, norm_g.reshape(1, d), w_main, w_dt)


def _tri_inclusive(ch):
    import numpy as np
    t = np.arange(ch)[:, None]
    s = np.arange(LANES)[None, :]
    return jnp.asarray((s <= t).astype(np.float32), dtype=BF16)


def _tril_mask(ch):
    import numpy as np
    t = np.arange(ch)[:, None]
    s = np.arange(LANES)[None, :]
    return jnp.asarray((s <= t).astype(np.float32))


def _level_sizes(ch):
    sizes = []
    sz = SUBLANES
    while 2 * sz <= ch:
        sizes.append(sz)
        sz *= 2
    return sizes


def _level_masks(ch):
    import numpy as np
    t = np.arange(ch)[:, None]
    s = np.arange(LANES)[None, :]
    out = []
    for sz in _level_sizes(ch):
        m = (t // (2 * sz) == s // (2 * sz)) & (t % (2 * sz) >= sz) & (s % (2 * sz) < sz) & (s < ch)
        out.append(m.astype(np.float32))
    if not out:
        out.append(np.zeros((ch, LANES), np.float32))
    return jnp.asarray(np.stack(out))


def _cumsum_rows(tri, x):
    hi, mid, lo = _split3(x)
    return (_dot(tri, _pad_rows(hi, LANES)) + _dot(tri, _pad_rows(mid, LANES))
            + _dot(tri, _pad_rows(lo, LANES)))


def _hgrn_kernel(*refs, ch, valid, nc, gb, has_init, layer):
    if has_init:
        (q_ref, f_ref, i_ref, g_ref, lb_ref, on_ref, tri_ref, masks_ref, s0_ref,
         o_ref, sout_ref, st_ref, b_scr, k_scr) = refs
    else:
        (q_ref, f_ref, i_ref, g_ref, lb_ref, on_ref, tri_ref, masks_ref,
         o_ref, sout_ref, st_ref, b_scr, k_scr) = refs
        s0_ref = None

    n_lb = lb_ref.shape[0]
    lb_rows = [lb_ref[i:i + 1, :] for i in range(n_lb)]
    lb_max = functools.reduce(jnp.maximum, lb_rows)
    lb_exp = [jnp.exp(r - lb_max) for r in lb_rows]
    lb = sum(lb_exp[:layer + 1]) / sum(lb_exp)
    onorm = on_ref[...]
    tri = tri_ref[...]
    sizes = _level_sizes(ch)
    lane8 = lax.broadcasted_iota(I32, (SUBLANES, LANES), 1)
    row8 = lax.broadcasted_iota(I32, (SUBLANES, LANES), 0)

    def chunk(g, ci):
        t0 = pl.multiple_of(ci * ch, ch)
        q = _silu(q_ref[g, pl.ds(t0, ch), :])
        fg = lb + (1.0 - lb) * jax.nn.sigmoid(f_ref[g, pl.ds(t0, ch), :])
        k = 1.0 - fg
        lf = jnp.log(fg)
        v = i_ref[g, pl.ds(t0, ch), :]
        if valid < ch:
            live = lax.broadcasted_iota(I32, (ch, LANES), 0) < valid
            lf = jnp.where(live, lf, 0.0)
            k = jnp.where(live, k, 0.0)
        b = _cumsum_rows(tri, lf)
        b_scr[...] = b
        k_scr[...] = k
        b_end = b_scr[pl.ds(ch - 1, 1), :]
        st = st_ref[...]

        o = _dot_nt((q * jnp.exp(b)).astype(BF16), st.astype(BF16))

        scores = jnp.zeros((ch, LANES), F32)
        for lvl, sz in enumerate(sizes):
            pieces = []
            for m in range(ch // (2 * sz)):
                r = b_scr[pl.ds(2 * sz * m + sz - 1, 1), :]
                pieces.append(jnp.broadcast_to(r, (2 * sz, LANES)))
            r_all = pieces[0] if len(pieces) == 1 else jnp.concatenate(pieces, axis=0)
            e = jnp.exp(-jnp.abs(b - r_all))
            s_l = _dot_nt((q * e).astype(BF16), _pad_rows((k * e).astype(BF16), LANES))
            scores = scores + masks_ref[lvl] * s_l
        blocks = []
        for jb in range(ch // SUBLANES):
            bb = b[SUBLANES * jb:SUBLANES * (jb + 1)]
            qb = q[SUBLANES * jb:SUBLANES * (jb + 1)]
            acc = jnp.zeros((SUBLANES, LANES), F32)
            for sl in range(SUBLANES):
                s = SUBLANES * jb + sl
                bs = b_scr[pl.ds(s, 1), :]
                ks = k_scr[pl.ds(s, 1), :]
                val = qb * ks * jnp.exp(jnp.minimum(bb - bs, 0.0))
                red = jnp.sum(val, axis=1, keepdims=True)
                acc = jnp.where((lane8 == s) & (row8 >= sl), red, acc)
            blocks.append(acc)
        scores = scores + (blocks[0] if len(blocks) == 1 else jnp.concatenate(blocks, axis=0))
        o = o + _dot(scores.astype(BF16), _pad_rows(v.astype(BF16), LANES))

        dk = k * jnp.exp(b_end - b)
        v_t = _pad_rows(v, LANES).T
        st_ref[...] = st * jnp.exp(b_end) + _dot(v_t.astype(BF16),
                                                 _pad_rows(dk.astype(BF16), LANES))

        on = o * lax.rsqrt(jnp.mean(o * o, axis=-1, keepdims=True) + EPS) * onorm
        o_ref[g, pl.ds(t0, ch), :] = on * _silu(g_ref[g, pl.ds(t0, ch), :])

    def seq(g, carry):
        if has_init:
            st_ref[...] = s0_ref[g, 0].T
        else:
            st_ref[...] = jnp.zeros(st_ref.shape, F32)
        if nc == 1:
            chunk(g, 0)
        else:
            lax.fori_loop(0, nc, lambda ci, c: (chunk(g, ci), c)[1], 0)
        sout_ref[g, 0] = st_ref[...].T
        return carry

    if gb == 1:
        seq(0, 0)
    else:
        lax.fori_loop(0, gb, seq, 0)


def _hgrn_heads(proj3, hgrn_lb, onorm_g, s0, *, ch, valid, gb, layer):
    bsz, t_len, _ = proj3.shape
    n_heads = onorm_g.shape[0] // HGRN_HEAD_DIM
    nc = t_len // ch
    has_init = s0 is not None

    def col(sec):
        return lambda b, h: (b, 0, sec * n_heads + h)

    in_specs = [pl.BlockSpec((gb, t_len, LANES), col(s)) for s in range(4)]
    in_specs += [
        pl.BlockSpec((hgrn_lb.shape[0], LANES), lambda b, h: (0, h)),
        pl.BlockSpec((1, LANES), lambda b, h: (0, h)),
        pl.BlockSpec((ch, LANES), lambda b, h: (0, 0)),
        pl.BlockSpec((max(len(_level_sizes(ch)), 1), ch, LANES), lambda b, h: (0, 0, 0)),
    ]
    args = [proj3, proj3, proj3, proj3, hgrn_lb, onorm_g.reshape(1, -1), _tri_inclusive(ch),
            _level_masks(ch)]
    if has_init:
        in_specs.append(pl.BlockSpec((gb, 1, LANES, LANES), lambda b, h: (b, h, 0, 0)))
        args.append(s0)
    kern = functools.partial(_hgrn_kernel, ch=ch, valid=valid, nc=nc, gb=gb, has_init=has_init,
                             layer=layer)
    return pl.pallas_call(
        kern,
        grid=(bsz // gb, n_heads),
        in_specs=in_specs,
        out_specs=[
            pl.BlockSpec((gb, t_len, LANES), lambda b, h: (b, 0, h)),
            pl.BlockSpec((gb, 1, LANES, LANES), lambda b, h: (b, h, 0, 0)),
        ],
        out_shape=[
            jax.ShapeDtypeStruct((bsz, t_len, n_heads * HGRN_HEAD_DIM), F32),
            jax.ShapeDtypeStruct((bsz, n_heads, LANES, LANES), F32),
        ],
        scratch_shapes=[pltpu.VMEM((LANES, LANES), F32), pltpu.VMEM((ch, LANES), F32),
                        pltpu.VMEM((ch, LANES), F32)],
        compiler_params=_cparams(("arbitrary", "arbitrary")),
        name="hgrn2_heads",
    )(*args)


def _softplus(x):
    return jnp.maximum(x, 0.0) + jnp.log1p(jnp.exp(-jnp.abs(x)))


def _ssd_kernel(*refs, ch, valid, nc, gb, has_init):
    if has_init:
        (z_ref, x_ref, b_ref, c_ref, dt_ref, cwx_ref, cwb_ref, cwc_ref, cbx_ref, cbb_ref, cbc_ref,
         xj_ref, bias_ref, alog_ref, d_ref, tri_ref, tril_ref, h0_ref, cx0_ref, cb0_ref, cc0_ref,
         y_ref, hout_ref, ht_ref, wx_ref, wb_ref, wc_ref) = refs
    else:
        (z_ref, x_ref, b_ref, c_ref, dt_ref, cwx_ref, cwb_ref, cwc_ref, cbx_ref, cbb_ref, cbc_ref,
         xj_ref, bias_ref, alog_ref, d_ref, tri_ref, tril_ref,
         y_ref, hout_ref, ht_ref, wx_ref, wb_ref, wc_ref) = refs
        h0_ref = cx0_ref = cb0_ref = cc0_ref = None

    tri = tri_ref[...]
    tril = tril_ref[...]
    xj = xj_ref[0]
    bias = bias_ref[0]
    a_neg = -jnp.exp(alog_ref[0])
    d_skip = d_ref[0]
    lane = lax.broadcasted_iota(I32, (LANES, LANES), 1)
    first_head = lane < SSD_HEAD_DIM
    tail = CONV_WIDTH - 1

    def conv(raw_ref, win_ref, cw_ref, cb_ref, g, t0):
        win_ref[pl.ds(SUBLANES, ch), :] = raw_ref[g, pl.ds(t0, ch), :]
        u = cb_ref[...]
        for j in range(CONV_WIDTH):
            u = u + win_ref[pl.ds(SUBLANES - tail + j, ch), :] * cw_ref[j:j + 1, :]
        if nc > 1:
            win_ref[pl.ds(0, SUBLANES), :] = win_ref[pl.ds(ch, SUBLANES), :]
        return _silu(u)

    def chunk(g, ci):
        t0 = pl.multiple_of(ci * ch, ch)
        xs = conv(x_ref, wx_ref, cwx_ref, cbx_ref, g, t0)
        bm = conv(b_ref, wb_ref, cwb_ref, cbb_ref, g, t0)
        cm = conv(c_ref, wc_ref, cwc_ref, cbc_ref, g, t0)
        dt = _softplus(_dot_exact_rhs(dt_ref[g, pl.ds(t0, ch), :], xj) + bias)
        if valid < ch:
            live = lax.broadcasted_iota(I32, (ch, LANES), 0) < valid
            dt = jnp.where(live, dt, 0.0)
        la_cs = _cumsum_rows(tri, dt * a_neg)
        xdt = xs * dt
        a_end = la_cs[ch - 1:ch, :]

        acs_t = _pad_rows(la_cs, LANES).T
        cb = _dot_nt(cm.astype(BF16), _pad_rows(bm.astype(BF16), LANES))
        xdt_pad = _pad_rows(xdt, LANES)
        y = jnp.zeros((ch, LANES), F32)
        for hl in range(2):
            col = la_cs[:, hl * SSD_HEAD_DIM:hl * SSD_HEAD_DIM + 1]
            row = acs_t[hl * SSD_HEAD_DIM:hl * SSD_HEAD_DIM + 1, :]
            lmat = jnp.exp(jnp.minimum(col - row, 0.0)) * tril
            keep = first_head if hl == 0 else jnp.logical_not(first_head)
            y = y + _dot((cb * lmat).astype(BF16), jnp.where(keep, xdt_pad, 0.0).astype(BF16))

        ht = ht_ref[...]
        y = y + _dot(cm.astype(BF16), ht.astype(BF16)) * jnp.exp(la_cs)
        b_t = _pad_rows(bm, LANES).T
        upd = _pad_rows((xdt * jnp.exp(a_end - la_cs)).astype(BF16), LANES)
        ht_ref[...] = ht * jnp.exp(a_end) + _dot(b_t.astype(BF16), upd)

        y = (y + d_skip * xs) * _silu(z_ref[g, pl.ds(t0, ch), :])
        y_ref[g, pl.ds(t0, ch), :] = y

    def seq(g, carry):
        for win_ref, c0_ref in ((wx_ref, cx0_ref), (wb_ref, cb0_ref), (wc_ref, cc0_ref)):
            win_ref[pl.ds(0, SUBLANES), :] = jnp.zeros((SUBLANES, LANES), F32)
            if has_init:
                win_ref[pl.ds(SUBLANES - tail, tail), :] = c0_ref[g]
        if has_init:
            ht_ref[...] = h0_ref[g, 0].T
        else:
            ht_ref[...] = jnp.zeros(ht_ref.shape, F32)
        if nc == 1:
            chunk(g, 0)
        else:
            lax.fori_loop(0, nc, lambda ci, c: (chunk(g, ci), c)[1], 0)
        hout_ref[g, 0] = ht_ref[...].T
        return carry

    if gb == 1:
        seq(0, 0)
    else:
        lax.fori_loop(0, gb, seq, 0)


def _pair_rows(v, n_pairs):
    return jnp.repeat(v.astype(F32), SSD_HEAD_DIM).reshape(n_pairs, 1, LANES)


def _pair_select(n_pairs):
    import numpy as np
    h = np.arange(LANES)[None, :, None]
    lane = np.arange(LANES)[None, None, :]
    j = np.arange(n_pairs)[:, None, None]
    return jnp.asarray((h == 2 * j + lane // SSD_HEAD_DIM).astype(np.float32), dtype=BF16)


def _ssd_pairs(proj3, dt3, conv_w, conv_b, dt_bias, a_log, d_skip, h0, conv0, *, mix_a, mix_b,
               ch, valid, gb):
    bsz, t_len, _ = proj3.shape
    n_pairs = mix_b // LANES
    pairs_per_group = n_pairs // SSD_GROUPS
    nc = t_len // ch
    has_init = h0 is not None
    z0 = 4 * mix_a // LANES
    x0 = z0 + n_pairs
    b0 = x0 + n_pairs
    c0 = b0 + SSD_GROUPS

    def seq_col(base, per_group):
        if per_group:
            return lambda b, j: (b, 0, base + j // pairs_per_group)
        return lambda b, j: (b, 0, base + j)

    def w_col(base, per_group):
        if per_group:
            return lambda b, j: (0, base + j // pairs_per_group)
        return lambda b, j: (0, base + j)

    blk = (gb, t_len, LANES)
    in_specs = [
        pl.BlockSpec(blk, seq_col(z0, False)),
        pl.BlockSpec(blk, seq_col(x0, False)),
        pl.BlockSpec(blk, seq_col(b0, True)),
        pl.BlockSpec(blk, seq_col(c0, True)),
        pl.BlockSpec(blk, lambda b, j: (b, 0, 0)),
        pl.BlockSpec((CONV_WIDTH, LANES), w_col(0, False)),
        pl.BlockSpec((CONV_WIDTH, LANES), w_col(n_pairs, True)),
        pl.BlockSpec((CONV_WIDTH, LANES), w_col(n_pairs + SSD_GROUPS, True)),
        pl.BlockSpec((1, LANES), w_col(0, False)),
        pl.BlockSpec((1, LANES), w_col(n_pairs, True)),
        pl.BlockSpec((1, LANES), w_col(n_pairs + SSD_GROUPS, True)),
        pl.BlockSpec((1, LANES, LANES), lambda b, j: (j, 0, 0)),
        pl.BlockSpec((1, 1, LANES), lambda b, j: (j, 0, 0)),
        pl.BlockSpec((1, 1, LANES), lambda b, j: (j, 0, 0)),
        pl.BlockSpec((1, 1, LANES), lambda b, j: (j, 0, 0)),
        pl.BlockSpec((ch, LANES), lambda b, j: (0, 0)),
        pl.BlockSpec((ch, LANES), lambda b, j: (0, 0)),
    ]
    conv_b2 = conv_b.reshape(1, -1)
    args = [proj3, proj3, proj3, proj3, dt3, conv_w, conv_w, conv_w, conv_b2, conv_b2, conv_b2,
            _pair_select(n_pairs), _pair_rows(dt_bias, n_pairs), _pair_rows(a_log, n_pairs),
            _pair_rows(d_skip, n_pairs), _tri_inclusive(ch), _tril_mask(ch)]
    if has_init:
        cblk = (gb, CONV_WIDTH - 1, LANES)
        in_specs += [
            pl.BlockSpec((gb, 1, LANES, LANES), lambda b, j: (b, j, 0, 0)),
            pl.BlockSpec(cblk, lambda b, j: (b, 0, j)),
            pl.BlockSpec(cblk, lambda b, j: (b, 0, n_pairs + j // pairs_per_group)),
            pl.BlockSpec(cblk, lambda b, j: (b, 0, n_pairs + SSD_GROUPS + j // pairs_per_group)),
        ]
        args += [h0, conv0, conv0, conv0]
    kern = functools.partial(_ssd_kernel, ch=ch, valid=valid, nc=nc, gb=gb, has_init=has_init)
    win = pltpu.VMEM((ch + SUBLANES, LANES), F32)
    return pl.pallas_call(
        kern,
        grid=(bsz // gb, n_pairs),
        in_specs=in_specs,
        out_specs=[
            pl.BlockSpec(blk, lambda b, j: (b, 0, j)),
            pl.BlockSpec((gb, 1, LANES, LANES), lambda b, j: (b, j, 0, 0)),
        ],
        out_shape=[
            jax.ShapeDtypeStruct((bsz, t_len, mix_b), F32),
            jax.ShapeDtypeStruct((bsz, n_pairs, LANES, LANES), F32),
        ],
        scratch_shapes=[pltpu.VMEM((LANES, LANES), F32), win, win, win],
        compiler_params=_cparams(("arbitrary", "arbitrary")),
        name="ssd_pairs",
    )(*args)


OUT_TM = 512
GROUP_LANE0 = N_EXPERTS
NEG_BIG = -1e30
NO_LANE = 4 * LANES


def _first_lane_of(mask, lane):
    return jnp.min(jnp.where(mask, lane, float(NO_LANE)), axis=1, keepdims=True)


def _route(logits):
    lane_i = lax.broadcasted_iota(I32, logits.shape, 1)
    lane = lane_i.astype(F32)
    is_grp = (lane_i >= GROUP_LANE0) & (lane_i < GROUP_LANE0 + N_EXPERT_GROUPS)
    lg = jnp.where(is_grp, logits, NEG_BIG)
    g_max = jnp.max(lg, axis=1, keepdims=True)
    g_sum = jnp.sum(jnp.where(is_grp, jnp.exp(lg - g_max), 0.0), axis=1, keepdims=True)
    g_idx = _first_lane_of(lg == g_max, lane) - GROUP_LANE0
    gp_top = 1.0 / g_sum
    lane_grp = jnp.right_shift(lane_i, 3).astype(F32)
    in_grp = (lane_i < N_EXPERTS) & (lane_grp == g_idx)
    le = jnp.where(in_grp, logits, NEG_BIG)
    e_max = jnp.max(le, axis=1, keepdims=True)
    e_exp = jnp.where(in_grp, jnp.exp(le - e_max), 0.0)
    ep = e_exp / jnp.sum(e_exp, axis=1, keepdims=True)
    ep = jnp.where(in_grp, ep, -1.0)
    p1 = jnp.max(ep, axis=1, keepdims=True)
    i1 = _first_lane_of(ep == p1, lane)
    ep2 = jnp.where(lane == i1, -1.0, ep)
    p2 = jnp.max(ep2, axis=1, keepdims=True)
    i2 = _first_lane_of(ep2 == p2, lane)
    den = p1 + p2
    return i1, i2, gp_top * p1 / den, gp_top * p2 / den


def _outproj_kernel(o_ref, yz_ref, x_ref, g1_ref, sc_ref, sh_ref, ng_ref, sng_ref, w_ref, wr_ref,
                    br_ref, tri_ref, cnt0_ref,
                    x1_ref, hn_ref, ids_ref, wts_ref, rank_ref, cnt_ref,
                    lhs_ref, x1s_ref, cnts_ref, *, n_col, gw):
    i = pl.program_id(0)
    j = pl.program_id(1)
    tm = lhs_ref.shape[0]
    tn = w_ref.shape[1]
    mix_a = o_ref.shape[1]

    @pl.when((i == 0) & (j == 0))
    def _():
        cnts_ref[...] = cnt0_ref[...]

    @pl.when(j == 0)
    def _():
        lhs_ref[:, 0:mix_a] = o_ref[...].astype(BF16)
        for g in range(yz_ref.shape[1] // gw):
            seg = yz_ref[:, g * gw:(g + 1) * gw]
            ms = jnp.mean(seg * seg, axis=-1, keepdims=True)
            seg = seg * lax.rsqrt(ms + EPS) * sng_ref[:, g * gw:(g + 1) * gw]
            lhs_ref[:, mix_a + g * gw:mix_a + (g + 1) * gw] = seg.astype(BF16)

    acc = _dot(lhs_ref[...], w_ref[...].astype(BF16))
    x1 = x_ref[...] + g1_ref[...] * acc.reshape(x_ref.shape)
    x1_ref[...] = x1
    x1s_ref[j] = x1.reshape(tm, tn)

    @pl.when(j == n_col - 1)
    def _():
        ssq = jnp.zeros((tm, 1), F32)
        for c in range(n_col):
            xc = x1s_ref[c]
            ssq = ssq + jnp.sum(xc * xc, axis=-1, keepdims=True)
        rs = lax.rsqrt(ssq / (n_col * tn) + EPS)
        gshape = x_ref.shape
        logits = jnp.zeros((tm, LANES), F32) + br_ref[...]
        for c in range(n_col):
            cols = slice(c * tn, (c + 1) * tn)
            hn = (x1s_ref[c] * rs * ng_ref[:, cols]).reshape(gshape)
            hn = (hn * (1.0 + sc_ref[:, :, cols]) + sh_ref[:, :, cols]).reshape(tm, tn)
            hn_ref[:, cols] = hn
            logits = logits + _dot_hp(hn, wr_ref[cols, :])
        i1, i2, w1, w2 = _route(logits)
        lane = lax.broadcasted_iota(I32, (tm, LANES), 1)
        hit1 = lane.astype(F32) == i1
        hit2 = lane.astype(F32) == i2
        onehot = jnp.where(hit1 | hit2, 1.0, 0.0).astype(BF16)
        before = _dot(tri_ref[...], onehot) + cnts_ref[0:1, :]
        r1 = jnp.sum(jnp.where(hit1, before, 0.0), axis=1, keepdims=True)
        r2 = jnp.sum(jnp.where(hit2, before, 0.0), axis=1, keepdims=True)
        ids_ref[...] = jnp.where(lane == 0, i1, jnp.where(lane == 1, i2, 0.0)).astype(I32)
        wts_ref[...] = jnp.where(lane == 0, w1, jnp.where(lane == 1, w2, 0.0))
        rank_ref[...] = jnp.where(lane == 0, r1, jnp.where(lane == 1, r2, 0.0)).astype(I32)
        total = cnts_ref[0:1, :] + jnp.sum(onehot.astype(F32), axis=0, keepdims=True)
        cnts_ref[...] = jnp.broadcast_to(total, cnts_ref.shape)
        cnt_ref[...] = jnp.broadcast_to(total, cnt_ref.shape)


def _strict_lower(n):
    import numpy as np
    t = np.arange(n)[:, None]
    s = np.arange(n)[None, :]
    return jnp.asarray((s < t).astype(np.float32), dtype=BF16)


def _out_projection(o3, yz3, x3, mod3, norm_g, ssm_norm_g, w_out, w_router, b_router, cnt0,
                    group_rows):
    bsz, t_len, d = x3.shape
    g_blk, r_blk = group_rows
    tm = g_blk * r_blk
    tn = PROJ_TN
    n_col = d // tn
    mix_a = o3.shape[-1]
    mix_b = yz3.shape[-1]
    n_tok = bsz * t_len
    tiles_per_seq = max(t_len // r_blk, 1)
    sec = d // tn

    def xmap(i, j):
        return (i // tiles_per_seq, i % tiles_per_seq, j)

    def modmap(s):
        return lambda i, j: (i // tiles_per_seq, 0, s)

    row = lambda i, j: (i, 0)
    const2 = lambda i, j: (0, 0)
    kern = functools.partial(_outproj_kernel, n_col=n_col, gw=mix_b // SSD_GROUPS)
    return pl.pallas_call(
        kern,
        grid=(n_tok // tm, n_col),
        in_specs=[
            pl.BlockSpec((tm, mix_a), row),
            pl.BlockSpec((tm, mix_b), row),
            pl.BlockSpec((g_blk, r_blk, tn), xmap),
            pl.BlockSpec((g_blk, 1, tn), lambda i, j: (i // tiles_per_seq, 0, 2 * sec + j)),
            pl.BlockSpec((g_blk, 1, d), modmap(4)),
            pl.BlockSpec((g_blk, 1, d), modmap(3)),
            pl.BlockSpec((1, d), const2),
            pl.BlockSpec((1, mix_b), const2),
            pl.BlockSpec((d, tn), lambda i, j: (0, j)),
            pl.BlockSpec((d, LANES), const2),
            pl.BlockSpec((1, LANES), const2),
            pl.BlockSpec((tm, tm), const2),
            pl.BlockSpec((SUBLANES, LANES), const2),
        ],
        out_specs=[
            pl.BlockSpec((g_blk, r_blk, tn), xmap),
            pl.BlockSpec((tm, d), row),
            pl.BlockSpec((tm, LANES), row),
            pl.BlockSpec((tm, LANES), row),
            pl.BlockSpec((tm, LANES), row),
            pl.BlockSpec((SUBLANES, LANES), const2),
        ],
        out_shape=[
            jax.ShapeDtypeStruct((bsz, t_len, d), F32),
            jax.ShapeDtypeStruct((n_tok, d), F32),
            jax.ShapeDtypeStruct((n_tok, LANES), I32),
            jax.ShapeDtypeStruct((n_tok, LANES), F32),
            jax.ShapeDtypeStruct((n_tok, LANES), I32),
            jax.ShapeDtypeStruct((SUBLANES, LANES), F32),
        ],
        scratch_shapes=[pltpu.VMEM((tm, d), BF16), pltpu.VMEM((n_col, tm, tn), F32),
                        pltpu.VMEM((SUBLANES, LANES), F32)],
        compiler_params=_cparams(("arbitrary", "arbitrary")),
        name="out_proj_router",
    )(o3.reshape(n_tok, mix_a), yz3.reshape(n_tok, mix_b), x3, mod3, mod3, mod3,
      norm_g.reshape(1, d), ssm_norm_g.reshape(1, mix_b), w_out, w_router, b_router,
      _strict_lower(tm), cnt0)


def _positions_kernel(ids_ref, rank_ref, cnt_ref, pos_ref, te_ref, nt_ref, *, block):
    lane8 = lax.broadcasted_iota(I32, (SUBLANES, LANES), 1)
    cnt = cnt_ref[...]
    tiles = jnp.floor((cnt + (MOE_TM - 1)) * (1.0 / MOE_TM))
    tiles = jnp.where(lane8 < N_EXPERTS, tiles, 0.0)
    incl = tiles
    shift = 1
    while shift < N_EXPERTS:
        incl = incl + jnp.where(lane8 >= shift, pltpu.roll(incl, shift, axis=1), 0.0)
        shift *= 2
    offs = ((incl - tiles) * MOE_TM)[0:1, :]
    nt_ref[...] = jnp.sum(tiles, axis=1, keepdims=True).astype(I32) + jnp.zeros(nt_ref.shape, I32)

    incl_col = jnp.broadcast_to(incl[0:1, :], (LANES, LANES)).T
    e_row = lax.broadcasted_iota(I32, (LANES, LANES), 0)
    i_lane = lax.broadcasted_iota(I32, (LANES, LANES), 1).astype(F32)
    done = jnp.where((incl_col <= i_lane) & (e_row < N_EXPERTS), 1.0, 0.0)
    te = jnp.minimum(jnp.sum(done, axis=0, keepdims=True), N_EXPERTS - 1.0)
    te_ref[...] = te.astype(I32) + jnp.zeros(te_ref.shape, I32)

    lane = lax.broadcasted_iota(I32, (block, LANES), 1)

    def body(bi, c):
        r0 = pl.multiple_of(bi * block, block)
        ids = ids_ref[pl.ds(r0, block), :]
        rank = rank_ref[pl.ds(r0, block), :]
        out = jnp.zeros((block, LANES), I32)
        for k in range(TOP_K):
            off = jnp.sum(jnp.where(lane == ids[:, k:k + 1], offs, 0.0), axis=1, keepdims=True)
            out = jnp.where(lane == k, off.astype(I32) + rank[:, k:k + 1], out)
        pos_ref[pl.ds(r0, block), :] = out
        return c

    lax.fori_loop(0, ids_ref.shape[0] // block, body, 0)


def _positions(ids, rank, cnt):
    n_tok = ids.shape[0]
    block = 512
    kern = functools.partial(_positions_kernel, block=block)
    full = lambda s: pl.BlockSpec(s, lambda: tuple(0 for _ in s))
    return pl.pallas_call(
        kern,
        in_specs=[full((n_tok, LANES)), full((n_tok, LANES)), full((SUBLANES, LANES))],
        out_specs=[full((n_tok, LANES)), full((SUBLANES, LANES)), full((SUBLANES, LANES))],
        out_shape=[jax.ShapeDtypeStruct((n_tok, LANES), I32),
                   jax.ShapeDtypeStruct((SUBLANES, LANES), I32),
                   jax.ShapeDtypeStruct((SUBLANES, LANES), I32)],
        compiler_params=pltpu.CompilerParams(vmem_limit_bytes=VMEM_LIMIT),
        name="route_positions",
    )(ids, rank, cnt)


def _row_copy(src_ref, src_row, dst_ref, dst_row, sem):
    return pltpu.make_async_copy(src_ref.at[pl.ds(src_row, 1), :], dst_ref.at[pl.ds(dst_row, 1), :],
                                 sem)


def _dispatch_kernel(pos_ref, hn_ref, xg_in_ref, xg_ref, sem):
    del xg_in_ref
    rows = hn_ref.shape[0]
    base = pl.program_id(0) * rows

    def issue(r, c):
        for k in range(TOP_K):
            _row_copy(hn_ref, r, xg_ref, pos_ref[(base + r) * TOP_K + k], sem).start()
        return c

    def drain(r, c):
        for k in range(TOP_K):
            _row_copy(hn_ref, r, xg_ref, 0, sem).wait()
        return c

    lax.fori_loop(0, rows, issue, 0)
    lax.fori_loop(0, rows, drain, 0)


def _dispatch(pos_flat, hn, xg):
    n_tok, d = hn.shape
    grid_spec = pltpu.PrefetchScalarGridSpec(
        num_scalar_prefetch=1,
        grid=(n_tok // ROW_DMA_TILE,),
        in_specs=[pl.BlockSpec((ROW_DMA_TILE, d), lambda i, pos: (i, 0)),
                  pl.BlockSpec(memory_space=pl.ANY)],
        out_specs=pl.BlockSpec(memory_space=pl.ANY),
        scratch_shapes=[pltpu.SemaphoreType.DMA(())],
    )
    return pl.pallas_call(
        _dispatch_kernel,
        grid_spec=grid_spec,
        out_shape=jax.ShapeDtypeStruct(xg.shape, xg.dtype),
        input_output_aliases={2: 0},
        compiler_params=_cparams(("arbitrary",)),
        name="moe_dispatch",
    )(pos_flat, hn, xg)


def _expert_changed(te_ref, t):
    return (t == 0) | (te_ref[t] != te_ref[jnp.maximum(t - 1, 0)])


def _moe_up_kernel(te_ref, nt_ref, x_ref, w1_ref, w3_ref, act_ref, w1b_ref, w3b_ref):
    t = pl.program_id(1)

    @pl.when(t < nt_ref[0])
    def _():
        @pl.when(_expert_changed(te_ref, t))
        def _():
            w1b_ref[...] = w1_ref[0].astype(BF16)
            w3b_ref[...] = w3_ref[0].astype(BF16)

        xb = x_ref[...].astype(BF16)
        a = _dot(xb, w1b_ref[...])
        b = _dot(xb, w3b_ref[...])
        act_ref[...] = (_silu(a) * b).astype(BF16)

    @pl.when(t >= nt_ref[0])
    def _():
        act_ref[...] = jnp.zeros(act_ref.shape, act_ref.dtype)


def _moe_down_kernel(te_ref, nt_ref, act_ref, w2_ref, out_ref, w2b_ref):
    t = pl.program_id(1)

    @pl.when(t < nt_ref[0])
    def _():
        @pl.when(_expert_changed(te_ref, t))
        def _():
            w2b_ref[...] = w2_ref[0].astype(BF16)

        out_ref[...] = _dot(act_ref[...], w2b_ref[...])

    @pl.when(t >= nt_ref[0])
    def _():
        out_ref[...] = jnp.zeros(out_ref.shape, out_ref.dtype)


def _moe_experts(te, nt, xg, w1, w3, w2):
    n_rows, d = xg.shape
    d_exp = w1.shape[-1]
    n_tiles = n_rows // MOE_TM
    halves = 2
    hn = d_exp // halves
    hd = d // halves

    def tile(h, t, te_ref, nt_ref):
        return jnp.minimum(t, nt_ref[0] - 1)

    act = pl.pallas_call(
        _moe_up_kernel,
        grid_spec=pltpu.PrefetchScalarGridSpec(
            num_scalar_prefetch=2,
            grid=(halves, n_tiles),
            in_specs=[
                pl.BlockSpec((MOE_TM, d), lambda h, t, te_r, nt_r: (tile(h, t, te_r, nt_r), 0)),
                pl.BlockSpec((1, d, hn), lambda h, t, te_r, nt_r: (te_r[tile(h, t, te_r, nt_r)], 0, h)),
                pl.BlockSpec((1, d, hn), lambda h, t, te_r, nt_r: (te_r[tile(h, t, te_r, nt_r)], 0, h)),
            ],
            out_specs=pl.BlockSpec((MOE_TM, hn), lambda h, t, te_r, nt_r: (t, h)),
            scratch_shapes=[pltpu.VMEM((d, hn), BF16), pltpu.VMEM((d, hn), BF16)],
        ),
        out_shape=jax.ShapeDtypeStruct((n_rows, d_exp), BF16),
        compiler_params=_cparams(("arbitrary", "arbitrary")),
        name="moe_up",
    )(te, nt, xg, w1, w3)

    return pl.pallas_call(
        _moe_down_kernel,
        grid_spec=pltpu.PrefetchScalarGridSpec(
            num_scalar_prefetch=2,
            grid=(halves, n_tiles),
            in_specs=[
                pl.BlockSpec((MOE_TM, d_exp), lambda h, t, te_r, nt_r: (tile(h, t, te_r, nt_r), 0)),
                pl.BlockSpec((1, d_exp, hd), lambda h, t, te_r, nt_r: (te_r[tile(h, t, te_r, nt_r)], 0, h)),
            ],
            out_specs=pl.BlockSpec((MOE_TM, hd), lambda h, t, te_r, nt_r: (t, h)),
            scratch_shapes=[pltpu.VMEM((d_exp, hd), BF16)],
        ),
        out_shape=jax.ShapeDtypeStruct((n_rows, d), F32),
        compiler_params=_cparams(("arbitrary", "arbitrary")),
        name="moe_down",
    )(te, nt, act, w2)


def _combine_kernel(pos_ref, x1_ref, g2_ref, wts_ref, fg_ref, eo_ref, y_ref, buf_ref, sem):
    rows = wts_ref.shape[0]
    base = pl.program_id(0) * rows

    def issue(r, c):
        for k in range(TOP_K):
            _row_copy(eo_ref, pos_ref[(base + r) * TOP_K + k], buf_ref.at[k], r, sem).start()
        return c

    def drain(r, c):
        for k in range(TOP_K):
            _row_copy(eo_ref, 0, buf_ref.at[k], r, sem).wait()
        return c

    lax.fori_loop(0, rows, issue, 0)
    lax.fori_loop(0, rows, drain, 0)
    w = wts_ref[...]
    moe = w[:, 0:1] * buf_ref[0] + w[:, 1:2] * buf_ref[1]
    x2 = x1_ref[...] + g2_ref[...] * moe.reshape(x1_ref.shape)
    ms = jnp.mean(x2 * x2, axis=-1, keepdims=True)
    y_ref[...] = x2 * lax.rsqrt(ms + EPS) * fg_ref[...]


def _combine(pos_flat, x1, mod3, wts, final_g, expert_out, group_rows):
    bsz, t_len, d = x1.shape
    g_blk, r_blk = group_rows
    tm = g_blk * r_blk
    tiles_per_seq = max(t_len // r_blk, 1)

    def xmap(i, pos):
        return (i // tiles_per_seq, i % tiles_per_seq, 0)

    return pl.pallas_call(
        _combine_kernel,
        grid_spec=pltpu.PrefetchScalarGridSpec(
            num_scalar_prefetch=1,
            grid=((bsz * t_len) // tm,),
            in_specs=[
                pl.BlockSpec((g_blk, r_blk, d), xmap),
                pl.BlockSpec((g_blk, 1, d), lambda i, pos: (i // tiles_per_seq, 0, 5)),
                pl.BlockSpec((tm, LANES), lambda i, pos: (i, 0)),
                pl.BlockSpec((1, d), lambda i, pos: (0, 0)),
                pl.BlockSpec(memory_space=pl.ANY),
            ],
            out_specs=pl.BlockSpec((g_blk, r_blk, d), xmap),
            scratch_shapes=[pltpu.VMEM((TOP_K, tm, d), F32), pltpu.SemaphoreType.DMA(())],
        ),
        out_shape=jax.ShapeDtypeStruct((bsz, t_len, d), F32),
        compiler_params=_cparams(("arbitrary",)),
        name="moe_combine",
    )(pos_flat, x1, mod3, wts, final_g.reshape(1, d), expert_out)


def kernel(x_prompt, x_sample, state_hgrn, state_ssm, state_conv, c_prompt, c_sample, ada_w, ada_b,
           norm1_g, norm2_g, w_in, hgrn_lb, hgrn_onorm_g, conv_w, conv_b, dt_bias, a_log, d_skip,
           ssm_norm_g, w_out, w_grp, b_grp, w_rt, b_rt, w1, w3, w2, final_g):
    depth = w_in.shape[0]
    assert depth == 1, "single-layer trunk"
    layer = 0
    bp, t_p, d = x_prompt.shape
    bs, t_s, _ = x_sample.shape
    mix_a = hgrn_onorm_g.shape[1]
    mix_b = ssm_norm_g.shape[1]
    conv_dim = conv_w.shape[2]
    n_main = 4 * mix_a + mix_b + conv_dim
    n_ssd_heads = dt_bias.shape[1]
    n_pairs = mix_b // LANES
    xbc0 = 4 * mix_a + mix_b
    tail = CONV_WIDTH - 1
    assert t_s <= SUBLANES and t_s >= tail and t_p % PROMPT_CHUNK == 0

    xs_pad = jnp.pad(x_sample, ((0, 0), (0, SUBLANES - t_s), (0, 0)))
    n_c = bp + bs
    c_rows = -(-n_c // SUBLANES) * SUBLANES
    c_all = jnp.pad(jnp.concatenate([c_prompt, c_sample], axis=0), ((0, c_rows - n_c), (0, 0)))

    mod = _modulation(c_all, ada_w[layer], ada_b[layer])
    mod_p = mod[:bp].reshape(bp, 1, -1)
    mod_s = mod[bp:n_c].reshape(bs, 1, -1)
    w_dt = jnp.pad(w_in[layer][:, n_main:], ((0, 0), (0, LANES - n_ssd_heads)))
    w_router = jnp.pad(jnp.concatenate([w_rt[layer], w_grp[layer]], axis=1),
                       ((0, 0), (0, LANES - N_EXPERTS - N_EXPERT_GROUPS)))
    b_router = jnp.pad(jnp.concatenate([b_rt[layer], b_grp[layer]]),
                       (0, LANES - N_EXPERTS - N_EXPERT_GROUPS)).reshape(1, LANES)

    groups = (
        (x_prompt, mod_p, None, None, None, t_p, PROMPT_CHUNK, PROMPT_CHUNK, 1, (1, PROJ_TM),
         (1, OUT_TM)),
        (xs_pad, mod_s, state_hgrn[layer], state_ssm[layer].reshape(bs, n_pairs, LANES, LANES),
         state_conv[layer], t_s, SUBLANES, t_s, 16, (PROJ_TM // SUBLANES, SUBLANES),
         (OUT_TM // SUBLANES, SUBLANES)),
    )

    cnt = jnp.zeros((SUBLANES, LANES), F32)
    per_group = []
    for x3, mod3, s0, h0, c0, t_real, ch, valid, gb, proj_gr, out_gr in groups:
        bsz, t_len, _ = x3.shape
        proj, dt = _in_projection(x3, mod3, norm1_g[layer], w_in[layer], w_dt, n_main, proj_gr)
        proj3 = proj.reshape(bsz, t_len, n_main)
        dt3 = dt.reshape(bsz, t_len, LANES)
        o3, s_new = _hgrn_heads(proj3, hgrn_lb, hgrn_onorm_g[layer], s0, ch=ch, valid=valid, gb=gb,
                                layer=layer)
        yz3, h_new = _ssd_pairs(proj3, dt3, conv_w[layer], conv_b[layer], dt_bias[layer],
                                a_log[layer], d_skip[layer], h0, c0, mix_a=mix_a, mix_b=mix_b,
                                ch=ch, valid=valid, gb=gb)
        conv_new = proj3[:, t_real - tail:t_real, xbc0:xbc0 + conv_dim]
        x1, hn, ids, wts, rank, cnt = _out_projection(
            o3, yz3, x3, mod3, norm2_g[layer], ssm_norm_g[layer], w_out[layer], w_router, b_router,
            cnt, out_gr)
        per_group.append(dict(x1=x1, hn=hn, ids=ids, wts=wts, rank=rank, mod3=mod3, out_gr=out_gr,
                              s_new=s_new, h_new=h_new, conv_new=conv_new))

    ids_all = jnp.concatenate([g["ids"] for g in per_group], axis=0)
    rank_all = jnp.concatenate([g["rank"] for g in per_group], axis=0)
    n_tok = ids_all.shape[0]
    pos, te, nt = _positions(ids_all, rank_all, cnt)
    pos_flat = pos[:, :TOP_K].reshape(-1)
    n_tiles = -(-(n_tok * TOP_K + N_EXPERTS * (MOE_TM - 1)) // MOE_TM)
    assert n_tiles <= LANES
    te_vec = te[0, :n_tiles]
    nt_vec = nt[0, :1]

    xg = jnp.zeros((n_tiles * MOE_TM, d), F32)
    row0 = 0
    for g in per_group:
        rows = g["hn"].shape[0]
        g["pos"] = pos_flat[row0 * TOP_K:(row0 + rows) * TOP_K]
        xg = _dispatch(g["pos"], g["hn"], xg)
        row0 += rows
    expert_out = _moe_experts(te_vec, nt_vec, xg, w1[layer], w3[layer], w2[layer])
    ys = [_combine(g["pos"], g["x1"], g["mod3"], g["wts"], final_g, expert_out, g["out_gr"])
          for g in per_group]

    gp, gs = per_group
    return (
        ys[0],
        ys[1][:, :t_s],
        gp["s_new"][None],
        gp["h_new"].reshape(1, bp, n_ssd_heads, SSD_HEAD_DIM, SSD_STATE),
        gp["conv_new"][None],
        gs["s_new"][None],
        gs["h_new"].reshape(1, bs, n_ssd_heads, SSD_HEAD_DIM, SSD_STATE),
        gs["conv_new"][None],
    )
```

```python
import functools

import jax
import jax.numpy as jnp
from jax import lax
from jax.experimental import pallas as pl
from jax.experimental.pallas import tpu as pltpu

F32 = jnp.float32
BF16 = jnp.bfloat16
I32 = jnp.int32
EPS = 1e-6
LOG2E = 1.4426950408889634

LANES = 128
SUBLANES = 8
VMEM_LIMIT = 56 * 1024 * 1024

HGRN_HEAD_DIM = 128
SSD_HEAD_DIM = 64
SSD_GROUPS = 2
SSD_STATE = 128
CONV_WIDTH = 4
N_EXPERT_GROUPS = 4
EXPERTS_PER_GROUP = 8
N_EXPERTS = N_EXPERT_GROUPS * EXPERTS_PER_GROUP
TOP_K = 2

PROJ_TM = 1024
PROJ_TN = 512
MOE_TM = 256
PROMPT_CHUNK = 64
MIXER_ROWS = 512
ROW_DMA_TILE = 256


def _cparams(sem):
    return pltpu.CompilerParams(dimension_semantics=sem, vmem_limit_bytes=VMEM_LIMIT)


def _dot(a, b):
    return jnp.dot(a, b, preferred_element_type=F32)


def _dot_nt(a, b):
    return lax.dot_general(a, b, (((1,), (1,)), ((), ())), preferred_element_type=F32)


def _split2(x):
    hi = x.astype(BF16)
    lo = (x - hi.astype(F32)).astype(BF16)
    return hi, lo


def _split3(x):
    hi = x.astype(BF16)
    r = x - hi.astype(F32)
    mid = r.astype(BF16)
    lo = (r - mid.astype(F32)).astype(BF16)
    return hi, mid, lo


def _dot_exact_lhs(m_bf16, x):
    hi, mid, lo = _split3(x)
    return _dot(m_bf16, hi) + _dot(m_bf16, mid) + _dot(m_bf16, lo)


def _dot_exact_rhs(x, m_bf16):
    hi, mid, lo = _split3(x)
    return _dot(hi, m_bf16) + _dot(mid, m_bf16) + _dot(lo, m_bf16)


def _dot_hp(a, w):
    ah, al = _split2(a)
    wh, wl = _split2(w)
    return _dot(ah, wh) + _dot(al, wh) + _dot(ah, wl)


def _silu(x):
    return x * jax.nn.sigmoid(x)


def _pad_rows(x, rows):
    if x.shape[0] == rows:
        return x
    return jnp.concatenate([x, jnp.zeros((rows - x.shape[0], x.shape[1]), x.dtype)], axis=0)


def _mod_kernel(c_ref, w_ref, b_ref, o_ref):
    a = _silu(c_ref[...])
    o_ref[...] = _dot_hp(a, w_ref[...]) + b_ref[...]


def _modulation(c_all, ada_w, ada_b):
    rows, d = c_all.shape
    n_out = ada_w.shape[1]
    tn = 1024
    return pl.pallas_call(
        _mod_kernel,
        grid=(n_out // tn,),
        in_specs=[
            pl.BlockSpec((rows, d), lambda j: (0, 0)),
            pl.BlockSpec((d, tn), lambda j: (0, j)),
            pl.BlockSpec((1, tn), lambda j: (0, j)),
        ],
        out_specs=pl.BlockSpec((rows, tn), lambda j: (0, j)),
        out_shape=jax.ShapeDtypeStruct((rows, n_out), F32),
        compiler_params=_cparams(("arbitrary",)),
        name="adaln_mod",
    )(c_all, ada_w, ada_b.reshape(1, n_out))


def _proj_kernel(x_ref, sc_ref, sh_ref, g_ref, w_ref, wdt_ref, o_ref, dt_ref, hn_ref):
    j = pl.program_id(1)

    @pl.when(j == 0)
    def _():
        x = x_ref[...]
        ms = jnp.mean(x * x, axis=-1, keepdims=True)
        hn = (x * lax.rsqrt(ms + EPS)) * g_ref[...] * (1.0 + sc_ref[...]) + sh_ref[...]
        hn = hn.reshape(hn_ref.shape)
        hn_ref[...] = hn.astype(BF16)
        dt_ref[...] = _dot_hp(hn, wdt_ref[...])

    o_ref[...] = _dot(hn_ref[...], w_ref[...].astype(BF16))


def _in_projection(x3, mod3, norm_g, w_main, w_dt, n_main, group_rows):
    bsz, t_len, d = x3.shape
    g_blk, r_blk = group_rows
    tm = g_blk * r_blk
    tiles_per_seq = max(t_len // r_blk, 1)
    n_row_tiles = (bsz * t_len) // tm

    def x_map(i, j):
        return (i // tiles_per_seq, i % tiles_per_seq, 0)

    def mod_map(sec):
        return lambda i, j: (i // tiles_per_seq, 0, sec)

    return pl.pallas_call(
        _proj_kernel,
        grid=(n_row_tiles, n_main // PROJ_TN),
        in_specs=[
            pl.BlockSpec((g_blk, r_blk, d), x_map),
            pl.BlockSpec((g_blk, 1, d), mod_map(1)),
            pl.BlockSpec((g_blk, 1, d), mod_map(0)),
            pl.BlockSpec((1, d), lambda i, j: (0, 0)),
            pl.BlockSpec((d, PROJ_TN), lambda i, j: (0, j)),
            pl.BlockSpec((d, LANES), lambda i, j: (0, 0)),
        ],
        out_specs=[
            pl.BlockSpec((tm, PROJ_TN), lambda i, j: (i, j)),
            pl.BlockSpec((tm, LANES), lambda i, j: (i, 0)),
        ],
        out_shape=[
            jax.ShapeDtypeStruct((bsz * t_len, n_main), F32),
            jax.ShapeDtypeStruct((bsz * t_len, LANES), F32),
        ],
        scratch_shapes=[pltpu.VMEM((tm, d), BF16)],
        compiler_params=_cparams(("arbitrary", "arbitrary")),
        name="in_proj",
    )(x3, mod3, mod3, norm_g.reshape(1, d), w_main, w_dt)


def _tri_inclusive(ch):
    import numpy as np
    t = np.arange(ch)[:, None]
    s = np.arange(LANES)[None, :]
    return jnp.asarray((s <= t).astype(np.float32), dtype=BF16)


def _tril_mask(ch):
    import numpy as np
    t = np.arange(ch)[:, None]
    s = np.arange(LANES)[None, :]
    return jnp.asarray((s <= t).astype(np.float32))


def _level_sizes(ch):
    sizes = []
    sz = SUBLANES
    while 2 * sz <= ch:
        sizes.append(sz)
        sz *= 2
    return sizes


def _level_masks(ch):
    import numpy as np
    t = np.arange(ch)[:, None]
    s = np.arange(LANES)[None, :]
    out = []
    for sz in _level_sizes(ch):
        m = (t // (2 * sz) == s // (2 * sz)) & (t % (2 * sz) >= sz) & (s % (2 * sz) < sz) & (s < ch)
        out.append(m.astype(np.float32))
    if not out:
        out.append(np.zeros((ch, LANES), np.float32))
    return jnp.asarray(np.stack(out))


def _cumsum_rows(tri, x):
    hi, mid, lo = _split3(x)
    return (_dot(tri, _pad_rows(hi, LANES)) + _dot(tri, _pad_rows(mid, LANES))
            + _dot(tri, _pad_rows(lo, LANES)))


def _hgrn_kernel(*refs, ch, valid, nc, gb, hb, has_init, layer):
    if has_init:
        (q_ref, f_ref, i_ref, g_ref, lb_ref, on_ref, tri_ref, masks_ref, s0_ref,
         o_ref, sout_ref, st_ref, b_scr, k_scr) = refs
    else:
        (q_ref, f_ref, i_ref, g_ref, lb_ref, on_ref, tri_ref, masks_ref,
         o_ref, sout_ref, st_ref, b_scr, k_scr) = refs
        s0_ref = None

    n_lb = lb_ref.shape[0]
    lb_rows = [lb_ref[i:i + 1, :] for i in range(n_lb)]
    lb_max = functools.reduce(jnp.maximum, lb_rows)
    lb_exp = [jnp.exp(r - lb_max) for r in lb_rows]
    lb = sum(lb_exp[:layer + 1]) / sum(lb_exp)
    onorm = on_ref[...]
    tri = tri_ref[...]
    sizes = _level_sizes(ch)
    lane8 = lax.broadcasted_iota(I32, (SUBLANES, LANES), 1)
    row8 = lax.broadcasted_iota(I32, (SUBLANES, LANES), 0)
    tb = pl.program_id(2)

    def chunk(c, g, hh, ci):
        cols = slice(hh * LANES, (hh + 1) * LANES)
        t0 = pl.multiple_of(ci * ch, ch)
        rows = pl.ds(t0, ch)
        lb_h = lb[:, cols]
        q = _silu(q_ref[g, rows, cols])
        fg = lb_h + (1.0 - lb_h) * jax.nn.sigmoid(f_ref[g, rows, cols])
        k = 1.0 - fg
        lf = jnp.log(fg)
        v = i_ref[g, rows, cols]
        if valid < ch:
            live = lax.broadcasted_iota(I32, (ch, LANES), 0) < valid
            lf = jnp.where(live, lf, 0.0)
            k = jnp.where(live, k, 0.0)
        b = _cumsum_rows(tri, lf)
        b2 = b * LOG2E
        b_scr[c] = b2
        k_scr[c] = k
        b_end = b_scr[c, pl.ds(ch - 1, 1), :]
        st = st_ref[c]

        o = _dot_nt((q * jnp.exp2(b2)).astype(BF16), st.astype(BF16))

        scores = jnp.zeros((ch, LANES), F32)
        for lvl, sz in enumerate(sizes):
            pieces = []
            for m in range(ch // (2 * sz)):
                r = b_scr[c, pl.ds(2 * sz * m + sz - 1, 1), :]
                pieces.append(jnp.broadcast_to(r, (2 * sz, LANES)))
            r_all = pieces[0] if len(pieces) == 1 else jnp.concatenate(pieces, axis=0)
            e = jnp.exp2(-jnp.abs(b2 - r_all))
            s_l = _dot_nt((q * e).astype(BF16), _pad_rows((k * e).astype(BF16), LANES))
            scores = scores + masks_ref[lvl] * s_l
        blocks = []
        for jb in range(ch // SUBLANES):
            bb = b2[SUBLANES * jb:SUBLANES * (jb + 1)]
            qb = q[SUBLANES * jb:SUBLANES * (jb + 1)]
            acc = jnp.zeros((SUBLANES, LANES), F32)
            for sl in range(SUBLANES):
                s = SUBLANES * jb + sl
                bs = b_scr[c, pl.ds(s, 1), :]
                ks = k_scr[c, pl.ds(s, 1), :]
                val = (qb * ks) * jnp.exp2(bb - bs)
                red = jnp.sum(val, axis=1, keepdims=True)
                acc = jnp.where(lane8 == s, red, acc)
            blocks.append(jnp.where(lane8 - SUBLANES * jb <= row8, acc, 0.0))
        scores = scores + (blocks[0] if len(blocks) == 1 else jnp.concatenate(blocks, axis=0))
        o = o + _dot(scores.astype(BF16), _pad_rows(v.astype(BF16), LANES))

        dk = k * jnp.exp2(b_end - b2)
        v_t = _pad_rows(v, LANES).T
        st_ref[c] = st * jnp.exp2(b_end) + _dot(v_t.astype(BF16),
                                                _pad_rows(dk.astype(BF16), LANES))

        on = o * lax.rsqrt(jnp.mean(o * o, axis=-1, keepdims=True) + EPS) * onorm[:, cols]
        o_ref[g, rows, cols] = on * _silu(g_ref[g, rows, cols])

    chains = [(g * hb + hh, g, hh) for g in range(gb) for hh in range(hb)]

    @pl.when(tb == 0)
    def _():
        for c, g, hh in chains:
            if has_init:
                st_ref[c] = s0_ref[g, hh].T
            else:
                st_ref[c] = jnp.zeros((LANES, LANES), F32)

    def step(ci, carry):
        for c, g, hh in chains:
            chunk(c, g, hh, ci)
        return carry

    if nc == 1:
        step(0, 0)
    else:
        lax.fori_loop(0, nc, step, 0)

    @pl.when(tb == pl.num_programs(2) - 1)
    def _():
        for c, g, hh in chains:
            sout_ref[g, hh] = st_ref[c].T


def _hgrn_heads(proj3, hgrn_lb, onorm_g, s0, *, ch, valid, gb, hb, tt, layer):
    bsz, t_len, _ = proj3.shape
    n_heads = onorm_g.shape[0] // HGRN_HEAD_DIM
    n_hg = n_heads // hb
    nc = tt // ch
    has_init = s0 is not None
    w = hb * LANES

    def col(sec):
        return lambda b, h, t: (b, t, sec * n_hg + h)

    in_specs = [pl.BlockSpec((gb, tt, w), col(s)) for s in range(4)]
    in_specs += [
        pl.BlockSpec((hgrn_lb.shape[0], w), lambda b, h, t: (0, h)),
        pl.BlockSpec((1, w), lambda b, h, t: (0, h)),
        pl.BlockSpec((ch, LANES), lambda b, h, t: (0, 0)),
        pl.BlockSpec((max(len(_level_sizes(ch)), 1), ch, LANES), lambda b, h, t: (0, 0, 0)),
    ]
    args = [proj3, proj3, proj3, proj3, hgrn_lb, onorm_g.reshape(1, -1), _tri_inclusive(ch),
            _level_masks(ch)]
    if has_init:
        in_specs.append(pl.BlockSpec((gb, hb, LANES, LANES), lambda b, h, t: (b, h, 0, 0)))
        args.append(s0)
    kern = functools.partial(_hgrn_kernel, ch=ch, valid=valid, nc=nc, gb=gb, hb=hb,
                             has_init=has_init, layer=layer)
    n_chain = gb * hb
    return pl.pallas_call(
        kern,
        grid=(bsz // gb, n_hg, t_len // tt),
        in_specs=in_specs,
        out_specs=[
            pl.BlockSpec((gb, tt, w), lambda b, h, t: (b, t, h)),
            pl.BlockSpec((gb, hb, LANES, LANES), lambda b, h, t: (b, h, 0, 0)),
        ],
        out_shape=[
            jax.ShapeDtypeStruct((bsz, t_len, n_heads * HGRN_HEAD_DIM), F32),
            jax.ShapeDtypeStruct((bsz, n_heads, LANES, LANES), F32),
        ],
        scratch_shapes=[pltpu.VMEM((n_chain, LANES, LANES), F32),
                        pltpu.VMEM((n_chain, ch, LANES), F32),
                        pltpu.VMEM((n_chain, ch, LANES), F32)],
        compiler_params=_cparams(("arbitrary", "arbitrary", "arbitrary")),
        name="hgrn2_heads",
    )(*args)


def _softplus(x):
    return jnp.maximum(x, 0.0) + jnp.log1p(jnp.exp(-jnp.abs(x)))


def _ssd_kernel(*refs, ch, valid, nc, gb, pb, has_init):
    if has_init:
        (z_ref, x_ref, b_ref, c_ref, dt_ref, cwx_ref, cwb_ref, cwc_ref, cbx_ref, cbb_ref, cbc_ref,
         xj_ref, bias_ref, alog_ref, d_ref, tri_ref, tril_ref, h0_ref, cx0_ref, cb0_ref, cc0_ref,
         y_ref, hout_ref, ht_ref, wx_ref, wb_ref, wc_ref) = refs
    else:
        (z_ref, x_ref, b_ref, c_ref, dt_ref, cwx_ref, cwb_ref, cwc_ref, cbx_ref, cbb_ref, cbc_ref,
         xj_ref, bias_ref, alog_ref, d_ref, tri_ref, tril_ref,
         y_ref, hout_ref, ht_ref, wx_ref, wb_ref, wc_ref) = refs
        h0_ref = cx0_ref = cb0_ref = cc0_ref = None

    tri = tri_ref[...]
    tril = tril_ref[...]
    lane = lax.broadcasted_iota(I32, (LANES, LANES), 1)
    first_head = lane < SSD_HEAD_DIM
    tail = CONV_WIDTH - 1
    tb = pl.program_id(2)

    def conv(raw_ref, win_ref, cw_ref, cb_ref, g, t0):
        win_ref[g, pl.ds(SUBLANES, ch), :] = raw_ref[g, pl.ds(t0, ch), :]
        u = cb_ref[...]
        for j in range(CONV_WIDTH):
            u = u + win_ref[g, pl.ds(SUBLANES - tail + j, ch), :] * cw_ref[j:j + 1, :]
        win_ref[g, pl.ds(0, SUBLANES), :] = win_ref[g, pl.ds(ch, SUBLANES), :]
        return _silu(u)

    def chunk(g, ci):
        t0 = pl.multiple_of(ci * ch, ch)
        rows = pl.ds(t0, ch)
        xs_all = conv(x_ref, wx_ref, cwx_ref, cbx_ref, g, t0)
        bm = conv(b_ref, wb_ref, cwb_ref, cbb_ref, g, t0)
        cm = conv(c_ref, wc_ref, cwc_ref, cbc_ref, g, t0)
        cb = _dot_nt(cm.astype(BF16), _pad_rows(bm.astype(BF16), LANES))
        b_t = _pad_rows(bm, LANES).T
        dt_raw = dt_ref[g, rows, :]
        for p in range(pb):
            c = g * pb + p
            cols = slice(p * LANES, (p + 1) * LANES)
            xs = xs_all[:, cols]
            dt = _softplus(_dot_exact_rhs(dt_raw, xj_ref[p]) + bias_ref[p])
            if valid < ch:
                live = lax.broadcasted_iota(I32, (ch, LANES), 0) < valid
                dt = jnp.where(live, dt, 0.0)
            la_cs = _cumsum_rows(tri, dt * (-jnp.exp(alog_ref[p])))
            xdt = xs * dt
            a_end = la_cs[ch - 1:ch, :]

            acs_t = _pad_rows(la_cs, LANES).T
            xdt_pad = _pad_rows(xdt, LANES)
            y = jnp.zeros((ch, LANES), F32)
            for hl in range(2):
                col = la_cs[:, hl * SSD_HEAD_DIM:hl * SSD_HEAD_DIM + 1]
                row = acs_t[hl * SSD_HEAD_DIM:hl * SSD_HEAD_DIM + 1, :]
                lmat = jnp.exp(jnp.minimum(col - row, 0.0)) * tril
                keep = first_head if hl == 0 else jnp.logical_not(first_head)
                y = y + _dot((cb * lmat).astype(BF16), jnp.where(keep, xdt_pad, 0.0).astype(BF16))

            ht = ht_ref[c]
            y = y + _dot(cm.astype(BF16), ht.astype(BF16)) * jnp.exp(la_cs)
            upd = _pad_rows((xdt * jnp.exp(a_end - la_cs)).astype(BF16), LANES)
            ht_ref[c] = ht * jnp.exp(a_end) + _dot(b_t.astype(BF16), upd)

            y = (y + d_ref[p] * xs) * _silu(z_ref[g, rows, cols])
            y_ref[g, rows, cols] = y

    @pl.when(tb == 0)
    def _():
        for g in range(gb):
            for win_ref, c0_ref in ((wx_ref, cx0_ref), (wb_ref, cb0_ref), (wc_ref, cc0_ref)):
                win_ref[g, pl.ds(0, SUBLANES), :] = jnp.zeros((SUBLANES, win_ref.shape[2]), F32)
                if has_init:
                    win_ref[g, pl.ds(SUBLANES - tail, tail), :] = c0_ref[g]
            for p in range(pb):
                if has_init:
                    ht_ref[g * pb + p] = h0_ref[g, p].T
                else:
                    ht_ref[g * pb + p] = jnp.zeros((LANES, LANES), F32)

    def step(ci, carry):
        for g in range(gb):
            chunk(g, ci)
        return carry

    if nc == 1:
        step(0, 0)
    else:
        lax.fori_loop(0, nc, step, 0)

    @pl.when(tb == pl.num_programs(2) - 1)
    def _():
        for g in range(gb):
            for p in range(pb):
                hout_ref[g, p] = ht_ref[g * pb + p].T


def _pair_rows(v, n_pairs):
    return jnp.repeat(v.astype(F32), SSD_HEAD_DIM).reshape(n_pairs, 1, LANES)


def _pair_select(n_pairs):
    import numpy as np
    h = np.arange(LANES)[None, :, None]
    lane = np.arange(LANES)[None, None, :]
    j = np.arange(n_pairs)[:, None, None]
    return jnp.asarray((h == 2 * j + lane // SSD_HEAD_DIM).astype(np.float32), dtype=BF16)


def _ssd_pairs(proj3, dt3, conv_w, conv_b, dt_bias, a_log, d_skip, h0, conv0, *, mix_a, mix_b,
               ch, valid, gb, tt):
    bsz, t_len, _ = proj3.shape
    n_pairs = mix_b // LANES
    pb = n_pairs // SSD_GROUPS
    w = pb * LANES
    nc = tt // ch
    has_init = h0 is not None
    assert (4 * mix_a) % w == 0
    z0 = 4 * mix_a // w
    x0 = z0 + SSD_GROUPS
    b0 = (4 * mix_a + 2 * mix_b) // LANES
    c0 = b0 + SSD_GROUPS

    seq_w = lambda base: (lambda b, g, t: (b, t, base + g))
    par = lambda base: (lambda b, g, t: (0, base + g))
    const2 = lambda b, g, t: (0, 0)

    in_specs = [
        pl.BlockSpec((gb, tt, w), seq_w(z0)),
        pl.BlockSpec((gb, tt, w), seq_w(x0)),
        pl.BlockSpec((gb, tt, LANES), seq_w(b0)),
        pl.BlockSpec((gb, tt, LANES), seq_w(c0)),
        pl.BlockSpec((gb, tt, LANES), lambda b, g, t: (b, t, 0)),
        pl.BlockSpec((CONV_WIDTH, w), par(0)),
        pl.BlockSpec((CONV_WIDTH, LANES), par(n_pairs)),
        pl.BlockSpec((CONV_WIDTH, LANES), par(n_pairs + SSD_GROUPS)),
        pl.BlockSpec((1, w), par(0)),
        pl.BlockSpec((1, LANES), par(n_pairs)),
        pl.BlockSpec((1, LANES), par(n_pairs + SSD_GROUPS)),
        pl.BlockSpec((pb, LANES, LANES), lambda b, g, t: (g, 0, 0)),
        pl.BlockSpec((pb, 1, LANES), lambda b, g, t: (g, 0, 0)),
        pl.BlockSpec((pb, 1, LANES), lambda b, g, t: (g, 0, 0)),
        pl.BlockSpec((pb, 1, LANES), lambda b, g, t: (g, 0, 0)),
        pl.BlockSpec((ch, LANES), const2),
        pl.BlockSpec((ch, LANES), const2),
    ]
    conv_b2 = conv_b.reshape(1, -1)
    args = [proj3, proj3, proj3, proj3, dt3, conv_w, conv_w, conv_w, conv_b2, conv_b2, conv_b2,
            _pair_select(n_pairs), _pair_rows(dt_bias, n_pairs), _pair_rows(a_log, n_pairs),
            _pair_rows(d_skip, n_pairs), _tri_inclusive(ch), _tril_mask(ch)]
    if has_init:
        tail = CONV_WIDTH - 1
        in_specs += [
            pl.BlockSpec((gb, pb, LANES, LANES), lambda b, g, t: (b, g, 0, 0)),
            pl.BlockSpec((gb, tail, w), lambda b, g, t: (b, 0, g)),
            pl.BlockSpec((gb, tail, LANES), lambda b, g, t: (b, 0, n_pairs + g)),
            pl.BlockSpec((gb, tail, LANES), lambda b, g, t: (b, 0, n_pairs + SSD_GROUPS + g)),
        ]
        args += [h0, conv0, conv0, conv0]
    kern = functools.partial(_ssd_kernel, ch=ch, valid=valid, nc=nc, gb=gb, pb=pb,
                             has_init=has_init)
    return pl.pallas_call(
        kern,
        grid=(bsz // gb, SSD_GROUPS, t_len // tt),
        in_specs=in_specs,
        out_specs=[
            pl.BlockSpec((gb, tt, w), lambda b, g, t: (b, t, g)),
            pl.BlockSpec((gb, pb, LANES, LANES), lambda b, g, t: (b, g, 0, 0)),
        ],
        out_shape=[
            jax.ShapeDtypeStruct((bsz, t_len, mix_b), F32),
            jax.ShapeDtypeStruct((bsz, n_pairs, LANES, LANES), F32),
        ],
        scratch_shapes=[pltpu.VMEM((gb * pb, LANES, LANES), F32),
                        pltpu.VMEM((gb, ch + SUBLANES, w), F32),
                        pltpu.VMEM((gb, ch + SUBLANES, LANES), F32),
                        pltpu.VMEM((gb, ch + SUBLANES, LANES), F32)],
        compiler_params=_cparams(("arbitrary", "arbitrary", "arbitrary")),
        name="ssd_pairs",
    )(*args)


OUT_TM = 512
GROUP_LANE0 = N_EXPERTS
NEG_BIG = -1e30
NO_LANE = 4 * LANES


def _first_lane_of(mask, lane):
    return jnp.min(jnp.where(mask, lane, float(NO_LANE)), axis=1, keepdims=True)


def _route(logits):
    lane_i = lax.broadcasted_iota(I32, logits.shape, 1)
    lane = lane_i.astype(F32)
    is_grp = (lane_i >= GROUP_LANE0) & (lane_i < GROUP_LANE0 + N_EXPERT_GROUPS)
    lg = jnp.where(is_grp, logits, NEG_BIG)
    g_max = jnp.max(lg, axis=1, keepdims=True)
    g_sum = jnp.sum(jnp.where(is_grp, jnp.exp(lg - g_max), 0.0), axis=1, keepdims=True)
    g_idx = _first_lane_of(lg == g_max, lane) - GROUP_LANE0
    gp_top = 1.0 / g_sum
    lane_grp = jnp.right_shift(lane_i, 3).astype(F32)
    in_grp = (lane_i < N_EXPERTS) & (lane_grp == g_idx)
    le = jnp.where(in_grp, logits, NEG_BIG)
    e_max = jnp.max(le, axis=1, keepdims=True)
    e_exp = jnp.where(in_grp, jnp.exp(le - e_max), 0.0)
    ep = e_exp / jnp.sum(e_exp, axis=1, keepdims=True)
    ep = jnp.where(in_grp, ep, -1.0)
    p1 = jnp.max(ep, axis=1, keepdims=True)
    i1 = _first_lane_of(ep == p1, lane)
    ep2 = jnp.where(lane == i1, -1.0, ep)
    p2 = jnp.max(ep2, axis=1, keepdims=True)
    i2 = _first_lane_of(ep2 == p2, lane)
    den = p1 + p2
    return i1, i2, gp_top * p1 / den, gp_top * p2 / den


def _outproj_kernel(o_ref, yz_ref, x_ref, g1_ref, sc_ref, sh_ref, ng_ref, sng_ref, w_ref, wr_ref,
                    br_ref, tri_ref, cnt0_ref,
                    x1_ref, hn_ref, ids_ref, wts_ref, rank_ref, cnt_ref,
                    lhs_ref, x1s_ref, cnts_ref, *, n_col, gw):
    i = pl.program_id(0)
    j = pl.program_id(1)
    tm = lhs_ref.shape[0]
    tn = w_ref.shape[1]
    mix_a = o_ref.shape[1]

    @pl.when((i == 0) & (j == 0))
    def _():
        cnts_ref[...] = cnt0_ref[...]

    @pl.when(j == 0)
    def _():
        lhs_ref[:, 0:mix_a] = o_ref[...].astype(BF16)
        for g in range(yz_ref.shape[1] // gw):
            seg = yz_ref[:, g * gw:(g + 1) * gw]
            ms = jnp.mean(seg * seg, axis=-1, keepdims=True)
            seg = seg * lax.rsqrt(ms + EPS) * sng_ref[:, g * gw:(g + 1) * gw]
            lhs_ref[:, mix_a + g * gw:mix_a + (g + 1) * gw] = seg.astype(BF16)

    acc = _dot(lhs_ref[...], w_ref[...].astype(BF16))
    x1 = x_ref[...] + g1_ref[...] * acc.reshape(x_ref.shape)
    x1_ref[...] = x1
    x1s_ref[j] = x1.reshape(tm, tn)

    @pl.when(j == n_col - 1)
    def _():
        ssq = jnp.zeros((tm, 1), F32)
        for c in range(n_col):
            xc = x1s_ref[c]
            ssq = ssq + jnp.sum(xc * xc, axis=-1, keepdims=True)
        rs = lax.rsqrt(ssq / (n_col * tn) + EPS)
        gshape = x_ref.shape
        logits = jnp.zeros((tm, LANES), F32) + br_ref[...]
        for c in range(n_col):
            cols = slice(c * tn, (c + 1) * tn)
            hn = (x1s_ref[c] * rs * ng_ref[:, cols]).reshape(gshape)
            hn = (hn * (1.0 + sc_ref[:, :, cols]) + sh_ref[:, :, cols]).reshape(tm, tn)
            hn_ref[:, cols] = hn
            logits = logits + _dot_hp(hn, wr_ref[cols, :])
        i1, i2, w1, w2 = _route(logits)
        lane = lax.broadcasted_iota(I32, (tm, LANES), 1)
        hit1 = lane.astype(F32) == i1
        hit2 = lane.astype(F32) == i2
        onehot = jnp.where(hit1 | hit2, 1.0, 0.0).astype(BF16)
        before = _dot(tri_ref[...], onehot) + cnts_ref[0:1, :]
        r1 = jnp.sum(jnp.where(hit1, before, 0.0), axis=1, keepdims=True)
        r2 = jnp.sum(jnp.where(hit2, before, 0.0), axis=1, keepdims=True)
        ids_ref[...] = jnp.where(lane == 0, i1, jnp.where(lane == 1, i2, 0.0)).astype(I32)
        wts_ref[...] = jnp.where(lane == 0, w1, jnp.where(lane == 1, w2, 0.0))
        rank_ref[...] = jnp.where(lane == 0, r1, jnp.where(lane == 1, r2, 0.0)).astype(I32)
        total = cnts_ref[0:1, :] + jnp.sum(onehot.astype(F32), axis=0, keepdims=True)
        cnts_ref[...] = jnp.broadcast_to(total, cnts_ref.shape)
        cnt_ref[...] = jnp.broadcast_to(total, cnt_ref.shape)


def _strict_lower(n):
    import numpy as np
    t = np.arange(n)[:, None]
    s = np.arange(n)[None, :]
    return jnp.asarray((s < t).astype(np.float32), dtype=BF16)


def _out_projection(o3, yz3, x3, mod3, norm_g, ssm_norm_g, w_out, w_router, b_router, cnt0,
                    group_rows):
    bsz, t_len, d = x3.shape
    g_blk, r_blk = group_rows
    tm = g_blk * r_blk
    tn = PROJ_TN
    n_col = d // tn
    mix_a = o3.shape[-1]
    mix_b = yz3.shape[-1]
    n_tok = bsz * t_len
    tiles_per_seq = max(t_len // r_blk, 1)
    sec = d // tn

    def xmap(i, j):
        return (i // tiles_per_seq, i % tiles_per_seq, j)

    def modmap(s):
        return lambda i, j: (i // tiles_per_seq, 0, s)

    row = lambda i, j: (i, 0)
    const2 = lambda i, j: (0, 0)
    kern = functools.partial(_outproj_kernel, n_col=n_col, gw=mix_b // SSD_GROUPS)
    return pl.pallas_call(
        kern,
        grid=(n_tok // tm, n_col),
        in_specs=[
            pl.BlockSpec((tm, mix_a), row),
            pl.BlockSpec((tm, mix_b), row),
            pl.BlockSpec((g_blk, r_blk, tn), xmap),
            pl.BlockSpec((g_blk, 1, tn), lambda i, j: (i // tiles_per_seq, 0, 2 * sec + j)),
            pl.BlockSpec((g_blk, 1, d), modmap(4)),
            pl.BlockSpec((g_blk, 1, d), modmap(3)),
            pl.BlockSpec((1, d), const2),
            pl.BlockSpec((1, mix_b), const2),
            pl.BlockSpec((d, tn), lambda i, j: (0, j)),
            pl.BlockSpec((d, LANES), const2),
            pl.BlockSpec((1, LANES), const2),
            pl.BlockSpec((tm, tm), const2),
            pl.BlockSpec((SUBLANES, LANES), const2),
        ],
        out_specs=[
            pl.BlockSpec((g_blk, r_blk, tn), xmap),
            pl.BlockSpec((tm, d), row),
            pl.BlockSpec((tm, LANES), row),
            pl.BlockSpec((tm, LANES), row),
            pl.BlockSpec((tm, LANES), row),
            pl.BlockSpec((SUBLANES, LANES), const2),
        ],
        out_shape=[
            jax.ShapeDtypeStruct((bsz, t_len, d), F32),
            jax.ShapeDtypeStruct((n_tok, d), F32),
            jax.ShapeDtypeStruct((n_tok, LANES), I32),
            jax.ShapeDtypeStruct((n_tok, LANES), F32),
            jax.ShapeDtypeStruct((n_tok, LANES), I32),
            jax.ShapeDtypeStruct((SUBLANES, LANES), F32),
        ],
        scratch_shapes=[pltpu.VMEM((tm, d), BF16), pltpu.VMEM((n_col, tm, tn), F32),
                        pltpu.VMEM((SUBLANES, LANES), F32)],
        compiler_params=_cparams(("arbitrary", "arbitrary")),
        name="out_proj_router",
    )(o3.reshape(n_tok, mix_a), yz3.reshape(n_tok, mix_b), x3, mod3, mod3, mod3,
      norm_g.reshape(1, d), ssm_norm_g.reshape(1, mix_b), w_out, w_router, b_router,
      _strict_lower(tm), cnt0)


def _positions_kernel(ids_ref, rank_ref, cnt_ref, pos_ref, te_ref, nt_ref, *, block):
    lane8 = lax.broadcasted_iota(I32, (SUBLANES, LANES), 1)
    cnt = cnt_ref[...]
    tiles = jnp.floor((cnt + (MOE_TM - 1)) * (1.0 / MOE_TM))
    tiles = jnp.where(lane8 < N_EXPERTS, tiles, 0.0)
    incl = tiles
    shift = 1
    while shift < N_EXPERTS:
        incl = incl + jnp.where(lane8 >= shift, pltpu.roll(incl, shift, axis=1), 0.0)
        shift *= 2
    offs = ((incl - tiles) * MOE_TM)[0:1, :]
    nt_ref[...] = jnp.sum(tiles, axis=1, keepdims=True).astype(I32) + jnp.zeros(nt_ref.shape, I32)

    incl_col = jnp.broadcast_to(incl[0:1, :], (LANES, LANES)).T
    e_row = lax.broadcasted_iota(I32, (LANES, LANES), 0)
    i_lane = lax.broadcasted_iota(I32, (LANES, LANES), 1).astype(F32)
    done = jnp.where((incl_col <= i_lane) & (e_row < N_EXPERTS), 1.0, 0.0)
    te = jnp.minimum(jnp.sum(done, axis=0, keepdims=True), N_EXPERTS - 1.0)
    te_ref[...] = te.astype(I32) + jnp.zeros(te_ref.shape, I32)

    lane = lax.broadcasted_iota(I32, (block, LANES), 1)

    def body(bi, c):
        r0 = pl.multiple_of(bi * block, block)
        ids = ids_ref[pl.ds(r0, block), :]
        rank = rank_ref[pl.ds(r0, block), :]
        out = jnp.zeros((block, LANES), I32)
        for k in range(TOP_K):
            off = jnp.sum(jnp.where(lane == ids[:, k:k + 1], offs, 0.0), axis=1, keepdims=True)
            out = jnp.where(lane == k, off.astype(I32) + rank[:, k:k + 1], out)
        pos_ref[pl.ds(r0, block), :] = out
        return c

    lax.fori_loop(0, ids_ref.shape[0] // block, body, 0)


def _positions(ids, rank, cnt):
    n_tok = ids.shape[0]
    block = 512
    kern = functools.partial(_positions_kernel, block=block)
    full = lambda s: pl.BlockSpec(s, lambda: tuple(0 for _ in s))
    return pl.pallas_call(
        kern,
        in_specs=[full((n_tok, LANES)), full((n_tok, LANES)), full((SUBLANES, LANES))],
        out_specs=[full((n_tok, LANES)), full((SUBLANES, LANES)), full((SUBLANES, LANES))],
        out_shape=[jax.ShapeDtypeStruct((n_tok, LANES), I32),
                   jax.ShapeDtypeStruct((SUBLANES, LANES), I32),
                   jax.ShapeDtypeStruct((SUBLANES, LANES), I32)],
        compiler_params=pltpu.CompilerParams(vmem_limit_bytes=VMEM_LIMIT),
        name="route_positions",
    )(ids, rank, cnt)


def _row_copy(src_ref, src_row, dst_ref, dst_row, sem):
    return pltpu.make_async_copy(src_ref.at[pl.ds(src_row, 1), :], dst_ref.at[pl.ds(dst_row, 1), :],
                                 sem)


def _dispatch_kernel(pos_ref, hn_ref, xg_in_ref, xg_ref, sem):
    del xg_in_ref
    rows = hn_ref.shape[0]
    base = pl.program_id(0) * rows

    def issue(r, c):
        for k in range(TOP_K):
            _row_copy(hn_ref, r, xg_ref, pos_ref[(base + r) * TOP_K + k], sem).start()
        return c

    def drain(r, c):
        for k in range(TOP_K):
            _row_copy(hn_ref, r, xg_ref, 0, sem).wait()
        return c

    lax.fori_loop(0, rows, issue, 0)
    lax.fori_loop(0, rows, drain, 0)


def _dispatch(pos_flat, hn, xg):
    n_tok, d = hn.shape
    grid_spec = pltpu.PrefetchScalarGridSpec(
        num_scalar_prefetch=1,
        grid=(n_tok // ROW_DMA_TILE,),
        in_specs=[pl.BlockSpec((ROW_DMA_TILE, d), lambda i, pos: (i, 0)),
                  pl.BlockSpec(memory_space=pl.ANY)],
        out_specs=pl.BlockSpec(memory_space=pl.ANY),
        scratch_shapes=[pltpu.SemaphoreType.DMA(())],
    )
    return pl.pallas_call(
        _dispatch_kernel,
        grid_spec=grid_spec,
        out_shape=jax.ShapeDtypeStruct(xg.shape, xg.dtype),
        input_output_aliases={2: 0},
        compiler_params=_cparams(("arbitrary",)),
        name="moe_dispatch",
    )(pos_flat, hn, xg)


def _expert_changed(te_ref, t):
    return (t == 0) | (te_ref[t] != te_ref[jnp.maximum(t - 1, 0)])


def _moe_up_kernel(te_ref, nt_ref, x_ref, w1_ref, w3_ref, act_ref, w1b_ref, w3b_ref):
    t = pl.program_id(1)

    @pl.when(t < nt_ref[0])
    def _():
        @pl.when(_expert_changed(te_ref, t))
        def _():
            w1b_ref[...] = w1_ref[0].astype(BF16)
            w3b_ref[...] = w3_ref[0].astype(BF16)

        xb = x_ref[...].astype(BF16)
        a = _dot(xb, w1b_ref[...])
        b = _dot(xb, w3b_ref[...])
        act_ref[...] = (_silu(a) * b).astype(BF16)

    @pl.when(t >= nt_ref[0])
    def _():
        act_ref[...] = jnp.zeros(act_ref.shape, act_ref.dtype)


def _moe_down_kernel(te_ref, nt_ref, act_ref, w2_ref, out_ref, w2b_ref):
    t = pl.program_id(1)

    @pl.when(t < nt_ref[0])
    def _():
        @pl.when(_expert_changed(te_ref, t))
        def _():
            w2b_ref[...] = w2_ref[0].astype(BF16)

        out_ref[...] = _dot(act_ref[...], w2b_ref[...])

    @pl.when(t >= nt_ref[0])
    def _():
        out_ref[...] = jnp.zeros(out_ref.shape, out_ref.dtype)


def _moe_experts(te, nt, xg, w1, w3, w2):
    n_rows, d = xg.shape
    d_exp = w1.shape[-1]
    n_tiles = n_rows // MOE_TM
    halves = 2
    hn = d_exp // halves
    hd = d // halves

    def tile(h, t, te_ref, nt_ref):
        return jnp.minimum(t, nt_ref[0] - 1)

    act = pl.pallas_call(
        _moe_up_kernel,
        grid_spec=pltpu.PrefetchScalarGridSpec(
            num_scalar_prefetch=2,
            grid=(halves, n_tiles),
            in_specs=[
                pl.BlockSpec((MOE_TM, d), lambda h, t, te_r, nt_r: (tile(h, t, te_r, nt_r), 0)),
                pl.BlockSpec((1, d, hn), lambda h, t, te_r, nt_r: (te_r[tile(h, t, te_r, nt_r)], 0, h)),
                pl.BlockSpec((1, d, hn), lambda h, t, te_r, nt_r: (te_r[tile(h, t, te_r, nt_r)], 0, h)),
            ],
            out_specs=pl.BlockSpec((MOE_TM, hn), lambda h, t, te_r, nt_r: (t, h)),
            scratch_shapes=[pltpu.VMEM((d, hn), BF16), pltpu.VMEM((d, hn), BF16)],
        ),
        out_shape=jax.ShapeDtypeStruct((n_rows, d_exp), BF16),
        compiler_params=_cparams(("arbitrary", "arbitrary")),
        name="moe_up",
    )(te, nt, xg, w1, w3)

    return pl.pallas_call(
        _moe_down_kernel,
        grid_spec=pltpu.PrefetchScalarGridSpec(
            num_scalar_prefetch=2,
            grid=(halves, n_tiles),
            in_specs=[
                pl.BlockSpec((MOE_TM, d_exp), lambda h, t, te_r, nt_r: (tile(h, t, te_r, nt_r), 0)),
                pl.BlockSpec((1, d_exp, hd), lambda h, t, te_r, nt_r: (te_r[tile(h, t, te_r, nt_r)], 0, h)),
            ],
            out_specs=pl.BlockSpec((MOE_TM, hd), lambda h, t, te_r, nt_r: (t, h)),
            scratch_shapes=[pltpu.VMEM((d_exp, hd), BF16)],
        ),
        out_shape=jax.ShapeDtypeStruct((n_rows, d), F32),
        compiler_params=_cparams(("arbitrary", "arbitrary")),
        name="moe_down",
    )(te, nt, act, w2)


def _combine_kernel(pos_ref, x1_ref, g2_ref, wts_ref, fg_ref, eo_ref, y_ref, buf_ref, sem):
    rows = wts_ref.shape[0]
    base = pl.program_id(0) * rows

    def issue(r, c):
        for k in range(TOP_K):
            _row_copy(eo_ref, pos_ref[(base + r) * TOP_K + k], buf_ref.at[k], r, sem).start()
        return c

    def drain(r, c):
        for k in range(TOP_K):
            _row_copy(eo_ref, 0, buf_ref.at[k], r, sem).wait()
        return c

    lax.fori_loop(0, rows, issue, 0)
    lax.fori_loop(0, rows, drain, 0)
    w = wts_ref[...]
    moe = w[:, 0:1] * buf_ref[0] + w[:, 1:2] * buf_ref[1]
    x2 = x1_ref[...] + g2_ref[...] * moe.reshape(x1_ref.shape)
    ms = jnp.mean(x2 * x2, axis=-1, keepdims=True)
    y_ref[...] = x2 * lax.rsqrt(ms + EPS) * fg_ref[...]


def _combine(pos_flat, x1, mod3, wts, final_g, expert_out, group_rows):
    bsz, t_len, d = x1.shape
    g_blk, r_blk = group_rows
    tm = g_blk * r_blk
    tiles_per_seq = max(t_len // r_blk, 1)

    def xmap(i, pos):
        return (i // tiles_per_seq, i % tiles_per_seq, 0)

    return pl.pallas_call(
        _combine_kernel,
        grid_spec=pltpu.PrefetchScalarGridSpec(
            num_scalar_prefetch=1,
            grid=((bsz * t_len) // tm,),
            in_specs=[
                pl.BlockSpec((g_blk, r_blk, d), xmap),
                pl.BlockSpec((g_blk, 1, d), lambda i, pos: (i // tiles_per_seq, 0, 5)),
                pl.BlockSpec((tm, LANES), lambda i, pos: (i, 0)),
                pl.BlockSpec((1, d), lambda i, pos: (0, 0)),
                pl.BlockSpec(memory_space=pl.ANY),
            ],
            out_specs=pl.BlockSpec((g_blk, r_blk, d), xmap),
            scratch_shapes=[pltpu.VMEM((TOP_K, tm, d), F32), pltpu.SemaphoreType.DMA(())],
        ),
        out_shape=jax.ShapeDtypeStruct((bsz, t_len, d), F32),
        compiler_params=_cparams(("arbitrary",)),
        name="moe_combine",
    )(pos_flat, x1, mod3, wts, final_g.reshape(1, d), expert_out)


def kernel(x_prompt, x_sample, state_hgrn, state_ssm, state_conv, c_prompt, c_sample, ada_w, ada_b,
           norm1_g, norm2_g, w_in, hgrn_lb, hgrn_onorm_g, conv_w, conv_b, dt_bias, a_log, d_skip,
           ssm_norm_g, w_out, w_grp, b_grp, w_rt, b_rt, w1, w3, w2, final_g):
    depth = w_in.shape[0]
    assert depth == 1, "single-layer trunk"
    layer = 0
    bp, t_p, d = x_prompt.shape
    bs, t_s, _ = x_sample.shape
    mix_a = hgrn_onorm_g.shape[1]
    mix_b = ssm_norm_g.shape[1]
    conv_dim = conv_w.shape[2]
    n_main = 4 * mix_a + mix_b + conv_dim
    n_ssd_heads = dt_bias.shape[1]
    n_pairs = mix_b // LANES
    xbc0 = 4 * mix_a + mix_b
    tail = CONV_WIDTH - 1
    assert t_s <= SUBLANES and t_s >= tail and t_p % PROMPT_CHUNK == 0

    xs_pad = jnp.pad(x_sample, ((0, 0), (0, SUBLANES - t_s), (0, 0)))
    n_c = bp + bs
    c_rows = -(-n_c // SUBLANES) * SUBLANES
    c_all = jnp.pad(jnp.concatenate([c_prompt, c_sample], axis=0), ((0, c_rows - n_c), (0, 0)))

    mod = _modulation(c_all, ada_w[layer], ada_b[layer])
    mod_p = mod[:bp].reshape(bp, 1, -1)
    mod_s = mod[bp:n_c].reshape(bs, 1, -1)
    w_dt = jnp.pad(w_in[layer][:, n_main:], ((0, 0), (0, LANES - n_ssd_heads)))
    w_router = jnp.pad(jnp.concatenate([w_rt[layer], w_grp[layer]], axis=1),
                       ((0, 0), (0, LANES - N_EXPERTS - N_EXPERT_GROUPS)))
    b_router = jnp.pad(jnp.concatenate([b_rt[layer], b_grp[layer]]),
                       (0, LANES - N_EXPERTS - N_EXPERT_GROUPS)).reshape(1, LANES)

    groups = (
        (x_prompt, mod_p, None, None, None, t_p, PROMPT_CHUNK, PROMPT_CHUNK, MIXER_ROWS,
         (1, 8), 2, (1, PROJ_TM), (1, OUT_TM)),
        (xs_pad, mod_s, state_hgrn[layer], state_ssm[layer].reshape(bs, n_pairs, LANES, LANES),
         state_conv[layer], t_s, SUBLANES, t_s, SUBLANES,
         (16, 1), 4, (PROJ_TM // SUBLANES, SUBLANES), (OUT_TM // SUBLANES, SUBLANES)),
    )

    cnt = jnp.zeros((SUBLANES, LANES), F32)
    per_group = []
    for x3, mod3, s0, h0, c0, t_real, ch, valid, tt, (hg_gb, hg_hb), ssd_gb, proj_gr, out_gr in groups:
        bsz, t_len, _ = x3.shape
        proj, dt = _in_projection(x3, mod3, norm1_g[layer], w_in[layer], w_dt, n_main, proj_gr)
        proj3 = proj.reshape(bsz, t_len, n_main)
        dt3 = dt.reshape(bsz, t_len, LANES)
        o3, s_new = _hgrn_heads(proj3, hgrn_lb, hgrn_onorm_g[layer], s0, ch=ch, valid=valid,
                                gb=hg_gb, hb=hg_hb, tt=tt, layer=layer)
        yz3, h_new = _ssd_pairs(proj3, dt3, conv_w[layer], conv_b[layer], dt_bias[layer],
                                a_log[layer], d_skip[layer], h0, c0, mix_a=mix_a, mix_b=mix_b,
                                ch=ch, valid=valid, gb=ssd_gb, tt=tt)
        conv_new = proj3[:, t_real - tail:t_real, xbc0:xbc0 + conv_dim]
        x1, hn, ids, wts, rank, cnt = _out_projection(
            o3, yz3, x3, mod3, norm2_g[layer], ssm_norm_g[layer], w_out[layer], w_router, b_router,
            cnt, out_gr)
        per_group.append(dict(x1=x1, hn=hn, ids=ids, wts=wts, rank=rank, mod3=mod3, out_gr=out_gr,
                              s_new=s_new, h_new=h_new, conv_new=conv_new))

    ids_all = jnp.concatenate([g["ids"] for g in per_group], axis=0)
    rank_all = jnp.concatenate([g["rank"] for g in per_group], axis=0)
    n_tok = ids_all.shape[0]
    pos, te, nt = _positions(ids_all, rank_all, cnt)
    pos_flat = pos[:, :TOP_K].reshape(-1)
    n_tiles = -(-(n_tok * TOP_K + N_EXPERTS * (MOE_TM - 1)) // MOE_TM)
    assert n_tiles <= LANES
    te_vec = te[0, :n_tiles]
    nt_vec = nt[0, :1]

    xg = jnp.zeros((n_tiles * MOE_TM, d), F32)
    row0 = 0
    for g in per_group:
        rows = g["hn"].shape[0]
        g["pos"] = pos_flat[row0 * TOP_K:(row0 + rows) * TOP_K]
        xg = _dispatch(g["pos"], g["hn"], xg)
        row0 += rows
    expert_out = _moe_experts(te_vec, nt_vec, xg, w1[layer], w3[layer], w2[layer])
    ys = [_combine(g["pos"], g["x1"], g["mod3"], g["wts"], final_g, expert_out, g["out_gr"])
          for g in per_group]

    gp, gs = per_group
    return (
        ys[0],
        ys[1][:, :t_s],
        gp["s_new"][None],
        gp["h_new"].reshape(1, bp, n_ssd_heads, SSD_HEAD_DIM, SSD_STATE),
        gp["conv_new"][None],
        gs["s_new"][None],
        gs["h_new"].reshape(1, bs, n_ssd_heads, SSD_HEAD_DIM, SSD_STATE),
        gs["conv_new"][None],
    )
```

```python
import functools

import jax
import jax.numpy as jnp
from jax import lax
from jax.experimental import pallas as pl
from jax.experimental.pallas import tpu as pltpu

F32 = jnp.float32
BF16 = jnp.bfloat16
I32 = jnp.int32
EPS = 1e-6
LOG2E = 1.4426950408889634

LANES = 128
SUBLANES = 8
VMEM_LIMIT = 56 * 1024 * 1024

HGRN_HEAD_DIM = 128
SSD_HEAD_DIM = 64
SSD_GROUPS = 2
SSD_STATE = 128
CONV_WIDTH = 4
N_EXPERT_GROUPS = 4
EXPERTS_PER_GROUP = 8
N_EXPERTS = N_EXPERT_GROUPS * EXPERTS_PER_GROUP
TOP_K = 2

PROJ_TM = 1024
PROJ_TN = 512
MOE_TM = 256
PROMPT_CHUNK = 64
MIXER_ROWS = 512
ROW_DMA_TILE = 256


def _cparams(sem):
    return pltpu.CompilerParams(dimension_semantics=sem, vmem_limit_bytes=VMEM_LIMIT)


def _dot(a, b):
    return jnp.dot(a, b, preferred_element_type=F32)


def _dot_nt(a, b):
    return lax.dot_general(a, b, (((1,), (1,)), ((), ())), preferred_element_type=F32)


def _split2(x):
    hi = x.astype(BF16)
    lo = (x - hi.astype(F32)).astype(BF16)
    return hi, lo


def _split3(x):
    hi = x.astype(BF16)
    r = x - hi.astype(F32)
    mid = r.astype(BF16)
    lo = (r - mid.astype(F32)).astype(BF16)
    return hi, mid, lo


def _dot_exact_lhs(m_bf16, x):
    hi, mid, lo = _split3(x)
    return _dot(m_bf16, hi) + _dot(m_bf16, mid) + _dot(m_bf16, lo)


def _dot_exact_rhs(x, m_bf16):
    hi, mid, lo = _split3(x)
    return _dot(hi, m_bf16) + _dot(mid, m_bf16) + _dot(lo, m_bf16)


def _dot_hp(a, w):
    ah, al = _split2(a)
    wh, wl = _split2(w)
    return _dot(ah, wh) + _dot(al, wh) + _dot(ah, wl)


def _silu(x):
    return x * jax.nn.sigmoid(x)


def _pad_rows(x, rows):
    if x.shape[0] == rows:
        return x
    return jnp.concatenate([x, jnp.zeros((rows - x.shape[0], x.shape[1]), x.dtype)], axis=0)


def _mod_kernel(c_ref, w_ref, b_ref, o_ref):
    a = _silu(c_ref[...])
    o_ref[...] = _dot_hp(a, w_ref[...]) + b_ref[...]


def _modulation(c_all, ada_w, ada_b):
    rows, d = c_all.shape
    n_out = ada_w.shape[1]
    tn = 1024
    return pl.pallas_call(
        _mod_kernel,
        grid=(n_out // tn,),
        in_specs=[
            pl.BlockSpec((rows, d), lambda j: (0, 0)),
            pl.BlockSpec((d, tn), lambda j: (0, j)),
            pl.BlockSpec((1, tn), lambda j: (0, j)),
        ],
        out_specs=pl.BlockSpec((rows, tn), lambda j: (0, j)),
        out_shape=jax.ShapeDtypeStruct((rows, n_out), F32),
        compiler_params=_cparams(("arbitrary",)),
        name="adaln_mod",
    )(c_all, ada_w, ada_b.reshape(1, n_out))


def _proj_kernel(x_ref, sc_ref, sh_ref, g_ref, w_ref, wdt_ref, o_ref, dt_ref, hn_ref):
    j = pl.program_id(1)

    @pl.when(j == 0)
    def _():
        x = x_ref[...]
        ms = jnp.mean(x * x, axis=-1, keepdims=True)
        hn = (x * lax.rsqrt(ms + EPS)) * g_ref[...] * (1.0 + sc_ref[...]) + sh_ref[...]
        hn = hn.reshape(hn_ref.shape)
        hn_ref[...] = hn.astype(BF16)
        dt_ref[...] = _dot_hp(hn, wdt_ref[...])

    o_ref[...] = _dot(hn_ref[...], w_ref[...].astype(BF16))


def _in_projection(x3, mod3, norm_g, w_main, w_dt, n_main, group_rows):
    bsz, t_len, d = x3.shape
    g_blk, r_blk = group_rows
    tm = g_blk * r_blk
    tiles_per_seq = max(t_len // r_blk, 1)
    n_row_tiles = (bsz * t_len) // tm

    def x_map(i, j):
        return (i // tiles_per_seq, i % tiles_per_seq, 0)

    def mod_map(sec):
        return lambda i, j: (i // tiles_per_seq, 0, sec)

    return pl.pallas_call(
        _proj_kernel,
        grid=(n_row_tiles, n_main // PROJ_TN),
        in_specs=[
            pl.BlockSpec((g_blk, r_blk, d), x_map),
            pl.BlockSpec((g_blk, 1, d), mod_map(1)),
            pl.BlockSpec((g_blk, 1, d), mod_map(0)),
            pl.BlockSpec((1, d), lambda i, j: (0, 0)),
            pl.BlockSpec((d, PROJ_TN), lambda i, j: (0, j)),
            pl.BlockSpec((d, LANES), lambda i, j: (0, 0)),
        ],
        out_specs=[
            pl.BlockSpec((tm, PROJ_TN), lambda i, j: (i, j)),
            pl.BlockSpec((tm, LANES), lambda i, j: (i, 0)),
        ],
        out_shape=[
            jax.ShapeDtypeStruct((bsz * t_len, n_main), F32),
            jax.ShapeDtypeStruct((bsz * t_len, LANES), F32),
        ],
        scratch_shapes=[pltpu.VMEM((tm, d), BF16)],
        compiler_params=_cparams(("arbitrary", "arbitrary")),
        name="in_proj",
    )(x3, mod3, mod3, norm_g.reshape(1, d), w_main, w_dt)


def _tri_inclusive(ch):
    import numpy as np
    t = np.arange(ch)[:, None]
    s = np.arange(LANES)[None, :]
    return jnp.asarray((s <= t).astype(np.float32), dtype=BF16)


def _tril_mask(ch):
    import numpy as np
    t = np.arange(ch)[:, None]
    s = np.arange(LANES)[None, :]
    return jnp.asarray((s <= t).astype(np.float32))


def _level_sizes(ch):
    sizes = []
    sz = SUBLANES
    while 2 * sz <= ch:
        sizes.append(sz)
        sz *= 2
    return sizes


def _level_masks(ch):
    import numpy as np
    t = np.arange(ch)[:, None]
    s = np.arange(LANES)[None, :]
    out = []
    for sz in _level_sizes(ch):
        m = (t // (2 * sz) == s // (2 * sz)) & (t % (2 * sz) >= sz) & (s % (2 * sz) < sz) & (s < ch)
        out.append(m.astype(np.float32))
    if not out:
        out.append(np.zeros((ch, LANES), np.float32))
    return jnp.asarray(np.stack(out))


def _cumsum_rows(tri, x):
    hi, mid, lo = _split3(x)
    return (_dot(tri, _pad_rows(hi, LANES)) + _dot(tri, _pad_rows(mid, LANES))
            + _dot(tri, _pad_rows(lo, LANES)))


def _hgrn_kernel(*refs, ch, valid, nc, gb, hb, has_init, layer):
    if has_init:
        (q_ref, f_ref, i_ref, g_ref, lb_ref, on_ref, tri_ref, masks_ref, s0_ref,
         o_ref, sout_ref, st_ref, b_scr, k_scr) = refs
    else:
        (q_ref, f_ref, i_ref, g_ref, lb_ref, on_ref, tri_ref, masks_ref,
         o_ref, sout_ref, st_ref, b_scr, k_scr) = refs
        s0_ref = None

    n_lb = lb_ref.shape[0]
    lb_rows = [lb_ref[i:i + 1, :] for i in range(n_lb)]
    lb_max = functools.reduce(jnp.maximum, lb_rows)
    lb_exp = [jnp.exp(r - lb_max) for r in lb_rows]
    lb = sum(lb_exp[:layer + 1]) / sum(lb_exp)
    onorm = on_ref[...]
    tri = tri_ref[...]
    sizes = _level_sizes(ch)
    lane8 = lax.broadcasted_iota(I32, (SUBLANES, LANES), 1)
    row8 = lax.broadcasted_iota(I32, (SUBLANES, LANES), 0)
    tb = pl.program_id(2)

    def chunk(c, g, hh, ci):
        cols = slice(hh * LANES, (hh + 1) * LANES)
        t0 = pl.multiple_of(ci * ch, ch)
        rows = pl.ds(t0, ch)
        lb_h = lb[:, cols]
        q = _silu(q_ref[g, rows, cols])
        fg = lb_h + (1.0 - lb_h) * jax.nn.sigmoid(f_ref[g, rows, cols])
        k = 1.0 - fg
        lf = jnp.log(fg)
        v = i_ref[g, rows, cols]
        if valid < ch:
            live = lax.broadcasted_iota(I32, (ch, LANES), 0) < valid
            lf = jnp.where(live, lf, 0.0)
            k = jnp.where(live, k, 0.0)
        b = _cumsum_rows(tri, lf)
        b2 = b * LOG2E
        b_scr[c] = b2
        k_scr[c] = k
        b_end = b_scr[c, pl.ds(ch - 1, 1), :]
        st = st_ref[c]

        o = _dot_nt((q * jnp.exp2(b2)).astype(BF16), st.astype(BF16))

        scores = jnp.zeros((ch, LANES), F32)
        for lvl, sz in enumerate(sizes):
            pieces = []
            for m in range(ch // (2 * sz)):
                r = b_scr[c, pl.ds(2 * sz * m + sz - 1, 1), :]
                pieces.append(jnp.broadcast_to(r, (2 * sz, LANES)))
            r_all = pieces[0] if len(pieces) == 1 else jnp.concatenate(pieces, axis=0)
            e = jnp.exp2(-jnp.abs(b2 - r_all))
            s_l = _dot_nt((q * e).astype(BF16), _pad_rows((k * e).astype(BF16), LANES))
            scores = scores + masks_ref[lvl] * s_l
        blocks = []
        for jb in range(ch // SUBLANES):
            bb = b2[SUBLANES * jb:SUBLANES * (jb + 1)]
            qb = q[SUBLANES * jb:SUBLANES * (jb + 1)]
            acc = jnp.zeros((SUBLANES, LANES), F32)
            for sl in range(SUBLANES):
                s = SUBLANES * jb + sl
                bs = b_scr[c, pl.ds(s, 1), :]
                ks = k_scr[c, pl.ds(s, 1), :]
                val = (qb * ks) * jnp.exp2(bb - bs)
                red = jnp.sum(val, axis=1, keepdims=True)
                acc = jnp.where(lane8 == s, red, acc)
            blocks.append(jnp.where(lane8 - SUBLANES * jb <= row8, acc, 0.0))
        scores = scores + (blocks[0] if len(blocks) == 1 else jnp.concatenate(blocks, axis=0))
        o = o + _dot(scores.astype(BF16), _pad_rows(v.astype(BF16), LANES))

        dk = k * jnp.exp2(b_end - b2)
        v_t = _pad_rows(v, LANES).T
        st_ref[c] = st * jnp.exp2(b_end) + _dot(v_t.astype(BF16),
                                                _pad_rows(dk.astype(BF16), LANES))

        on = o * lax.rsqrt(jnp.mean(o * o, axis=-1, keepdims=True) + EPS) * onorm[:, cols]
        o_ref[g, rows, cols] = on * _silu(g_ref[g, rows, cols])

    chains = [(g * hb + hh, g, hh) for g in range(gb) for hh in range(hb)]

    @pl.when(tb == 0)
    def _():
        for c, g, hh in chains:
            if has_init:
                st_ref[c] = s0_ref[g, hh].T
            else:
                st_ref[c] = jnp.zeros((LANES, LANES), F32)

    def step(ci, carry):
        for c, g, hh in chains:
            chunk(c, g, hh, ci)
        return carry

    if nc == 1:
        step(0, 0)
    else:
        lax.fori_loop(0, nc, step, 0)

    @pl.when(tb == pl.num_programs(2) - 1)
    def _():
        for c, g, hh in chains:
            sout_ref[g, hh] = st_ref[c].T


def _hgrn_heads(proj3, hgrn_lb, onorm_g, s0, *, ch, valid, gb, hb, tt, layer):
    bsz, t_len, _ = proj3.shape
    n_heads = onorm_g.shape[0] // HGRN_HEAD_DIM
    n_hg = n_heads // hb
    nc = tt // ch
    has_init = s0 is not None
    w = hb * LANES

    def col(sec):
        return lambda b, h, t: (b, t, sec * n_hg + h)

    in_specs = [pl.BlockSpec((gb, tt, w), col(s)) for s in range(4)]
    in_specs += [
        pl.BlockSpec((hgrn_lb.shape[0], w), lambda b, h, t: (0, h)),
        pl.BlockSpec((1, w), lambda b, h, t: (0, h)),
        pl.BlockSpec((ch, LANES), lambda b, h, t: (0, 0)),
        pl.BlockSpec((max(len(_level_sizes(ch)), 1), ch, LANES), lambda b, h, t: (0, 0, 0)),
    ]
    args = [proj3, proj3, proj3, proj3, hgrn_lb, onorm_g.reshape(1, -1), _tri_inclusive(ch),
            _level_masks(ch)]
    if has_init:
        in_specs.append(pl.BlockSpec((gb, hb, LANES, LANES), lambda b, h, t: (b, h, 0, 0)))
        args.append(s0)
    kern = functools.partial(_hgrn_kernel, ch=ch, valid=valid, nc=nc, gb=gb, hb=hb,
                             has_init=has_init, layer=layer)
    n_chain = gb * hb
    return pl.pallas_call(
        kern,
        grid=(bsz // gb, n_hg, t_len // tt),
        in_specs=in_specs,
        out_specs=[
            pl.BlockSpec((gb, tt, w), lambda b, h, t: (b, t, h)),
            pl.BlockSpec((gb, hb, LANES, LANES), lambda b, h, t: (b, h, 0, 0)),
        ],
        out_shape=[
            jax.ShapeDtypeStruct((bsz, t_len, n_heads * HGRN_HEAD_DIM), F32),
            jax.ShapeDtypeStruct((bsz, n_heads, LANES, LANES), F32),
        ],
        scratch_shapes=[pltpu.VMEM((n_chain, LANES, LANES), F32),
                        pltpu.VMEM((n_chain, ch, LANES), F32),
                        pltpu.VMEM((n_chain, ch, LANES), F32)],
        compiler_params=_cparams(("arbitrary", "arbitrary", "arbitrary")),
        name="hgrn2_heads",
    )(*args)


def _softplus(x):
    return jnp.maximum(x, 0.0) + jnp.log1p(jnp.exp(-jnp.abs(x)))


def _ssd_kernel(*refs, ch, valid, nc, gb, pb, has_init):
    if has_init:
        (z_ref, x_ref, b_ref, c_ref, dt_ref, cwx_ref, cwb_ref, cwc_ref, cbx_ref, cbb_ref, cbc_ref,
         xj_ref, bias_ref, alog_ref, d_ref, tri_ref, tril_ref, h0_ref, cx0_ref, cb0_ref, cc0_ref,
         y_ref, hout_ref, ht_ref, wx_ref, wb_ref, wc_ref) = refs
    else:
        (z_ref, x_ref, b_ref, c_ref, dt_ref, cwx_ref, cwb_ref, cwc_ref, cbx_ref, cbb_ref, cbc_ref,
         xj_ref, bias_ref, alog_ref, d_ref, tri_ref, tril_ref,
         y_ref, hout_ref, ht_ref, wx_ref, wb_ref, wc_ref) = refs
        h0_ref = cx0_ref = cb0_ref = cc0_ref = None

    tri = tri_ref[...]
    tril = tril_ref[...]
    lane = lax.broadcasted_iota(I32, (LANES, LANES), 1)
    first_head = lane < SSD_HEAD_DIM
    tail = CONV_WIDTH - 1
    tb = pl.program_id(2)

    def conv(raw_ref, win_ref, cw_ref, cb_ref, g, t0):
        win_ref[g, pl.ds(SUBLANES, ch), :] = raw_ref[g, pl.ds(t0, ch), :]
        u = cb_ref[...]
        for j in range(CONV_WIDTH):
            u = u + win_ref[g, pl.ds(SUBLANES - tail + j, ch), :] * cw_ref[j:j + 1, :]
        win_ref[g, pl.ds(0, SUBLANES), :] = win_ref[g, pl.ds(ch, SUBLANES), :]
        return _silu(u)

    def chunk(g, ci):
        t0 = pl.multiple_of(ci * ch, ch)
        rows = pl.ds(t0, ch)
        xs_all = conv(x_ref, wx_ref, cwx_ref, cbx_ref, g, t0)
        bm = conv(b_ref, wb_ref, cwb_ref, cbb_ref, g, t0)
        cm = conv(c_ref, wc_ref, cwc_ref, cbc_ref, g, t0)
        cb = _dot_nt(cm.astype(BF16), _pad_rows(bm.astype(BF16), LANES))
        b_t = _pad_rows(bm, LANES).T
        dt_raw = dt_ref[g, rows, :]
        for p in range(pb):
            c = g * pb + p
            cols = slice(p * LANES, (p + 1) * LANES)
            xs = xs_all[:, cols]
            dt = _softplus(_dot_exact_rhs(dt_raw, xj_ref[p]) + bias_ref[p])
            if valid < ch:
                live = lax.broadcasted_iota(I32, (ch, LANES), 0) < valid
                dt = jnp.where(live, dt, 0.0)
            la_cs = _cumsum_rows(tri, dt * (-jnp.exp(alog_ref[p])))
            xdt = xs * dt
            a_end = la_cs[ch - 1:ch, :]

            acs_t = _pad_rows(la_cs, LANES).T
            xdt_pad = _pad_rows(xdt, LANES)
            y = jnp.zeros((ch, LANES), F32)
            for hl in range(2):
                col = la_cs[:, hl * SSD_HEAD_DIM:hl * SSD_HEAD_DIM + 1]
                row = acs_t[hl * SSD_HEAD_DIM:hl * SSD_HEAD_DIM + 1, :]
                lmat = jnp.exp(jnp.minimum(col - row, 0.0)) * tril
                keep = first_head if hl == 0 else jnp.logical_not(first_head)
                y = y + _dot((cb * lmat).astype(BF16), jnp.where(keep, xdt_pad, 0.0).astype(BF16))

            ht = ht_ref[c]
            y = y + _dot(cm.astype(BF16), ht.astype(BF16)) * jnp.exp(la_cs)
            upd = _pad_rows((xdt * jnp.exp(a_end - la_cs)).astype(BF16), LANES)
            ht_ref[c] = ht * jnp.exp(a_end) + _dot(b_t.astype(BF16), upd)

            y = (y + d_ref[p] * xs) * _silu(z_ref[g, rows, cols])
            y_ref[g, rows, cols] = y

    @pl.when(tb == 0)
    def _():
        for g in range(gb):
            for win_ref, c0_ref in ((wx_ref, cx0_ref), (wb_ref, cb0_ref), (wc_ref, cc0_ref)):
                win_ref[g, pl.ds(0, SUBLANES), :] = jnp.zeros((SUBLANES, win_ref.shape[2]), F32)
                if has_init:
                    win_ref[g, pl.ds(SUBLANES - tail, tail), :] = c0_ref[g]
            for p in range(pb):
                if has_init:
                    ht_ref[g * pb + p] = h0_ref[g, p].T
                else:
                    ht_ref[g * pb + p] = jnp.zeros((LANES, LANES), F32)

    def step(ci, carry):
        for g in range(gb):
            chunk(g, ci)
        return carry

    if nc == 1:
        step(0, 0)
    else:
        lax.fori_loop(0, nc, step, 0)

    @pl.when(tb == pl.num_programs(2) - 1)
    def _():
        for g in range(gb):
            for p in range(pb):
                hout_ref[g, p] = ht_ref[g * pb + p].T


def _pair_rows(v, n_pairs):
    return jnp.repeat(v.astype(F32), SSD_HEAD_DIM).reshape(n_pairs, 1, LANES)


def _pair_select(n_pairs):
    import numpy as np
    h = np.arange(LANES)[None, :, None]
    lane = np.arange(LANES)[None, None, :]
    j = np.arange(n_pairs)[:, None, None]
    return jnp.asarray((h == 2 * j + lane // SSD_HEAD_DIM).astype(np.float32), dtype=BF16)


def _ssd_pairs(proj3, dt3, conv_w, conv_b, dt_bias, a_log, d_skip, h0, conv0, *, mix_a, mix_b,
               ch, valid, gb, tt):
    bsz, t_len, _ = proj3.shape
    n_pairs = mix_b // LANES
    pb = n_pairs // SSD_GROUPS
    w = pb * LANES
    nc = tt // ch
    has_init = h0 is not None
    assert (4 * mix_a) % w == 0
    z0 = 4 * mix_a // w
    x0 = z0 + SSD_GROUPS
    b0 = (4 * mix_a + 2 * mix_b) // LANES
    c0 = b0 + SSD_GROUPS

    seq_w = lambda base: (lambda b, g, t: (b, t, base + g))
    par = lambda base: (lambda b, g, t: (0, base + g))
    const2 = lambda b, g, t: (0, 0)

    in_specs = [
        pl.BlockSpec((gb, tt, w), seq_w(z0)),
        pl.BlockSpec((gb, tt, w), seq_w(x0)),
        pl.BlockSpec((gb, tt, LANES), seq_w(b0)),
        pl.BlockSpec((gb, tt, LANES), seq_w(c0)),
        pl.BlockSpec((gb, tt, LANES), lambda b, g, t: (b, t, 0)),
        pl.BlockSpec((CONV_WIDTH, w), par(0)),
        pl.BlockSpec((CONV_WIDTH, LANES), par(n_pairs)),
        pl.BlockSpec((CONV_WIDTH, LANES), par(n_pairs + SSD_GROUPS)),
        pl.BlockSpec((1, w), par(0)),
        pl.BlockSpec((1, LANES), par(n_pairs)),
        pl.BlockSpec((1, LANES), par(n_pairs + SSD_GROUPS)),
        pl.BlockSpec((pb, LANES, LANES), lambda b, g, t: (g, 0, 0)),
        pl.BlockSpec((pb, 1, LANES), lambda b, g, t: (g, 0, 0)),
        pl.BlockSpec((pb, 1, LANES), lambda b, g, t: (g, 0, 0)),
        pl.BlockSpec((pb, 1, LANES), lambda b, g, t: (g, 0, 0)),
        pl.BlockSpec((ch, LANES), const2),
        pl.BlockSpec((ch, LANES), const2),
    ]
    conv_b2 = conv_b.reshape(1, -1)
    args = [proj3, proj3, proj3, proj3, dt3, conv_w, conv_w, conv_w, conv_b2, conv_b2, conv_b2,
            _pair_select(n_pairs), _pair_rows(dt_bias, n_pairs), _pair_rows(a_log, n_pairs),
            _pair_rows(d_skip, n_pairs), _tri_inclusive(ch), _tril_mask(ch)]
    if has_init:
        tail = CONV_WIDTH - 1
        in_specs += [
            pl.BlockSpec((gb, pb, LANES, LANES), lambda b, g, t: (b, g, 0, 0)),
            pl.BlockSpec((gb, tail, w), lambda b, g, t: (b, 0, g)),
            pl.BlockSpec((gb, tail, LANES), lambda b, g, t: (b, 0, n_pairs + g)),
            pl.BlockSpec((gb, tail, LANES), lambda b, g, t: (b, 0, n_pairs + SSD_GROUPS + g)),
        ]
        args += [h0, conv0, conv0, conv0]
    kern = functools.partial(_ssd_kernel, ch=ch, valid=valid, nc=nc, gb=gb, pb=pb,
                             has_init=has_init)
    return pl.pallas_call(
        kern,
        grid=(bsz // gb, SSD_GROUPS, t_len // tt),
        in_specs=in_specs,
        out_specs=[
            pl.BlockSpec((gb, tt, w), lambda b, g, t: (b, t, g)),
            pl.BlockSpec((gb, pb, LANES, LANES), lambda b, g, t: (b, g, 0, 0)),
        ],
        out_shape=[
            jax.ShapeDtypeStruct((bsz, t_len, mix_b), F32),
            jax.ShapeDtypeStruct((bsz, n_pairs, LANES, LANES), F32),
        ],
        scratch_shapes=[pltpu.VMEM((gb * pb, LANES, LANES), F32),
                        pltpu.VMEM((gb, ch + SUBLANES, w), F32),
                        pltpu.VMEM((gb, ch + SUBLANES, LANES), F32),
                        pltpu.VMEM((gb, ch + SUBLANES, LANES), F32)],
        compiler_params=_cparams(("arbitrary", "arbitrary", "arbitrary")),
        name="ssd_pairs",
    )(*args)


OUT_TM = 512
GROUP_LANE0 = N_EXPERTS
NEG_BIG = -1e30
NO_LANE = 4 * LANES


def _first_lane_of(mask, lane):
    return jnp.min(jnp.where(mask, lane, float(NO_LANE)), axis=1, keepdims=True)


def _route(logits):
    lane_i = lax.broadcasted_iota(I32, logits.shape, 1)
    lane = lane_i.astype(F32)
    is_grp = (lane_i >= GROUP_LANE0) & (lane_i < GROUP_LANE0 + N_EXPERT_GROUPS)
    lg = jnp.where(is_grp, logits, NEG_BIG)
    g_max = jnp.max(lg, axis=1, keepdims=True)
    g_sum = jnp.sum(jnp.where(is_grp, jnp.exp(lg - g_max), 0.0), axis=1, keepdims=True)
    g_idx = _first_lane_of(lg == g_max, lane) - GROUP_LANE0
    gp_top = 1.0 / g_sum
    lane_grp = jnp.right_shift(lane_i, 3).astype(F32)
    in_grp = (lane_i < N_EXPERTS) & (lane_grp == g_idx)
    le = jnp.where(in_grp, logits, NEG_BIG)
    e_max = jnp.max(le, axis=1, keepdims=True)
    e_exp = jnp.where(in_grp, jnp.exp(le - e_max), 0.0)
    ep = e_exp / jnp.sum(e_exp, axis=1, keepdims=True)
    ep = jnp.where(in_grp, ep, -1.0)
    p1 = jnp.max(ep, axis=1, keepdims=True)
    i1 = _first_lane_of(ep == p1, lane)
    ep2 = jnp.where(lane == i1, -1.0, ep)
    p2 = jnp.max(ep2, axis=1, keepdims=True)
    i2 = _first_lane_of(ep2 == p2, lane)
    den = p1 + p2
    return i1, i2, gp_top * p1 / den, gp_top * p2 / den


def _outproj_kernel(o_ref, yz_ref, x_ref, g1_ref, sc_ref, sh_ref, ng_ref, sng_ref, w_ref, wr_ref,
                    br_ref, tri_ref, cnt0_ref,
                    x1_ref, hn_ref, ids_ref, wts_ref, rank_ref, cnt_ref,
                    lhs_ref, x1s_ref, cnts_ref, *, n_col, gw):
    i = pl.program_id(0)
    j = pl.program_id(1)
    tm = lhs_ref.shape[0]
    tn = w_ref.shape[1]
    mix_a = o_ref.shape[1]

    @pl.when((i == 0) & (j == 0))
    def _():
        cnts_ref[...] = cnt0_ref[...]

    @pl.when(j == 0)
    def _():
        lhs_ref[:, 0:mix_a] = o_ref[...].astype(BF16)
        for g in range(yz_ref.shape[1] // gw):
            seg = yz_ref[:, g * gw:(g + 1) * gw]
            ms = jnp.mean(seg * seg, axis=-1, keepdims=True)
            seg = seg * lax.rsqrt(ms + EPS) * sng_ref[:, g * gw:(g + 1) * gw]
            lhs_ref[:, mix_a + g * gw:mix_a + (g + 1) * gw] = seg.astype(BF16)

    acc = _dot(lhs_ref[...], w_ref[...].astype(BF16))
    x1 = x_ref[...] + g1_ref[...] * acc.reshape(x_ref.shape)
    x1_ref[...] = x1
    x1s_ref[j] = x1.reshape(tm, tn)

    @pl.when(j == n_col - 1)
    def _():
        ssq = jnp.zeros((tm, 1), F32)
        for c in range(n_col):
            xc = x1s_ref[c]
            ssq = ssq + jnp.sum(xc * xc, axis=-1, keepdims=True)
        rs = lax.rsqrt(ssq / (n_col * tn) + EPS)
        gshape = x_ref.shape
        logits = jnp.zeros((tm, LANES), F32) + br_ref[...]
        for c in range(n_col):
            cols = slice(c * tn, (c + 1) * tn)
            hn = (x1s_ref[c] * rs * ng_ref[:, cols]).reshape(gshape)
            hn = (hn * (1.0 + sc_ref[:, :, cols]) + sh_ref[:, :, cols]).reshape(tm, tn)
            hn_ref[:, cols] = hn
            logits = logits + _dot_hp(hn, wr_ref[cols, :])
        i1, i2, w1, w2 = _route(logits)
        lane = lax.broadcasted_iota(I32, (tm, LANES), 1)
        hit1 = lane.astype(F32) == i1
        hit2 = lane.astype(F32) == i2
        onehot = jnp.where(hit1 | hit2, 1.0, 0.0).astype(BF16)
        before = _dot(tri_ref[...], onehot) + cnts_ref[0:1, :]
        r1 = jnp.sum(jnp.where(hit1, before, 0.0), axis=1, keepdims=True)
        r2 = jnp.sum(jnp.where(hit2, before, 0.0), axis=1, keepdims=True)
        ids_ref[...] = jnp.where(lane == 0, i1, jnp.where(lane == 1, i2, 0.0)).astype(I32)
        wts_ref[...] = jnp.where(lane == 0, w1, jnp.where(lane == 1, w2, 0.0))
        rank_ref[...] = jnp.where(lane == 0, r1, jnp.where(lane == 1, r2, 0.0)).astype(I32)
        total = cnts_ref[0:1, :] + jnp.sum(onehot.astype(F32), axis=0, keepdims=True)
        cnts_ref[...] = jnp.broadcast_to(total, cnts_ref.shape)
        cnt_ref[...] = jnp.broadcast_to(total, cnt_ref.shape)


def _strict_lower(n):
    import numpy as np
    t = np.arange(n)[:, None]
    s = np.arange(n)[None, :]
    return jnp.asarray((s < t).astype(np.float32), dtype=BF16)


def _out_projection(o3, yz3, x3, mod3, norm_g, ssm_norm_g, w_out, w_router, b_router, cnt0,
                    group_rows):
    bsz, t_len, d = x3.shape
    g_blk, r_blk = group_rows
    tm = g_blk * r_blk
    tn = PROJ_TN
    n_col = d // tn
    mix_a = o3.shape[-1]
    mix_b = yz3.shape[-1]
    n_tok = bsz * t_len
    tiles_per_seq = max(t_len // r_blk, 1)
    sec = d // tn

    def xmap(i, j):
        return (i // tiles_per_seq, i % tiles_per_seq, j)

    def modmap(s):
        return lambda i, j: (i // tiles_per_seq, 0, s)

    row = lambda i, j: (i, 0)
    const2 = lambda i, j: (0, 0)
    kern = functools.partial(_outproj_kernel, n_col=n_col, gw=mix_b // SSD_GROUPS)
    return pl.pallas_call(
        kern,
        grid=(n_tok // tm, n_col),
        in_specs=[
            pl.BlockSpec((tm, mix_a), row),
            pl.BlockSpec((tm, mix_b), row),
            pl.BlockSpec((g_blk, r_blk, tn), xmap),
            pl.BlockSpec((g_blk, 1, tn), lambda i, j: (i // tiles_per_seq, 0, 2 * sec + j)),
            pl.BlockSpec((g_blk, 1, d), modmap(4)),
            pl.BlockSpec((g_blk, 1, d), modmap(3)),
            pl.BlockSpec((1, d), const2),
            pl.BlockSpec((1, mix_b), const2),
            pl.BlockSpec((d, tn), lambda i, j: (0, j)),
            pl.BlockSpec((d, LANES), const2),
            pl.BlockSpec((1, LANES), const2),
            pl.BlockSpec((tm, tm), const2),
            pl.BlockSpec((SUBLANES, LANES), const2),
        ],
        out_specs=[
            pl.BlockSpec((g_blk, r_blk, tn), xmap),
            pl.BlockSpec((tm, d), row),
            pl.BlockSpec((tm, LANES), row),
            pl.BlockSpec((tm, LANES), row),
            pl.BlockSpec((tm, LANES), row),
            pl.BlockSpec((SUBLANES, LANES), const2),
        ],
        out_shape=[
            jax.ShapeDtypeStruct((bsz, t_len, d), F32),
            jax.ShapeDtypeStruct((n_tok, d), F32),
            jax.ShapeDtypeStruct((n_tok, LANES), I32),
            jax.ShapeDtypeStruct((n_tok, LANES), F32),
            jax.ShapeDtypeStruct((n_tok, LANES), I32),
            jax.ShapeDtypeStruct((SUBLANES, LANES), F32),
        ],
        scratch_shapes=[pltpu.VMEM((tm, d), BF16), pltpu.VMEM((n_col, tm, tn), F32),
                        pltpu.VMEM((SUBLANES, LANES), F32)],
        compiler_params=_cparams(("arbitrary", "arbitrary")),
        name="out_proj_router",
    )(o3.reshape(n_tok, mix_a), yz3.reshape(n_tok, mix_b), x3, mod3, mod3, mod3,
      norm_g.reshape(1, d), ssm_norm_g.reshape(1, mix_b), w_out, w_router, b_router,
      _strict_lower(tm), cnt0)


def _positions_kernel(ids_ref, rank_ref, cnt_ref, pos_ref, te_ref, nt_ref, *, block):
    lane8 = lax.broadcasted_iota(I32, (SUBLANES, LANES), 1)
    cnt = cnt_ref[...]
    tiles = jnp.floor((cnt + (MOE_TM - 1)) * (1.0 / MOE_TM))
    tiles = jnp.where(lane8 < N_EXPERTS, tiles, 0.0)
    incl = tiles
    shift = 1
    while shift < N_EXPERTS:
        incl = incl + jnp.where(lane8 >= shift, pltpu.roll(incl, shift, axis=1), 0.0)
        shift *= 2
    offs = ((incl - tiles) * MOE_TM)[0:1, :]
    nt_ref[...] = jnp.sum(tiles, axis=1, keepdims=True).astype(I32) + jnp.zeros(nt_ref.shape, I32)

    incl_col = jnp.broadcast_to(incl[0:1, :], (LANES, LANES)).T
    e_row = lax.broadcasted_iota(I32, (LANES, LANES), 0)
    i_lane = lax.broadcasted_iota(I32, (LANES, LANES), 1).astype(F32)
    done = jnp.where((incl_col <= i_lane) & (e_row < N_EXPERTS), 1.0, 0.0)
    te = jnp.minimum(jnp.sum(done, axis=0, keepdims=True), N_EXPERTS - 1.0)
    te_ref[...] = te.astype(I32) + jnp.zeros(te_ref.shape, I32)

    lane = lax.broadcasted_iota(I32, (block, LANES), 1)

    def body(bi, c):
        r0 = pl.multiple_of(bi * block, block)
        ids = ids_ref[pl.ds(r0, block), :]
        rank = rank_ref[pl.ds(r0, block), :]
        out = jnp.zeros((block, LANES), I32)
        for k in range(TOP_K):
            off = jnp.sum(jnp.where(lane == ids[:, k:k + 1], offs, 0.0), axis=1, keepdims=True)
            out = jnp.where(lane == k, off.astype(I32) + rank[:, k:k + 1], out)
        pos_ref[pl.ds(r0, block), :] = out
        return c

    lax.fori_loop(0, ids_ref.shape[0] // block, body, 0)


def _positions(ids, rank, cnt):
    n_tok = ids.shape[0]
    block = 512
    kern = functools.partial(_positions_kernel, block=block)
    full = lambda s: pl.BlockSpec(s, lambda: tuple(0 for _ in s))
    return pl.pallas_call(
        kern,
        in_specs=[full((n_tok, LANES)), full((n_tok, LANES)), full((SUBLANES, LANES))],
        out_specs=[full((n_tok, LANES)), full((SUBLANES, LANES)), full((SUBLANES, LANES))],
        out_shape=[jax.ShapeDtypeStruct((n_tok, LANES), I32),
                   jax.ShapeDtypeStruct((SUBLANES, LANES), I32),
                   jax.ShapeDtypeStruct((SUBLANES, LANES), I32)],
        compiler_params=pltpu.CompilerParams(vmem_limit_bytes=VMEM_LIMIT),
        name="route_positions",
    )(ids, rank, cnt)


def _row_copy(src_ref, src_row, dst_ref, dst_row, sem):
    return pltpu.make_async_copy(src_ref.at[pl.ds(src_row, 1), :], dst_ref.at[pl.ds(dst_row, 1), :],
                                 sem)


def _dispatch_kernel(pos_ref, hn_ref, xg_in_ref, xg_ref, sem):
    del xg_in_ref
    rows = hn_ref.shape[0]
    base = pl.program_id(0) * rows

    def issue(r, c):
        for k in range(TOP_K):
            _row_copy(hn_ref, r, xg_ref, pos_ref[(base + r) * TOP_K + k], sem).start(priority=k)
        return c

    def drain(r, c):
        for k in range(TOP_K):
            _row_copy(hn_ref, r, xg_ref, 0, sem).wait()
        return c

    lax.fori_loop(0, rows, issue, 0, unroll=8)
    lax.fori_loop(0, rows, drain, 0, unroll=8)


def _dispatch(pos_flat, hn, xg):
    n_tok, d = hn.shape
    grid_spec = pltpu.PrefetchScalarGridSpec(
        num_scalar_prefetch=1,
        grid=(n_tok // ROW_DMA_TILE,),
        in_specs=[pl.BlockSpec((ROW_DMA_TILE, d), lambda i, pos: (i, 0)),
                  pl.BlockSpec(memory_space=pl.ANY)],
        out_specs=pl.BlockSpec(memory_space=pl.ANY),
        scratch_shapes=[pltpu.SemaphoreType.DMA(())],
    )
    return pl.pallas_call(
        _dispatch_kernel,
        grid_spec=grid_spec,
        out_shape=jax.ShapeDtypeStruct(xg.shape, xg.dtype),
        input_output_aliases={2: 0},
        compiler_params=_cparams(("arbitrary",)),
        name="moe_dispatch",
    )(pos_flat, hn, xg)


def _moe_up_kernel(te_ref, nt_ref, x_ref, w1_ref, w3_ref, act_ref):
    t = pl.program_id(1)

    @pl.when(t < nt_ref[0])
    def _():
        xb = x_ref[...].astype(BF16)
        a = _dot(xb, w1_ref[0].astype(BF16))
        b = _dot(xb, w3_ref[0].astype(BF16))
        act_ref[...] = (_silu(a) * b).astype(BF16)

    @pl.when(t >= nt_ref[0])
    def _():
        act_ref[...] = jnp.zeros(act_ref.shape, act_ref.dtype)


def _moe_down_kernel(te_ref, nt_ref, act_ref, w2_ref, out_ref):
    t = pl.program_id(1)

    @pl.when(t < nt_ref[0])
    def _():
        out_ref[...] = _dot(act_ref[...], w2_ref[0].astype(BF16))

    @pl.when(t >= nt_ref[0])
    def _():
        out_ref[...] = jnp.zeros(out_ref.shape, out_ref.dtype)


def _moe_experts(te, nt, xg, w1, w3, w2):
    n_rows, d = xg.shape
    d_exp = w1.shape[-1]
    n_tiles = n_rows // MOE_TM
    halves = 2
    hn = d_exp // halves
    hd = d // halves

    def tile(h, t, te_ref, nt_ref):
        return jnp.minimum(t, nt_ref[0] - 1)

    act = pl.pallas_call(
        _moe_up_kernel,
        grid_spec=pltpu.PrefetchScalarGridSpec(
            num_scalar_prefetch=2,
            grid=(halves, n_tiles),
            in_specs=[
                pl.BlockSpec((MOE_TM, d), lambda h, t, te_r, nt_r: (tile(h, t, te_r, nt_r), 0)),
                pl.BlockSpec((1, d, hn), lambda h, t, te_r, nt_r: (te_r[tile(h, t, te_r, nt_r)], 0, h)),
                pl.BlockSpec((1, d, hn), lambda h, t, te_r, nt_r: (te_r[tile(h, t, te_r, nt_r)], 0, h)),
            ],
            out_specs=pl.BlockSpec((MOE_TM, hn), lambda h, t, te_r, nt_r: (t, h)),
        ),
        out_shape=jax.ShapeDtypeStruct((n_rows, d_exp), BF16),
        compiler_params=_cparams(("arbitrary", "arbitrary")),
        name="moe_up",
    )(te, nt, xg, w1, w3)

    return pl.pallas_call(
        _moe_down_kernel,
        grid_spec=pltpu.PrefetchScalarGridSpec(
            num_scalar_prefetch=2,
            grid=(halves, n_tiles),
            in_specs=[
                pl.BlockSpec((MOE_TM, d_exp), lambda h, t, te_r, nt_r: (tile(h, t, te_r, nt_r), 0)),
                pl.BlockSpec((1, d_exp, hd), lambda h, t, te_r, nt_r: (te_r[tile(h, t, te_r, nt_r)], 0, h)),
            ],
            out_specs=pl.BlockSpec((MOE_TM, hd), lambda h, t, te_r, nt_r: (t, h)),
        ),
        out_shape=jax.ShapeDtypeStruct((n_rows, d), F32),
        compiler_params=_cparams(("arbitrary", "arbitrary")),
        name="moe_down",
    )(te, nt, act, w2)


def _combine_kernel(pos_ref, x1_ref, g2_ref, wts_ref, fg_ref, eo_ref, y_ref, buf_ref, sem):
    rows = wts_ref.shape[0]
    base = pl.program_id(0) * rows

    def issue(r, c):
        for k in range(TOP_K):
            _row_copy(eo_ref, pos_ref[(base + r) * TOP_K + k], buf_ref.at[k], r, sem).start(
                priority=k)
        return c

    def drain(r, c):
        for k in range(TOP_K):
            _row_copy(eo_ref, 0, buf_ref.at[k], r, sem).wait()
        return c

    lax.fori_loop(0, rows, issue, 0, unroll=8)
    lax.fori_loop(0, rows, drain, 0, unroll=8)
    w = wts_ref[...]
    moe = w[:, 0:1] * buf_ref[0] + w[:, 1:2] * buf_ref[1]
    x2 = x1_ref[...] + g2_ref[...] * moe.reshape(x1_ref.shape)
    ms = jnp.mean(x2 * x2, axis=-1, keepdims=True)
    y_ref[...] = x2 * lax.rsqrt(ms + EPS) * fg_ref[...]


def _combine(pos_flat, x1, mod3, wts, final_g, expert_out, group_rows):
    bsz, t_len, d = x1.shape
    g_blk, r_blk = group_rows
    tm = g_blk * r_blk
    tiles_per_seq = max(t_len // r_blk, 1)

    def xmap(i, pos):
        return (i // tiles_per_seq, i % tiles_per_seq, 0)

    return pl.pallas_call(
        _combine_kernel,
        grid_spec=pltpu.PrefetchScalarGridSpec(
            num_scalar_prefetch=1,
            grid=((bsz * t_len) // tm,),
            in_specs=[
                pl.BlockSpec((g_blk, r_blk, d), xmap),
                pl.BlockSpec((g_blk, 1, d), lambda i, pos: (i // tiles_per_seq, 0, 5)),
                pl.BlockSpec((tm, LANES), lambda i, pos: (i, 0)),
                pl.BlockSpec((1, d), lambda i, pos: (0, 0)),
                pl.BlockSpec(memory_space=pl.ANY),
            ],
            out_specs=pl.BlockSpec((g_blk, r_blk, d), xmap),
            scratch_shapes=[pltpu.VMEM((TOP_K, tm, d), F32), pltpu.SemaphoreType.DMA(())],
        ),
        out_shape=jax.ShapeDtypeStruct((bsz, t_len, d), F32),
        compiler_params=_cparams(("arbitrary",)),
        name="moe_combine",
    )(pos_flat, x1, mod3, wts, final_g.reshape(1, d), expert_out)


def kernel(x_prompt, x_sample, state_hgrn, state_ssm, state_conv, c_prompt, c_sample, ada_w, ada_b,
           norm1_g, norm2_g, w_in, hgrn_lb, hgrn_onorm_g, conv_w, conv_b, dt_bias, a_log, d_skip,
           ssm_norm_g, w_out, w_grp, b_grp, w_rt, b_rt, w1, w3, w2, final_g):
    depth = w_in.shape[0]
    assert depth == 1, "single-layer trunk"
    layer = 0
    bp, t_p, d = x_prompt.shape
    bs, t_s, _ = x_sample.shape
    mix_a = hgrn_onorm_g.shape[1]
    mix_b = ssm_norm_g.shape[1]
    conv_dim = conv_w.shape[2]
    n_main = 4 * mix_a + mix_b + conv_dim
    n_ssd_heads = dt_bias.shape[1]
    n_pairs = mix_b // LANES
    xbc0 = 4 * mix_a + mix_b
    tail = CONV_WIDTH - 1
    assert t_s <= SUBLANES and t_s >= tail and t_p % PROMPT_CHUNK == 0

    xs_pad = jnp.pad(x_sample, ((0, 0), (0, SUBLANES - t_s), (0, 0)))
    n_c = bp + bs
    c_rows = -(-n_c // SUBLANES) * SUBLANES
    c_all = jnp.pad(jnp.concatenate([c_prompt, c_sample], axis=0), ((0, c_rows - n_c), (0, 0)))

    mod = _modulation(c_all, ada_w[layer], ada_b[layer])
    mod_p = mod[:bp].reshape(bp, 1, -1)
    mod_s = mod[bp:n_c].reshape(bs, 1, -1)
    w_dt = jnp.pad(w_in[layer][:, n_main:], ((0, 0), (0, LANES - n_ssd_heads)))
    w_router = jnp.pad(jnp.concatenate([w_rt[layer], w_grp[layer]], axis=1),
                       ((0, 0), (0, LANES - N_EXPERTS - N_EXPERT_GROUPS)))
    b_router = jnp.pad(jnp.concatenate([b_rt[layer], b_grp[layer]]),
                       (0, LANES - N_EXPERTS - N_EXPERT_GROUPS)).reshape(1, LANES)

    groups = (
        (x_prompt, mod_p, None, None, None, t_p, PROMPT_CHUNK, PROMPT_CHUNK, MIXER_ROWS,
         (1, 8), 2, (1, PROJ_TM), (1, OUT_TM)),
        (xs_pad, mod_s, state_hgrn[layer], state_ssm[layer].reshape(bs, n_pairs, LANES, LANES),
         state_conv[layer], t_s, SUBLANES, t_s, SUBLANES,
         (16, 1), 4, (PROJ_TM // SUBLANES, SUBLANES), (OUT_TM // SUBLANES, SUBLANES)),
    )

    cnt = jnp.zeros((SUBLANES, LANES), F32)
    per_group = []
    for x3, mod3, s0, h0, c0, t_real, ch, valid, tt, (hg_gb, hg_hb), ssd_gb, proj_gr, out_gr in groups:
        bsz, t_len, _ = x3.shape
        proj, dt = _in_projection(x3, mod3, norm1_g[layer], w_in[layer], w_dt, n_main, proj_gr)
        proj3 = proj.reshape(bsz, t_len, n_main)
        dt3 = dt.reshape(bsz, t_len, LANES)
        o3, s_new = _hgrn_heads(proj3, hgrn_lb, hgrn_onorm_g[layer], s0, ch=ch, valid=valid,
                                gb=hg_gb, hb=hg_hb, tt=tt, layer=layer)
        yz3, h_new = _ssd_pairs(proj3, dt3, conv_w[layer], conv_b[layer], dt_bias[layer],
                                a_log[layer], d_skip[layer], h0, c0, mix_a=mix_a, mix_b=mix_b,
                                ch=ch, valid=valid, gb=ssd_gb, tt=tt)
        conv_new = proj3[:, t_real - tail:t_real, xbc0:xbc0 + conv_dim]
        x1, hn, ids, wts, rank, cnt = _out_projection(
            o3, yz3, x3, mod3, norm2_g[layer], ssm_norm_g[layer], w_out[layer], w_router, b_router,
            cnt, out_gr)
        per_group.append(dict(x1=x1, hn=hn, ids=ids, wts=wts, rank=rank, mod3=mod3, out_gr=out_gr,
                              s_new=s_new, h_new=h_new, conv_new=conv_new))

    ids_all = jnp.concatenate([g["ids"] for g in per_group], axis=0)
    rank_all = jnp.concatenate([g["rank"] for g in per_group], axis=0)
    n_tok = ids_all.shape[0]
    pos, te, nt = _positions(ids_all, rank_all, cnt)
    pos_flat = pos[:, :TOP_K].reshape(-1)
    n_tiles = -(-(n_tok * TOP_K + N_EXPERTS * (MOE_TM - 1)) // MOE_TM)
    assert n_tiles <= LANES
    te_vec = te[0, :n_tiles]
    nt_vec = nt[0, :1]

    xg = jnp.zeros((n_tiles * MOE_TM, d), F32)
    row0 = 0
    for g in per_group:
        rows = g["hn"].shape[0]
        g["pos"] = pos_flat[row0 * TOP_K:(row0 + rows) * TOP_K]
        xg = _dispatch(g["pos"], g["hn"], xg)
        row0 += rows
    expert_out = _moe_experts(te_vec, nt_vec, xg, w1[layer], w3[layer], w2[layer])
    ys = [_combine(g["pos"], g["x1"], g["mod3"], g["wts"], final_g, expert_out, g["out_gr"])
          for g in per_group]

    gp, gs = per_group
    return (
        ys[0],
        ys[1][:, :t_s],
        gp["s_new"][None],
        gp["h_new"].reshape(1, bp, n_ssd_heads, SSD_HEAD_DIM, SSD_STATE),
        gp["conv_new"][None],
        gs["s_new"][None],
        gs["h_new"].reshape(1, bs, n_ssd_heads, SSD_HEAD_DIM, SSD_STATE),
        gs["conv_new"][None],
    )
```

```python
import functools

import jax
import jax.numpy as jnp
from jax import lax
from jax.experimental import pallas as pl
from jax.experimental.pallas import tpu as pltpu

F32 = jnp.float32
BF16 = jnp.bfloat16
I32 = jnp.int32
EPS = 1e-6
LOG2E = 1.4426950408889634

LANES = 128
SUBLANES = 8
VMEM_LIMIT = 56 * 1024 * 1024

HGRN_HEAD_DIM = 128
SSD_HEAD_DIM = 64
SSD_GROUPS = 2
SSD_STATE = 128
CONV_WIDTH = 4
N_EXPERT_GROUPS = 4
EXPERTS_PER_GROUP = 8
N_EXPERTS = N_EXPERT_GROUPS * EXPERTS_PER_GROUP
TOP_K = 2

PROJ_TM = 1024
PROJ_TN = 512
MOE_TM = 256
MOE_COL_PASSES = 1
PROMPT_CHUNK = 64
MIXER_ROWS = 512
ROW_DMA_TILE = 256


def _cparams(sem):
    return pltpu.CompilerParams(dimension_semantics=sem, vmem_limit_bytes=VMEM_LIMIT)


def _dot(a, b):
    return jnp.dot(a, b, preferred_element_type=F32)


def _dot_nt(a, b):
    return lax.dot_general(a, b, (((1,), (1,)), ((), ())), preferred_element_type=F32)


def _split2(x):
    hi = x.astype(BF16)
    lo = (x - hi.astype(F32)).astype(BF16)
    return hi, lo


def _split3(x):
    hi = x.astype(BF16)
    r = x - hi.astype(F32)
    mid = r.astype(BF16)
    lo = (r - mid.astype(F32)).astype(BF16)
    return hi, mid, lo


def _dot_exact_lhs(m_bf16, x):
    hi, mid, lo = _split3(x)
    return _dot(m_bf16, hi) + _dot(m_bf16, mid) + _dot(m_bf16, lo)


def _dot_exact_rhs(x, m_bf16):
    hi, mid, lo = _split3(x)
    return _dot(hi, m_bf16) + _dot(mid, m_bf16) + _dot(lo, m_bf16)


def _dot_hp(a, w):
    ah, al = _split2(a)
    wh, wl = _split2(w)
    return _dot(ah, wh) + _dot(al, wh) + _dot(ah, wl)


def _silu(x):
    return x * jax.nn.sigmoid(x)


def _mod_kernel(c_ref, w_ref, b_ref, o_ref):
    a = _silu(c_ref[...])
    o_ref[...] = _dot_hp(a, w_ref[...]) + b_ref[...]


def _modulation(c_all, ada_w, ada_b):
    rows, d = c_all.shape
    n_out = ada_w.shape[1]
    tn = 1024
    return pl.pallas_call(
        _mod_kernel,
        grid=(n_out // tn,),
        in_specs=[
            pl.BlockSpec((rows, d), lambda j: (0, 0)),
            pl.BlockSpec((d, tn), lambda j: (0, j)),
            pl.BlockSpec((1, tn), lambda j: (0, j)),
        ],
        out_specs=pl.BlockSpec((rows, tn), lambda j: (0, j)),
        out_shape=jax.ShapeDtypeStruct((rows, n_out), F32),
        compiler_params=_cparams(("arbitrary",)),
        name="adaln_mod",
    )(c_all, ada_w, ada_b.reshape(1, n_out))


def _proj_kernel(x_ref, sc_ref, sh_ref, g_ref, w_ref, wdt_ref, o_ref, dt_ref, hn_ref):
    j = pl.program_id(1)

    @pl.when(j == 0)
    def _():
        x = x_ref[...]
        ms = jnp.mean(x * x, axis=-1, keepdims=True)
        hn = (x * lax.rsqrt(ms + EPS)) * g_ref[...] * (1.0 + sc_ref[...]) + sh_ref[...]
        hn = hn.reshape(hn_ref.shape)
        hn_ref[...] = hn.astype(BF16)
        dt_ref[...] = _dot_hp(hn, wdt_ref[...])

    o_ref[...] = _dot(hn_ref[...], w_ref[...].astype(BF16))


def _in_projection(x3, mod3, norm_g, w_main, w_dt, n_main, group_rows):
    bsz, t_len, d = x3.shape
    g_blk, r_blk = group_rows
    tm = g_blk * r_blk
    tiles_per_seq = max(t_len // r_blk, 1)
    n_row_tiles = (bsz * t_len) // tm

    def x_map(i, j):
        return (i // tiles_per_seq, i % tiles_per_seq, 0)

    def mod_map(sec):
        return lambda i, j: (i // tiles_per_seq, 0, sec)

    return pl.pallas_call(
        _proj_kernel,
        grid=(n_row_tiles, n_main // PROJ_TN),
        in_specs=[
            pl.BlockSpec((g_blk, r_blk, d), x_map),
            pl.BlockSpec((g_blk, 1, d), mod_map(1)),
            pl.BlockSpec((g_blk, 1, d), mod_map(0)),
            pl.BlockSpec((1, d), lambda i, j: (0, 0)),
            pl.BlockSpec((d, PROJ_TN), lambda i, j: (0, j)),
            pl.BlockSpec((d, LANES), lambda i, j: (0, 0)),
        ],
        out_specs=[
            pl.BlockSpec((tm, PROJ_TN), lambda i, j: (i, j)),
            pl.BlockSpec((tm, LANES), lambda i, j: (i, 0)),
        ],
        out_shape=[
            jax.ShapeDtypeStruct((bsz * t_len, n_main), F32),
            jax.ShapeDtypeStruct((bsz * t_len, LANES), F32),
        ],
        scratch_shapes=[pltpu.VMEM((tm, d), BF16)],
        compiler_params=_cparams(("arbitrary", "arbitrary")),
        name="in_proj",
    )(x3, mod3, mod3, norm_g.reshape(1, d), w_main, w_dt)


def _level_sizes(ch):
    sizes = []
    sz = SUBLANES
    while 2 * sz <= ch:
        sizes.append(sz)
        sz *= 2
    return sizes


def _level_masks(ch, n_rows):
    import numpy as np
    t = np.arange(n_rows)[:, None]
    s = np.arange(n_rows)[None, :]
    out = []
    for sz in _level_sizes(ch):
        m = (t // (2 * sz) == s // (2 * sz)) & (t % (2 * sz) >= sz) & (s % (2 * sz) < sz)
        out.append(m.astype(np.float32))
    if not out:
        out.append(np.zeros((n_rows, n_rows), np.float32))
    return jnp.asarray(np.stack(out))


def _hgrn_kernel(*refs, ch, valid, nc, gb, hb, has_init, layer):
    if has_init:
        (q_ref, f_ref, i_ref, g_ref, lb_ref, on_ref, tri_ref, masks_ref, s0_ref,
         o_ref, sout_ref, st_ref, b_scr, k_scr) = refs
    else:
        (q_ref, f_ref, i_ref, g_ref, lb_ref, on_ref, tri_ref, masks_ref,
         o_ref, sout_ref, st_ref, b_scr, k_scr) = refs
        s0_ref = None

    n_lb = lb_ref.shape[0]
    lb_rows = [lb_ref[i:i + 1, :] for i in range(n_lb)]
    lb_max = functools.reduce(jnp.maximum, lb_rows)
    lb_exp = [jnp.exp(r - lb_max) for r in lb_rows]
    lb = sum(lb_exp[:layer + 1]) / sum(lb_exp)
    onorm = on_ref[...]
    tri = tri_ref[...]
    sizes = _level_sizes(ch)
    lane8 = lax.broadcasted_iota(I32, (SUBLANES, LANES), 1)
    row8 = lax.broadcasted_iota(I32, (SUBLANES, LANES), 0)
    tb = pl.program_id(2)

    n_rows = gb * ch
    row_id = lax.broadcasted_iota(I32, (n_rows, LANES), 0)

    def stack(fn):
        parts = [fn(g) for g in range(gb)]
        return parts[0] if gb == 1 else jnp.concatenate(parts, axis=0)

    def chunk(hh, ci):
        cols = slice(hh * LANES, (hh + 1) * LANES)
        t0 = pl.multiple_of(ci * ch, ch)
        rows = pl.ds(t0, ch)
        lb_h = lb[:, cols]
        q = _silu(stack(lambda g: q_ref[g, rows, cols]))
        fg = lb_h + (1.0 - lb_h) * jax.nn.sigmoid(stack(lambda g: f_ref[g, rows, cols]))
        k = 1.0 - fg
        lf = jnp.log(fg)
        v = stack(lambda g: i_ref[g, rows, cols])
        if valid < ch:
            live = (row_id & (ch - 1)) < valid
            lf = jnp.where(live, lf, 0.0)
            k = jnp.where(live, k, 0.0)
        b2 = _dot_exact_lhs(tri, lf) * LOG2E
        b_scr[hh] = b2
        k_scr[hh] = k
        b_end_rows = [b_scr[hh, pl.ds(g * ch + ch - 1, 1), :] for g in range(gb)]
        b_end = stack(lambda g: jnp.broadcast_to(b_end_rows[g], (ch, LANES)))

        qe = (q * jnp.exp2(b2)).astype(BF16)
        o = stack(lambda g: _dot_nt(qe[g * ch:(g + 1) * ch], st_ref[g * hb + hh].astype(BF16)))

        scores = jnp.zeros((n_rows, n_rows), F32)
        for lvl, sz in enumerate(sizes):
            pieces = []
            for m in range(n_rows // (2 * sz)):
                r = b_scr[hh, pl.ds(2 * sz * m + sz - 1, 1), :]
                pieces.append(jnp.broadcast_to(r, (2 * sz, LANES)))
            r_all = pieces[0] if len(pieces) == 1 else jnp.concatenate(pieces, axis=0)
            e = jnp.exp2(-jnp.abs(b2 - r_all))
            s_l = _dot_nt((q * e).astype(BF16), (k * e).astype(BF16))
            scores = scores + masks_ref[lvl] * s_l
        blocks = []
        for jb in range(n_rows // SUBLANES):
            bb = b2[SUBLANES * jb:SUBLANES * (jb + 1)]
            qb = q[SUBLANES * jb:SUBLANES * (jb + 1)]
            acc = jnp.zeros((SUBLANES, LANES), F32)
            for sl in range(SUBLANES):
                s = SUBLANES * jb + sl
                bs = b_scr[hh, pl.ds(s, 1), :]
                ks = k_scr[hh, pl.ds(s, 1), :]
                val = (qb * ks) * jnp.exp2(bb - bs)
                red = jnp.sum(val, axis=1, keepdims=True)
                acc = jnp.where(lane8 == s, red, acc)
            blocks.append(jnp.where(lane8 - SUBLANES * jb <= row8, acc, 0.0))
        scores = scores + jnp.concatenate(blocks, axis=0)
        o = o + _dot(scores.astype(BF16), v.astype(BF16))

        dk = k * jnp.exp2(b_end - b2)
        v_t = v.T.astype(BF16)
        for g in range(gb):
            own = (row_id >= g * ch) & (row_id < (g + 1) * ch)
            c = g * hb + hh
            st_ref[c] = (st_ref[c] * jnp.exp2(b_end_rows[g])
                         + _dot(v_t, jnp.where(own, dk, 0.0).astype(BF16)))

        on = o * lax.rsqrt(jnp.mean(o * o, axis=-1, keepdims=True) + EPS) * onorm[:, cols]
        out = on * _silu(stack(lambda g: g_ref[g, rows, cols]))
        for g in range(gb):
            o_ref[g, rows, cols] = out[g * ch:(g + 1) * ch]

    chains = [(g * hb + hh, g, hh) for g in range(gb) for hh in range(hb)]

    @pl.when(tb == 0)
    def _():
        for c, g, hh in chains:
            if has_init:
                st_ref[c] = s0_ref[g, hh].T
            else:
                st_ref[c] = jnp.zeros((LANES, LANES), F32)

    def step(ci, carry):
        for hh in range(hb):
            chunk(hh, ci)
        return carry

    if nc == 1:
        step(0, 0)
    else:
        lax.fori_loop(0, nc, step, 0)

    @pl.when(tb == pl.num_programs(2) - 1)
    def _():
        for c, g, hh in chains:
            sout_ref[g, hh] = st_ref[c].T


def _hgrn_heads(proj3, hgrn_lb, onorm_g, s0, *, ch, valid, gb, hb, tt, layer):
    bsz, t_len, _ = proj3.shape
    n_heads = onorm_g.shape[0] // HGRN_HEAD_DIM
    n_hg = n_heads // hb
    nc = tt // ch
    n_rows = gb * ch
    assert n_rows == LANES
    has_init = s0 is not None
    w = hb * LANES

    def col(sec):
        return lambda b, h, t: (b, t, sec * n_hg + h)

    in_specs = [pl.BlockSpec((gb, tt, w), col(s)) for s in range(4)]
    in_specs += [
        pl.BlockSpec((hgrn_lb.shape[0], w), lambda b, h, t: (0, h)),
        pl.BlockSpec((1, w), lambda b, h, t: (0, h)),
        pl.BlockSpec((n_rows, n_rows), lambda b, h, t: (0, 0)),
        pl.BlockSpec((max(len(_level_sizes(ch)), 1), n_rows, n_rows), lambda b, h, t: (0, 0, 0)),
    ]
    args = [proj3, proj3, proj3, proj3, hgrn_lb, onorm_g.reshape(1, -1),
            jnp.asarray(_block_tri(gb, ch), dtype=BF16), _level_masks(ch, n_rows)]
    if has_init:
        in_specs.append(pl.BlockSpec((gb, hb, LANES, LANES), lambda b, h, t: (b, h, 0, 0)))
        args.append(s0)
    kern = functools.partial(_hgrn_kernel, ch=ch, valid=valid, nc=nc, gb=gb, hb=hb,
                             has_init=has_init, layer=layer)
    n_chain = gb * hb
    return pl.pallas_call(
        kern,
        grid=(bsz // gb, n_hg, t_len // tt),
        in_specs=in_specs,
        out_specs=[
            pl.BlockSpec((gb, tt, w), lambda b, h, t: (b, t, h)),
            pl.BlockSpec((gb, hb, LANES, LANES), lambda b, h, t: (b, h, 0, 0)),
        ],
        out_shape=[
            jax.ShapeDtypeStruct((bsz, t_len, n_heads * HGRN_HEAD_DIM), F32),
            jax.ShapeDtypeStruct((bsz, n_heads, LANES, LANES), F32),
        ],
        scratch_shapes=[pltpu.VMEM((n_chain, LANES, LANES), F32),
                        pltpu.VMEM((hb, n_rows, LANES), F32),
                        pltpu.VMEM((hb, n_rows, LANES), F32)],
        compiler_params=_cparams(("arbitrary", "arbitrary", "arbitrary")),
        name="hgrn2_heads",
    )(*args)


def _softplus(x):
    return jnp.maximum(x, 0.0) + jnp.log1p(jnp.exp(-jnp.abs(x)))


def _ssd_kernel(*refs, ch, valid, nc, gb, pb, has_init):
    if has_init:
        (z_ref, x_ref, b_ref, c_ref, dt_ref, cwx_ref, cwb_ref, cwc_ref, cbx_ref, cbb_ref, cbc_ref,
         xj_ref, bias_ref, alog_ref, d_ref, tri_ref, tril_ref, h0_ref, cx0_ref, cb0_ref, cc0_ref,
         y_ref, hout_ref, ht_ref, wx_ref, wb_ref, wc_ref) = refs
    else:
        (z_ref, x_ref, b_ref, c_ref, dt_ref, cwx_ref, cwb_ref, cwc_ref, cbx_ref, cbb_ref, cbc_ref,
         xj_ref, bias_ref, alog_ref, d_ref, tri_ref, tril_ref,
         y_ref, hout_ref, ht_ref, wx_ref, wb_ref, wc_ref) = refs
        h0_ref = cx0_ref = cb0_ref = cc0_ref = None

    tri = tri_ref[...]
    tril = tril_ref[...]
    lane = lax.broadcasted_iota(I32, (LANES, LANES), 1)
    first_head = lane < SSD_HEAD_DIM
    tail = CONV_WIDTH - 1
    tb = pl.program_id(2)

    def conv(raw_ref, win_ref, cw_ref, cb_ref, g, t0):
        win_ref[g, pl.ds(SUBLANES, ch), :] = raw_ref[g, pl.ds(t0, ch), :]
        u = cb_ref[...]
        for j in range(CONV_WIDTH):
            u = u + win_ref[g, pl.ds(SUBLANES - tail + j, ch), :] * cw_ref[j:j + 1, :]
        win_ref[g, pl.ds(0, SUBLANES), :] = win_ref[g, pl.ds(ch, SUBLANES), :]
        return _silu(u)

    n_rows = gb * ch
    w = pb * LANES
    xj = xj_ref[0]
    bias = bias_ref[0]
    a_neg = -jnp.exp(alog_ref[0])
    d_skip = d_ref[0]
    row_id = lax.broadcasted_iota(I32, (n_rows, w), 0)

    def stack(fn):
        parts = [fn(g) for g in range(gb)]
        return parts[0] if gb == 1 else jnp.concatenate(parts, axis=0)

    def step(ci, carry):
        t0 = pl.multiple_of(ci * ch, ch)
        rows = pl.ds(t0, ch)
        xs = stack(lambda g: conv(x_ref, wx_ref, cwx_ref, cbx_ref, g, t0))
        bm = stack(lambda g: conv(b_ref, wb_ref, cwb_ref, cbb_ref, g, t0))
        cm = stack(lambda g: conv(c_ref, wc_ref, cwc_ref, cbc_ref, g, t0))
        dt = _softplus(_dot_exact_rhs(stack(lambda g: dt_ref[g, rows, :]), xj) + bias)
        if valid < ch:
            dt = jnp.where((row_id & (ch - 1)) < valid, dt, 0.0)
        la_cs = _dot_exact_lhs(tri, dt * a_neg)
        xdt = xs * dt
        a_end_rows = [la_cs[g * ch + ch - 1:g * ch + ch, :] for g in range(gb)]
        a_end = stack(lambda g: jnp.broadcast_to(a_end_rows[g], (ch, w)))

        cb = _dot_nt(cm.astype(BF16), bm.astype(BF16))
        acs_t = la_cs.T
        y_parts = []
        for p in range(pb):
            cols = slice(p * LANES, (p + 1) * LANES)
            scores = []
            for hl in range(2):
                at = p * LANES + hl * SSD_HEAD_DIM
                lmat = jnp.exp(jnp.minimum(la_cs[:, at:at + 1] - acs_t[at:at + 1, :], 0.0)) * tril
                scores.append((cb * lmat).astype(BF16))
            xp = xdt[:, cols]
            rhs = jnp.concatenate([jnp.where(first_head, xp, 0.0),
                                   jnp.where(first_head, 0.0, xp)], axis=0).astype(BF16)
            y_parts.append(_dot(jnp.concatenate(scores, axis=1), rhs))
        y = y_parts[0] if pb == 1 else jnp.concatenate(y_parts, axis=1)

        y_off = stack(lambda g: _dot(cm[g * ch:(g + 1) * ch].astype(BF16), ht_ref[g].astype(BF16)))
        y = y + y_off * jnp.exp(la_cs)
        b_t = bm.T.astype(BF16)
        upd = xdt * jnp.exp(a_end - la_cs)
        for g in range(gb):
            own = (row_id >= g * ch) & (row_id < (g + 1) * ch)
            ht_ref[g] = (ht_ref[g] * jnp.exp(a_end_rows[g])
                         + _dot(b_t, jnp.where(own, upd, 0.0).astype(BF16)))

        y = (y + d_skip * xs) * _silu(stack(lambda g: z_ref[g, rows, :]))
        for g in range(gb):
            y_ref[g, rows, :] = y[g * ch:(g + 1) * ch]
        return carry

    @pl.when(tb == 0)
    def _():
        for g in range(gb):
            for win_ref, c0_ref in ((wx_ref, cx0_ref), (wb_ref, cb0_ref), (wc_ref, cc0_ref)):
                win_ref[g, pl.ds(0, SUBLANES), :] = jnp.zeros((SUBLANES, win_ref.shape[2]), F32)
                if has_init:
                    win_ref[g, pl.ds(SUBLANES - tail, tail), :] = c0_ref[g]
            for p in range(pb):
                cols = slice(p * LANES, (p + 1) * LANES)
                if has_init:
                    ht_ref[g, :, cols] = h0_ref[g, p].T
                else:
                    ht_ref[g, :, cols] = jnp.zeros((LANES, LANES), F32)

    if nc == 1:
        step(0, 0)
    else:
        lax.fori_loop(0, nc, step, 0)

    @pl.when(tb == pl.num_programs(2) - 1)
    def _():
        for g in range(gb):
            for p in range(pb):
                hout_ref[g, p] = ht_ref[g, :, p * LANES:(p + 1) * LANES].T


def _group_rows(v):
    return jnp.repeat(v.astype(F32), SSD_HEAD_DIM).reshape(SSD_GROUPS, 1, -1)


def _head_select(n_heads):
    import numpy as np
    w = n_heads // SSD_GROUPS * SSD_HEAD_DIM
    h = np.arange(LANES)[None, :, None]
    lane = np.arange(w)[None, None, :]
    g = np.arange(SSD_GROUPS)[:, None, None]
    return jnp.asarray((h == g * (n_heads // SSD_GROUPS) + lane // SSD_HEAD_DIM).astype(np.float32),
                       dtype=BF16)


def _block_tri(gb, ch):
    import numpy as np
    t = np.arange(gb * ch)[:, None]
    s = np.arange(gb * ch)[None, :]
    return ((t // ch == s // ch) & (s <= t)).astype(np.float32)


def _ssd_pairs(proj3, dt3, conv_w, conv_b, dt_bias, a_log, d_skip, h0, conv0, *, mix_a, mix_b,
               ch, valid, gb, tt):
    bsz, t_len, _ = proj3.shape
    n_pairs = mix_b // LANES
    pb = n_pairs // SSD_GROUPS
    w = pb * LANES
    nc = tt // ch
    n_rows = gb * ch
    has_init = h0 is not None
    assert (4 * mix_a) % w == 0 and n_rows == LANES
    z0 = 4 * mix_a // w
    x0 = z0 + SSD_GROUPS
    b0 = (4 * mix_a + 2 * mix_b) // LANES
    c0 = b0 + SSD_GROUPS

    seq_w = lambda base: (lambda b, g, t: (b, t, base + g))
    par = lambda base: (lambda b, g, t: (0, base + g))
    const2 = lambda b, g, t: (0, 0)

    in_specs = [
        pl.BlockSpec((gb, tt, w), seq_w(z0)),
        pl.BlockSpec((gb, tt, w), seq_w(x0)),
        pl.BlockSpec((gb, tt, LANES), seq_w(b0)),
        pl.BlockSpec((gb, tt, LANES), seq_w(c0)),
        pl.BlockSpec((gb, tt, LANES), lambda b, g, t: (b, t, 0)),
        pl.BlockSpec((CONV_WIDTH, w), par(0)),
        pl.BlockSpec((CONV_WIDTH, LANES), par(n_pairs)),
        pl.BlockSpec((CONV_WIDTH, LANES), par(n_pairs + SSD_GROUPS)),
        pl.BlockSpec((1, w), par(0)),
        pl.BlockSpec((1, LANES), par(n_pairs)),
        pl.BlockSpec((1, LANES), par(n_pairs + SSD_GROUPS)),
        pl.BlockSpec((1, LANES, w), lambda b, g, t: (g, 0, 0)),
        pl.BlockSpec((1, 1, w), lambda b, g, t: (g, 0, 0)),
        pl.BlockSpec((1, 1, w), lambda b, g, t: (g, 0, 0)),
        pl.BlockSpec((1, 1, w), lambda b, g, t: (g, 0, 0)),
        pl.BlockSpec((n_rows, n_rows), const2),
        pl.BlockSpec((n_rows, n_rows), const2),
    ]
    conv_b2 = conv_b.reshape(1, -1)
    block_tri = _block_tri(gb, ch)
    args = [proj3, proj3, proj3, proj3, dt3, conv_w, conv_w, conv_w, conv_b2, conv_b2, conv_b2,
            _head_select(dt_bias.shape[0]), _group_rows(dt_bias), _group_rows(a_log),
            _group_rows(d_skip), jnp.asarray(block_tri, dtype=BF16), jnp.asarray(block_tri)]
    if has_init:
        tail = CONV_WIDTH - 1
        in_specs += [
            pl.BlockSpec((gb, pb, LANES, LANES), lambda b, g, t: (b, g, 0, 0)),
            pl.BlockSpec((gb, tail, w), lambda b, g, t: (b, 0, g)),
            pl.BlockSpec((gb, tail, LANES), lambda b, g, t: (b, 0, n_pairs + g)),
            pl.BlockSpec((gb, tail, LANES), lambda b, g, t: (b, 0, n_pairs + SSD_GROUPS + g)),
        ]
        args += [h0, conv0, conv0, conv0]
    kern = functools.partial(_ssd_kernel, ch=ch, valid=valid, nc=nc, gb=gb, pb=pb,
                             has_init=has_init)
    return pl.pallas_call(
        kern,
        grid=(bsz // gb, SSD_GROUPS, t_len // tt),
        in_specs=in_specs,
        out_specs=[
            pl.BlockSpec((gb, tt, w), lambda b, g, t: (b, t, g)),
            pl.BlockSpec((gb, pb, LANES, LANES), lambda b, g, t: (b, g, 0, 0)),
        ],
        out_shape=[
            jax.ShapeDtypeStruct((bsz, t_len, mix_b), F32),
            jax.ShapeDtypeStruct((bsz, n_pairs, LANES, LANES), F32),
        ],
        scratch_shapes=[pltpu.VMEM((gb, LANES, w), F32),
                        pltpu.VMEM((gb, ch + SUBLANES, w), F32),
                        pltpu.VMEM((gb, ch + SUBLANES, LANES), F32),
                        pltpu.VMEM((gb, ch + SUBLANES, LANES), F32)],
        compiler_params=_cparams(("arbitrary", "arbitrary", "arbitrary")),
        name="ssd_pairs",
    )(*args)


OUT_TM = 512
GROUP_LANE0 = N_EXPERTS
NEG_BIG = -1e30
NO_LANE = 4 * LANES


def _first_lane_of(mask, lane):
    return jnp.min(jnp.where(mask, lane, float(NO_LANE)), axis=1, keepdims=True)


def _route(logits):
    lane_i = lax.broadcasted_iota(I32, logits.shape, 1)
    lane = lane_i.astype(F32)
    is_grp = (lane_i >= GROUP_LANE0) & (lane_i < GROUP_LANE0 + N_EXPERT_GROUPS)
    lg = jnp.where(is_grp, logits, NEG_BIG)
    g_max = jnp.max(lg, axis=1, keepdims=True)
    g_sum = jnp.sum(jnp.where(is_grp, jnp.exp(lg - g_max), 0.0), axis=1, keepdims=True)
    g_idx = _first_lane_of(lg == g_max, lane) - GROUP_LANE0
    gp_top = 1.0 / g_sum
    lane_grp = jnp.right_shift(lane_i, 3).astype(F32)
    in_grp = (lane_i < N_EXPERTS) & (lane_grp == g_idx)
    le = jnp.where(in_grp, logits, NEG_BIG)
    e_max = jnp.max(le, axis=1, keepdims=True)
    e_exp = jnp.where(in_grp, jnp.exp(le - e_max), 0.0)
    ep = e_exp / jnp.sum(e_exp, axis=1, keepdims=True)
    ep = jnp.where(in_grp, ep, -1.0)
    p1 = jnp.max(ep, axis=1, keepdims=True)
    i1 = _first_lane_of(ep == p1, lane)
    ep2 = jnp.where(lane == i1, -1.0, ep)
    p2 = jnp.max(ep2, axis=1, keepdims=True)
    i2 = _first_lane_of(ep2 == p2, lane)
    den = p1 + p2
    return i1, i2, gp_top * p1 / den, gp_top * p2 / den


def _outproj_kernel(o_ref, yz_ref, x_ref, g1_ref, sc_ref, sh_ref, ng_ref, sng_ref, w_ref, wr_ref,
                    br_ref, tri_ref, cnt0_ref,
                    x1_ref, hn_ref, ids_ref, wts_ref, rank_ref, cnt_ref,
                    lhs_ref, x1s_ref, cnts_ref, *, n_col, gw):
    i = pl.program_id(0)
    j = pl.program_id(1)
    tm = lhs_ref.shape[0]
    tn = w_ref.shape[1]
    mix_a = o_ref.shape[1]

    @pl.when((i == 0) & (j == 0))
    def _():
        cnts_ref[...] = cnt0_ref[...]

    @pl.when(j == 0)
    def _():
        lhs_ref[:, 0:mix_a] = o_ref[...].astype(BF16)
        for g in range(yz_ref.shape[1] // gw):
            seg = yz_ref[:, g * gw:(g + 1) * gw]
            ms = jnp.mean(seg * seg, axis=-1, keepdims=True)
            seg = seg * lax.rsqrt(ms + EPS) * sng_ref[:, g * gw:(g + 1) * gw]
            lhs_ref[:, mix_a + g * gw:mix_a + (g + 1) * gw] = seg.astype(BF16)

    acc = _dot(lhs_ref[...], w_ref[...].astype(BF16))
    x1 = x_ref[...] + g1_ref[...] * acc.reshape(x_ref.shape)
    x1_ref[...] = x1
    x1s_ref[j] = x1.reshape(tm, tn)

    @pl.when(j == n_col - 1)
    def _():
        ssq = jnp.zeros((tm, 1), F32)
        for c in range(n_col):
            xc = x1s_ref[c]
            ssq = ssq + jnp.sum(xc * xc, axis=-1, keepdims=True)
        rs = lax.rsqrt(ssq / (n_col * tn) + EPS)
        gshape = x_ref.shape
        logits = jnp.zeros((tm, LANES), F32) + br_ref[...]
        for c in range(n_col):
            cols = slice(c * tn, (c + 1) * tn)
            hn = (x1s_ref[c] * rs * ng_ref[:, cols]).reshape(gshape)
            hn = (hn * (1.0 + sc_ref[:, :, cols]) + sh_ref[:, :, cols]).reshape(tm, tn)
            hn_ref[:, cols] = hn
            logits = logits + _dot_hp(hn, wr_ref[cols, :])
        i1, i2, w1, w2 = _route(logits)
        lane = lax.broadcasted_iota(I32, (tm, LANES), 1)
        hit1 = lane.astype(F32) == i1
        hit2 = lane.astype(F32) == i2
        onehot = jnp.where(hit1 | hit2, 1.0, 0.0).astype(BF16)
        before = _dot(tri_ref[...], onehot) + cnts_ref[0:1, :]
        r1 = jnp.sum(jnp.where(hit1, before, 0.0), axis=1, keepdims=True)
        r2 = jnp.sum(jnp.where(hit2, before, 0.0), axis=1, keepdims=True)
        ids_ref[...] = jnp.where(lane == 0, i1, jnp.where(lane == 1, i2, 0.0)).astype(I32)
        wts_ref[...] = jnp.where(lane == 0, w1, jnp.where(lane == 1, w2, 0.0))
        rank_ref[...] = jnp.where(lane == 0, r1, jnp.where(lane == 1, r2, 0.0)).astype(I32)
        total = cnts_ref[0:1, :] + jnp.sum(onehot.astype(F32), axis=0, keepdims=True)
        cnts_ref[...] = jnp.broadcast_to(total, cnts_ref.shape)
        cnt_ref[...] = jnp.broadcast_to(total, cnt_ref.shape)


def _strict_lower(n):
    import numpy as np
    t = np.arange(n)[:, None]
    s = np.arange(n)[None, :]
    return jnp.asarray((s < t).astype(np.float32), dtype=BF16)


def _out_projection(o3, yz3, x3, mod3, norm_g, ssm_norm_g, w_out, w_router, b_router, cnt0,
                    group_rows):
    bsz, t_len, d = x3.shape
    g_blk, r_blk = group_rows
    tm = g_blk * r_blk
    tn = PROJ_TN
    n_col = d // tn
    mix_a = o3.shape[-1]
    mix_b = yz3.shape[-1]
    n_tok = bsz * t_len
    tiles_per_seq = max(t_len // r_blk, 1)
    sec = d // tn

    def xmap(i, j):
        return (i // tiles_per_seq, i % tiles_per_seq, j)

    def modmap(s):
        return lambda i, j: (i // tiles_per_seq, 0, s)

    row = lambda i, j: (i, 0)
    const2 = lambda i, j: (0, 0)
    kern = functools.partial(_outproj_kernel, n_col=n_col, gw=mix_b // SSD_GROUPS)
    return pl.pallas_call(
        kern,
        grid=(n_tok // tm, n_col),
        in_specs=[
            pl.BlockSpec((tm, mix_a), row),
            pl.BlockSpec((tm, mix_b), row),
            pl.BlockSpec((g_blk, r_blk, tn), xmap),
            pl.BlockSpec((g_blk, 1, tn), lambda i, j: (i // tiles_per_seq, 0, 2 * sec + j)),
            pl.BlockSpec((g_blk, 1, d), modmap(4)),
            pl.BlockSpec((g_blk, 1, d), modmap(3)),
            pl.BlockSpec((1, d), const2),
            pl.BlockSpec((1, mix_b), const2),
            pl.BlockSpec((d, tn), lambda i, j: (0, j)),
            pl.BlockSpec((d, LANES), const2),
            pl.BlockSpec((1, LANES), const2),
            pl.BlockSpec((tm, tm), const2),
            pl.BlockSpec((SUBLANES, LANES), const2),
        ],
        out_specs=[
            pl.BlockSpec((g_blk, r_blk, tn), xmap),
            pl.BlockSpec((tm, d), row),
            pl.BlockSpec((tm, LANES), row),
            pl.BlockSpec((tm, LANES), row),
            pl.BlockSpec((tm, LANES), row),
            pl.BlockSpec((SUBLANES, LANES), const2),
        ],
        out_shape=[
            jax.ShapeDtypeStruct((bsz, t_len, d), F32),
            jax.ShapeDtypeStruct((n_tok, d), F32),
            jax.ShapeDtypeStruct((n_tok, LANES), I32),
            jax.ShapeDtypeStruct((n_tok, LANES), F32),
            jax.ShapeDtypeStruct((n_tok, LANES), I32),
            jax.ShapeDtypeStruct((SUBLANES, LANES), F32),
        ],
        scratch_shapes=[pltpu.VMEM((tm, d), BF16), pltpu.VMEM((n_col, tm, tn), F32),
                        pltpu.VMEM((SUBLANES, LANES), F32)],
        compiler_params=_cparams(("arbitrary", "arbitrary")),
        name="out_proj_router",
    )(o3.reshape(n_tok, mix_a), yz3.reshape(n_tok, mix_b), x3, mod3, mod3, mod3,
      norm_g.reshape(1, d), ssm_norm_g.reshape(1, mix_b), w_out, w_router, b_router,
      _strict_lower(tm), cnt0)


def _positions_kernel(ids_ref, rank_ref, cnt_ref, pos_ref, te_ref, nt_ref, *, block):
    lane8 = lax.broadcasted_iota(I32, (SUBLANES, LANES), 1)
    cnt = cnt_ref[...]
    tiles = jnp.floor((cnt + (MOE_TM - 1)) * (1.0 / MOE_TM))
    tiles = jnp.where(lane8 < N_EXPERTS, tiles, 0.0)
    incl = tiles
    shift = 1
    while shift < N_EXPERTS:
        incl = incl + jnp.where(lane8 >= shift, pltpu.roll(incl, shift, axis=1), 0.0)
        shift *= 2
    offs = ((incl - tiles) * MOE_TM)[0:1, :]
    nt_ref[...] = jnp.sum(tiles, axis=1, keepdims=True).astype(I32) + jnp.zeros(nt_ref.shape, I32)

    incl_col = jnp.broadcast_to(incl[0:1, :], (LANES, LANES)).T
    e_row = lax.broadcasted_iota(I32, (LANES, LANES), 0)
    i_lane = lax.broadcasted_iota(I32, (LANES, LANES), 1).astype(F32)
    done = jnp.where((incl_col <= i_lane) & (e_row < N_EXPERTS), 1.0, 0.0)
    te = jnp.minimum(jnp.sum(done, axis=0, keepdims=True), N_EXPERTS - 1.0)
    te_ref[...] = te.astype(I32) + jnp.zeros(te_ref.shape, I32)

    lane = lax.broadcasted_iota(I32, (block, LANES), 1)

    def body(bi, c):
        r0 = pl.multiple_of(bi * block, block)
        ids = ids_ref[pl.ds(r0, block), :]
        rank = rank_ref[pl.ds(r0, block), :]
        out = jnp.zeros((block, LANES), I32)
        for k in range(TOP_K):
            off = jnp.sum(jnp.where(lane == ids[:, k:k + 1], offs, 0.0), axis=1, keepdims=True)
            out = jnp.where(lane == k, off.astype(I32) + rank[:, k:k + 1], out)
        pos_ref[pl.ds(r0, block), :] = out
        return c

    lax.fori_loop(0, ids_ref.shape[0] // block, body, 0)


def _positions(ids, rank, cnt):
    n_tok = ids.shape[0]
    block = 512
    kern = functools.partial(_positions_kernel, block=block)
    full = lambda s: pl.BlockSpec(s, lambda: tuple(0 for _ in s))
    return pl.pallas_call(
        kern,
        in_specs=[full((n_tok, LANES)), full((n_tok, LANES)), full((SUBLANES, LANES))],
        out_specs=[full((n_tok, LANES)), full((SUBLANES, LANES)), full((SUBLANES, LANES))],
        out_shape=[jax.ShapeDtypeStruct((n_tok, LANES), I32),
                   jax.ShapeDtypeStruct((SUBLANES, LANES), I32),
                   jax.ShapeDtypeStruct((SUBLANES, LANES), I32)],
        compiler_params=pltpu.CompilerParams(vmem_limit_bytes=VMEM_LIMIT),
        name="route_positions",
    )(ids, rank, cnt)


def _row_copy(src_ref, src_row, dst_ref, dst_row, sem):
    return pltpu.make_async_copy(src_ref.at[pl.ds(src_row, 1), :], dst_ref.at[pl.ds(dst_row, 1), :],
                                 sem)


def _dispatch_kernel(pos_ref, hn_ref, xg_in_ref, xg_ref, sem):
    del xg_in_ref
    rows = hn_ref.shape[0]
    base = pl.program_id(0) * rows

    def issue(r, c):
        for k in range(TOP_K):
            _row_copy(hn_ref, r, xg_ref, pos_ref[(base + r) * TOP_K + k], sem).start(priority=k)
        return c

    def drain(r, c):
        for k in range(TOP_K):
            _row_copy(hn_ref, r, xg_ref, 0, sem).wait()
        return c

    lax.fori_loop(0, rows, issue, 0, unroll=8)
    lax.fori_loop(0, rows, drain, 0, unroll=8)


def _dispatch(pos_flat, hn, xg):
    n_tok, d = hn.shape
    grid_spec = pltpu.PrefetchScalarGridSpec(
        num_scalar_prefetch=1,
        grid=(n_tok // ROW_DMA_TILE,),
        in_specs=[pl.BlockSpec((ROW_DMA_TILE, d), lambda i, pos: (i, 0)),
                  pl.BlockSpec(memory_space=pl.ANY)],
        out_specs=pl.BlockSpec(memory_space=pl.ANY),
        scratch_shapes=[pltpu.SemaphoreType.DMA(())],
    )
    return pl.pallas_call(
        _dispatch_kernel,
        grid_spec=grid_spec,
        out_shape=jax.ShapeDtypeStruct(xg.shape, xg.dtype),
        input_output_aliases={2: 0},
        compiler_params=_cparams(("arbitrary",)),
        name="moe_dispatch",
    )(pos_flat, hn, xg)


def _moe_up_kernel(te_ref, nt_ref, x_ref, w1_ref, w3_ref, act_ref):
    t = pl.program_id(1)

    @pl.when(t < nt_ref[0])
    def _():
        xb = x_ref[...].astype(BF16)
        a = _dot(xb, w1_ref[0].astype(BF16))
        b = _dot(xb, w3_ref[0].astype(BF16))
        act_ref[...] = (_silu(a) * b).astype(BF16)

    @pl.when(t >= nt_ref[0])
    def _():
        act_ref[...] = jnp.zeros(act_ref.shape, act_ref.dtype)


def _moe_down_kernel(te_ref, nt_ref, act_ref, w2_ref, out_ref):
    t = pl.program_id(1)

    @pl.when(t < nt_ref[0])
    def _():
        out_ref[...] = _dot(act_ref[...], w2_ref[0].astype(BF16))

    @pl.when(t >= nt_ref[0])
    def _():
        out_ref[...] = jnp.zeros(out_ref.shape, out_ref.dtype)


def _moe_experts(te, nt, xg, w1, w3, w2):
    n_rows, d = xg.shape
    d_exp = w1.shape[-1]
    n_tiles = n_rows // MOE_TM
    halves = MOE_COL_PASSES
    hn = d_exp // halves
    hd = d // halves

    def tile(h, t, te_ref, nt_ref):
        return jnp.minimum(t, nt_ref[0] - 1)

    act = pl.pallas_call(
        _moe_up_kernel,
        grid_spec=pltpu.PrefetchScalarGridSpec(
            num_scalar_prefetch=2,
            grid=(halves, n_tiles),
            in_specs=[
                pl.BlockSpec((MOE_TM, d), lambda h, t, te_r, nt_r: (tile(h, t, te_r, nt_r), 0)),
                pl.BlockSpec((1, d, hn), lambda h, t, te_r, nt_r: (te_r[tile(h, t, te_r, nt_r)], 0, h)),
                pl.BlockSpec((1, d, hn), lambda h, t, te_r, nt_r: (te_r[tile(h, t, te_r, nt_r)], 0, h)),
            ],
            out_specs=pl.BlockSpec((MOE_TM, hn), lambda h, t, te_r, nt_r: (t, h)),
        ),
        out_shape=jax.ShapeDtypeStruct((n_rows, d_exp), BF16),
        compiler_params=_cparams(("arbitrary", "arbitrary")),
        name="moe_up",
    )(te, nt, xg, w1, w3)

    return pl.pallas_call(
        _moe_down_kernel,
        grid_spec=pltpu.PrefetchScalarGridSpec(
            num_scalar_prefetch=2,
            grid=(halves, n_tiles),
            in_specs=[
                pl.BlockSpec((MOE_TM, d_exp), lambda h, t, te_r, nt_r: (tile(h, t, te_r, nt_r), 0)),
                pl.BlockSpec((1, d_exp, hd), lambda h, t, te_r, nt_r: (te_r[tile(h, t, te_r, nt_r)], 0, h)),
            ],
            out_specs=pl.BlockSpec((MOE_TM, hd), lambda h, t, te_r, nt_r: (t, h)),
        ),
        out_shape=jax.ShapeDtypeStruct((n_rows, d), F32),
        compiler_params=_cparams(("arbitrary", "arbitrary")),
        name="moe_down",
    )(te, nt, act, w2)


def _combine_kernel(pos_ref, x1_ref, g2_ref, wts_ref, fg_ref, eo_ref, y_ref, buf_ref, sem):
    rows = wts_ref.shape[0]
    base = pl.program_id(0) * rows

    def issue(r, c):
        for k in range(TOP_K):
            _row_copy(eo_ref, pos_ref[(base + r) * TOP_K + k], buf_ref.at[k], r, sem).start(
                priority=k)
        return c

    def drain(r, c):
        for k in range(TOP_K):
            _row_copy(eo_ref, 0, buf_ref.at[k], r, sem).wait()
        return c

    lax.fori_loop(0, rows, issue, 0, unroll=8)
    lax.fori_loop(0, rows, drain, 0, unroll=8)
    w = wts_ref[...]
    moe = w[:, 0:1] * buf_ref[0] + w[:, 1:2] * buf_ref[1]
    x2 = x1_ref[...] + g2_ref[...] * moe.reshape(x1_ref.shape)
    ms = jnp.mean(x2 * x2, axis=-1, keepdims=True)
    y_ref[...] = x2 * lax.rsqrt(ms + EPS) * fg_ref[...]


def _combine(pos_flat, x1, mod3, wts, final_g, expert_out, group_rows):
    bsz, t_len, d = x1.shape
    g_blk, r_blk = group_rows
    tm = g_blk * r_blk
    tiles_per_seq = max(t_len // r_blk, 1)

    def xmap(i, pos):
        return (i // tiles_per_seq, i % tiles_per_seq, 0)

    return pl.pallas_call(
        _combine_kernel,
        grid_spec=pltpu.PrefetchScalarGridSpec(
            num_scalar_prefetch=1,
            grid=((bsz * t_len) // tm,),
            in_specs=[
                pl.BlockSpec((g_blk, r_blk, d), xmap),
                pl.BlockSpec((g_blk, 1, d), lambda i, pos: (i // tiles_per_seq, 0, 5)),
                pl.BlockSpec((tm, LANES), lambda i, pos: (i, 0)),
                pl.BlockSpec((1, d), lambda i, pos: (0, 0)),
                pl.BlockSpec(memory_space=pl.ANY),
            ],
            out_specs=pl.BlockSpec((g_blk, r_blk, d), xmap),
            scratch_shapes=[pltpu.VMEM((TOP_K, tm, d), F32), pltpu.SemaphoreType.DMA(())],
        ),
        out_shape=jax.ShapeDtypeStruct((bsz, t_len, d), F32),
        compiler_params=_cparams(("arbitrary",)),
        name="moe_combine",
    )(pos_flat, x1, mod3, wts, final_g.reshape(1, d), expert_out)


def kernel(x_prompt, x_sample, state_hgrn, state_ssm, state_conv, c_prompt, c_sample, ada_w, ada_b,
           norm1_g, norm2_g, w_in, hgrn_lb, hgrn_onorm_g, conv_w, conv_b, dt_bias, a_log, d_skip,
           ssm_norm_g, w_out, w_grp, b_grp, w_rt, b_rt, w1, w3, w2, final_g):
    depth = w_in.shape[0]
    assert depth == 1, "single-layer trunk"
    layer = 0
    bp, t_p, d = x_prompt.shape
    bs, t_s, _ = x_sample.shape
    mix_a = hgrn_onorm_g.shape[1]
    mix_b = ssm_norm_g.shape[1]
    conv_dim = conv_w.shape[2]
    n_main = 4 * mix_a + mix_b + conv_dim
    n_ssd_heads = dt_bias.shape[1]
    n_pairs = mix_b // LANES
    xbc0 = 4 * mix_a + mix_b
    tail = CONV_WIDTH - 1
    assert t_s <= SUBLANES and t_s >= tail and t_p % PROMPT_CHUNK == 0

    xs_pad = jnp.pad(x_sample, ((0, 0), (0, SUBLANES - t_s), (0, 0)))
    n_c = bp + bs
    c_rows = -(-n_c // SUBLANES) * SUBLANES
    c_all = jnp.pad(jnp.concatenate([c_prompt, c_sample], axis=0), ((0, c_rows - n_c), (0, 0)))

    mod = _modulation(c_all, ada_w[layer], ada_b[layer])
    mod_p = mod[:bp].reshape(bp, 1, -1)
    mod_s = mod[bp:n_c].reshape(bs, 1, -1)
    w_dt = jnp.pad(w_in[layer][:, n_main:], ((0, 0), (0, LANES - n_ssd_heads)))
    w_router = jnp.pad(jnp.concatenate([w_rt[layer], w_grp[layer]], axis=1),
                       ((0, 0), (0, LANES - N_EXPERTS - N_EXPERT_GROUPS)))
    b_router = jnp.pad(jnp.concatenate([b_rt[layer], b_grp[layer]]),
                       (0, LANES - N_EXPERTS - N_EXPERT_GROUPS)).reshape(1, LANES)

    groups = (
        (x_prompt, mod_p, None, None, None, t_p, PROMPT_CHUNK, PROMPT_CHUNK, MIXER_ROWS,
         (LANES // PROMPT_CHUNK, 4), LANES // PROMPT_CHUNK, (1, PROJ_TM), (1, OUT_TM)),
        (xs_pad, mod_s, state_hgrn[layer], state_ssm[layer].reshape(bs, n_pairs, LANES, LANES),
         state_conv[layer], t_s, SUBLANES, t_s, SUBLANES,
         (LANES // SUBLANES, 2), LANES // SUBLANES, (PROJ_TM // SUBLANES, SUBLANES),
         (OUT_TM // SUBLANES, SUBLANES)),
    )

    cnt = jnp.zeros((SUBLANES, LANES), F32)
    per_group = []
    for x3, mod3, s0, h0, c0, t_real, ch, valid, tt, (hg_gb, hg_hb), ssd_gb, proj_gr, out_gr in groups:
        bsz, t_len, _ = x3.shape
        proj, dt = _in_projection(x3, mod3, norm1_g[layer], w_in[layer], w_dt, n_main, proj_gr)
        proj3 = proj.reshape(bsz, t_len, n_main)
        dt3 = dt.reshape(bsz, t_len, LANES)
        o3, s_new = _hgrn_heads(proj3, hgrn_lb, hgrn_onorm_g[layer], s0, ch=ch, valid=valid,
                                gb=hg_gb, hb=hg_hb, tt=tt, layer=layer)
        yz3, h_new = _ssd_pairs(proj3, dt3, conv_w[layer], conv_b[layer], dt_bias[layer],
                                a_log[layer], d_skip[layer], h0, c0, mix_a=mix_a, mix_b=mix_b,
                                ch=ch, valid=valid, gb=ssd_gb, tt=tt)
        conv_new = proj3[:, t_real - tail:t_real, xbc0:xbc0 + conv_dim]
        x1, hn, ids, wts, rank, cnt = _out_projection(
            o3, yz3, x3, mod3, norm2_g[layer], ssm_norm_g[layer], w_out[layer], w_router, b_router,
            cnt, out_gr)
        per_group.append(dict(x1=x1, hn=hn, ids=ids, wts=wts, rank=rank, mod3=mod3, out_gr=out_gr,
                              s_new=s_new, h_new=h_new, conv_new=conv_new))

    ids_all = jnp.concatenate([g["ids"] for g in per_group], axis=0)
    rank_all = jnp.concatenate([g["rank"] for g in per_group], axis=0)
    n_tok = ids_all.shape[0]
    pos, te, nt = _positions(ids_all, rank_all, cnt)
    pos_flat = pos[:, :TOP_K].reshape(-1)
    n_tiles = -(-(n_tok * TOP_K + N_EXPERTS * (MOE_TM - 1)) // MOE_TM)
    assert n_tiles <= LANES
    te_vec = te[0, :n_tiles]
    nt_vec = nt[0, :1]

    xg = jnp.zeros((n_tiles * MOE_TM, d), F32)
    row0 = 0
    for g in per_group:
        rows = g["hn"].shape[0]
        g["pos"] = pos_flat[row0 * TOP_K:(row0 + rows) * TOP_K]
        xg = _dispatch(g["pos"], g["hn"], xg)
        row0 += rows
    expert_out = _moe_experts(te_vec, nt_vec, xg, w1[layer], w3[layer], w2[layer])
    ys = [_combine(g["pos"], g["x1"], g["mod3"], g["wts"], final_g, expert_out, g["out_gr"])
          for g in per_group]

    gp, gs = per_group
    return (
        ys[0],
        ys[1][:, :t_s],
        gp["s_new"][None],
        gp["h_new"].reshape(1, bp, n_ssd_heads, SSD_HEAD_DIM, SSD_STATE),
        gp["conv_new"][None],
        gs["s_new"][None],
        gs["h_new"].reshape(1, bs, n_ssd_heads, SSD_HEAD_DIM, SSD_STATE),
        gs["conv_new"][None],
    )
```

```python
import functools

import jax
import jax.numpy as jnp
from jax import lax
from jax.experimental import pallas as pl
from jax.experimental.pallas import tpu as pltpu

F32 = jnp.float32
BF16 = jnp.bfloat16
I32 = jnp.int32
EPS = 1e-6
LOG2E = 1.4426950408889634

LANES = 128
SUBLANES = 8
VMEM_LIMIT = 56 * 1024 * 1024

HGRN_HEAD_DIM = 128
SSD_HEAD_DIM = 64
SSD_GROUPS = 2
SSD_STATE = 128
CONV_WIDTH = 4
N_EXPERT_GROUPS = 4
EXPERTS_PER_GROUP = 8
N_EXPERTS = N_EXPERT_GROUPS * EXPERTS_PER_GROUP
TOP_K = 2

PROJ_TM = 1024
PROJ_TN = 512
MOE_TM = 256
PROMPT_CHUNK = 64
MIXER_ROWS = 512
ROW_DMA_TILE = 256


def _cparams(sem):
    return pltpu.CompilerParams(dimension_semantics=sem, vmem_limit_bytes=VMEM_LIMIT)


def _dot(a, b):
    return jnp.dot(a, b, preferred_element_type=F32)


def _dot_nt(a, b):
    return lax.dot_general(a, b, (((1,), (1,)), ((), ())), preferred_element_type=F32)


def _split2(x):
    hi = x.astype(BF16)
    lo = (x - hi.astype(F32)).astype(BF16)
    return hi, lo


def _split3(x):
    hi = x.astype(BF16)
    r = x - hi.astype(F32)
    mid = r.astype(BF16)
    lo = (r - mid.astype(F32)).astype(BF16)
    return hi, mid, lo


def _dot_exact_lhs(m_bf16, x):
    hi, mid, lo = _split3(x)
    return _dot(m_bf16, hi) + _dot(m_bf16, mid) + _dot(m_bf16, lo)


def _dot_exact_rhs(x, m_bf16):
    hi, mid, lo = _split3(x)
    return _dot(hi, m_bf16) + _dot(mid, m_bf16) + _dot(lo, m_bf16)


def _dot_hp(a, w):
    ah, al = _split2(a)
    wh, wl = _split2(w)
    return _dot(ah, wh) + _dot(al, wh) + _dot(ah, wl)


def _silu(x):
    return x * jax.nn.sigmoid(x)


def _mod_kernel(c_ref, w_ref, b_ref, o_ref):
    a = _silu(c_ref[...])
    o_ref[...] = _dot_hp(a, w_ref[...]) + b_ref[...]


def _modulation(c_all, ada_w, ada_b):
    rows, d = c_all.shape
    n_out = ada_w.shape[1]
    tn = 1024
    return pl.pallas_call(
        _mod_kernel,
        grid=(n_out // tn,),
        in_specs=[
            pl.BlockSpec((rows, d), lambda j: (0, 0)),
            pl.BlockSpec((d, tn), lambda j: (0, j)),
            pl.BlockSpec((1, tn), lambda j: (0, j)),
        ],
        out_specs=pl.BlockSpec((rows, tn), lambda j: (0, j)),
        out_shape=jax.ShapeDtypeStruct((rows, n_out), F32),
        compiler_params=_cparams(("arbitrary",)),
        name="adaln_mod",
    )(c_all, ada_w, ada_b.reshape(1, n_out))


def _proj_kernel(x_ref, sc_ref, sh_ref, g_ref, w_ref, wdt_ref, o_ref, dt_ref, hn_ref):
    j = pl.program_id(1)

    @pl.when(j == 0)
    def _():
        x = x_ref[...]
        ms = jnp.mean(x * x, axis=-1, keepdims=True)
        hn = (x * lax.rsqrt(ms + EPS)) * g_ref[...] * (1.0 + sc_ref[...]) + sh_ref[...]
        hn = hn.reshape(hn_ref.shape)
        hn_ref[...] = hn.astype(BF16)
        ah, al = _split2(hn)
        wh, wl = _split2(wdt_ref[...])
        dt_ref[...] = _dot_nt(ah, wh) + _dot_nt(al, wh) + _dot_nt(ah, wl)

    o_ref[...] = _dot_nt(hn_ref[...], w_ref[...].astype(BF16))


def _in_projection(x3, mod3, norm_g, w_t, w_dt_t, n_main, group_rows):
    bsz, t_len, d = x3.shape
    g_blk, r_blk = group_rows
    tm = g_blk * r_blk
    tiles_per_seq = max(t_len // r_blk, 1)
    n_row_tiles = (bsz * t_len) // tm

    def x_map(i, j):
        return (i // tiles_per_seq, i % tiles_per_seq, 0)

    def mod_map(sec):
        return lambda i, j: (i // tiles_per_seq, 0, sec)

    return pl.pallas_call(
        _proj_kernel,
        grid=(n_row_tiles, n_main // PROJ_TN),
        in_specs=[
            pl.BlockSpec((g_blk, r_blk, d), x_map),
            pl.BlockSpec((g_blk, 1, d), mod_map(1)),
            pl.BlockSpec((g_blk, 1, d), mod_map(0)),
            pl.BlockSpec((1, d), lambda i, j: (0, 0)),
            pl.BlockSpec((PROJ_TN, d), lambda i, j: (j, 0)),
            pl.BlockSpec((LANES, d), lambda i, j: (0, 0)),
        ],
        out_specs=[
            pl.BlockSpec((tm, PROJ_TN), lambda i, j: (i, j)),
            pl.BlockSpec((tm, LANES), lambda i, j: (i, 0)),
        ],
        out_shape=[
            jax.ShapeDtypeStruct((bsz * t_len, n_main), F32),
            jax.ShapeDtypeStruct((bsz * t_len, LANES), F32),
        ],
        scratch_shapes=[pltpu.VMEM((tm, d), BF16)],
        compiler_params=_cparams(("arbitrary", "arbitrary")),
        name="in_proj",
    )(x3, mod3, mod3, norm_g.reshape(1, d), w_t, w_dt_t)


def _level_sizes(ch):
    sizes = []
    sz = SUBLANES
    while 2 * sz <= ch:
        sizes.append(sz)
        sz *= 2
    return sizes


def _level_masks(ch, n_rows):
    import numpy as np
    t = np.arange(n_rows)[:, None]
    s = np.arange(n_rows)[None, :]
    out = []
    for sz in _level_sizes(ch):
        m = (t // (2 * sz) == s // (2 * sz)) & (t % (2 * sz) >= sz) & (s % (2 * sz) < sz)
        out.append(m.astype(np.float32))
    if not out:
        out.append(np.zeros((n_rows, n_rows), np.float32))
    return jnp.asarray(np.stack(out))


def _hgrn_kernel(*refs, ch, valid, nc, gb, hb, has_init, layer):
    if has_init:
        (q_ref, f_ref, i_ref, g_ref, lb_ref, on_ref, tri_ref, masks_ref, s0_ref,
         o_ref, sout_ref, st_ref, b_scr, k_scr) = refs
    else:
        (q_ref, f_ref, i_ref, g_ref, lb_ref, on_ref, tri_ref, masks_ref,
         o_ref, sout_ref, st_ref, b_scr, k_scr) = refs
        s0_ref = None

    n_lb = lb_ref.shape[0]
    lb_rows = [lb_ref[i:i + 1, :] for i in range(n_lb)]
    lb_max = functools.reduce(jnp.maximum, lb_rows)
    lb_exp = [jnp.exp(r - lb_max) for r in lb_rows]
    lb = sum(lb_exp[:layer + 1]) / sum(lb_exp)
    onorm = on_ref[...]
    tri = tri_ref[...]
    sizes = _level_sizes(ch)
    lane8 = lax.broadcasted_iota(I32, (SUBLANES, LANES), 1)
    row8 = lax.broadcasted_iota(I32, (SUBLANES, LANES), 0)
    tb = pl.program_id(2)

    n_rows = gb * ch
    row_id = lax.broadcasted_iota(I32, (n_rows, LANES), 0)

    def stack(fn):
        parts = [fn(g) for g in range(gb)]
        return parts[0] if gb == 1 else jnp.concatenate(parts, axis=0)

    def chunk(hh, ci):
        cols = slice(hh * LANES, (hh + 1) * LANES)
        t0 = pl.multiple_of(ci * ch, ch)
        rows = pl.ds(t0, ch)
        lb_h = lb[:, cols]
        q = _silu(stack(lambda g: q_ref[g, rows, cols]))
        fg = lb_h + (1.0 - lb_h) * jax.nn.sigmoid(stack(lambda g: f_ref[g, rows, cols]))
        k = 1.0 - fg
        lf = jnp.log(fg)
        v = stack(lambda g: i_ref[g, rows, cols])
        if valid < ch:
            live = (row_id & (ch - 1)) < valid
            lf = jnp.where(live, lf, 0.0)
            k = jnp.where(live, k, 0.0)
        b2 = _dot_exact_lhs(tri, lf) * LOG2E
        b_scr[hh] = b2
        k_scr[hh] = k
        b_end_rows = [b_scr[hh, pl.ds(g * ch + ch - 1, 1), :] for g in range(gb)]
        b_end = stack(lambda g: jnp.broadcast_to(b_end_rows[g], (ch, LANES)))

        qe = (q * jnp.exp2(b2)).astype(BF16)
        o = stack(lambda g: _dot_nt(qe[g * ch:(g + 1) * ch], st_ref[g * hb + hh].astype(BF16)))

        scores = jnp.zeros((n_rows, n_rows), F32)
        for lvl, sz in enumerate(sizes):
            pieces = []
            for m in range(n_rows // (2 * sz)):
                r = b_scr[hh, pl.ds(2 * sz * m + sz - 1, 1), :]
                pieces.append(jnp.broadcast_to(r, (2 * sz, LANES)))
            r_all = pieces[0] if len(pieces) == 1 else jnp.concatenate(pieces, axis=0)
            e = jnp.exp2(-jnp.abs(b2 - r_all))
            s_l = _dot_nt((q * e).astype(BF16), (k * e).astype(BF16))
            scores = scores + masks_ref[lvl] * s_l
        blocks = []
        for jb in range(n_rows // SUBLANES):
            bb = b2[SUBLANES * jb:SUBLANES * (jb + 1)]
            qb = q[SUBLANES * jb:SUBLANES * (jb + 1)]
            acc = jnp.zeros((SUBLANES, LANES), F32)
            for sl in range(SUBLANES):
                s = SUBLANES * jb + sl
                bs = b_scr[hh, pl.ds(s, 1), :]
                ks = k_scr[hh, pl.ds(s, 1), :]
                val = (qb * ks) * jnp.exp2(bb - bs)
                red = jnp.sum(val, axis=1, keepdims=True)
                acc = jnp.where(lane8 == s, red, acc)
            blocks.append(jnp.where(lane8 - SUBLANES * jb <= row8, acc, 0.0))
        scores = scores + jnp.concatenate(blocks, axis=0)
        o = o + _dot(scores.astype(BF16), v.astype(BF16))

        dk = k * jnp.exp2(b_end - b2)
        v_t = v.T.astype(BF16)
        for g in range(gb):
            own = (row_id >= g * ch) & (row_id < (g + 1) * ch)
            c = g * hb + hh
            st_ref[c] = (st_ref[c] * jnp.exp2(b_end_rows[g])
                         + _dot(v_t, jnp.where(own, dk, 0.0).astype(BF16)))

        on = o * lax.rsqrt(jnp.mean(o * o, axis=-1, keepdims=True) + EPS) * onorm[:, cols]
        out = on * _silu(stack(lambda g: g_ref[g, rows, cols]))
        for g in range(gb):
            o_ref[g, rows, cols] = out[g * ch:(g + 1) * ch]

    chains = [(g * hb + hh, g, hh) for g in range(gb) for hh in range(hb)]

    @pl.when(tb == 0)
    def _():
        for c, g, hh in chains:
            if has_init:
                st_ref[c] = s0_ref[g, hh].T
            else:
                st_ref[c] = jnp.zeros((LANES, LANES), F32)

    def step(ci, carry):
        for hh in range(hb):
            chunk(hh, ci)
        return carry

    if nc == 1:
        step(0, 0)
    else:
        lax.fori_loop(0, nc, step, 0)

    @pl.when(tb == pl.num_programs(2) - 1)
    def _():
        for c, g, hh in chains:
            sout_ref[g, hh] = st_ref[c].T


def _hgrn_heads(proj3, hgrn_lb, onorm_g, s0, *, ch, valid, gb, hb, tt, layer):
    bsz, t_len, _ = proj3.shape
    n_heads = onorm_g.shape[0] // HGRN_HEAD_DIM
    n_hg = n_heads // hb
    nc = tt // ch
    n_rows = gb * ch
    assert n_rows == LANES
    has_init = s0 is not None
    w = hb * LANES

    def col(sec):
        return lambda b, h, t: (b, t, sec * n_hg + h)

    in_specs = [pl.BlockSpec((gb, tt, w), col(s)) for s in range(4)]
    in_specs += [
        pl.BlockSpec((hgrn_lb.shape[0], w), lambda b, h, t: (0, h)),
        pl.BlockSpec((1, w), lambda b, h, t: (0, h)),
        pl.BlockSpec((n_rows, n_rows), lambda b, h, t: (0, 0)),
        pl.BlockSpec((max(len(_level_sizes(ch)), 1), n_rows, n_rows), lambda b, h, t: (0, 0, 0)),
    ]
    args = [proj3, proj3, proj3, proj3, hgrn_lb, onorm_g.reshape(1, -1),
            jnp.asarray(_block_tri(gb, ch), dtype=BF16), _level_masks(ch, n_rows)]
    if has_init:
        in_specs.append(pl.BlockSpec((gb, hb, LANES, LANES), lambda b, h, t: (b, h, 0, 0)))
        args.append(s0)
    kern = functools.partial(_hgrn_kernel, ch=ch, valid=valid, nc=nc, gb=gb, hb=hb,
                             has_init=has_init, layer=layer)
    n_chain = gb * hb
    return pl.pallas_call(
        kern,
        grid=(bsz // gb, n_hg, t_len // tt),
        in_specs=in_specs,
        out_specs=[
            pl.BlockSpec((gb, tt, w), lambda b, h, t: (b, t, h)),
            pl.BlockSpec((gb, hb, LANES, LANES), lambda b, h, t: (b, h, 0, 0)),
        ],
        out_shape=[
            jax.ShapeDtypeStruct((bsz, t_len, n_heads * HGRN_HEAD_DIM), F32),
            jax.ShapeDtypeStruct((bsz, n_heads, LANES, LANES), F32),
        ],
        scratch_shapes=[pltpu.VMEM((n_chain, LANES, LANES), F32),
                        pltpu.VMEM((hb, n_rows, LANES), F32),
                        pltpu.VMEM((hb, n_rows, LANES), F32)],
        compiler_params=_cparams(("arbitrary", "arbitrary", "arbitrary")),
        name="hgrn2_heads",
    )(*args)


def _softplus(x):
    return jnp.maximum(x, 0.0) + jnp.log1p(jnp.exp(-jnp.abs(x)))


def _ssd_kernel(*refs, ch, valid, nc, gb, pb, has_init):
    if has_init:
        (z_ref, x_ref, b_ref, c_ref, dt_ref, cwx_ref, cwb_ref, cwc_ref, cbx_ref, cbb_ref, cbc_ref,
         xj_ref, bias_ref, alog_ref, d_ref, tri_ref, tril_ref, h0_ref, cx0_ref, cb0_ref, cc0_ref,
         y_ref, hout_ref, ht_ref, wx_ref, wb_ref, wc_ref) = refs
    else:
        (z_ref, x_ref, b_ref, c_ref, dt_ref, cwx_ref, cwb_ref, cwc_ref, cbx_ref, cbb_ref, cbc_ref,
         xj_ref, bias_ref, alog_ref, d_ref, tri_ref, tril_ref,
         y_ref, hout_ref, ht_ref, wx_ref, wb_ref, wc_ref) = refs
        h0_ref = cx0_ref = cb0_ref = cc0_ref = None

    tri = tri_ref[...]
    tril = tril_ref[...]
    lane = lax.broadcasted_iota(I32, (LANES, LANES), 1)
    first_head = lane < SSD_HEAD_DIM
    tail = CONV_WIDTH - 1
    tb = pl.program_id(2)

    def conv(raw_ref, win_ref, cw_ref, cb_ref, g, t0):
        win_ref[g, pl.ds(SUBLANES, ch), :] = raw_ref[g, pl.ds(t0, ch), :]
        u = cb_ref[...]
        for j in range(CONV_WIDTH):
            u = u + win_ref[g, pl.ds(SUBLANES - tail + j, ch), :] * cw_ref[j:j + 1, :]
        win_ref[g, pl.ds(0, SUBLANES), :] = win_ref[g, pl.ds(ch, SUBLANES), :]
        return _silu(u)

    n_rows = gb * ch
    w = pb * LANES
    xj = xj_ref[0]
    bias = bias_ref[0]
    a_neg = -jnp.exp(alog_ref[0])
    d_skip = d_ref[0]
    row_id = lax.broadcasted_iota(I32, (n_rows, w), 0)

    def stack(fn):
        parts = [fn(g) for g in range(gb)]
        return parts[0] if gb == 1 else jnp.concatenate(parts, axis=0)

    def step(ci, carry):
        t0 = pl.multiple_of(ci * ch, ch)
        rows = pl.ds(t0, ch)
        xs = stack(lambda g: conv(x_ref, wx_ref, cwx_ref, cbx_ref, g, t0))
        bm = stack(lambda g: conv(b_ref, wb_ref, cwb_ref, cbb_ref, g, t0))
        cm = stack(lambda g: conv(c_ref, wc_ref, cwc_ref, cbc_ref, g, t0))
        dt = _softplus(_dot_exact_rhs(stack(lambda g: dt_ref[g, rows, :]), xj) + bias)
        if valid < ch:
            dt = jnp.where((row_id & (ch - 1)) < valid, dt, 0.0)
        la_cs = _dot_exact_lhs(tri, dt * a_neg)
        xdt = xs * dt
        a_end_rows = [la_cs[g * ch + ch - 1:g * ch + ch, :] for g in range(gb)]
        a_end = stack(lambda g: jnp.broadcast_to(a_end_rows[g], (ch, w)))

        cb = _dot_nt(cm.astype(BF16), bm.astype(BF16))
        acs_t = la_cs.T
        y_parts = []
        for p in range(pb):
            cols = slice(p * LANES, (p + 1) * LANES)
            scores = []
            for hl in range(2):
                at = p * LANES + hl * SSD_HEAD_DIM
                lmat = jnp.exp(jnp.minimum(la_cs[:, at:at + 1] - acs_t[at:at + 1, :], 0.0)) * tril
                scores.append((cb * lmat).astype(BF16))
            xp = xdt[:, cols]
            rhs = jnp.concatenate([jnp.where(first_head, xp, 0.0),
                                   jnp.where(first_head, 0.0, xp)], axis=0).astype(BF16)
            y_parts.append(_dot(jnp.concatenate(scores, axis=1), rhs))
        y = y_parts[0] if pb == 1 else jnp.concatenate(y_parts, axis=1)

        y_off = stack(lambda g: _dot(cm[g * ch:(g + 1) * ch].astype(BF16), ht_ref[g].astype(BF16)))
        y = y + y_off * jnp.exp(la_cs)
        b_t = bm.T.astype(BF16)
        upd = xdt * jnp.exp(a_end - la_cs)
        for g in range(gb):
            own = (row_id >= g * ch) & (row_id < (g + 1) * ch)
            ht_ref[g] = (ht_ref[g] * jnp.exp(a_end_rows[g])
                         + _dot(b_t, jnp.where(own, upd, 0.0).astype(BF16)))

        y = (y + d_skip * xs) * _silu(stack(lambda g: z_ref[g, rows, :]))
        for g in range(gb):
            y_ref[g, rows, :] = y[g * ch:(g + 1) * ch]
        return carry

    @pl.when(tb == 0)
    def _():
        for g in range(gb):
            for win_ref, c0_ref in ((wx_ref, cx0_ref), (wb_ref, cb0_ref), (wc_ref, cc0_ref)):
                win_ref[g, pl.ds(0, SUBLANES), :] = jnp.zeros((SUBLANES, win_ref.shape[2]), F32)
                if has_init:
                    win_ref[g, pl.ds(SUBLANES - tail, tail), :] = c0_ref[g]
            for p in range(pb):
                cols = slice(p * LANES, (p + 1) * LANES)
                if has_init:
                    ht_ref[g, :, cols] = h0_ref[g, p].T
                else:
                    ht_ref[g, :, cols] = jnp.zeros((LANES, LANES), F32)

    if nc == 1:
        step(0, 0)
    else:
        lax.fori_loop(0, nc, step, 0)

    @pl.when(tb == pl.num_programs(2) - 1)
    def _():
        for g in range(gb):
            for p in range(pb):
                hout_ref[g, p] = ht_ref[g, :, p * LANES:(p + 1) * LANES].T


def _group_rows(v):
    return jnp.repeat(v.astype(F32), SSD_HEAD_DIM).reshape(SSD_GROUPS, 1, -1)


def _head_select(n_heads):
    import numpy as np
    w = n_heads // SSD_GROUPS * SSD_HEAD_DIM
    h = np.arange(LANES)[None, :, None]
    lane = np.arange(w)[None, None, :]
    g = np.arange(SSD_GROUPS)[:, None, None]
    return jnp.asarray((h == g * (n_heads // SSD_GROUPS) + lane // SSD_HEAD_DIM).astype(np.float32),
                       dtype=BF16)


def _block_tri(gb, ch):
    import numpy as np
    t = np.arange(gb * ch)[:, None]
    s = np.arange(gb * ch)[None, :]
    return ((t // ch == s // ch) & (s <= t)).astype(np.float32)


def _ssd_pairs(proj3, dt3, conv_w, conv_b, dt_bias, a_log, d_skip, h0, conv0, *, mix_a, mix_b,
               ch, valid, gb, tt):
    bsz, t_len, _ = proj3.shape
    n_pairs = mix_b // LANES
    pb = n_pairs // SSD_GROUPS
    w = pb * LANES
    nc = tt // ch
    n_rows = gb * ch
    has_init = h0 is not None
    assert (4 * mix_a) % w == 0 and n_rows == LANES
    z0 = 4 * mix_a // w
    x0 = z0 + SSD_GROUPS
    b0 = (4 * mix_a + 2 * mix_b) // LANES
    c0 = b0 + SSD_GROUPS

    seq_w = lambda base: (lambda b, g, t: (b, t, base + g))
    par = lambda base: (lambda b, g, t: (0, base + g))
    const2 = lambda b, g, t: (0, 0)

    in_specs = [
        pl.BlockSpec((gb, tt, w), seq_w(z0)),
        pl.BlockSpec((gb, tt, w), seq_w(x0)),
        pl.BlockSpec((gb, tt, LANES), seq_w(b0)),
        pl.BlockSpec((gb, tt, LANES), seq_w(c0)),
        pl.BlockSpec((gb, tt, LANES), lambda b, g, t: (b, t, 0)),
        pl.BlockSpec((CONV_WIDTH, w), par(0)),
        pl.BlockSpec((CONV_WIDTH, LANES), par(n_pairs)),
        pl.BlockSpec((CONV_WIDTH, LANES), par(n_pairs + SSD_GROUPS)),
        pl.BlockSpec((1, w), par(0)),
        pl.BlockSpec((1, LANES), par(n_pairs)),
        pl.BlockSpec((1, LANES), par(n_pairs + SSD_GROUPS)),
        pl.BlockSpec((1, LANES, w), lambda b, g, t: (g, 0, 0)),
        pl.BlockSpec((1, 1, w), lambda b, g, t: (g, 0, 0)),
        pl.BlockSpec((1, 1, w), lambda b, g, t: (g, 0, 0)),
        pl.BlockSpec((1, 1, w), lambda b, g, t: (g, 0, 0)),
        pl.BlockSpec((n_rows, n_rows), const2),
        pl.BlockSpec((n_rows, n_rows), const2),
    ]
    conv_b2 = conv_b.reshape(1, -1)
    block_tri = _block_tri(gb, ch)
    args = [proj3, proj3, proj3, proj3, dt3, conv_w, conv_w, conv_w, conv_b2, conv_b2, conv_b2,
            _head_select(dt_bias.shape[0]), _group_rows(dt_bias), _group_rows(a_log),
            _group_rows(d_skip), jnp.asarray(block_tri, dtype=BF16), jnp.asarray(block_tri)]
    if has_init:
        tail = CONV_WIDTH - 1
        in_specs += [
            pl.BlockSpec((gb, pb, LANES, LANES), lambda b, g, t: (b, g, 0, 0)),
            pl.BlockSpec((gb, tail, w), lambda b, g, t: (b, 0, g)),
            pl.BlockSpec((gb, tail, LANES), lambda b, g, t: (b, 0, n_pairs + g)),
            pl.BlockSpec((gb, tail, LANES), lambda b, g, t: (b, 0, n_pairs + SSD_GROUPS + g)),
        ]
        args += [h0, conv0, conv0, conv0]
    kern = functools.partial(_ssd_kernel, ch=ch, valid=valid, nc=nc, gb=gb, pb=pb,
                             has_init=has_init)
    return pl.pallas_call(
        kern,
        grid=(bsz // gb, SSD_GROUPS, t_len // tt),
        in_specs=in_specs,
        out_specs=[
            pl.BlockSpec((gb, tt, w), lambda b, g, t: (b, t, g)),
            pl.BlockSpec((gb, pb, LANES, LANES), lambda b, g, t: (b, g, 0, 0)),
        ],
        out_shape=[
            jax.ShapeDtypeStruct((bsz, t_len, mix_b), F32),
            jax.ShapeDtypeStruct((bsz, n_pairs, LANES, LANES), F32),
        ],
        scratch_shapes=[pltpu.VMEM((gb, LANES, w), F32),
                        pltpu.VMEM((gb, ch + SUBLANES, w), F32),
                        pltpu.VMEM((gb, ch + SUBLANES, LANES), F32),
                        pltpu.VMEM((gb, ch + SUBLANES, LANES), F32)],
        compiler_params=_cparams(("arbitrary", "arbitrary", "arbitrary")),
        name="ssd_pairs",
    )(*args)


OUT_TM = 512
GROUP_LANE0 = N_EXPERTS
NEG_BIG = -1e30
NO_LANE = 4 * LANES


def _first_lane_of(mask, lane):
    return jnp.min(jnp.where(mask, lane, float(NO_LANE)), axis=1, keepdims=True)


def _route(logits):
    lane_i = lax.broadcasted_iota(I32, logits.shape, 1)
    lane = lane_i.astype(F32)
    is_grp = (lane_i >= GROUP_LANE0) & (lane_i < GROUP_LANE0 + N_EXPERT_GROUPS)
    lg = jnp.where(is_grp, logits, NEG_BIG)
    g_max = jnp.max(lg, axis=1, keepdims=True)
    g_sum = jnp.sum(jnp.where(is_grp, jnp.exp(lg - g_max), 0.0), axis=1, keepdims=True)
    g_idx = _first_lane_of(lg == g_max, lane) - GROUP_LANE0
    gp_top = 1.0 / g_sum
    lane_grp = jnp.right_shift(lane_i, 3).astype(F32)
    in_grp = (lane_i < N_EXPERTS) & (lane_grp == g_idx)
    le = jnp.where(in_grp, logits, NEG_BIG)
    e_max = jnp.max(le, axis=1, keepdims=True)
    e_exp = jnp.where(in_grp, jnp.exp(le - e_max), 0.0)
    ep = e_exp / jnp.sum(e_exp, axis=1, keepdims=True)
    ep = jnp.where(in_grp, ep, -1.0)
    p1 = jnp.max(ep, axis=1, keepdims=True)
    i1 = _first_lane_of(ep == p1, lane)
    ep2 = jnp.where(lane == i1, -1.0, ep)
    p2 = jnp.max(ep2, axis=1, keepdims=True)
    i2 = _first_lane_of(ep2 == p2, lane)
    den = p1 + p2
    return i1, i2, gp_top * p1 / den, gp_top * p2 / den


def _outproj_kernel(o_ref, yz_ref, x_ref, g1_ref, sc_ref, sh_ref, ng_ref, sng_ref, w_ref, wr_ref,
                    br_ref, tri_ref, cnt0_ref,
                    x1_ref, hn_ref, ids_ref, wts_ref, rank_ref, cnt_ref,
                    lhs_ref, x1s_ref, cnts_ref, *, n_col, gw):
    i = pl.program_id(0)
    j = pl.program_id(1)
    tm = lhs_ref.shape[0]
    tn = w_ref.shape[1]
    mix_a = o_ref.shape[1]

    @pl.when((i == 0) & (j == 0))
    def _():
        cnts_ref[...] = cnt0_ref[...]

    @pl.when(j == 0)
    def _():
        lhs_ref[:, 0:mix_a] = o_ref[...].astype(BF16)
        for g in range(yz_ref.shape[1] // gw):
            seg = yz_ref[:, g * gw:(g + 1) * gw]
            ms = jnp.mean(seg * seg, axis=-1, keepdims=True)
            seg = seg * lax.rsqrt(ms + EPS) * sng_ref[:, g * gw:(g + 1) * gw]
            lhs_ref[:, mix_a + g * gw:mix_a + (g + 1) * gw] = seg.astype(BF16)

    acc = _dot(lhs_ref[...], w_ref[...].astype(BF16))
    x1 = x_ref[...] + g1_ref[...] * acc.reshape(x_ref.shape)
    x1_ref[...] = x1
    x1s_ref[j] = x1.reshape(tm, tn)

    @pl.when(j == n_col - 1)
    def _():
        ssq = jnp.zeros((tm, 1), F32)
        for c in range(n_col):
            xc = x1s_ref[c]
            ssq = ssq + jnp.sum(xc * xc, axis=-1, keepdims=True)
        rs = lax.rsqrt(ssq / (n_col * tn) + EPS)
        gshape = x_ref.shape
        logits = jnp.zeros((tm, LANES), F32) + br_ref[...]
        for c in range(n_col):
            cols = slice(c * tn, (c + 1) * tn)
            hn = (x1s_ref[c] * rs * ng_ref[:, cols]).reshape(gshape)
            hn = (hn * (1.0 + sc_ref[:, :, cols]) + sh_ref[:, :, cols]).reshape(tm, tn)
            hn_ref[:, cols] = hn
            logits = logits + _dot_hp(hn, wr_ref[cols, :])
        i1, i2, w1, w2 = _route(logits)
        lane = lax.broadcasted_iota(I32, (tm, LANES), 1)
        hit1 = lane.astype(F32) == i1
        hit2 = lane.astype(F32) == i2
        onehot = jnp.where(hit1 | hit2, 1.0, 0.0).astype(BF16)
        before = _dot(tri_ref[...], onehot) + cnts_ref[0:1, :]
        r1 = jnp.sum(jnp.where(hit1, before, 0.0), axis=1, keepdims=True)
        r2 = jnp.sum(jnp.where(hit2, before, 0.0), axis=1, keepdims=True)
        ids_ref[...] = jnp.where(lane == 0, i1, jnp.where(lane == 1, i2, 0.0)).astype(I32)
        wts_ref[...] = jnp.where(lane == 0, w1, jnp.where(lane == 1, w2, 0.0))
        rank_ref[...] = jnp.where(lane == 0, r1, jnp.where(lane == 1, r2, 0.0)).astype(I32)
        total = cnts_ref[0:1, :] + jnp.sum(onehot.astype(F32), axis=0, keepdims=True)
        cnts_ref[...] = jnp.broadcast_to(total, cnts_ref.shape)
        cnt_ref[...] = jnp.broadcast_to(total, cnt_ref.shape)


def _strict_lower(n):
    import numpy as np
    t = np.arange(n)[:, None]
    s = np.arange(n)[None, :]
    return jnp.asarray((s < t).astype(np.float32), dtype=BF16)


def _out_projection(o3, yz3, x3, mod3, norm_g, ssm_norm_g, w_out, w_router, b_router, cnt0,
                    group_rows):
    bsz, t_len, d = x3.shape
    g_blk, r_blk = group_rows
    tm = g_blk * r_blk
    tn = PROJ_TN
    n_col = d // tn
    mix_a = o3.shape[-1]
    mix_b = yz3.shape[-1]
    n_tok = bsz * t_len
    tiles_per_seq = max(t_len // r_blk, 1)
    sec = d // tn

    def xmap(i, j):
        return (i // tiles_per_seq, i % tiles_per_seq, j)

    def modmap(s):
        return lambda i, j: (i // tiles_per_seq, 0, s)

    row = lambda i, j: (i, 0)
    const2 = lambda i, j: (0, 0)
    kern = functools.partial(_outproj_kernel, n_col=n_col, gw=mix_b // SSD_GROUPS)
    return pl.pallas_call(
        kern,
        grid=(n_tok // tm, n_col),
        in_specs=[
            pl.BlockSpec((tm, mix_a), row),
            pl.BlockSpec((tm, mix_b), row),
            pl.BlockSpec((g_blk, r_blk, tn), xmap),
            pl.BlockSpec((g_blk, 1, tn), lambda i, j: (i // tiles_per_seq, 0, 2 * sec + j)),
            pl.BlockSpec((g_blk, 1, d), modmap(4)),
            pl.BlockSpec((g_blk, 1, d), modmap(3)),
            pl.BlockSpec((1, d), const2),
            pl.BlockSpec((1, mix_b), const2),
            pl.BlockSpec((d, tn), lambda i, j: (0, j)),
            pl.BlockSpec((d, LANES), const2),
            pl.BlockSpec((1, LANES), const2),
            pl.BlockSpec((tm, tm), const2),
            pl.BlockSpec((SUBLANES, LANES), const2),
        ],
        out_specs=[
            pl.BlockSpec((g_blk, r_blk, tn), xmap),
            pl.BlockSpec((tm, d), row),
            pl.BlockSpec((tm, LANES), row),
            pl.BlockSpec((tm, LANES), row),
            pl.BlockSpec((tm, LANES), row),
            pl.BlockSpec((SUBLANES, LANES), const2),
        ],
        out_shape=[
            jax.ShapeDtypeStruct((bsz, t_len, d), F32),
            jax.ShapeDtypeStruct((n_tok, d), F32),
            jax.ShapeDtypeStruct((n_tok, LANES), I32),
            jax.ShapeDtypeStruct((n_tok, LANES), F32),
            jax.ShapeDtypeStruct((n_tok, LANES), I32),
            jax.ShapeDtypeStruct((SUBLANES, LANES), F32),
        ],
        scratch_shapes=[pltpu.VMEM((tm, d), BF16), pltpu.VMEM((n_col, tm, tn), F32),
                        pltpu.VMEM((SUBLANES, LANES), F32)],
        compiler_params=_cparams(("arbitrary", "arbitrary")),
        name="out_proj_router",
    )(o3.reshape(n_tok, mix_a), yz3.reshape(n_tok, mix_b), x3, mod3, mod3, mod3,
      norm_g.reshape(1, d), ssm_norm_g.reshape(1, mix_b), w_out, w_router, b_router,
      _strict_lower(tm), cnt0)


def _positions_kernel(ids_ref, rank_ref, cnt_ref, pos_ref, te_ref, nt_ref, *, block):
    lane8 = lax.broadcasted_iota(I32, (SUBLANES, LANES), 1)
    cnt = cnt_ref[...]
    tiles = jnp.floor((cnt + (MOE_TM - 1)) * (1.0 / MOE_TM))
    tiles = jnp.where(lane8 < N_EXPERTS, tiles, 0.0)
    incl = tiles
    shift = 1
    while shift < N_EXPERTS:
        incl = incl + jnp.where(lane8 >= shift, pltpu.roll(incl, shift, axis=1), 0.0)
        shift *= 2
    offs = ((incl - tiles) * MOE_TM)[0:1, :]
    nt_ref[...] = jnp.sum(tiles, axis=1, keepdims=True).astype(I32) + jnp.zeros(nt_ref.shape, I32)

    incl_col = jnp.broadcast_to(incl[0:1, :], (LANES, LANES)).T
    e_row = lax.broadcasted_iota(I32, (LANES, LANES), 0)
    i_lane = lax.broadcasted_iota(I32, (LANES, LANES), 1).astype(F32)
    done = jnp.where((incl_col <= i_lane) & (e_row < N_EXPERTS), 1.0, 0.0)
    te = jnp.minimum(jnp.sum(done, axis=0, keepdims=True), N_EXPERTS - 1.0)
    te_ref[...] = te.astype(I32) + jnp.zeros(te_ref.shape, I32)

    lane = lax.broadcasted_iota(I32, (block, LANES), 1)

    def body(bi, c):
        r0 = pl.multiple_of(bi * block, block)
        ids = ids_ref[pl.ds(r0, block), :]
        rank = rank_ref[pl.ds(r0, block), :]
        out = jnp.zeros((block, LANES), I32)
        for k in range(TOP_K):
            off = jnp.sum(jnp.where(lane == ids[:, k:k + 1], offs, 0.0), axis=1, keepdims=True)
            out = jnp.where(lane == k, off.astype(I32) + rank[:, k:k + 1], out)
        pos_ref[pl.ds(r0, block), :] = out
        return c

    lax.fori_loop(0, ids_ref.shape[0] // block, body, 0)


def _positions(ids, rank, cnt):
    n_tok = ids.shape[0]
    block = 512
    kern = functools.partial(_positions_kernel, block=block)
    full = lambda s: pl.BlockSpec(s, lambda: tuple(0 for _ in s))
    return pl.pallas_call(
        kern,
        in_specs=[full((n_tok, LANES)), full((n_tok, LANES)), full((SUBLANES, LANES))],
        out_specs=[full((n_tok, LANES)), full((SUBLANES, LANES)), full((SUBLANES, LANES))],
        out_shape=[jax.ShapeDtypeStruct((n_tok, LANES), I32),
                   jax.ShapeDtypeStruct((SUBLANES, LANES), I32),
                   jax.ShapeDtypeStruct((SUBLANES, LANES), I32)],
        compiler_params=pltpu.CompilerParams(vmem_limit_bytes=VMEM_LIMIT),
        name="route_positions",
    )(ids, rank, cnt)


def _row_copy(src_ref, src_row, dst_ref, dst_row, sem):
    return pltpu.make_async_copy(src_ref.at[pl.ds(src_row, 1), :], dst_ref.at[pl.ds(dst_row, 1), :],
                                 sem)


def _dispatch_kernel(pos_ref, hn_ref, xg_in_ref, xg_ref, sem):
    del xg_in_ref
    rows = hn_ref.shape[0]
    base = pl.program_id(0) * rows

    def issue(r, c):
        for k in range(TOP_K):
            _row_copy(hn_ref, r, xg_ref, pos_ref[(base + r) * TOP_K + k], sem).start(priority=k)
        return c

    def drain(r, c):
        for k in range(TOP_K):
            _row_copy(hn_ref, r, xg_ref, 0, sem).wait()
        return c

    lax.fori_loop(0, rows, issue, 0, unroll=8)
    lax.fori_loop(0, rows, drain, 0, unroll=8)


def _dispatch(pos_flat, hn, xg):
    n_tok, d = hn.shape
    grid_spec = pltpu.PrefetchScalarGridSpec(
        num_scalar_prefetch=1,
        grid=(n_tok // ROW_DMA_TILE,),
        in_specs=[pl.BlockSpec((ROW_DMA_TILE, d), lambda i, pos: (i, 0)),
                  pl.BlockSpec(memory_space=pl.ANY)],
        out_specs=pl.BlockSpec(memory_space=pl.ANY),
        scratch_shapes=[pltpu.SemaphoreType.DMA(())],
    )
    return pl.pallas_call(
        _dispatch_kernel,
        grid_spec=grid_spec,
        out_shape=jax.ShapeDtypeStruct(xg.shape, xg.dtype),
        input_output_aliases={2: 0},
        compiler_params=_cparams(("arbitrary",)),
        name="moe_dispatch",
    )(pos_flat, hn, xg)


def _weight_copies(w_hbm_refs, expert, wbuf_ref, slot, sem_ref):
    return [pltpu.make_async_copy(w_ref.at[expert], wbuf_ref.at[slot, i], sem_ref.at[slot])
            for i, w_ref in enumerate(w_hbm_refs)]


def _expert_weights_slot(te_ref, nt, t, w_hbm_refs, wbuf_ref, sem_ref, slot_ref):
    expert = te_ref[t]

    @pl.when(t == 0)
    def _():
        slot_ref[0] = 0
        for cp in _weight_copies(w_hbm_refs, expert, wbuf_ref, 0, sem_ref):
            cp.start()

    @pl.when((t == 0) | (expert != te_ref[jnp.maximum(t - 1, 0)]))
    def _():
        @pl.when(t > 0)
        def _():
            slot_ref[0] = 1 - slot_ref[0]

        slot = slot_ref[0]
        for cp in _weight_copies(w_hbm_refs, expert, wbuf_ref, slot, sem_ref):
            cp.wait()
        nxt = lax.while_loop(lambda u: (u < nt) & (te_ref[jnp.minimum(u, nt - 1)] == expert),
                             lambda u: u + 1, t + 1)

        @pl.when(nxt < nt)
        def _():
            for cp in _weight_copies(w_hbm_refs, te_ref[jnp.minimum(nxt, nt - 1)], wbuf_ref,
                                     1 - slot, sem_ref):
                cp.start()

    return slot_ref[0]


def _moe_up_kernel(te_ref, nt_ref, x_ref, w1_ref, w3_ref, act_ref, wbuf_ref, sem_ref, slot_ref):
    t = pl.program_id(0)
    nt = nt_ref[0]

    @pl.when(t < nt)
    def _():
        slot = _expert_weights_slot(te_ref, nt, t, (w1_ref, w3_ref), wbuf_ref, sem_ref, slot_ref)
        xb = x_ref[...].astype(BF16)
        a = _dot(xb, wbuf_ref[slot, 0].astype(BF16))
        b = _dot(xb, wbuf_ref[slot, 1].astype(BF16))
        act_ref[...] = (_silu(a) * b).astype(BF16)

    @pl.when(t >= nt)
    def _():
        act_ref[...] = jnp.zeros(act_ref.shape, act_ref.dtype)


def _moe_down_kernel(te_ref, nt_ref, act_ref, w2_ref, out_ref, wbuf_ref, sem_ref, slot_ref):
    t = pl.program_id(0)
    nt = nt_ref[0]

    @pl.when(t < nt)
    def _():
        slot = _expert_weights_slot(te_ref, nt, t, (w2_ref,), wbuf_ref, sem_ref, slot_ref)
        out_ref[...] = _dot(act_ref[...], wbuf_ref[slot, 0].astype(BF16))

    @pl.when(t >= nt)
    def _():
        out_ref[...] = jnp.zeros(out_ref.shape, out_ref.dtype)


def _moe_experts(te, nt, xg, w1, w3, w2):
    n_rows, d = xg.shape
    d_exp = w1.shape[-1]
    n_tiles = n_rows // MOE_TM
    live_tile = lambda t, te_r, nt_r: (jnp.minimum(t, nt_r[0] - 1), 0)
    any_spec = pl.BlockSpec(memory_space=pl.ANY)

    act = pl.pallas_call(
        _moe_up_kernel,
        grid_spec=pltpu.PrefetchScalarGridSpec(
            num_scalar_prefetch=2,
            grid=(n_tiles,),
            in_specs=[pl.BlockSpec((MOE_TM, d), live_tile), any_spec, any_spec],
            out_specs=pl.BlockSpec((MOE_TM, d_exp), lambda t, te_r, nt_r: (t, 0)),
            scratch_shapes=[pltpu.VMEM((2, 2, d, d_exp), F32), pltpu.SemaphoreType.DMA((2,)),
                            pltpu.SMEM((1,), I32)],
        ),
        out_shape=jax.ShapeDtypeStruct((n_rows, d_exp), BF16),
        compiler_params=_cparams(("arbitrary",)),
        name="moe_up",
    )(te, nt, xg, w1, w3)

    return pl.pallas_call(
        _moe_down_kernel,
        grid_spec=pltpu.PrefetchScalarGridSpec(
            num_scalar_prefetch=2,
            grid=(n_tiles,),
            in_specs=[pl.BlockSpec((MOE_TM, d_exp), live_tile), any_spec],
            out_specs=pl.BlockSpec((MOE_TM, d), lambda t, te_r, nt_r: (t, 0)),
            scratch_shapes=[pltpu.VMEM((2, 1, d_exp, d), F32), pltpu.SemaphoreType.DMA((2,)),
                            pltpu.SMEM((1,), I32)],
        ),
        out_shape=jax.ShapeDtypeStruct((n_rows, d), F32),
        compiler_params=_cparams(("arbitrary",)),
        name="moe_down",
    )(te, nt, act, w2)


def _combine_kernel(pos_ref, x1_ref, g2_ref, wts_ref, fg_ref, eo_ref, y_ref, buf_ref, sem):
    rows = wts_ref.shape[0]
    base = pl.program_id(0) * rows

    def issue(r, c):
        for k in range(TOP_K):
            _row_copy(eo_ref, pos_ref[(base + r) * TOP_K + k], buf_ref.at[k], r, sem).start(
                priority=k)
        return c

    def drain(r, c):
        for k in range(TOP_K):
            _row_copy(eo_ref, 0, buf_ref.at[k], r, sem).wait()
        return c

    lax.fori_loop(0, rows, issue, 0, unroll=8)
    lax.fori_loop(0, rows, drain, 0, unroll=8)
    w = wts_ref[...]
    moe = w[:, 0:1] * buf_ref[0] + w[:, 1:2] * buf_ref[1]
    x2 = x1_ref[...] + g2_ref[...] * moe.reshape(x1_ref.shape)
    ms = jnp.mean(x2 * x2, axis=-1, keepdims=True)
    y_ref[...] = x2 * lax.rsqrt(ms + EPS) * fg_ref[...]


def _combine(pos_flat, x1, mod3, wts, final_g, expert_out, group_rows):
    bsz, t_len, d = x1.shape
    g_blk, r_blk = group_rows
    tm = g_blk * r_blk
    tiles_per_seq = max(t_len // r_blk, 1)

    def xmap(i, pos):
        return (i // tiles_per_seq, i % tiles_per_seq, 0)

    return pl.pallas_call(
        _combine_kernel,
        grid_spec=pltpu.PrefetchScalarGridSpec(
            num_scalar_prefetch=1,
            grid=((bsz * t_len) // tm,),
            in_specs=[
                pl.BlockSpec((g_blk, r_blk, d), xmap),
                pl.BlockSpec((g_blk, 1, d), lambda i, pos: (i // tiles_per_seq, 0, 5)),
                pl.BlockSpec((tm, LANES), lambda i, pos: (i, 0)),
                pl.BlockSpec((1, d), lambda i, pos: (0, 0)),
                pl.BlockSpec(memory_space=pl.ANY),
            ],
            out_specs=pl.BlockSpec((g_blk, r_blk, d), xmap),
            scratch_shapes=[pltpu.VMEM((TOP_K, tm, d), F32), pltpu.SemaphoreType.DMA(())],
        ),
        out_shape=jax.ShapeDtypeStruct((bsz, t_len, d), F32),
        compiler_params=_cparams(("arbitrary",)),
        name="moe_combine",
    )(pos_flat, x1, mod3, wts, final_g.reshape(1, d), expert_out)


def kernel(x_prompt, x_sample, state_hgrn, state_ssm, state_conv, c_prompt, c_sample, ada_w, ada_b,
           norm1_g, norm2_g, w_in, hgrn_lb, hgrn_onorm_g, conv_w, conv_b, dt_bias, a_log, d_skip,
           ssm_norm_g, w_out, w_grp, b_grp, w_rt, b_rt, w1, w3, w2, final_g):
    depth = w_in.shape[0]
    assert depth == 1, "single-layer trunk"
    layer = 0
    bp, t_p, d = x_prompt.shape
    bs, t_s, _ = x_sample.shape
    mix_a = hgrn_onorm_g.shape[1]
    mix_b = ssm_norm_g.shape[1]
    conv_dim = conv_w.shape[2]
    n_main = 4 * mix_a + mix_b + conv_dim
    n_ssd_heads = dt_bias.shape[1]
    n_pairs = mix_b // LANES
    xbc0 = 4 * mix_a + mix_b
    tail = CONV_WIDTH - 1
    assert t_s <= SUBLANES and t_s >= tail and t_p % PROMPT_CHUNK == 0

    xs_pad = jnp.pad(x_sample, ((0, 0), (0, SUBLANES - t_s), (0, 0)))
    n_c = bp + bs
    c_rows = -(-n_c // SUBLANES) * SUBLANES
    c_all = jnp.pad(jnp.concatenate([c_prompt, c_sample], axis=0), ((0, c_rows - n_c), (0, 0)))

    mod = _modulation(c_all, ada_w[layer], ada_b[layer])
    mod_p = mod[:bp].reshape(bp, 1, -1)
    mod_s = mod[bp:n_c].reshape(bs, 1, -1)
    w_in_t = jnp.swapaxes(w_in[layer], 0, 1)
    w_dt_t = jnp.pad(w_in_t[n_main:], ((0, LANES - n_ssd_heads), (0, 0)))
    w_router = jnp.pad(jnp.concatenate([w_rt[layer], w_grp[layer]], axis=1),
                       ((0, 0), (0, LANES - N_EXPERTS - N_EXPERT_GROUPS)))
    b_router = jnp.pad(jnp.concatenate([b_rt[layer], b_grp[layer]]),
                       (0, LANES - N_EXPERTS - N_EXPERT_GROUPS)).reshape(1, LANES)

    groups = (
        (x_prompt, mod_p, None, None, None, t_p, PROMPT_CHUNK, PROMPT_CHUNK, MIXER_ROWS,
         (LANES // PROMPT_CHUNK, 4), LANES // PROMPT_CHUNK, (1, PROJ_TM), (1, OUT_TM)),
        (xs_pad, mod_s, state_hgrn[layer], state_ssm[layer].reshape(bs, n_pairs, LANES, LANES),
         state_conv[layer], t_s, SUBLANES, t_s, SUBLANES,
         (LANES // SUBLANES, 2), LANES // SUBLANES, (PROJ_TM // SUBLANES, SUBLANES),
         (OUT_TM // SUBLANES, SUBLANES)),
    )

    cnt = jnp.zeros((SUBLANES, LANES), F32)
    per_group = []
    for x3, mod3, s0, h0, c0, t_real, ch, valid, tt, (hg_gb, hg_hb), ssd_gb, proj_gr, out_gr in groups:
        bsz, t_len, _ = x3.shape
        proj, dt = _in_projection(x3, mod3, norm1_g[layer], w_in_t, w_dt_t, n_main, proj_gr)
        proj3 = proj.reshape(bsz, t_len, n_main)
        dt3 = dt.reshape(bsz, t_len, LANES)
        o3, s_new = _hgrn_heads(proj3, hgrn_lb, hgrn_onorm_g[layer], s0, ch=ch, valid=valid,
                                gb=hg_gb, hb=hg_hb, tt=tt, layer=layer)
        yz3, h_new = _ssd_pairs(proj3, dt3, conv_w[layer], conv_b[layer], dt_bias[layer],
                                a_log[layer], d_skip[layer], h0, c0, mix_a=mix_a, mix_b=mix_b,
                                ch=ch, valid=valid, gb=ssd_gb, tt=tt)
        conv_new = proj3[:, t_real - tail:t_real, xbc0:xbc0 + conv_dim]
        x1, hn, ids, wts, rank, cnt = _out_projection(
            o3, yz3, x3, mod3, norm2_g[layer], ssm_norm_g[layer], w_out[layer], w_router, b_router,
            cnt, out_gr)
        per_group.append(dict(x1=x1, hn=hn, ids=ids, wts=wts, rank=rank, mod3=mod3, out_gr=out_gr,
                              s_new=s_new, h_new=h_new, conv_new=conv_new))

    ids_all = jnp.concatenate([g["ids"] for g in per_group], axis=0)
    rank_all = jnp.concatenate([g["rank"] for g in per_group], axis=0)
    n_tok = ids_all.shape[0]
    pos, te, nt = _positions(ids_all, rank_all, cnt)
    pos_flat = pos[:, :TOP_K].reshape(-1)
    n_tiles = -(-(n_tok * TOP_K + N_EXPERTS * (MOE_TM - 1)) // MOE_TM)
    assert n_tiles <= LANES
    te_vec = te[0, :n_tiles]
    nt_vec = nt[0, :1]

    xg = jnp.zeros((n_tiles * MOE_TM, d), F32)
    row0 = 0
    for g in per_group:
        rows = g["hn"].shape[0]
        g["pos"] = pos_flat[row0 * TOP_K:(row0 + rows) * TOP_K]
        xg = _dispatch(g["pos"], g["hn"], xg)
        row0 += rows
    expert_out = _moe_experts(te_vec, nt_vec, xg, w1[layer], w3[layer], w2[layer])
    ys = [_combine(g["pos"], g["x1"], g["mod3"], g["wts"], final_g, expert_out, g["out_gr"])
          for g in per_group]

    gp, gs = per_group
    return (
        ys[0],
        ys[1][:, :t_s],
        gp["s_new"][None],
        gp["h_new"].reshape(1, bp, n_ssd_heads, SSD_HEAD_DIM, SSD_STATE),
        gp["conv_new"][None],
        gs["s_new"][None],
        gs["h_new"].reshape(1, bs, n_ssd_heads, SSD_HEAD_DIM, SSD_STATE),
        gs["conv_new"][None],
    )
```

```python
import functools

import jax
import jax.numpy as jnp
from jax import lax
from jax.experimental import pallas as pl
from jax.experimental.pallas import tpu as pltpu

F32 = jnp.float32
BF16 = jnp.bfloat16
I32 = jnp.int32
EPS = 1e-6
LOG2E = 1.4426950408889634

LANES = 128
SUBLANES = 8
VMEM_LIMIT = 56 * 1024 * 1024

HGRN_HEAD_DIM = 128
SSD_HEAD_DIM = 64
SSD_GROUPS = 2
SSD_STATE = 128
CONV_WIDTH = 4
N_EXPERT_GROUPS = 4
EXPERTS_PER_GROUP = 8
N_EXPERTS = N_EXPERT_GROUPS * EXPERTS_PER_GROUP
TOP_K = 2

PROJ_SEQS = 128
NORM_ROWS = 256
PROJ_TN = 512
MOE_TM = 256
PROMPT_CHUNK = 64
MIXER_ROWS = 512
ROW_DMA_TILE = 256


def _cparams(sem):
    return pltpu.CompilerParams(dimension_semantics=sem, vmem_limit_bytes=VMEM_LIMIT)


def _dot(a, b):
    return jnp.dot(a, b, preferred_element_type=F32)


def _dot_nt(a, b):
    return lax.dot_general(a, b, (((1,), (1,)), ((), ())), preferred_element_type=F32)


def _split2(x):
    hi = x.astype(BF16)
    lo = (x - hi.astype(F32)).astype(BF16)
    return hi, lo


def _split3(x):
    hi = x.astype(BF16)
    r = x - hi.astype(F32)
    mid = r.astype(BF16)
    lo = (r - mid.astype(F32)).astype(BF16)
    return hi, mid, lo


def _dot_exact_lhs(m_bf16, x):
    hi, mid, lo = _split3(x)
    return _dot(m_bf16, hi) + _dot(m_bf16, mid) + _dot(m_bf16, lo)


def _dot_exact_rhs(x, m_bf16):
    hi, mid, lo = _split3(x)
    return _dot(hi, m_bf16) + _dot(mid, m_bf16) + _dot(lo, m_bf16)


def _dot_hp(a, w):
    ah, al = _split2(a)
    wh, wl = _split2(w)
    return _dot(ah, wh) + _dot(al, wh) + _dot(ah, wl)


def _silu(x):
    return x * jax.nn.sigmoid(x)


def _mod_kernel(c_ref, w_ref, b_ref, o_ref):
    a = _silu(c_ref[...])
    o_ref[...] = _dot_hp(a, w_ref[...]) + b_ref[...]


def _modulation(c_all, ada_w, ada_b):
    rows, d = c_all.shape
    n_out = ada_w.shape[1]
    tn = 1024
    return pl.pallas_call(
        _mod_kernel,
        grid=(n_out // tn,),
        in_specs=[
            pl.BlockSpec((rows, d), lambda j: (0, 0)),
            pl.BlockSpec((d, tn), lambda j: (0, j)),
            pl.BlockSpec((1, tn), lambda j: (0, j)),
        ],
        out_specs=pl.BlockSpec((rows, tn), lambda j: (0, j)),
        out_shape=jax.ShapeDtypeStruct((rows, n_out), F32),
        compiler_params=_cparams(("arbitrary",)),
        name="adaln_mod",
    )(c_all, ada_w, ada_b.reshape(1, n_out))


def _proj_kernel(x_hbm_ref, sc_ref, sh_ref, g_ref, w_ref, wdt_ref, o_ref, dt_ref, hn_ref, xs_ref,
                 sem):
    i = pl.program_id(0)
    j = pl.program_id(1)
    g_blk, r_blk, d = xs_ref.shape

    def x_copy(tile):
        return pltpu.make_async_copy(x_hbm_ref.at[pl.ds(tile * g_blk, g_blk)], xs_ref, sem)

    @pl.when(j == 0)
    def _():
        @pl.when(i == 0)
        def _():
            x_copy(0).start()

        x_copy(i).wait()
        wh, wl = _split2(wdt_ref[...])
        n_rows = g_blk * r_blk
        step = min(n_rows, NORM_ROWS)
        for c in range(n_rows // step):
            if g_blk == 1:
                x = xs_ref[:, c * step:(c + 1) * step, :]
                sc, sh = sc_ref[...], sh_ref[...]
            else:
                gs = slice(c * step // r_blk, (c + 1) * step // r_blk)
                x, sc, sh = xs_ref[gs], sc_ref[gs], sh_ref[gs]
            ms = jnp.mean(x * x, axis=-1, keepdims=True)
            hn = ((x * lax.rsqrt(ms + EPS)) * g_ref[...] * (1.0 + sc) + sh).reshape(step, d)
            rows = slice(c * step, (c + 1) * step)
            hn_ref[rows, :] = hn.astype(BF16)
            ah, al = _split2(hn)
            dt_ref[rows, :] = _dot_nt(ah, wh) + _dot_nt(al, wh) + _dot_nt(ah, wl)

        @pl.when(i + 1 < pl.num_programs(0))
        def _():
            x_copy(i + 1).start()

    o_ref[...] = _dot_nt(hn_ref[...], w_ref[...].astype(BF16))


def _in_projection(x3, mod3, norm_g, w_t, w_dt_t, n_main, g_blk):
    bsz, t_len, d = x3.shape
    tm = g_blk * t_len
    n_row_tiles = bsz // g_blk

    def mod_map(sec):
        return lambda i, j: (i, 0, sec)

    return pl.pallas_call(
        _proj_kernel,
        grid=(n_row_tiles, n_main // PROJ_TN),
        in_specs=[
            pl.BlockSpec(memory_space=pl.ANY),
            pl.BlockSpec((g_blk, 1, d), mod_map(1)),
            pl.BlockSpec((g_blk, 1, d), mod_map(0)),
            pl.BlockSpec((1, d), lambda i, j: (0, 0)),
            pl.BlockSpec((PROJ_TN, d), lambda i, j: (j, 0)),
            pl.BlockSpec((LANES, d), lambda i, j: (0, 0)),
        ],
        out_specs=[
            pl.BlockSpec((tm, PROJ_TN), lambda i, j: (i, j)),
            pl.BlockSpec((tm, LANES), lambda i, j: (i, 0)),
        ],
        out_shape=[
            jax.ShapeDtypeStruct((bsz * t_len, n_main), F32),
            jax.ShapeDtypeStruct((bsz * t_len, LANES), F32),
        ],
        scratch_shapes=[pltpu.VMEM((tm, d), BF16), pltpu.VMEM((g_blk, t_len, d), F32),
                        pltpu.SemaphoreType.DMA(())],
        compiler_params=_cparams(("arbitrary", "arbitrary")),
        name="in_proj",
    )(x3, mod3, mod3, norm_g.reshape(1, d), w_t, w_dt_t)


def _level_sizes(ch):
    sizes = []
    sz = SUBLANES
    while 2 * sz <= ch:
        sizes.append(sz)
        sz *= 2
    return sizes


def _level_masks(ch, n_rows):
    import numpy as np
    t = np.arange(n_rows)[:, None]
    s = np.arange(n_rows)[None, :]
    out = []
    for sz in _level_sizes(ch):
        m = (t // (2 * sz) == s // (2 * sz)) & (t % (2 * sz) >= sz) & (s % (2 * sz) < sz)
        out.append(m.astype(np.float32))
    if not out:
        out.append(np.zeros((n_rows, n_rows), np.float32))
    return jnp.asarray(np.stack(out))


def _hgrn_kernel(*refs, ch, valid, nc, gb, hb, has_init, layer):
    if has_init:
        (q_ref, f_ref, i_ref, g_ref, lb_ref, on_ref, tri_ref, masks_ref, s0_ref,
         o_ref, sout_ref, st_ref, b_scr, k_scr) = refs
    else:
        (q_ref, f_ref, i_ref, g_ref, lb_ref, on_ref, tri_ref, masks_ref,
         o_ref, sout_ref, st_ref, b_scr, k_scr) = refs
        s0_ref = None

    n_lb = lb_ref.shape[0]
    lb_rows = [lb_ref[i:i + 1, :] for i in range(n_lb)]
    lb_max = functools.reduce(jnp.maximum, lb_rows)
    lb_exp = [jnp.exp(r - lb_max) for r in lb_rows]
    lb = sum(lb_exp[:layer + 1]) / sum(lb_exp)
    onorm = on_ref[...]
    tri = tri_ref[...]
    sizes = _level_sizes(ch)
    lane8 = lax.broadcasted_iota(I32, (SUBLANES, LANES), 1)
    row8 = lax.broadcasted_iota(I32, (SUBLANES, LANES), 0)
    tb = pl.program_id(2)

    n_rows = gb * ch
    row_id = lax.broadcasted_iota(I32, (n_rows, LANES), 0)

    def stack(fn):
        parts = [fn(g) for g in range(gb)]
        return parts[0] if gb == 1 else jnp.concatenate(parts, axis=0)

    def chunk(hh, ci):
        cols = slice(hh * LANES, (hh + 1) * LANES)
        t0 = pl.multiple_of(ci * ch, ch)
        rows = pl.ds(t0, ch)
        lb_h = lb[:, cols]
        q = _silu(stack(lambda g: q_ref[g, rows, cols]))
        fg = lb_h + (1.0 - lb_h) * jax.nn.sigmoid(stack(lambda g: f_ref[g, rows, cols]))
        k = 1.0 - fg
        lf = jnp.log(fg)
        v = stack(lambda g: i_ref[g, rows, cols])
        if valid < ch:
            live = (row_id & (ch - 1)) < valid
            lf = jnp.where(live, lf, 0.0)
            k = jnp.where(live, k, 0.0)
        b2 = _dot_exact_lhs(tri, lf) * LOG2E
        b_scr[hh] = b2
        k_scr[hh] = k
        b_end_rows = [b_scr[hh, pl.ds(g * ch + ch - 1, 1), :] for g in range(gb)]
        b_end = stack(lambda g: jnp.broadcast_to(b_end_rows[g], (ch, LANES)))

        qe = (q * jnp.exp2(b2)).astype(BF16)
        o = stack(lambda g: _dot_nt(qe[g * ch:(g + 1) * ch], st_ref[g * hb + hh].astype(BF16)))

        scores = jnp.zeros((n_rows, n_rows), F32)
        for lvl, sz in enumerate(sizes):
            pieces = []
            for m in range(n_rows // (2 * sz)):
                r = b_scr[hh, pl.ds(2 * sz * m + sz - 1, 1), :]
                pieces.append(jnp.broadcast_to(r, (2 * sz, LANES)))
            r_all = pieces[0] if len(pieces) == 1 else jnp.concatenate(pieces, axis=0)
            e = jnp.exp2(-jnp.abs(b2 - r_all))
            s_l = _dot_nt((q * e).astype(BF16), (k * e).astype(BF16))
            scores = scores + masks_ref[lvl] * s_l
        blocks = []
        for jb in range(n_rows // SUBLANES):
            bb = b2[SUBLANES * jb:SUBLANES * (jb + 1)]
            qb = q[SUBLANES * jb:SUBLANES * (jb + 1)]
            acc = jnp.zeros((SUBLANES, LANES), F32)
            for sl in range(SUBLANES):
                s = SUBLANES * jb + sl
                bs = b_scr[hh, pl.ds(s, 1), :]
                ks = k_scr[hh, pl.ds(s, 1), :]
                val = (qb * ks) * jnp.exp2(bb - bs)
                red = jnp.sum(val, axis=1, keepdims=True)
                acc = jnp.where(lane8 == s, red, acc)
            blocks.append(jnp.where(lane8 - SUBLANES * jb <= row8, acc, 0.0))
        scores = scores + jnp.concatenate(blocks, axis=0)
        o = o + _dot(scores.astype(BF16), v.astype(BF16))

        dk = k * jnp.exp2(b_end - b2)
        v_t = v.T.astype(BF16)
        for g in range(gb):
            own = (row_id >= g * ch) & (row_id < (g + 1) * ch)
            c = g * hb + hh
            st_ref[c] = (st_ref[c] * jnp.exp2(b_end_rows[g])
                         + _dot(v_t, jnp.where(own, dk, 0.0).astype(BF16)))

        on = o * lax.rsqrt(jnp.mean(o * o, axis=-1, keepdims=True) + EPS) * onorm[:, cols]
        out = on * _silu(stack(lambda g: g_ref[g, rows, cols]))
        for g in range(gb):
            o_ref[g, rows, cols] = out[g * ch:(g + 1) * ch].astype(o_ref.dtype)

    chains = [(g * hb + hh, g, hh) for g in range(gb) for hh in range(hb)]

    @pl.when(tb == 0)
    def _():
        for c, g, hh in chains:
            if has_init:
                st_ref[c] = s0_ref[g, hh].T
            else:
                st_ref[c] = jnp.zeros((LANES, LANES), F32)

    def step(ci, carry):
        for hh in range(hb):
            chunk(hh, ci)
        return carry

    if nc == 1:
        step(0, 0)
    else:
        lax.fori_loop(0, nc, step, 0)

    @pl.when(tb == pl.num_programs(2) - 1)
    def _():
        for c, g, hh in chains:
            sout_ref[g, hh] = st_ref[c].T


def _hgrn_heads(proj3, hgrn_lb, onorm_g, s0, *, ch, valid, gb, hb, tt, layer):
    bsz, t_len, _ = proj3.shape
    n_heads = onorm_g.shape[0] // HGRN_HEAD_DIM
    n_hg = n_heads // hb
    nc = tt // ch
    n_rows = gb * ch
    assert n_rows == LANES
    has_init = s0 is not None
    w = hb * LANES

    def col(sec):
        return lambda b, h, t: (b, t, sec * n_hg + h)

    in_specs = [pl.BlockSpec((gb, tt, w), col(s)) for s in range(4)]
    in_specs += [
        pl.BlockSpec((hgrn_lb.shape[0], w), lambda b, h, t: (0, h)),
        pl.BlockSpec((1, w), lambda b, h, t: (0, h)),
        pl.BlockSpec((n_rows, n_rows), lambda b, h, t: (0, 0)),
        pl.BlockSpec((max(len(_level_sizes(ch)), 1), n_rows, n_rows), lambda b, h, t: (0, 0, 0)),
    ]
    args = [proj3, proj3, proj3, proj3, hgrn_lb, onorm_g.reshape(1, -1),
            jnp.asarray(_block_tri(gb, ch), dtype=BF16), _level_masks(ch, n_rows)]
    if has_init:
        in_specs.append(pl.BlockSpec((gb, hb, LANES, LANES), lambda b, h, t: (b, h, 0, 0)))
        args.append(s0)
    kern = functools.partial(_hgrn_kernel, ch=ch, valid=valid, nc=nc, gb=gb, hb=hb,
                             has_init=has_init, layer=layer)
    n_chain = gb * hb
    return pl.pallas_call(
        kern,
        grid=(bsz // gb, n_hg, t_len // tt),
        in_specs=in_specs,
        out_specs=[
            pl.BlockSpec((gb, tt, w), lambda b, h, t: (b, t, h)),
            pl.BlockSpec((gb, hb, LANES, LANES), lambda b, h, t: (b, h, 0, 0)),
        ],
        out_shape=[
            jax.ShapeDtypeStruct((bsz, t_len, n_heads * HGRN_HEAD_DIM), BF16),
            jax.ShapeDtypeStruct((bsz, n_heads, LANES, LANES), F32),
        ],
        scratch_shapes=[pltpu.VMEM((n_chain, LANES, LANES), F32),
                        pltpu.VMEM((hb, n_rows, LANES), F32),
                        pltpu.VMEM((hb, n_rows, LANES), F32)],
        compiler_params=_cparams(("arbitrary", "arbitrary", "arbitrary")),
        name="hgrn2_heads",
    )(*args)


def _softplus(x):
    return jnp.maximum(x, 0.0) + jnp.log1p(jnp.exp(-jnp.abs(x)))


def _ssd_kernel(*refs, ch, valid, nc, gb, pb, has_init):
    if has_init:
        (z_ref, x_ref, b_ref, c_ref, dt_ref, cwx_ref, cwb_ref, cwc_ref, cbx_ref, cbb_ref, cbc_ref,
         xj_ref, bias_ref, alog_ref, d_ref, ng_ref, tri_ref, tril_ref, h0_ref, cx0_ref, cb0_ref,
         cc0_ref, y_ref, hout_ref, ht_ref, wx_ref, wb_ref, wc_ref) = refs
    else:
        (z_ref, x_ref, b_ref, c_ref, dt_ref, cwx_ref, cwb_ref, cwc_ref, cbx_ref, cbb_ref, cbc_ref,
         xj_ref, bias_ref, alog_ref, d_ref, ng_ref, tri_ref, tril_ref,
         y_ref, hout_ref, ht_ref, wx_ref, wb_ref, wc_ref) = refs
        h0_ref = cx0_ref = cb0_ref = cc0_ref = None

    tri = tri_ref[...]
    tril = tril_ref[...]
    lane = lax.broadcasted_iota(I32, (LANES, LANES), 1)
    first_head = lane < SSD_HEAD_DIM
    tail = CONV_WIDTH - 1
    tb = pl.program_id(2)

    def conv(raw_ref, win_ref, cw_ref, cb_ref, g, t0):
        win_ref[g, pl.ds(SUBLANES, ch), :] = raw_ref[g, pl.ds(t0, ch), :]
        u = cb_ref[...]
        for j in range(CONV_WIDTH):
            u = u + win_ref[g, pl.ds(SUBLANES - tail + j, ch), :] * cw_ref[j:j + 1, :]
        win_ref[g, pl.ds(0, SUBLANES), :] = win_ref[g, pl.ds(ch, SUBLANES), :]
        return _silu(u)

    n_rows = gb * ch
    w = pb * LANES
    xj = xj_ref[0]
    bias = bias_ref[0]
    a_neg = -jnp.exp(alog_ref[0])
    d_skip = d_ref[0]
    row_id = lax.broadcasted_iota(I32, (n_rows, w), 0)

    def stack(fn):
        parts = [fn(g) for g in range(gb)]
        return parts[0] if gb == 1 else jnp.concatenate(parts, axis=0)

    def step(ci, carry):
        t0 = pl.multiple_of(ci * ch, ch)
        rows = pl.ds(t0, ch)
        xs = stack(lambda g: conv(x_ref, wx_ref, cwx_ref, cbx_ref, g, t0))
        bm = stack(lambda g: conv(b_ref, wb_ref, cwb_ref, cbb_ref, g, t0))
        cm = stack(lambda g: conv(c_ref, wc_ref, cwc_ref, cbc_ref, g, t0))
        dt = _softplus(_dot_exact_rhs(stack(lambda g: dt_ref[g, rows, :]), xj) + bias)
        if valid < ch:
            dt = jnp.where((row_id & (ch - 1)) < valid, dt, 0.0)
        la_cs = _dot_exact_lhs(tri, dt * a_neg)
        xdt = xs * dt
        a_end_rows = [la_cs[g * ch + ch - 1:g * ch + ch, :] for g in range(gb)]
        a_end = stack(lambda g: jnp.broadcast_to(a_end_rows[g], (ch, w)))

        cb = _dot_nt(cm.astype(BF16), bm.astype(BF16))
        acs_t = la_cs.T
        y_parts = []
        for p in range(pb):
            cols = slice(p * LANES, (p + 1) * LANES)
            scores = []
            for hl in range(2):
                at = p * LANES + hl * SSD_HEAD_DIM
                lmat = jnp.exp(jnp.minimum(la_cs[:, at:at + 1] - acs_t[at:at + 1, :], 0.0)) * tril
                scores.append((cb * lmat).astype(BF16))
            xp = xdt[:, cols]
            rhs = jnp.concatenate([jnp.where(first_head, xp, 0.0),
                                   jnp.where(first_head, 0.0, xp)], axis=0).astype(BF16)
            y_parts.append(_dot(jnp.concatenate(scores, axis=1), rhs))
        y = y_parts[0] if pb == 1 else jnp.concatenate(y_parts, axis=1)

        y_off = stack(lambda g: _dot(cm[g * ch:(g + 1) * ch].astype(BF16), ht_ref[g].astype(BF16)))
        y = y + y_off * jnp.exp(la_cs)
        b_t = bm.T.astype(BF16)
        upd = xdt * jnp.exp(a_end - la_cs)
        for g in range(gb):
            own = (row_id >= g * ch) & (row_id < (g + 1) * ch)
            ht_ref[g] = (ht_ref[g] * jnp.exp(a_end_rows[g])
                         + _dot(b_t, jnp.where(own, upd, 0.0).astype(BF16)))

        y = (y + d_skip * xs) * _silu(stack(lambda g: z_ref[g, rows, :]))
        y = y * lax.rsqrt(jnp.mean(y * y, axis=-1, keepdims=True) + EPS) * ng_ref[0]
        for g in range(gb):
            y_ref[g, rows, :] = y[g * ch:(g + 1) * ch].astype(y_ref.dtype)
        return carry

    @pl.when(tb == 0)
    def _():
        for g in range(gb):
            for win_ref, c0_ref in ((wx_ref, cx0_ref), (wb_ref, cb0_ref), (wc_ref, cc0_ref)):
                win_ref[g, pl.ds(0, SUBLANES), :] = jnp.zeros((SUBLANES, win_ref.shape[2]), F32)
                if has_init:
                    win_ref[g, pl.ds(SUBLANES - tail, tail), :] = c0_ref[g]
            for p in range(pb):
                cols = slice(p * LANES, (p + 1) * LANES)
                if has_init:
                    ht_ref[g, :, cols] = h0_ref[g, p].T
                else:
                    ht_ref[g, :, cols] = jnp.zeros((LANES, LANES), F32)

    if nc == 1:
        step(0, 0)
    else:
        lax.fori_loop(0, nc, step, 0)

    @pl.when(tb == pl.num_programs(2) - 1)
    def _():
        for g in range(gb):
            for p in range(pb):
                hout_ref[g, p] = ht_ref[g, :, p * LANES:(p + 1) * LANES].T


def _group_rows(v):
    return jnp.repeat(v.astype(F32), SSD_HEAD_DIM).reshape(SSD_GROUPS, 1, -1)


def _head_select(n_heads):
    import numpy as np
    w = n_heads // SSD_GROUPS * SSD_HEAD_DIM
    h = np.arange(LANES)[None, :, None]
    lane = np.arange(w)[None, None, :]
    g = np.arange(SSD_GROUPS)[:, None, None]
    return jnp.asarray((h == g * (n_heads // SSD_GROUPS) + lane // SSD_HEAD_DIM).astype(np.float32),
                       dtype=BF16)


def _block_tri(gb, ch):
    import numpy as np
    t = np.arange(gb * ch)[:, None]
    s = np.arange(gb * ch)[None, :]
    return ((t // ch == s // ch) & (s <= t)).astype(np.float32)


def _ssd_pairs(proj3, dt3, conv_w, conv_b, dt_bias, a_log, d_skip, norm_g, h0, conv0, *, mix_a,
               mix_b, ch, valid, gb, tt):
    bsz, t_len, _ = proj3.shape
    n_pairs = mix_b // LANES
    pb = n_pairs // SSD_GROUPS
    w = pb * LANES
    nc = tt // ch
    n_rows = gb * ch
    has_init = h0 is not None
    assert (4 * mix_a) % w == 0 and n_rows == LANES
    z0 = 4 * mix_a // w
    x0 = z0 + SSD_GROUPS
    b0 = (4 * mix_a + 2 * mix_b) // LANES
    c0 = b0 + SSD_GROUPS

    seq_w = lambda base: (lambda b, g, t: (b, t, base + g))
    par = lambda base: (lambda b, g, t: (0, base + g))
    const2 = lambda b, g, t: (0, 0)

    in_specs = [
        pl.BlockSpec((gb, tt, w), seq_w(z0)),
        pl.BlockSpec((gb, tt, w), seq_w(x0)),
        pl.BlockSpec((gb, tt, LANES), seq_w(b0)),
        pl.BlockSpec((gb, tt, LANES), seq_w(c0)),
        pl.BlockSpec((gb, tt, LANES), lambda b, g, t: (b, t, 0)),
        pl.BlockSpec((CONV_WIDTH, w), par(0)),
        pl.BlockSpec((CONV_WIDTH, LANES), par(n_pairs)),
        pl.BlockSpec((CONV_WIDTH, LANES), par(n_pairs + SSD_GROUPS)),
        pl.BlockSpec((1, w), par(0)),
        pl.BlockSpec((1, LANES), par(n_pairs)),
        pl.BlockSpec((1, LANES), par(n_pairs + SSD_GROUPS)),
        pl.BlockSpec((1, LANES, w), lambda b, g, t: (g, 0, 0)),
        pl.BlockSpec((1, 1, w), lambda b, g, t: (g, 0, 0)),
        pl.BlockSpec((1, 1, w), lambda b, g, t: (g, 0, 0)),
        pl.BlockSpec((1, 1, w), lambda b, g, t: (g, 0, 0)),
        pl.BlockSpec((1, 1, w), lambda b, g, t: (g, 0, 0)),
        pl.BlockSpec((n_rows, n_rows), const2),
        pl.BlockSpec((n_rows, n_rows), const2),
    ]
    conv_b2 = conv_b.reshape(1, -1)
    block_tri = _block_tri(gb, ch)
    args = [proj3, proj3, proj3, proj3, dt3, conv_w, conv_w, conv_w, conv_b2, conv_b2, conv_b2,
            _head_select(dt_bias.shape[0]), _group_rows(dt_bias), _group_rows(a_log),
            _group_rows(d_skip), norm_g.astype(F32).reshape(SSD_GROUPS, 1, w),
            jnp.asarray(block_tri, dtype=BF16), jnp.asarray(block_tri)]
    if has_init:
        tail = CONV_WIDTH - 1
        in_specs += [
            pl.BlockSpec((gb, pb, LANES, LANES), lambda b, g, t: (b, g, 0, 0)),
            pl.BlockSpec((gb, tail, w), lambda b, g, t: (b, 0, g)),
            pl.BlockSpec((gb, tail, LANES), lambda b, g, t: (b, 0, n_pairs + g)),
            pl.BlockSpec((gb, tail, LANES), lambda b, g, t: (b, 0, n_pairs + SSD_GROUPS + g)),
        ]
        args += [h0, conv0, conv0, conv0]
    kern = functools.partial(_ssd_kernel, ch=ch, valid=valid, nc=nc, gb=gb, pb=pb,
                             has_init=has_init)
    return pl.pallas_call(
        kern,
        grid=(bsz // gb, SSD_GROUPS, t_len // tt),
        in_specs=in_specs,
        out_specs=[
            pl.BlockSpec((gb, tt, w), lambda b, g, t: (b, t, g)),
            pl.BlockSpec((gb, pb, LANES, LANES), lambda b, g, t: (b, g, 0, 0)),
        ],
        out_shape=[
            jax.ShapeDtypeStruct((bsz, t_len, mix_b), BF16),
            jax.ShapeDtypeStruct((bsz, n_pairs, LANES, LANES), F32),
        ],
        scratch_shapes=[pltpu.VMEM((gb, LANES, w), F32),
                        pltpu.VMEM((gb, ch + SUBLANES, w), F32),
                        pltpu.VMEM((gb, ch + SUBLANES, LANES), F32),
                        pltpu.VMEM((gb, ch + SUBLANES, LANES), F32)],
        compiler_params=_cparams(("arbitrary", "arbitrary", "arbitrary")),
        name="ssd_pairs",
    )(*args)


OUT_TM = 512
WEIGHT_STAGE_ROWS = 512
GROUP_LANE0 = N_EXPERTS
NEG_BIG = -1e30
NO_LANE = 4 * LANES


def _first_lane_of(mask, lane):
    return jnp.min(jnp.where(mask, lane, float(NO_LANE)), axis=1, keepdims=True)


def _route(logits):
    lane_i = lax.broadcasted_iota(I32, logits.shape, 1)
    lane = lane_i.astype(F32)
    is_grp = (lane_i >= GROUP_LANE0) & (lane_i < GROUP_LANE0 + N_EXPERT_GROUPS)
    lg = jnp.where(is_grp, logits, NEG_BIG)
    g_max = jnp.max(lg, axis=1, keepdims=True)
    g_sum = jnp.sum(jnp.where(is_grp, jnp.exp(lg - g_max), 0.0), axis=1, keepdims=True)
    g_idx = _first_lane_of(lg == g_max, lane) - GROUP_LANE0
    gp_top = 1.0 / g_sum
    lane_grp = jnp.right_shift(lane_i, 3).astype(F32)
    in_grp = (lane_i < N_EXPERTS) & (lane_grp == g_idx)
    le = jnp.where(in_grp, logits, NEG_BIG)
    e_max = jnp.max(le, axis=1, keepdims=True)
    e_exp = jnp.where(in_grp, jnp.exp(le - e_max), 0.0)
    ep = e_exp / jnp.sum(e_exp, axis=1, keepdims=True)
    ep = jnp.where(in_grp, ep, -1.0)
    p1 = jnp.max(ep, axis=1, keepdims=True)
    i1 = _first_lane_of(ep == p1, lane)
    ep2 = jnp.where(lane == i1, -1.0, ep)
    p2 = jnp.max(ep2, axis=1, keepdims=True)
    i2 = _first_lane_of(ep2 == p2, lane)
    den = p1 + p2
    return i1, i2, gp_top * p1 / den, gp_top * p2 / den


def _outproj_kernel(o_ref, y_ref, x_ref, g1_ref, sc_ref, sh_ref, ng_ref, w_hbm_ref, wr_ref, br_ref,
                    tri_ref, cnt0_ref,
                    x1_ref, hn_ref, ids_ref, wts_ref, rank_ref, cnt_ref,
                    wbf_ref, stage_ref, sem, cnts_ref):
    i = pl.program_id(0)
    tm, d = hn_ref.shape
    mix_a = o_ref.shape[1]
    tn = PROJ_TN
    n_col = d // tn
    gshape = (x_ref.shape[0], x_ref.shape[1], tn)

    @pl.when(i == 0)
    def _():
        cnts_ref[...] = cnt0_ref[...]
        rows = stage_ref.shape[0]
        for k in range(d // rows):
            cp = pltpu.make_async_copy(w_hbm_ref.at[pl.ds(k * rows, rows), :], stage_ref, sem)
            cp.start()
            cp.wait()
            wbf_ref[k * rows:(k + 1) * rows, :] = stage_ref[...].astype(BF16)

    o = o_ref[...]
    y = y_ref[...]
    ssq = jnp.zeros((tm, 1), F32)
    for c in range(n_col):
        cols = slice(c * tn, (c + 1) * tn)
        acc = _dot(o, wbf_ref[0:mix_a, cols]) + _dot(y, wbf_ref[mix_a:d, cols])
        x1 = x_ref[:, :, cols] + g1_ref[:, :, cols] * acc.reshape(gshape)
        x1_ref[:, :, cols] = x1
        x1 = x1.reshape(tm, tn)
        ssq = ssq + jnp.sum(x1 * x1, axis=-1, keepdims=True)
    rs = lax.rsqrt(ssq / d + EPS)
    logits = jnp.zeros((tm, LANES), F32) + br_ref[...]
    for c in range(n_col):
        cols = slice(c * tn, (c + 1) * tn)
        hn = (x1_ref[:, :, cols].reshape(tm, tn) * rs * ng_ref[:, cols]).reshape(gshape)
        hn = (hn * (1.0 + sc_ref[:, :, cols]) + sh_ref[:, :, cols]).reshape(tm, tn)
        hn_ref[:, cols] = hn
        logits = logits + _dot_hp(hn, wr_ref[cols, :])
    i1, i2, w1, w2 = _route(logits)
    lane = lax.broadcasted_iota(I32, (tm, LANES), 1)
    hit1 = lane.astype(F32) == i1
    hit2 = lane.astype(F32) == i2
    onehot = jnp.where(hit1 | hit2, 1.0, 0.0).astype(BF16)
    before = _dot(tri_ref[...], onehot) + cnts_ref[0:1, :]
    r1 = jnp.sum(jnp.where(hit1, before, 0.0), axis=1, keepdims=True)
    r2 = jnp.sum(jnp.where(hit2, before, 0.0), axis=1, keepdims=True)
    ids_ref[...] = jnp.where(lane == 0, i1, jnp.where(lane == 1, i2, 0.0)).astype(I32)
    wts_ref[...] = jnp.where(lane == 0, w1, jnp.where(lane == 1, w2, 0.0))
    rank_ref[...] = jnp.where(lane == 0, r1, jnp.where(lane == 1, r2, 0.0)).astype(I32)
    total = cnts_ref[0:1, :] + jnp.sum(onehot.astype(F32), axis=0, keepdims=True)
    cnts_ref[...] = jnp.broadcast_to(total, cnts_ref.shape)
    cnt_ref[...] = jnp.broadcast_to(total, cnt_ref.shape)


def _strict_lower(n):
    import numpy as np
    t = np.arange(n)[:, None]
    s = np.arange(n)[None, :]
    return jnp.asarray((s < t).astype(np.float32), dtype=BF16)


def _out_projection(o3, y3, x3, mod3, norm_g, w_out, w_router, b_router, cnt0, group_rows):
    bsz, t_len, d = x3.shape
    g_blk, r_blk = group_rows
    tm = g_blk * r_blk
    mix_a = o3.shape[-1]
    mix_b = y3.shape[-1]
    n_tok = bsz * t_len
    tiles_per_seq = max(t_len // r_blk, 1)

    xmap = lambda i: (i // tiles_per_seq, i % tiles_per_seq, 0)
    modmap = lambda s: (lambda i: (i // tiles_per_seq, 0, s))
    row = lambda i: (i, 0)
    const2 = lambda i: (0, 0)
    return pl.pallas_call(
        _outproj_kernel,
        grid=(n_tok // tm,),
        in_specs=[
            pl.BlockSpec((tm, mix_a), row),
            pl.BlockSpec((tm, mix_b), row),
            pl.BlockSpec((g_blk, r_blk, d), xmap),
            pl.BlockSpec((g_blk, 1, d), modmap(2)),
            pl.BlockSpec((g_blk, 1, d), modmap(4)),
            pl.BlockSpec((g_blk, 1, d), modmap(3)),
            pl.BlockSpec((1, d), const2),
            pl.BlockSpec(memory_space=pl.ANY),
            pl.BlockSpec((d, LANES), const2),
            pl.BlockSpec((1, LANES), const2),
            pl.BlockSpec((tm, tm), const2),
            pl.BlockSpec((SUBLANES, LANES), const2),
        ],
        out_specs=[
            pl.BlockSpec((g_blk, r_blk, d), xmap),
            pl.BlockSpec((tm, d), row),
            pl.BlockSpec((tm, LANES), row),
            pl.BlockSpec((tm, LANES), row),
            pl.BlockSpec((tm, LANES), row),
            pl.BlockSpec((SUBLANES, LANES), const2),
        ],
        out_shape=[
            jax.ShapeDtypeStruct((bsz, t_len, d), F32),
            jax.ShapeDtypeStruct((n_tok, d), F32),
            jax.ShapeDtypeStruct((n_tok, LANES), I32),
            jax.ShapeDtypeStruct((n_tok, LANES), F32),
            jax.ShapeDtypeStruct((n_tok, LANES), I32),
            jax.ShapeDtypeStruct((SUBLANES, LANES), F32),
        ],
        scratch_shapes=[pltpu.VMEM((d, d), BF16), pltpu.VMEM((WEIGHT_STAGE_ROWS, d), F32),
                        pltpu.SemaphoreType.DMA(()), pltpu.VMEM((SUBLANES, LANES), F32)],
        compiler_params=_cparams(("arbitrary",)),
        name="out_proj_router",
    )(o3.reshape(n_tok, mix_a), y3.reshape(n_tok, mix_b), x3, mod3, mod3, mod3,
      norm_g.reshape(1, d), w_out, w_router, b_router, _strict_lower(tm), cnt0)


def _positions_kernel(ids_ref, rank_ref, cnt_ref, pos_ref, te_ref, nt_ref, *, block):
    lane8 = lax.broadcasted_iota(I32, (SUBLANES, LANES), 1)
    cnt = cnt_ref[...]
    tiles = jnp.floor((cnt + (MOE_TM - 1)) * (1.0 / MOE_TM))
    tiles = jnp.where(lane8 < N_EXPERTS, tiles, 0.0)
    incl = tiles
    shift = 1
    while shift < N_EXPERTS:
        incl = incl + jnp.where(lane8 >= shift, pltpu.roll(incl, shift, axis=1), 0.0)
        shift *= 2
    offs = ((incl - tiles) * MOE_TM)[0:1, :]
    nt_ref[...] = jnp.sum(tiles, axis=1, keepdims=True).astype(I32) + jnp.zeros(nt_ref.shape, I32)

    incl_col = jnp.broadcast_to(incl[0:1, :], (LANES, LANES)).T
    e_row = lax.broadcasted_iota(I32, (LANES, LANES), 0)
    i_lane = lax.broadcasted_iota(I32, (LANES, LANES), 1).astype(F32)
    done = jnp.where((incl_col <= i_lane) & (e_row < N_EXPERTS), 1.0, 0.0)
    te = jnp.minimum(jnp.sum(done, axis=0, keepdims=True), N_EXPERTS - 1.0)
    te_ref[...] = te.astype(I32) + jnp.zeros(te_ref.shape, I32)

    lane = lax.broadcasted_iota(I32, (block, LANES), 1)

    def body(bi, c):
        r0 = pl.multiple_of(bi * block, block)
        ids = ids_ref[pl.ds(r0, block), :]
        rank = rank_ref[pl.ds(r0, block), :]
        out = jnp.zeros((block, LANES), I32)
        for k in range(TOP_K):
            off = jnp.sum(jnp.where(lane == ids[:, k:k + 1], offs, 0.0), axis=1, keepdims=True)
            out = jnp.where(lane == k, off.astype(I32) + rank[:, k:k + 1], out)
        pos_ref[pl.ds(r0, block), :] = out
        return c

    lax.fori_loop(0, ids_ref.shape[0] // block, body, 0)


def _positions(ids, rank, cnt):
    n_tok = ids.shape[0]
    block = 512
    kern = functools.partial(_positions_kernel, block=block)
    full = lambda s: pl.BlockSpec(s, lambda: tuple(0 for _ in s))
    return pl.pallas_call(
        kern,
        in_specs=[full((n_tok, LANES)), full((n_tok, LANES)), full((SUBLANES, LANES))],
        out_specs=[full((n_tok, LANES)), full((SUBLANES, LANES)), full((SUBLANES, LANES))],
        out_shape=[jax.ShapeDtypeStruct((n_tok, LANES), I32),
                   jax.ShapeDtypeStruct((SUBLANES, LANES), I32),
                   jax.ShapeDtypeStruct((SUBLANES, LANES), I32)],
        compiler_params=pltpu.CompilerParams(vmem_limit_bytes=VMEM_LIMIT),
        name="route_positions",
    )(ids, rank, cnt)


def _row_copy(src_ref, src_row, dst_ref, dst_row, sem):
    return pltpu.make_async_copy(src_ref.at[pl.ds(src_row, 1), :], dst_ref.at[pl.ds(dst_row, 1), :],
                                 sem)


def _dispatch_kernel(pos_ref, hn_ref, xg_in_ref, xg_ref, sem):
    del xg_in_ref
    rows = hn_ref.shape[0]
    base = pl.program_id(0) * rows

    def issue(r, c):
        for k in range(TOP_K):
            _row_copy(hn_ref, r, xg_ref, pos_ref[(base + r) * TOP_K + k], sem).start(priority=k)
        return c

    def drain(r, c):
        for k in range(TOP_K):
            _row_copy(hn_ref, r, xg_ref, 0, sem).wait()
        return c

    lax.fori_loop(0, rows, issue, 0, unroll=8)
    lax.fori_loop(0, rows, drain, 0, unroll=8)


def _dispatch(pos_flat, hn, xg):
    n_tok, d = hn.shape
    grid_spec = pltpu.PrefetchScalarGridSpec(
        num_scalar_prefetch=1,
        grid=(n_tok // ROW_DMA_TILE,),
        in_specs=[pl.BlockSpec((ROW_DMA_TILE, d), lambda i, pos: (i, 0)),
                  pl.BlockSpec(memory_space=pl.ANY)],
        out_specs=pl.BlockSpec(memory_space=pl.ANY),
        scratch_shapes=[pltpu.SemaphoreType.DMA(())],
    )
    return pl.pallas_call(
        _dispatch_kernel,
        grid_spec=grid_spec,
        out_shape=jax.ShapeDtypeStruct(xg.shape, xg.dtype),
        input_output_aliases={2: 0},
        compiler_params=_cparams(("arbitrary",)),
        name="moe_dispatch",
    )(pos_flat, hn, xg)


def _weight_copies(w_hbm_refs, expert, wbuf_ref, slot, sem_ref):
    return [pltpu.make_async_copy(w_ref.at[expert], wbuf_ref.at[slot, i], sem_ref.at[slot])
            for i, w_ref in enumerate(w_hbm_refs)]


def _expert_weights_slot(te_ref, nt, t, w_hbm_refs, wbuf_ref, sem_ref, slot_ref):
    expert = te_ref[t]

    @pl.when(t == 0)
    def _():
        slot_ref[0] = 0
        for cp in _weight_copies(w_hbm_refs, expert, wbuf_ref, 0, sem_ref):
            cp.start()

    @pl.when((t == 0) | (expert != te_ref[jnp.maximum(t - 1, 0)]))
    def _():
        @pl.when(t > 0)
        def _():
            slot_ref[0] = 1 - slot_ref[0]

        slot = slot_ref[0]
        for cp in _weight_copies(w_hbm_refs, expert, wbuf_ref, slot, sem_ref):
            cp.wait()
        nxt = lax.while_loop(lambda u: (u < nt) & (te_ref[jnp.minimum(u, nt - 1)] == expert),
                             lambda u: u + 1, t + 1)

        @pl.when(nxt < nt)
        def _():
            for cp in _weight_copies(w_hbm_refs, te_ref[jnp.minimum(nxt, nt - 1)], wbuf_ref,
                                     1 - slot, sem_ref):
                cp.start()

    return slot_ref[0]


def _moe_up_kernel(te_ref, nt_ref, x_ref, w1_ref, w3_ref, act_ref, wbuf_ref, sem_ref, slot_ref):
    t = pl.program_id(0)
    nt = nt_ref[0]

    @pl.when(t < nt)
    def _():
        slot = _expert_weights_slot(te_ref, nt, t, (w1_ref, w3_ref), wbuf_ref, sem_ref, slot_ref)
        xb = x_ref[...].astype(BF16)
        a = _dot(xb, wbuf_ref[slot, 0].astype(BF16))
        b = _dot(xb, wbuf_ref[slot, 1].astype(BF16))
        act_ref[...] = (_silu(a) * b).astype(BF16)

    @pl.when(t >= nt)
    def _():
        act_ref[...] = jnp.zeros(act_ref.shape, act_ref.dtype)


def _moe_down_kernel(te_ref, nt_ref, act_ref, w2_ref, out_ref, wbuf_ref, sem_ref, slot_ref):
    t = pl.program_id(0)
    nt = nt_ref[0]

    @pl.when(t < nt)
    def _():
        slot = _expert_weights_slot(te_ref, nt, t, (w2_ref,), wbuf_ref, sem_ref, slot_ref)
        out_ref[...] = _dot(act_ref[...], wbuf_ref[slot, 0].astype(BF16))

    @pl.when(t >= nt)
    def _():
        out_ref[...] = jnp.zeros(out_ref.shape, out_ref.dtype)


def _moe_experts(te, nt, xg, w1, w3, w2):
    n_rows, d = xg.shape
    d_exp = w1.shape[-1]
    n_tiles = n_rows // MOE_TM
    live_tile = lambda t, te_r, nt_r: (jnp.minimum(t, nt_r[0] - 1), 0)
    any_spec = pl.BlockSpec(memory_space=pl.ANY)

    act = pl.pallas_call(
        _moe_up_kernel,
        grid_spec=pltpu.PrefetchScalarGridSpec(
            num_scalar_prefetch=2,
            grid=(n_tiles,),
            in_specs=[pl.BlockSpec((MOE_TM, d), live_tile), any_spec, any_spec],
            out_specs=pl.BlockSpec((MOE_TM, d_exp), lambda t, te_r, nt_r: (t, 0)),
            scratch_shapes=[pltpu.VMEM((2, 2, d, d_exp), F32), pltpu.SemaphoreType.DMA((2,)),
                            pltpu.SMEM((1,), I32)],
        ),
        out_shape=jax.ShapeDtypeStruct((n_rows, d_exp), BF16),
        compiler_params=_cparams(("arbitrary",)),
        name="moe_up",
    )(te, nt, xg, w1, w3)

    return pl.pallas_call(
        _moe_down_kernel,
        grid_spec=pltpu.PrefetchScalarGridSpec(
            num_scalar_prefetch=2,
            grid=(n_tiles,),
            in_specs=[pl.BlockSpec((MOE_TM, d_exp), live_tile), any_spec],
            out_specs=pl.BlockSpec((MOE_TM, d), lambda t, te_r, nt_r: (t, 0)),
            scratch_shapes=[pltpu.VMEM((2, 1, d_exp, d), F32), pltpu.SemaphoreType.DMA((2,)),
                            pltpu.SMEM((1,), I32)],
        ),
        out_shape=jax.ShapeDtypeStruct((n_rows, d), F32),
        compiler_params=_cparams(("arbitrary",)),
        name="moe_down",
    )(te, nt, act, w2)


def _combine_kernel(pos_ref, x1_ref, g2_ref, wts_ref, fg_ref, eo_ref, y_ref, buf_ref, sem):
    rows = wts_ref.shape[0]
    base = pl.program_id(0) * rows

    def issue(r, c):
        for k in range(TOP_K):
            _row_copy(eo_ref, pos_ref[(base + r) * TOP_K + k], buf_ref.at[k], r, sem).start(
                priority=k)
        return c

    def drain(r, c):
        for k in range(TOP_K):
            _row_copy(eo_ref, 0, buf_ref.at[k], r, sem).wait()
        return c

    lax.fori_loop(0, rows, issue, 0, unroll=8)
    lax.fori_loop(0, rows, drain, 0, unroll=8)
    w = wts_ref[...]
    moe = w[:, 0:1] * buf_ref[0] + w[:, 1:2] * buf_ref[1]
    x2 = x1_ref[...] + g2_ref[...] * moe.reshape(x1_ref.shape)
    ms = jnp.mean(x2 * x2, axis=-1, keepdims=True)
    y_ref[...] = x2 * lax.rsqrt(ms + EPS) * fg_ref[...]


def _combine(pos_flat, x1, mod3, wts, final_g, expert_out, group_rows):
    bsz, t_len, d = x1.shape
    g_blk, r_blk = group_rows
    tm = g_blk * r_blk
    tiles_per_seq = max(t_len // r_blk, 1)

    def xmap(i, pos):
        return (i // tiles_per_seq, i % tiles_per_seq, 0)

    return pl.pallas_call(
        _combine_kernel,
        grid_spec=pltpu.PrefetchScalarGridSpec(
            num_scalar_prefetch=1,
            grid=((bsz * t_len) // tm,),
            in_specs=[
                pl.BlockSpec((g_blk, r_blk, d), xmap),
                pl.BlockSpec((g_blk, 1, d), lambda i, pos: (i // tiles_per_seq, 0, 5)),
                pl.BlockSpec((tm, LANES), lambda i, pos: (i, 0)),
                pl.BlockSpec((1, d), lambda i, pos: (0, 0)),
                pl.BlockSpec(memory_space=pl.ANY),
            ],
            out_specs=pl.BlockSpec((g_blk, r_blk, d), xmap),
            scratch_shapes=[pltpu.VMEM((TOP_K, tm, d), F32), pltpu.SemaphoreType.DMA(())],
        ),
        out_shape=jax.ShapeDtypeStruct((bsz, t_len, d), F32),
        compiler_params=_cparams(("arbitrary",)),
        name="moe_combine",
    )(pos_flat, x1, mod3, wts, final_g.reshape(1, d), expert_out)


def kernel(x_prompt, x_sample, state_hgrn, state_ssm, state_conv, c_prompt, c_sample, ada_w, ada_b,
           norm1_g, norm2_g, w_in, hgrn_lb, hgrn_onorm_g, conv_w, conv_b, dt_bias, a_log, d_skip,
           ssm_norm_g, w_out, w_grp, b_grp, w_rt, b_rt, w1, w3, w2, final_g):
    depth = w_in.shape[0]
    assert depth == 1, "single-layer trunk"
    layer = 0
    bp, t_p, d = x_prompt.shape
    bs, t_s, _ = x_sample.shape
    mix_a = hgrn_onorm_g.shape[1]
    mix_b = ssm_norm_g.shape[1]
    conv_dim = conv_w.shape[2]
    n_main = 4 * mix_a + mix_b + conv_dim
    n_ssd_heads = dt_bias.shape[1]
    n_pairs = mix_b // LANES
    xbc0 = 4 * mix_a + mix_b
    tail = CONV_WIDTH - 1
    assert t_s <= SUBLANES and t_s >= tail and t_p % PROMPT_CHUNK == 0

    xs_pad = jnp.pad(x_sample, ((0, 0), (0, SUBLANES - t_s), (0, 0)))
    n_c = bp + bs
    c_rows = -(-n_c // SUBLANES) * SUBLANES
    c_all = jnp.pad(jnp.concatenate([c_prompt, c_sample], axis=0), ((0, c_rows - n_c), (0, 0)))

    mod = _modulation(c_all, ada_w[layer], ada_b[layer])
    mod_p = mod[:bp].reshape(bp, 1, -1)
    mod_s = mod[bp:n_c].reshape(bs, 1, -1)
    w_in_t = jnp.swapaxes(w_in[layer], 0, 1)
    w_dt_t = jnp.pad(w_in_t[n_main:], ((0, LANES - n_ssd_heads), (0, 0)))
    w_router = jnp.pad(jnp.concatenate([w_rt[layer], w_grp[layer]], axis=1),
                       ((0, 0), (0, LANES - N_EXPERTS - N_EXPERT_GROUPS)))
    b_router = jnp.pad(jnp.concatenate([b_rt[layer], b_grp[layer]]),
                       (0, LANES - N_EXPERTS - N_EXPERT_GROUPS)).reshape(1, LANES)

    groups = (
        (x_prompt, mod_p, None, None, None, t_p, PROMPT_CHUNK, PROMPT_CHUNK, MIXER_ROWS,
         (LANES // PROMPT_CHUNK, 4), LANES // PROMPT_CHUNK, 1, (1, OUT_TM)),
        (xs_pad, mod_s, state_hgrn[layer], state_ssm[layer].reshape(bs, n_pairs, LANES, LANES),
         state_conv[layer], t_s, SUBLANES, t_s, SUBLANES,
         (LANES // SUBLANES, 2), LANES // SUBLANES, PROJ_SEQS, (OUT_TM // SUBLANES, SUBLANES)),
    )

    cnt = jnp.zeros((SUBLANES, LANES), F32)
    per_group = []
    for x3, mod3, s0, h0, c0, t_real, ch, valid, tt, (hg_gb, hg_hb), ssd_gb, proj_gr, out_gr in groups:
        bsz, t_len, _ = x3.shape
        proj, dt = _in_projection(x3, mod3, norm1_g[layer], w_in_t, w_dt_t, n_main, proj_gr)
        proj3 = proj.reshape(bsz, t_len, n_main)
        dt3 = dt.reshape(bsz, t_len, LANES)
        o3, s_new = _hgrn_heads(proj3, hgrn_lb, hgrn_onorm_g[layer], s0, ch=ch, valid=valid,
                                gb=hg_gb, hb=hg_hb, tt=tt, layer=layer)
        y3, h_new = _ssd_pairs(proj3, dt3, conv_w[layer], conv_b[layer], dt_bias[layer],
                               a_log[layer], d_skip[layer], ssm_norm_g[layer], h0, c0, mix_a=mix_a,
                               mix_b=mix_b, ch=ch, valid=valid, gb=ssd_gb, tt=tt)
        conv_new = proj3[:, t_real - tail:t_real, xbc0:xbc0 + conv_dim]
        x1, hn, ids, wts, rank, cnt = _out_projection(
            o3, y3, x3, mod3, norm2_g[layer], w_out[layer], w_router, b_router, cnt, out_gr)
        per_group.append(dict(x1=x1, hn=hn, ids=ids, wts=wts, rank=rank, mod3=mod3, out_gr=out_gr,
                              s_new=s_new, h_new=h_new, conv_new=conv_new))

    ids_all = jnp.concatenate([g["ids"] for g in per_group], axis=0)
    rank_all = jnp.concatenate([g["rank"] for g in per_group], axis=0)
    n_tok = ids_all.shape[0]
    pos, te, nt = _positions(ids_all, rank_all, cnt)
    pos_flat = pos[:, :TOP_K].reshape(-1)
    n_tiles = -(-(n_tok * TOP_K + N_EXPERTS * (MOE_TM - 1)) // MOE_TM)
    assert n_tiles <= LANES
    te_vec = te[0, :n_tiles]
    nt_vec = nt[0, :1]

    xg = jnp.zeros((n_tiles * MOE_TM, d), F32)
    row0 = 0
    for g in per_group:
        rows = g["hn"].shape[0]
        g["pos"] = pos_flat[row0 * TOP_K:(row0 + rows) * TOP_K]
        xg = _dispatch(g["pos"], g["hn"], xg)
        row0 += rows
    expert_out = _moe_experts(te_vec, nt_vec, xg, w1[layer], w3[layer], w2[layer])
    ys = [_combine(g["pos"], g["x1"], g["mod3"], g["wts"], final_g, expert_out, g["out_gr"])
          for g in per_group]

    gp, gs = per_group
    return (
        ys[0],
        ys[1][:, :t_s],
        gp["s_new"][None],
        gp["h_new"].reshape(1, bp, n_ssd_heads, SSD_HEAD_DIM, SSD_STATE),
        gp["conv_new"][None],
        gs["s_new"][None],
        gs["h_new"].reshape(1, bs, n_ssd_heads, SSD_HEAD_DIM, SSD_STATE),
        gs["conv_new"][None],
    )
```

```python
import functools

import jax
import jax.numpy as jnp
from jax import lax
from jax.experimental import pallas as pl
from jax.experimental.pallas import tpu as pltpu

F32 = jnp.float32
BF16 = jnp.bfloat16
I32 = jnp.int32
EPS = 1e-6
LOG2E = 1.4426950408889634

LANES = 128
SUBLANES = 8
VMEM_LIMIT = 56 * 1024 * 1024

HGRN_HEAD_DIM = 128
SSD_HEAD_DIM = 64
SSD_GROUPS = 2
SSD_STATE = 128
CONV_WIDTH = 4
N_EXPERT_GROUPS = 4
EXPERTS_PER_GROUP = 8
N_EXPERTS = N_EXPERT_GROUPS * EXPERTS_PER_GROUP
TOP_K = 2

PROJ_SEQS = 128
NORM_ROWS = 256
PROJ_TN = 512
MOE_TM = 256
PROMPT_CHUNK = 64
MIXER_ROWS = 512
ROW_DMA_TILE = 256


def _cparams(sem):
    return pltpu.CompilerParams(dimension_semantics=sem, vmem_limit_bytes=VMEM_LIMIT)


def _dot(a, b):
    return jnp.dot(a, b, preferred_element_type=F32)


def _dot_nt(a, b):
    return lax.dot_general(a, b, (((1,), (1,)), ((), ())), preferred_element_type=F32)


def _split2(x):
    hi = x.astype(BF16)
    lo = (x - hi.astype(F32)).astype(BF16)
    return hi, lo


def _split3(x):
    hi = x.astype(BF16)
    r = x - hi.astype(F32)
    mid = r.astype(BF16)
    lo = (r - mid.astype(F32)).astype(BF16)
    return hi, mid, lo


def _dot_exact_lhs(m_bf16, x):
    hi, mid, lo = _split3(x)
    return _dot(m_bf16, hi) + _dot(m_bf16, mid) + _dot(m_bf16, lo)


def _dot_exact_rhs(x, m_bf16):
    hi, mid, lo = _split3(x)
    return _dot(hi, m_bf16) + _dot(mid, m_bf16) + _dot(lo, m_bf16)


def _dot_hp(a, w):
    ah, al = _split2(a)
    wh, wl = _split2(w)
    return _dot(ah, wh) + _dot(al, wh) + _dot(ah, wl)


def _silu(x):
    return x * jax.nn.sigmoid(x)


def _mod_kernel(c_ref, w_ref, b_ref, o_ref):
    a = _silu(c_ref[...])
    o_ref[...] = _dot_hp(a, w_ref[...]) + b_ref[...]


def _modulation(c_all, ada_w, ada_b):
    rows, d = c_all.shape
    n_out = ada_w.shape[1]
    tn = 1024
    return pl.pallas_call(
        _mod_kernel,
        grid=(n_out // tn,),
        in_specs=[
            pl.BlockSpec((rows, d), lambda j: (0, 0)),
            pl.BlockSpec((d, tn), lambda j: (0, j)),
            pl.BlockSpec((1, tn), lambda j: (0, j)),
        ],
        out_specs=pl.BlockSpec((rows, tn), lambda j: (0, j)),
        out_shape=jax.ShapeDtypeStruct((rows, n_out), F32),
        compiler_params=_cparams(("arbitrary",)),
        name="adaln_mod",
    )(c_all, ada_w, ada_b.reshape(1, n_out))


def _proj_kernel(x_hbm_ref, sc_ref, sh_ref, g_ref, w_ref, wdt_ref, o_ref, dt_ref, hn_ref, xs_ref,
                 sem):
    i = pl.program_id(0)
    j = pl.program_id(1)
    g_blk, r_blk, d = xs_ref.shape

    def x_copy(tile):
        return pltpu.make_async_copy(x_hbm_ref.at[pl.ds(tile * g_blk, g_blk)], xs_ref, sem)

    @pl.when(j == 0)
    def _():
        @pl.when(i == 0)
        def _():
            x_copy(0).start()

        x_copy(i).wait()
        wh, wl = _split2(wdt_ref[...])
        n_rows = g_blk * r_blk
        step = min(n_rows, NORM_ROWS)
        for c in range(n_rows // step):
            if g_blk == 1:
                x = xs_ref[:, c * step:(c + 1) * step, :]
                sc, sh = sc_ref[...], sh_ref[...]
            else:
                gs = slice(c * step // r_blk, (c + 1) * step // r_blk)
                x, sc, sh = xs_ref[gs], sc_ref[gs], sh_ref[gs]
            ms = jnp.mean(x * x, axis=-1, keepdims=True)
            hn = ((x * lax.rsqrt(ms + EPS)) * g_ref[...] * (1.0 + sc) + sh).reshape(step, d)
            rows = slice(c * step, (c + 1) * step)
            hn_ref[rows, :] = hn.astype(BF16)
            ah, al = _split2(hn)
            dt_ref[rows, :] = _dot_nt(ah, wh) + _dot_nt(al, wh) + _dot_nt(ah, wl)

        @pl.when(i + 1 < pl.num_programs(0))
        def _():
            x_copy(i + 1).start()

    o_ref[...] = _dot_nt(hn_ref[...], w_ref[...].astype(BF16))


def _in_projection(x3, mod3, norm_g, w_t, w_dt_t, n_main, g_blk):
    bsz, t_len, d = x3.shape
    tm = g_blk * t_len
    n_row_tiles = bsz // g_blk

    def mod_map(sec):
        return lambda i, j: (i, 0, sec)

    return pl.pallas_call(
        _proj_kernel,
        grid=(n_row_tiles, n_main // PROJ_TN),
        in_specs=[
            pl.BlockSpec(memory_space=pl.ANY),
            pl.BlockSpec((g_blk, 1, d), mod_map(1)),
            pl.BlockSpec((g_blk, 1, d), mod_map(0)),
            pl.BlockSpec((1, d), lambda i, j: (0, 0)),
            pl.BlockSpec((PROJ_TN, d), lambda i, j: (j, 0)),
            pl.BlockSpec((LANES, d), lambda i, j: (0, 0)),
        ],
        out_specs=[
            pl.BlockSpec((tm, PROJ_TN), lambda i, j: (i, j)),
            pl.BlockSpec((tm, LANES), lambda i, j: (i, 0)),
        ],
        out_shape=[
            jax.ShapeDtypeStruct((bsz * t_len, n_main), F32),
            jax.ShapeDtypeStruct((bsz * t_len, LANES), F32),
        ],
        scratch_shapes=[pltpu.VMEM((tm, d), BF16), pltpu.VMEM((g_blk, t_len, d), F32),
                        pltpu.SemaphoreType.DMA(())],
        compiler_params=_cparams(("arbitrary", "arbitrary")),
        name="in_proj",
    )(x3, mod3, mod3, norm_g.reshape(1, d), w_t, w_dt_t)


def _level_sizes(ch):
    sizes = []
    sz = SUBLANES
    while 2 * sz <= ch:
        sizes.append(sz)
        sz *= 2
    return sizes


def _level_masks(ch, n_rows):
    import numpy as np
    t = np.arange(n_rows)[:, None]
    s = np.arange(n_rows)[None, :]
    out = []
    for sz in _level_sizes(ch):
        m = (t // (2 * sz) == s // (2 * sz)) & (t % (2 * sz) >= sz) & (s % (2 * sz) < sz)
        out.append(m.astype(np.float32))
    if not out:
        out.append(np.zeros((n_rows, n_rows), np.float32))
    return jnp.asarray(np.stack(out))


def _hgrn_kernel(*refs, ch, valid, nc, gb, hb, has_init, layer):
    if has_init:
        (q_ref, f_ref, i_ref, g_ref, lb_ref, on_ref, tri_ref, masks_ref, s0_ref,
         o_ref, sout_ref, st_ref, b_scr, k_scr) = refs
    else:
        (q_ref, f_ref, i_ref, g_ref, lb_ref, on_ref, tri_ref, masks_ref,
         o_ref, sout_ref, st_ref, b_scr, k_scr) = refs
        s0_ref = None

    n_lb = lb_ref.shape[0]
    lb_rows = [lb_ref[i:i + 1, :] for i in range(n_lb)]
    lb_max = functools.reduce(jnp.maximum, lb_rows)
    lb_exp = [jnp.exp(r - lb_max) for r in lb_rows]
    lb = sum(lb_exp[:layer + 1]) / sum(lb_exp)
    onorm = on_ref[...]
    tri = tri_ref[...]
    sizes = _level_sizes(ch)
    lane8 = lax.broadcasted_iota(I32, (SUBLANES, LANES), 1)
    row8 = lax.broadcasted_iota(I32, (SUBLANES, LANES), 0)
    tb = pl.program_id(2)

    n_rows = gb * ch
    row_id = lax.broadcasted_iota(I32, (n_rows, LANES), 0)

    def stack(fn):
        parts = [fn(g) for g in range(gb)]
        return parts[0] if gb == 1 else jnp.concatenate(parts, axis=0)

    def chunk(hh, ci):
        cols = slice(hh * LANES, (hh + 1) * LANES)
        t0 = pl.multiple_of(ci * ch, ch)
        rows = pl.ds(t0, ch)
        lb_h = lb[:, cols]
        q = _silu(stack(lambda g: q_ref[g, rows, cols]))
        fg = lb_h + (1.0 - lb_h) * jax.nn.sigmoid(stack(lambda g: f_ref[g, rows, cols]))
        k = 1.0 - fg
        lf = jnp.log(fg)
        v = stack(lambda g: i_ref[g, rows, cols])
        if valid < ch:
            live = (row_id & (ch - 1)) < valid
            lf = jnp.where(live, lf, 0.0)
            k = jnp.where(live, k, 0.0)
        b2 = _dot_exact_lhs(tri, lf) * LOG2E
        b_scr[hh] = b2
        k_scr[hh] = k
        b_end_rows = [b_scr[hh, pl.ds(g * ch + ch - 1, 1), :] for g in range(gb)]
        b_end = stack(lambda g: jnp.broadcast_to(b_end_rows[g], (ch, LANES)))

        qe = (q * jnp.exp2(b2)).astype(BF16)
        o = stack(lambda g: _dot_nt(qe[g * ch:(g + 1) * ch], st_ref[g * hb + hh].astype(BF16)))

        scores = jnp.zeros((n_rows, n_rows), F32)
        for lvl, sz in enumerate(sizes):
            pieces = []
            for m in range(n_rows // (2 * sz)):
                r = b_scr[hh, pl.ds(2 * sz * m + sz - 1, 1), :]
                pieces.append(jnp.broadcast_to(r, (2 * sz, LANES)))
            r_all = pieces[0] if len(pieces) == 1 else jnp.concatenate(pieces, axis=0)
            e = jnp.exp2(-jnp.abs(b2 - r_all))
            s_l = _dot_nt((q * e).astype(BF16), (k * e).astype(BF16))
            scores = scores + masks_ref[lvl] * s_l
        blocks = []
        for jb in range(n_rows // SUBLANES):
            bb = b2[SUBLANES * jb:SUBLANES * (jb + 1)]
            qb = q[SUBLANES * jb:SUBLANES * (jb + 1)]
            acc = jnp.zeros((SUBLANES, LANES), F32)
            for sl in range(SUBLANES):
                s = SUBLANES * jb + sl
                bs = b_scr[hh, pl.ds(s, 1), :]
                ks = k_scr[hh, pl.ds(s, 1), :]
                val = (qb * ks) * jnp.exp2(bb - bs)
                red = jnp.sum(val, axis=1, keepdims=True)
                acc = jnp.where(lane8 == s, red, acc)
            blocks.append(jnp.where(lane8 - SUBLANES * jb <= row8, acc, 0.0))
        scores = scores + jnp.concatenate(blocks, axis=0)
        o = o + _dot(scores.astype(BF16), v.astype(BF16))

        dk = k * jnp.exp2(b_end - b2)
        v_t = v.T.astype(BF16)
        for g in range(gb):
            own = (row_id >= g * ch) & (row_id < (g + 1) * ch)
            c = g * hb + hh
            st_ref[c] = (st_ref[c] * jnp.exp2(b_end_rows[g])
                         + _dot(v_t, jnp.where(own, dk, 0.0).astype(BF16)))

        on = o * lax.rsqrt(jnp.mean(o * o, axis=-1, keepdims=True) + EPS) * onorm[:, cols]
        out = on * _silu(stack(lambda g: g_ref[g, rows, cols]))
        for g in range(gb):
            o_ref[g, rows, cols] = out[g * ch:(g + 1) * ch].astype(o_ref.dtype)

    chains = [(g * hb + hh, g, hh) for g in range(gb) for hh in range(hb)]

    @pl.when(tb == 0)
    def _():
        for c, g, hh in chains:
            if has_init:
                st_ref[c] = s0_ref[g, hh].T
            else:
                st_ref[c] = jnp.zeros((LANES, LANES), F32)

    def step(ci, carry):
        for hh in range(hb):
            chunk(hh, ci)
        return carry

    if nc == 1:
        step(0, 0)
    else:
        lax.fori_loop(0, nc, step, 0)

    @pl.when(tb == pl.num_programs(2) - 1)
    def _():
        for c, g, hh in chains:
            sout_ref[g, hh] = st_ref[c].T


def _hgrn_heads(proj3, hgrn_lb, onorm_g, s0, *, ch, valid, gb, hb, tt, layer):
    bsz, t_len, _ = proj3.shape
    n_heads = onorm_g.shape[0] // HGRN_HEAD_DIM
    n_hg = n_heads // hb
    nc = tt // ch
    n_rows = gb * ch
    assert n_rows == LANES
    has_init = s0 is not None
    w = hb * LANES

    def col(sec):
        return lambda b, h, t: (b, t, sec * n_hg + h)

    in_specs = [pl.BlockSpec((gb, tt, w), col(s)) for s in range(4)]
    in_specs += [
        pl.BlockSpec((hgrn_lb.shape[0], w), lambda b, h, t: (0, h)),
        pl.BlockSpec((1, w), lambda b, h, t: (0, h)),
        pl.BlockSpec((n_rows, n_rows), lambda b, h, t: (0, 0)),
        pl.BlockSpec((max(len(_level_sizes(ch)), 1), n_rows, n_rows), lambda b, h, t: (0, 0, 0)),
    ]
    args = [proj3, proj3, proj3, proj3, hgrn_lb, onorm_g.reshape(1, -1),
            jnp.asarray(_block_tri(gb, ch), dtype=BF16), _level_masks(ch, n_rows)]
    if has_init:
        in_specs.append(pl.BlockSpec((gb, hb, LANES, LANES), lambda b, h, t: (b, h, 0, 0)))
        args.append(s0)
    kern = functools.partial(_hgrn_kernel, ch=ch, valid=valid, nc=nc, gb=gb, hb=hb,
                             has_init=has_init, layer=layer)
    n_chain = gb * hb
    return pl.pallas_call(
        kern,
        grid=(bsz // gb, n_hg, t_len // tt),
        in_specs=in_specs,
        out_specs=[
            pl.BlockSpec((gb, tt, w), lambda b, h, t: (b, t, h)),
            pl.BlockSpec((gb, hb, LANES, LANES), lambda b, h, t: (b, h, 0, 0)),
        ],
        out_shape=[
            jax.ShapeDtypeStruct((bsz, t_len, n_heads * HGRN_HEAD_DIM), BF16),
            jax.ShapeDtypeStruct((bsz, n_heads, LANES, LANES), F32),
        ],
        scratch_shapes=[pltpu.VMEM((n_chain, LANES, LANES), F32),
                        pltpu.VMEM((hb, n_rows, LANES), F32),
                        pltpu.VMEM((hb, n_rows, LANES), F32)],
        compiler_params=_cparams(("arbitrary", "arbitrary", "arbitrary")),
        name="hgrn2_heads",
    )(*args)


def _softplus(x):
    return jnp.maximum(x, 0.0) + jnp.log1p(jnp.exp(-jnp.abs(x)))


def _ssd_kernel(*refs, ch, valid, nc, gb, pb, has_init):
    if has_init:
        (z_ref, x_ref, b_ref, c_ref, dt_ref, cwx_ref, cwb_ref, cwc_ref, cbx_ref, cbb_ref, cbc_ref,
         xj_ref, bias_ref, alog_ref, d_ref, ng_ref, tri_ref, tril_ref, h0_ref, cx0_ref, cb0_ref,
         cc0_ref, y_ref, hout_ref, ht_ref, wx_ref, wb_ref, wc_ref) = refs
    else:
        (z_ref, x_ref, b_ref, c_ref, dt_ref, cwx_ref, cwb_ref, cwc_ref, cbx_ref, cbb_ref, cbc_ref,
         xj_ref, bias_ref, alog_ref, d_ref, ng_ref, tri_ref, tril_ref,
         y_ref, hout_ref, ht_ref, wx_ref, wb_ref, wc_ref) = refs
        h0_ref = cx0_ref = cb0_ref = cc0_ref = None

    tri = tri_ref[...]
    tril = tril_ref[...]
    lane = lax.broadcasted_iota(I32, (LANES, LANES), 1)
    first_head = lane < SSD_HEAD_DIM
    tail = CONV_WIDTH - 1
    tb = pl.program_id(2)

    def conv(raw_ref, win_ref, cw_ref, cb_ref, g, t0):
        win_ref[g, pl.ds(SUBLANES, ch), :] = raw_ref[g, pl.ds(t0, ch), :]
        u = cb_ref[...]
        for j in range(CONV_WIDTH):
            u = u + win_ref[g, pl.ds(SUBLANES - tail + j, ch), :] * cw_ref[j:j + 1, :]
        win_ref[g, pl.ds(0, SUBLANES), :] = win_ref[g, pl.ds(ch, SUBLANES), :]
        return _silu(u)

    n_rows = gb * ch
    w = pb * LANES
    xj = xj_ref[0]
    bias = bias_ref[0]
    a_neg = -jnp.exp(alog_ref[0])
    d_skip = d_ref[0]
    row_id = lax.broadcasted_iota(I32, (n_rows, w), 0)

    def stack(fn):
        parts = [fn(g) for g in range(gb)]
        return parts[0] if gb == 1 else jnp.concatenate(parts, axis=0)

    def step(ci, carry):
        t0 = pl.multiple_of(ci * ch, ch)
        rows = pl.ds(t0, ch)
        xs = stack(lambda g: conv(x_ref, wx_ref, cwx_ref, cbx_ref, g, t0))
        bm = stack(lambda g: conv(b_ref, wb_ref, cwb_ref, cbb_ref, g, t0))
        cm = stack(lambda g: conv(c_ref, wc_ref, cwc_ref, cbc_ref, g, t0))
        dt = _softplus(_dot_exact_rhs(stack(lambda g: dt_ref[g, rows, :]), xj) + bias)
        if valid < ch:
            dt = jnp.where((row_id & (ch - 1)) < valid, dt, 0.0)
        la_cs = _dot_exact_lhs(tri, dt * a_neg)
        xdt = xs * dt
        a_end_rows = [la_cs[g * ch + ch - 1:g * ch + ch, :] for g in range(gb)]
        a_end = stack(lambda g: jnp.broadcast_to(a_end_rows[g], (ch, w)))

        cb = _dot_nt(cm.astype(BF16), bm.astype(BF16))
        acs_t = la_cs.T
        y_parts = []
        for p in range(pb):
            cols = slice(p * LANES, (p + 1) * LANES)
            scores = []
            for hl in range(2):
                at = p * LANES + hl * SSD_HEAD_DIM
                lmat = jnp.exp(jnp.minimum(la_cs[:, at:at + 1] - acs_t[at:at + 1, :], 0.0)) * tril
                scores.append((cb * lmat).astype(BF16))
            xp = xdt[:, cols]
            rhs = jnp.concatenate([jnp.where(first_head, xp, 0.0),
                                   jnp.where(first_head, 0.0, xp)], axis=0).astype(BF16)
            y_parts.append(_dot(jnp.concatenate(scores, axis=1), rhs))
        y = y_parts[0] if pb == 1 else jnp.concatenate(y_parts, axis=1)

        y_off = stack(lambda g: _dot(cm[g * ch:(g + 1) * ch].astype(BF16), ht_ref[g].astype(BF16)))
        y = y + y_off * jnp.exp(la_cs)
        b_t = bm.T.astype(BF16)
        upd = xdt * jnp.exp(a_end - la_cs)
        for g in range(gb):
            own = (row_id >= g * ch) & (row_id < (g + 1) * ch)
            ht_ref[g] = (ht_ref[g] * jnp.exp(a_end_rows[g])
                         + _dot(b_t, jnp.where(own, upd, 0.0).astype(BF16)))

        y = (y + d_skip * xs) * _silu(stack(lambda g: z_ref[g, rows, :]))
        y = y * lax.rsqrt(jnp.mean(y * y, axis=-1, keepdims=True) + EPS) * ng_ref[0]
        for g in range(gb):
            y_ref[g, rows, :] = y[g * ch:(g + 1) * ch].astype(y_ref.dtype)
        return carry

    @pl.when(tb == 0)
    def _():
        for g in range(gb):
            for win_ref, c0_ref in ((wx_ref, cx0_ref), (wb_ref, cb0_ref), (wc_ref, cc0_ref)):
                win_ref[g, pl.ds(0, SUBLANES), :] = jnp.zeros((SUBLANES, win_ref.shape[2]), F32)
                if has_init:
                    win_ref[g, pl.ds(SUBLANES - tail, tail), :] = c0_ref[g]
            for p in range(pb):
                cols = slice(p * LANES, (p + 1) * LANES)
                if has_init:
                    ht_ref[g, :, cols] = h0_ref[g, p].T
                else:
                    ht_ref[g, :, cols] = jnp.zeros((LANES, LANES), F32)

    if nc == 1:
        step(0, 0)
    else:
        lax.fori_loop(0, nc, step, 0)

    @pl.when(tb == pl.num_programs(2) - 1)
    def _():
        for g in range(gb):
            for p in range(pb):
                hout_ref[g, p] = ht_ref[g, :, p * LANES:(p + 1) * LANES].T


def _group_rows(v):
    return jnp.repeat(v.astype(F32), SSD_HEAD_DIM).reshape(SSD_GROUPS, 1, -1)


def _head_select(n_heads):
    import numpy as np
    w = n_heads // SSD_GROUPS * SSD_HEAD_DIM
    h = np.arange(LANES)[None, :, None]
    lane = np.arange(w)[None, None, :]
    g = np.arange(SSD_GROUPS)[:, None, None]
    return jnp.asarray((h == g * (n_heads // SSD_GROUPS) + lane // SSD_HEAD_DIM).astype(np.float32),
                       dtype=BF16)


def _block_tri(gb, ch):
    import numpy as np
    t = np.arange(gb * ch)[:, None]
    s = np.arange(gb * ch)[None, :]
    return ((t // ch == s // ch) & (s <= t)).astype(np.float32)


def _ssd_pairs(proj3, dt3, conv_w, conv_b, dt_bias, a_log, d_skip, norm_g, h0, conv0, *, mix_a,
               mix_b, ch, valid, gb, tt):
    bsz, t_len, _ = proj3.shape
    n_pairs = mix_b // LANES
    pb = n_pairs // SSD_GROUPS
    w = pb * LANES
    nc = tt // ch
    n_rows = gb * ch
    has_init = h0 is not None
    assert (4 * mix_a) % w == 0 and n_rows == LANES
    z0 = 4 * mix_a // w
    x0 = z0 + SSD_GROUPS
    b0 = (4 * mix_a + 2 * mix_b) // LANES
    c0 = b0 + SSD_GROUPS

    seq_w = lambda base: (lambda b, g, t: (b, t, base + g))
    par = lambda base: (lambda b, g, t: (0, base + g))
    const2 = lambda b, g, t: (0, 0)

    in_specs = [
        pl.BlockSpec((gb, tt, w), seq_w(z0)),
        pl.BlockSpec((gb, tt, w), seq_w(x0)),
        pl.BlockSpec((gb, tt, LANES), seq_w(b0)),
        pl.BlockSpec((gb, tt, LANES), seq_w(c0)),
        pl.BlockSpec((gb, tt, LANES), lambda b, g, t: (b, t, 0)),
        pl.BlockSpec((CONV_WIDTH, w), par(0)),
        pl.BlockSpec((CONV_WIDTH, LANES), par(n_pairs)),
        pl.BlockSpec((CONV_WIDTH, LANES), par(n_pairs + SSD_GROUPS)),
        pl.BlockSpec((1, w), par(0)),
        pl.BlockSpec((1, LANES), par(n_pairs)),
        pl.BlockSpec((1, LANES), par(n_pairs + SSD_GROUPS)),
        pl.BlockSpec((1, LANES, w), lambda b, g, t: (g, 0, 0)),
        pl.BlockSpec((1, 1, w), lambda b, g, t: (g, 0, 0)),
        pl.BlockSpec((1, 1, w), lambda b, g, t: (g, 0, 0)),
        pl.BlockSpec((1, 1, w), lambda b, g, t: (g, 0, 0)),
        pl.BlockSpec((1, 1, w), lambda b, g, t: (g, 0, 0)),
        pl.BlockSpec((n_rows, n_rows), const2),
        pl.BlockSpec((n_rows, n_rows), const2),
    ]
    conv_b2 = conv_b.reshape(1, -1)
    block_tri = _block_tri(gb, ch)
    args = [proj3, proj3, proj3, proj3, dt3, conv_w, conv_w, conv_w, conv_b2, conv_b2, conv_b2,
            _head_select(dt_bias.shape[0]), _group_rows(dt_bias), _group_rows(a_log),
            _group_rows(d_skip), norm_g.astype(F32).reshape(SSD_GROUPS, 1, w),
            jnp.asarray(block_tri, dtype=BF16), jnp.asarray(block_tri)]
    if has_init:
        tail = CONV_WIDTH - 1
        in_specs += [
            pl.BlockSpec((gb, pb, LANES, LANES), lambda b, g, t: (b, g, 0, 0)),
            pl.BlockSpec((gb, tail, w), lambda b, g, t: (b, 0, g)),
            pl.BlockSpec((gb, tail, LANES), lambda b, g, t: (b, 0, n_pairs + g)),
            pl.BlockSpec((gb, tail, LANES), lambda b, g, t: (b, 0, n_pairs + SSD_GROUPS + g)),
        ]
        args += [h0, conv0, conv0, conv0]
    kern = functools.partial(_ssd_kernel, ch=ch, valid=valid, nc=nc, gb=gb, pb=pb,
                             has_init=has_init)
    return pl.pallas_call(
        kern,
        grid=(bsz // gb, SSD_GROUPS, t_len // tt),
        in_specs=in_specs,
        out_specs=[
            pl.BlockSpec((gb, tt, w), lambda b, g, t: (b, t, g)),
            pl.BlockSpec((gb, pb, LANES, LANES), lambda b, g, t: (b, g, 0, 0)),
        ],
        out_shape=[
            jax.ShapeDtypeStruct((bsz, t_len, mix_b), BF16),
            jax.ShapeDtypeStruct((bsz, n_pairs, LANES, LANES), F32),
        ],
        scratch_shapes=[pltpu.VMEM((gb, LANES, w), F32),
                        pltpu.VMEM((gb, ch + SUBLANES, w), F32),
                        pltpu.VMEM((gb, ch + SUBLANES, LANES), F32),
                        pltpu.VMEM((gb, ch + SUBLANES, LANES), F32)],
        compiler_params=_cparams(("arbitrary", "arbitrary", "arbitrary")),
        name="ssd_pairs",
    )(*args)


OUT_TM = 512
WEIGHT_STAGE_ROWS = 512
GROUP_LANE0 = N_EXPERTS
NEG_BIG = -1e30
NO_LANE = 4 * LANES


def _first_lane_of(mask, lane):
    return jnp.min(jnp.where(mask, lane, float(NO_LANE)), axis=1, keepdims=True)


def _route(logits):
    lane_i = lax.broadcasted_iota(I32, logits.shape, 1)
    lane = lane_i.astype(F32)
    is_grp = (lane_i >= GROUP_LANE0) & (lane_i < GROUP_LANE0 + N_EXPERT_GROUPS)
    lg = jnp.where(is_grp, logits, NEG_BIG)
    g_max = jnp.max(lg, axis=1, keepdims=True)
    g_sum = jnp.sum(jnp.where(is_grp, jnp.exp(lg - g_max), 0.0), axis=1, keepdims=True)
    g_idx = _first_lane_of(lg == g_max, lane) - GROUP_LANE0
    gp_top = 1.0 / g_sum
    lane_grp = jnp.right_shift(lane_i, 3).astype(F32)
    in_grp = (lane_i < N_EXPERTS) & (lane_grp == g_idx)
    le = jnp.where(in_grp, logits, NEG_BIG)
    e_max = jnp.max(le, axis=1, keepdims=True)
    e_exp = jnp.where(in_grp, jnp.exp(le - e_max), 0.0)
    ep = e_exp / jnp.sum(e_exp, axis=1, keepdims=True)
    ep = jnp.where(in_grp, ep, -1.0)
    p1 = jnp.max(ep, axis=1, keepdims=True)
    i1 = _first_lane_of(ep == p1, lane)
    ep2 = jnp.where(lane == i1, -1.0, ep)
    p2 = jnp.max(ep2, axis=1, keepdims=True)
    i2 = _first_lane_of(ep2 == p2, lane)
    den = p1 + p2
    return i1, i2, gp_top * p1 / den, gp_top * p2 / den


def _outproj_kernel(o_ref, y_ref, x_ref, g1_ref, sc_ref, sh_ref, ng_ref, w_hbm_ref, wr_ref, br_ref,
                    tri_ref, cnt0_ref,
                    x1_ref, hn_ref, ids_ref, wts_ref, rank_ref, cnt_ref,
                    wbf_ref, stage_ref, sem, cnts_ref):
    i = pl.program_id(0)
    tm, d = hn_ref.shape
    mix_a = o_ref.shape[1]
    tn = PROJ_TN
    n_col = d // tn
    gshape = (x_ref.shape[0], x_ref.shape[1], tn)

    @pl.when(i == 0)
    def _():
        cnts_ref[...] = cnt0_ref[...]
        rows = stage_ref.shape[0]
        for k in range(d // rows):
            cp = pltpu.make_async_copy(w_hbm_ref.at[pl.ds(k * rows, rows), :], stage_ref, sem)
            cp.start()
            cp.wait()
            wbf_ref[k * rows:(k + 1) * rows, :] = stage_ref[...].astype(BF16)

    o = o_ref[...]
    y = y_ref[...]
    ssq = jnp.zeros((tm, 1), F32)
    for c in range(n_col):
        cols = slice(c * tn, (c + 1) * tn)
        acc = _dot(o, wbf_ref[0:mix_a, cols]) + _dot(y, wbf_ref[mix_a:d, cols])
        x1 = x_ref[:, :, cols] + g1_ref[:, :, cols] * acc.reshape(gshape)
        x1_ref[:, :, cols] = x1
        x1 = x1.reshape(tm, tn)
        ssq = ssq + jnp.sum(x1 * x1, axis=-1, keepdims=True)
    rs = lax.rsqrt(ssq / d + EPS)
    logits = jnp.zeros((tm, LANES), F32) + br_ref[...]
    for c in range(n_col):
        cols = slice(c * tn, (c + 1) * tn)
        hn = (x1_ref[:, :, cols].reshape(tm, tn) * rs * ng_ref[:, cols]).reshape(gshape)
        hn = (hn * (1.0 + sc_ref[:, :, cols]) + sh_ref[:, :, cols]).reshape(tm, tn)
        hn_ref[:, cols] = hn
        logits = logits + _dot_hp(hn, wr_ref[cols, :])
    i1, i2, w1, w2 = _route(logits)
    lane = lax.broadcasted_iota(I32, (tm, LANES), 1)
    hit1 = lane.astype(F32) == i1
    hit2 = lane.astype(F32) == i2
    onehot = jnp.where(hit1 | hit2, 1.0, 0.0).astype(BF16)
    before = _dot(tri_ref[...], onehot) + cnts_ref[0:1, :]
    r1 = jnp.sum(jnp.where(hit1, before, 0.0), axis=1, keepdims=True)
    r2 = jnp.sum(jnp.where(hit2, before, 0.0), axis=1, keepdims=True)
    ids_ref[...] = jnp.where(lane == 0, i1, jnp.where(lane == 1, i2, 0.0)).astype(I32)
    wts_ref[...] = jnp.where(lane == 0, w1, jnp.where(lane == 1, w2, 0.0))
    rank_ref[...] = jnp.where(lane == 0, r1, jnp.where(lane == 1, r2, 0.0)).astype(I32)
    total = cnts_ref[0:1, :] + jnp.sum(onehot.astype(F32), axis=0, keepdims=True)
    cnts_ref[...] = jnp.broadcast_to(total, cnts_ref.shape)
    cnt_ref[...] = jnp.broadcast_to(total, cnt_ref.shape)


def _strict_lower(n):
    import numpy as np
    t = np.arange(n)[:, None]
    s = np.arange(n)[None, :]
    return jnp.asarray((s < t).astype(np.float32), dtype=BF16)


def _out_projection(o3, y3, x3, mod3, norm_g, w_out, w_router, b_router, cnt0, group_rows):
    bsz, t_len, d = x3.shape
    g_blk, r_blk = group_rows
    tm = g_blk * r_blk
    mix_a = o3.shape[-1]
    mix_b = y3.shape[-1]
    n_tok = bsz * t_len
    tiles_per_seq = max(t_len // r_blk, 1)

    xmap = lambda i: (i // tiles_per_seq, i % tiles_per_seq, 0)
    modmap = lambda s: (lambda i: (i // tiles_per_seq, 0, s))
    row = lambda i: (i, 0)
    const2 = lambda i: (0, 0)
    return pl.pallas_call(
        _outproj_kernel,
        grid=(n_tok // tm,),
        in_specs=[
            pl.BlockSpec((tm, mix_a), row),
            pl.BlockSpec((tm, mix_b), row),
            pl.BlockSpec((g_blk, r_blk, d), xmap),
            pl.BlockSpec((g_blk, 1, d), modmap(2)),
            pl.BlockSpec((g_blk, 1, d), modmap(4)),
            pl.BlockSpec((g_blk, 1, d), modmap(3)),
            pl.BlockSpec((1, d), const2),
            pl.BlockSpec(memory_space=pl.ANY),
            pl.BlockSpec((d, LANES), const2),
            pl.BlockSpec((1, LANES), const2),
            pl.BlockSpec((tm, tm), const2),
            pl.BlockSpec((SUBLANES, LANES), const2),
        ],
        out_specs=[
            pl.BlockSpec((g_blk, r_blk, d), xmap),
            pl.BlockSpec((tm, d), row),
            pl.BlockSpec((tm, LANES), row),
            pl.BlockSpec((tm, LANES), row),
            pl.BlockSpec((tm, LANES), row),
            pl.BlockSpec((SUBLANES, LANES), const2),
        ],
        out_shape=[
            jax.ShapeDtypeStruct((bsz, t_len, d), F32),
            jax.ShapeDtypeStruct((n_tok, d), F32),
            jax.ShapeDtypeStruct((n_tok, LANES), I32),
            jax.ShapeDtypeStruct((n_tok, LANES), F32),
            jax.ShapeDtypeStruct((n_tok, LANES), I32),
            jax.ShapeDtypeStruct((SUBLANES, LANES), F32),
        ],
        scratch_shapes=[pltpu.VMEM((d, d), BF16), pltpu.VMEM((WEIGHT_STAGE_ROWS, d), F32),
                        pltpu.SemaphoreType.DMA(()), pltpu.VMEM((SUBLANES, LANES), F32)],
        compiler_params=_cparams(("arbitrary",)),
        name="out_proj_router",
    )(o3.reshape(n_tok, mix_a), y3.reshape(n_tok, mix_b), x3, mod3, mod3, mod3,
      norm_g.reshape(1, d), w_out, w_router, b_router, _strict_lower(tm), cnt0)


def _positions_kernel(ids_ref, rank_ref, cnt_ref, pos_ref, te_ref, nt_ref, lt_ref, *, block):
    lane8 = lax.broadcasted_iota(I32, (SUBLANES, LANES), 1)
    cnt = cnt_ref[...]
    tiles = jnp.floor((cnt + (MOE_TM - 1)) * (1.0 / MOE_TM))
    tiles = jnp.where(lane8 < N_EXPERTS, tiles, 0.0)
    incl = tiles
    shift = 1
    while shift < N_EXPERTS:
        incl = incl + jnp.where(lane8 >= shift, pltpu.roll(incl, shift, axis=1), 0.0)
        shift *= 2
    offs = ((incl - tiles) * MOE_TM)[0:1, :]
    nt_ref[...] = jnp.sum(tiles, axis=1, keepdims=True).astype(I32) + jnp.zeros(nt_ref.shape, I32)
    lt_ref[...] = jnp.where(tiles > 0.0, incl - 1.0, -1.0).astype(I32)

    incl_col = jnp.broadcast_to(incl[0:1, :], (LANES, LANES)).T
    e_row = lax.broadcasted_iota(I32, (LANES, LANES), 0)
    i_lane = lax.broadcasted_iota(I32, (LANES, LANES), 1).astype(F32)
    done = jnp.where((incl_col <= i_lane) & (e_row < N_EXPERTS), 1.0, 0.0)
    te = jnp.minimum(jnp.sum(done, axis=0, keepdims=True), N_EXPERTS - 1.0)
    te_ref[...] = te.astype(I32) + jnp.zeros(te_ref.shape, I32)

    lane = lax.broadcasted_iota(I32, (block, LANES), 1)

    def body(bi, c):
        r0 = pl.multiple_of(bi * block, block)
        ids = ids_ref[pl.ds(r0, block), :]
        rank = rank_ref[pl.ds(r0, block), :]
        out = jnp.zeros((block, LANES), I32)
        for k in range(TOP_K):
            off = jnp.sum(jnp.where(lane == ids[:, k:k + 1], offs, 0.0), axis=1, keepdims=True)
            out = jnp.where(lane == k, off.astype(I32) + rank[:, k:k + 1], out)
        pos_ref[pl.ds(r0, block), :] = out
        return c

    lax.fori_loop(0, ids_ref.shape[0] // block, body, 0)


def _positions(ids, rank, cnt):
    n_tok = ids.shape[0]
    block = 512
    kern = functools.partial(_positions_kernel, block=block)
    full = lambda s: pl.BlockSpec(s, lambda: tuple(0 for _ in s))
    return pl.pallas_call(
        kern,
        in_specs=[full((n_tok, LANES)), full((n_tok, LANES)), full((SUBLANES, LANES))],
        out_specs=[full((n_tok, LANES))] + [full((SUBLANES, LANES))] * 3,
        out_shape=[jax.ShapeDtypeStruct((n_tok, LANES), I32)]
        + [jax.ShapeDtypeStruct((SUBLANES, LANES), I32)] * 3,
        compiler_params=pltpu.CompilerParams(vmem_limit_bytes=VMEM_LIMIT),
        name="route_positions",
    )(ids, rank, cnt)


def _row_copy(src_ref, src_row, dst_ref, dst_row, sem):
    return pltpu.make_async_copy(src_ref.at[pl.ds(src_row, 1), :], dst_ref.at[pl.ds(dst_row, 1), :],
                                 sem)


def _dispatch_kernel(pos_ref, lt_ref, nt_ref, *refs, steps, n_tiles):
    n_groups = len(steps)
    hn_refs = refs[:n_groups]
    xg_ref, sem, zero_ref, zsem = refs[n_groups:]
    rows = hn_refs[0].shape[0]
    i = pl.program_id(0)

    def zero_copy(tile):
        return pltpu.make_async_copy(zero_ref, xg_ref.at[pl.ds(tile * MOE_TM, MOE_TM), :], zsem)

    @pl.when(i == 0)
    def _():
        zero_ref[...] = jnp.zeros(zero_ref.shape, zero_ref.dtype)
        nt = nt_ref[0]
        for e in range(N_EXPERTS):
            @pl.when(lt_ref[e] >= 0)
            def _():
                zero_copy(jnp.maximum(lt_ref[e], 0)).start()
        lax.fori_loop(nt, n_tiles, lambda t, c: (zero_copy(t).start(), c)[1], 0)
        for e in range(N_EXPERTS):
            @pl.when(lt_ref[e] >= 0)
            def _():
                zero_copy(0).wait()
        lax.fori_loop(nt, n_tiles, lambda t, c: (zero_copy(0).wait(), c)[1], 0)

    first = 0
    for hn_ref, n_steps in zip(hn_refs, steps):
        @pl.when((i >= first) & (i < first + n_steps))
        def _(hn_ref=hn_ref):
            base = i * rows

            def issue(r, c):
                for k in range(TOP_K):
                    _row_copy(hn_ref, r, xg_ref, pos_ref[(base + r) * TOP_K + k], sem).start(
                        priority=k)
                return c

            def drain(r, c):
                for k in range(TOP_K):
                    _row_copy(hn_ref, r, xg_ref, 0, sem).wait()
                return c

            lax.fori_loop(0, rows, issue, 0, unroll=8)
            lax.fori_loop(0, rows, drain, 0, unroll=8)

        first += n_steps


def _dispatch(pos_flat, last_tile, n_live, hns, n_tiles):
    d = hns[0].shape[1]
    steps = tuple(h.shape[0] // ROW_DMA_TILE for h in hns)
    firsts = [sum(steps[:g]) for g in range(len(steps))]

    def group_map(g):
        return lambda i, pos, lt, nt: (jnp.clip(i - firsts[g], 0, steps[g] - 1), 0)

    return pl.pallas_call(
        functools.partial(_dispatch_kernel, steps=steps, n_tiles=n_tiles),
        grid_spec=pltpu.PrefetchScalarGridSpec(
            num_scalar_prefetch=3,
            grid=(sum(steps),),
            in_specs=[pl.BlockSpec((ROW_DMA_TILE, d), group_map(g)) for g in range(len(hns))],
            out_specs=pl.BlockSpec(memory_space=pl.ANY),
            scratch_shapes=[pltpu.SemaphoreType.DMA(()), pltpu.VMEM((MOE_TM, d), F32),
                            pltpu.SemaphoreType.DMA(())],
        ),
        out_shape=jax.ShapeDtypeStruct((n_tiles * MOE_TM, d), F32),
        compiler_params=_cparams(("arbitrary",)),
        name="moe_dispatch",
    )(pos_flat, last_tile, n_live, *hns)


def _weight_copies(w_hbm_refs, expert, wbuf_ref, slot, sem_ref):
    return [pltpu.make_async_copy(w_ref.at[expert], wbuf_ref.at[slot, i], sem_ref.at[slot])
            for i, w_ref in enumerate(w_hbm_refs)]


def _expert_weights_slot(te_ref, nt, t, w_hbm_refs, wbuf_ref, sem_ref, slot_ref):
    expert = te_ref[t]

    @pl.when(t == 0)
    def _():
        slot_ref[0] = 0
        for cp in _weight_copies(w_hbm_refs, expert, wbuf_ref, 0, sem_ref):
            cp.start()

    @pl.when((t == 0) | (expert != te_ref[jnp.maximum(t - 1, 0)]))
    def _():
        @pl.when(t > 0)
        def _():
            slot_ref[0] = 1 - slot_ref[0]

        slot = slot_ref[0]
        for cp in _weight_copies(w_hbm_refs, expert, wbuf_ref, slot, sem_ref):
            cp.wait()
        nxt = lax.while_loop(lambda u: (u < nt) & (te_ref[jnp.minimum(u, nt - 1)] == expert),
                             lambda u: u + 1, t + 1)

        @pl.when(nxt < nt)
        def _():
            for cp in _weight_copies(w_hbm_refs, te_ref[jnp.minimum(nxt, nt - 1)], wbuf_ref,
                                     1 - slot, sem_ref):
                cp.start()

    return slot_ref[0]


def _moe_up_kernel(te_ref, nt_ref, x_ref, w1_ref, w3_ref, act_ref, wbuf_ref, sem_ref, slot_ref):
    t = pl.program_id(0)
    nt = nt_ref[0]

    @pl.when(t < nt)
    def _():
        slot = _expert_weights_slot(te_ref, nt, t, (w1_ref, w3_ref), wbuf_ref, sem_ref, slot_ref)
        xb = x_ref[...].astype(BF16)
        a = _dot(xb, wbuf_ref[slot, 0].astype(BF16))
        b = _dot(xb, wbuf_ref[slot, 1].astype(BF16))
        act_ref[...] = (_silu(a) * b).astype(BF16)

    @pl.when(t >= nt)
    def _():
        act_ref[...] = jnp.zeros(act_ref.shape, act_ref.dtype)


def _moe_down_kernel(te_ref, nt_ref, act_ref, w2_ref, out_ref, wbuf_ref, sem_ref, slot_ref):
    t = pl.program_id(0)
    nt = nt_ref[0]

    @pl.when(t < nt)
    def _():
        slot = _expert_weights_slot(te_ref, nt, t, (w2_ref,), wbuf_ref, sem_ref, slot_ref)
        out_ref[...] = _dot(act_ref[...], wbuf_ref[slot, 0].astype(BF16))

    @pl.when(t >= nt)
    def _():
        out_ref[...] = jnp.zeros(out_ref.shape, out_ref.dtype)


def _moe_experts(te, nt, xg, w1, w3, w2):
    n_rows, d = xg.shape
    d_exp = w1.shape[-1]
    n_tiles = n_rows // MOE_TM
    live_tile = lambda t, te_r, nt_r: (jnp.minimum(t, nt_r[0] - 1), 0)
    any_spec = pl.BlockSpec(memory_space=pl.ANY)

    act = pl.pallas_call(
        _moe_up_kernel,
        grid_spec=pltpu.PrefetchScalarGridSpec(
            num_scalar_prefetch=2,
            grid=(n_tiles,),
            in_specs=[pl.BlockSpec((MOE_TM, d), live_tile), any_spec, any_spec],
            out_specs=pl.BlockSpec((MOE_TM, d_exp), lambda t, te_r, nt_r: (t, 0)),
            scratch_shapes=[pltpu.VMEM((2, 2, d, d_exp), F32), pltpu.SemaphoreType.DMA((2,)),
                            pltpu.SMEM((1,), I32)],
        ),
        out_shape=jax.ShapeDtypeStruct((n_rows, d_exp), BF16),
        compiler_params=_cparams(("arbitrary",)),
        name="moe_up",
    )(te, nt, xg, w1, w3)

    return pl.pallas_call(
        _moe_down_kernel,
        grid_spec=pltpu.PrefetchScalarGridSpec(
            num_scalar_prefetch=2,
            grid=(n_tiles,),
            in_specs=[pl.BlockSpec((MOE_TM, d_exp), live_tile), any_spec],
            out_specs=pl.BlockSpec((MOE_TM, d), lambda t, te_r, nt_r: (t, 0)),
            scratch_shapes=[pltpu.VMEM((2, 1, d_exp, d), F32), pltpu.SemaphoreType.DMA((2,)),
                            pltpu.SMEM((1,), I32)],
        ),
        out_shape=jax.ShapeDtypeStruct((n_rows, d), F32),
        compiler_params=_cparams(("arbitrary",)),
        name="moe_down",
    )(te, nt, act, w2)


def _combine_kernel(pos_ref, x1_ref, g2_ref, wts_ref, fg_ref, eo_ref, y_ref, buf_ref, sem):
    rows = wts_ref.shape[0]
    base = pl.program_id(0) * rows

    def issue(r, c):
        for k in range(TOP_K):
            _row_copy(eo_ref, pos_ref[(base + r) * TOP_K + k], buf_ref.at[k], r, sem).start(
                priority=k)
        return c

    def drain(r, c):
        for k in range(TOP_K):
            _row_copy(eo_ref, 0, buf_ref.at[k], r, sem).wait()
        return c

    lax.fori_loop(0, rows, issue, 0, unroll=8)
    lax.fori_loop(0, rows, drain, 0, unroll=8)
    w = wts_ref[...]
    moe = w[:, 0:1] * buf_ref[0] + w[:, 1:2] * buf_ref[1]
    x2 = x1_ref[...] + g2_ref[...] * moe.reshape(x1_ref.shape)
    ms = jnp.mean(x2 * x2, axis=-1, keepdims=True)
    y_ref[...] = x2 * lax.rsqrt(ms + EPS) * fg_ref[...]


def _combine(pos_flat, x1, mod3, wts, final_g, expert_out, group_rows):
    bsz, t_len, d = x1.shape
    g_blk, r_blk = group_rows
    tm = g_blk * r_blk
    tiles_per_seq = max(t_len // r_blk, 1)

    def xmap(i, pos):
        return (i // tiles_per_seq, i % tiles_per_seq, 0)

    return pl.pallas_call(
        _combine_kernel,
        grid_spec=pltpu.PrefetchScalarGridSpec(
            num_scalar_prefetch=1,
            grid=((bsz * t_len) // tm,),
            in_specs=[
                pl.BlockSpec((g_blk, r_blk, d), xmap),
                pl.BlockSpec((g_blk, 1, d), lambda i, pos: (i // tiles_per_seq, 0, 5)),
                pl.BlockSpec((tm, LANES), lambda i, pos: (i, 0)),
                pl.BlockSpec((1, d), lambda i, pos: (0, 0)),
                pl.BlockSpec(memory_space=pl.ANY),
            ],
            out_specs=pl.BlockSpec((g_blk, r_blk, d), xmap),
            scratch_shapes=[pltpu.VMEM((TOP_K, tm, d), F32), pltpu.SemaphoreType.DMA(())],
        ),
        out_shape=jax.ShapeDtypeStruct((bsz, t_len, d), F32),
        compiler_params=_cparams(("arbitrary",)),
        name="moe_combine",
    )(pos_flat, x1, mod3, wts, final_g.reshape(1, d), expert_out)


def kernel(x_prompt, x_sample, state_hgrn, state_ssm, state_conv, c_prompt, c_sample, ada_w, ada_b,
           norm1_g, norm2_g, w_in, hgrn_lb, hgrn_onorm_g, conv_w, conv_b, dt_bias, a_log, d_skip,
           ssm_norm_g, w_out, w_grp, b_grp, w_rt, b_rt, w1, w3, w2, final_g):
    depth = w_in.shape[0]
    assert depth == 1, "single-layer trunk"
    layer = 0
    bp, t_p, d = x_prompt.shape
    bs, t_s, _ = x_sample.shape
    mix_a = hgrn_onorm_g.shape[1]
    mix_b = ssm_norm_g.shape[1]
    conv_dim = conv_w.shape[2]
    n_main = 4 * mix_a + mix_b + conv_dim
    n_ssd_heads = dt_bias.shape[1]
    n_pairs = mix_b // LANES
    xbc0 = 4 * mix_a + mix_b
    tail = CONV_WIDTH - 1
    assert t_s <= SUBLANES and t_s >= tail and t_p % PROMPT_CHUNK == 0

    xs_pad = jnp.pad(x_sample, ((0, 0), (0, SUBLANES - t_s), (0, 0)))
    n_c = bp + bs
    c_rows = -(-n_c // SUBLANES) * SUBLANES
    c_all = jnp.pad(jnp.concatenate([c_prompt, c_sample], axis=0), ((0, c_rows - n_c), (0, 0)))

    mod = _modulation(c_all, ada_w[layer], ada_b[layer])
    mod_p = mod[:bp].reshape(bp, 1, -1)
    mod_s = mod[bp:n_c].reshape(bs, 1, -1)
    w_in_t = jnp.swapaxes(w_in[layer], 0, 1)
    w_dt_t = jnp.pad(w_in_t[n_main:], ((0, LANES - n_ssd_heads), (0, 0)))
    w_router = jnp.pad(jnp.concatenate([w_rt[layer], w_grp[layer]], axis=1),
                       ((0, 0), (0, LANES - N_EXPERTS - N_EXPERT_GROUPS)))
    b_router = jnp.pad(jnp.concatenate([b_rt[layer], b_grp[layer]]),
                       (0, LANES - N_EXPERTS - N_EXPERT_GROUPS)).reshape(1, LANES)

    groups = (
        (x_prompt, mod_p, None, None, None, t_p, PROMPT_CHUNK, PROMPT_CHUNK, MIXER_ROWS,
         (LANES // PROMPT_CHUNK, 8), LANES // PROMPT_CHUNK, 1, (1, OUT_TM)),
        (xs_pad, mod_s, state_hgrn[layer], state_ssm[layer].reshape(bs, n_pairs, LANES, LANES),
         state_conv[layer], t_s, SUBLANES, t_s, SUBLANES,
         (LANES // SUBLANES, 2), LANES // SUBLANES, PROJ_SEQS, (OUT_TM // SUBLANES, SUBLANES)),
    )

    cnt = jnp.zeros((SUBLANES, LANES), F32)
    per_group = []
    for x3, mod3, s0, h0, c0, t_real, ch, valid, tt, (hg_gb, hg_hb), ssd_gb, proj_gr, out_gr in groups:
        bsz, t_len, _ = x3.shape
        proj, dt = _in_projection(x3, mod3, norm1_g[layer], w_in_t, w_dt_t, n_main, proj_gr)
        proj3 = proj.reshape(bsz, t_len, n_main)
        dt3 = dt.reshape(bsz, t_len, LANES)
        o3, s_new = _hgrn_heads(proj3, hgrn_lb, hgrn_onorm_g[layer], s0, ch=ch, valid=valid,
                                gb=hg_gb, hb=hg_hb, tt=tt, layer=layer)
        y3, h_new = _ssd_pairs(proj3, dt3, conv_w[layer], conv_b[layer], dt_bias[layer],
                               a_log[layer], d_skip[layer], ssm_norm_g[layer], h0, c0, mix_a=mix_a,
                               mix_b=mix_b, ch=ch, valid=valid, gb=ssd_gb, tt=tt)
        conv_new = proj3[:, t_real - tail:t_real, xbc0:xbc0 + conv_dim]
        x1, hn, ids, wts, rank, cnt = _out_projection(
            o3, y3, x3, mod3, norm2_g[layer], w_out[layer], w_router, b_router, cnt, out_gr)
        per_group.append(dict(x1=x1, hn=hn, ids=ids, wts=wts, rank=rank, mod3=mod3, out_gr=out_gr,
                              s_new=s_new, h_new=h_new, conv_new=conv_new))

    ids_all = jnp.concatenate([g["ids"] for g in per_group], axis=0)
    rank_all = jnp.concatenate([g["rank"] for g in per_group], axis=0)
    n_tok = ids_all.shape[0]
    pos, te, nt, lt = _positions(ids_all, rank_all, cnt)
    pos_flat = pos[:, :TOP_K].reshape(-1)
    n_tiles = -(-(n_tok * TOP_K + N_EXPERTS * (MOE_TM - 1)) // MOE_TM)
    assert n_tiles <= LANES
    te_vec = te[0, :n_tiles]
    nt_vec = nt[0, :1]
    lt_vec = lt[0, :N_EXPERTS]

    row0 = 0
    for g in per_group:
        rows = g["hn"].shape[0]
        g["pos"] = pos_flat[row0 * TOP_K:(row0 + rows) * TOP_K]
        row0 += rows
    xg = _dispatch(pos_flat, lt_vec, nt_vec, [g["hn"] for g in per_group], n_tiles)
    expert_out = _moe_experts(te_vec, nt_vec, xg, w1[layer], w3[layer], w2[layer])
    ys = [_combine(g["pos"], g["x1"], g["mod3"], g["wts"], final_g, expert_out, g["out_gr"])
          for g in per_group]

    gp, gs = per_group
    return (
        ys[0],
        ys[1][:, :t_s],
        gp["s_new"][None],
        gp["h_new"].reshape(1, bp, n_ssd_heads, SSD_HEAD_DIM, SSD_STATE),
        gp["conv_new"][None],
        gs["s_new"][None],
        gs["h_new"].reshape(1, bs, n_ssd_heads, SSD_HEAD_DIM, SSD_STATE),
        gs["conv_new"][None],
    )
```

```python
import functools

import jax
import jax.numpy as jnp
from jax import lax
from jax.experimental import pallas as pl
from jax.experimental.pallas import tpu as pltpu

F32 = jnp.float32
BF16 = jnp.bfloat16
I32 = jnp.int32
EPS = 1e-6
LOG2E = 1.4426950408889634

LANES = 128
SUBLANES = 8
VMEM_LIMIT = 56 * 1024 * 1024

HGRN_HEAD_DIM = 128
SSD_HEAD_DIM = 64
SSD_GROUPS = 2
SSD_STATE = 128
CONV_WIDTH = 4
N_EXPERT_GROUPS = 4
EXPERTS_PER_GROUP = 8
N_EXPERTS = N_EXPERT_GROUPS * EXPERTS_PER_GROUP
TOP_K = 2

PROJ_SEQS = 128
NORM_ROWS = 256
PROJ_TN = 512
MOE_TM = 256
PROMPT_CHUNK = 64
MIXER_ROWS = 512
ROW_DMA_TILE = 256


def _cparams(sem):
    return pltpu.CompilerParams(dimension_semantics=sem, vmem_limit_bytes=VMEM_LIMIT)


def _dot(a, b):
    return jnp.dot(a, b, preferred_element_type=F32)


def _dot_nt(a, b):
    return lax.dot_general(a, b, (((1,), (1,)), ((), ())), preferred_element_type=F32)


def _split2(x):
    hi = x.astype(BF16)
    lo = (x - hi.astype(F32)).astype(BF16)
    return hi, lo


def _split3(x):
    hi = x.astype(BF16)
    r = x - hi.astype(F32)
    mid = r.astype(BF16)
    lo = (r - mid.astype(F32)).astype(BF16)
    return hi, mid, lo


def _dot_exact_lhs(m_bf16, x):
    hi, mid, lo = _split3(x)
    return _dot(m_bf16, hi) + _dot(m_bf16, mid) + _dot(m_bf16, lo)


def _dot_exact_rhs(x, m_bf16):
    hi, mid, lo = _split3(x)
    return _dot(hi, m_bf16) + _dot(mid, m_bf16) + _dot(lo, m_bf16)


def _dot_hp(a, w):
    ah, al = _split2(a)
    wh, wl = _split2(w)
    return _dot(ah, wh) + _dot(al, wh) + _dot(ah, wl)


def _silu(x):
    return x * jax.nn.sigmoid(x)


def _mod_kernel(c_ref, w_ref, b_ref, o_ref):
    a = _silu(c_ref[...])
    o_ref[...] = _dot_hp(a, w_ref[...]) + b_ref[...]


def _modulation(c_all, ada_w, ada_b):
    rows, d = c_all.shape
    n_out = ada_w.shape[1]
    tn = 1024
    return pl.pallas_call(
        _mod_kernel,
        grid=(n_out // tn,),
        in_specs=[
            pl.BlockSpec((rows, d), lambda j: (0, 0)),
            pl.BlockSpec((d, tn), lambda j: (0, j)),
            pl.BlockSpec((1, tn), lambda j: (0, j)),
        ],
        out_specs=pl.BlockSpec((rows, tn), lambda j: (0, j)),
        out_shape=jax.ShapeDtypeStruct((rows, n_out), F32),
        compiler_params=_cparams(("arbitrary",)),
        name="adaln_mod",
    )(c_all, ada_w, ada_b.reshape(1, n_out))


def _proj_kernel(x_hbm_ref, sc_ref, sh_ref, g_ref, w_ref, wdt_ref, o_ref, dt_ref, hn_ref, xs_ref,
                 sem):
    i = pl.program_id(0)
    j = pl.program_id(1)
    g_blk, r_blk, d = xs_ref.shape

    def x_copy(tile):
        return pltpu.make_async_copy(x_hbm_ref.at[pl.ds(tile * g_blk, g_blk)], xs_ref, sem)

    @pl.when(j == 0)
    def _():
        @pl.when(i == 0)
        def _():
            x_copy(0).start()

        x_copy(i).wait()
        wh, wl = _split2(wdt_ref[...])
        n_rows = g_blk * r_blk
        step = min(n_rows, NORM_ROWS)
        for c in range(n_rows // step):
            if g_blk == 1:
                x = xs_ref[:, c * step:(c + 1) * step, :]
                sc, sh = sc_ref[...], sh_ref[...]
            else:
                gs = slice(c * step // r_blk, (c + 1) * step // r_blk)
                x, sc, sh = xs_ref[gs], sc_ref[gs], sh_ref[gs]
            ms = jnp.mean(x * x, axis=-1, keepdims=True)
            hn = ((x * lax.rsqrt(ms + EPS)) * g_ref[...] * (1.0 + sc) + sh).reshape(step, d)
            rows = slice(c * step, (c + 1) * step)
            hn_ref[rows, :] = hn.astype(BF16)
            ah, al = _split2(hn)
            dt_ref[rows, :] = _dot_nt(ah, wh) + _dot_nt(al, wh) + _dot_nt(ah, wl)

        @pl.when(i + 1 < pl.num_programs(0))
        def _():
            x_copy(i + 1).start()

    o_ref[...] = _dot_nt(hn_ref[...], w_ref[...].astype(BF16))


def _in_projection(x3, mod3, norm_g, w_t, w_dt_t, n_main, g_blk):
    bsz, t_len, d = x3.shape
    tm = g_blk * t_len
    n_row_tiles = bsz // g_blk

    def mod_map(sec):
        return lambda i, j: (i, 0, sec)

    return pl.pallas_call(
        _proj_kernel,
        grid=(n_row_tiles, n_main // PROJ_TN),
        in_specs=[
            pl.BlockSpec(memory_space=pl.ANY),
            pl.BlockSpec((g_blk, 1, d), mod_map(1)),
            pl.BlockSpec((g_blk, 1, d), mod_map(0)),
            pl.BlockSpec((1, d), lambda i, j: (0, 0)),
            pl.BlockSpec((PROJ_TN, d), lambda i, j: (j, 0)),
            pl.BlockSpec((LANES, d), lambda i, j: (0, 0)),
        ],
        out_specs=[
            pl.BlockSpec((tm, PROJ_TN), lambda i, j: (i, j)),
            pl.BlockSpec((tm, LANES), lambda i, j: (i, 0)),
        ],
        out_shape=[
            jax.ShapeDtypeStruct((bsz * t_len, n_main), F32),
            jax.ShapeDtypeStruct((bsz * t_len, LANES), F32),
        ],
        scratch_shapes=[pltpu.VMEM((tm, d), BF16), pltpu.VMEM((g_blk, t_len, d), F32),
                        pltpu.SemaphoreType.DMA(())],
        compiler_params=_cparams(("arbitrary", "arbitrary")),
        name="in_proj",
    )(x3, mod3, mod3, norm_g.reshape(1, d), w_t, w_dt_t)


def _level_sizes(ch):
    sizes = []
    sz = SUBLANES
    while 2 * sz <= ch:
        sizes.append(sz)
        sz *= 2
    return sizes


def _level_masks(ch, n_rows):
    import numpy as np
    t = np.arange(n_rows)[:, None]
    s = np.arange(n_rows)[None, :]
    out = []
    for sz in _level_sizes(ch):
        m = (t // (2 * sz) == s // (2 * sz)) & (t % (2 * sz) >= sz) & (s % (2 * sz) < sz)
        out.append(m.astype(np.float32))
    if not out:
        out.append(np.zeros((n_rows, n_rows), np.float32))
    return jnp.asarray(np.stack(out))


def _hgrn_kernel(*refs, ch, valid, nc, gb, hb, has_init, layer):
    if has_init:
        (q_ref, f_ref, i_ref, g_ref, lb_ref, on_ref, tri_ref, masks_ref, s0_ref,
         o_ref, sout_ref, st_ref, b_scr, k_scr) = refs
    else:
        (q_ref, f_ref, i_ref, g_ref, lb_ref, on_ref, tri_ref, masks_ref,
         o_ref, sout_ref, st_ref, b_scr, k_scr) = refs
        s0_ref = None

    n_lb = lb_ref.shape[0]
    lb_rows = [lb_ref[i:i + 1, :] for i in range(n_lb)]
    lb_max = functools.reduce(jnp.maximum, lb_rows)
    lb_exp = [jnp.exp(r - lb_max) for r in lb_rows]
    lb = sum(lb_exp[:layer + 1]) / sum(lb_exp)
    onorm = on_ref[...]
    tri = tri_ref[...]
    sizes = _level_sizes(ch)
    lane8 = lax.broadcasted_iota(I32, (SUBLANES, LANES), 1)
    row8 = lax.broadcasted_iota(I32, (SUBLANES, LANES), 0)
    tb = pl.program_id(2)

    n_rows = gb * ch
    row_id = lax.broadcasted_iota(I32, (n_rows, LANES), 0)

    def stack(fn):
        parts = [fn(g) for g in range(gb)]
        return parts[0] if gb == 1 else jnp.concatenate(parts, axis=0)

    def chunk(hh, ci):
        cols = slice(hh * LANES, (hh + 1) * LANES)
        t0 = pl.multiple_of(ci * ch, ch)
        rows = pl.ds(t0, ch)
        lb_h = lb[:, cols]
        q = _silu(stack(lambda g: q_ref[g, rows, cols]))
        fg = lb_h + (1.0 - lb_h) * jax.nn.sigmoid(stack(lambda g: f_ref[g, rows, cols]))
        k = 1.0 - fg
        lf = jnp.log(fg)
        v = stack(lambda g: i_ref[g, rows, cols])
        if valid < ch:
            live = (row_id & (ch - 1)) < valid
            lf = jnp.where(live, lf, 0.0)
            k = jnp.where(live, k, 0.0)
        b2 = _dot_exact_lhs(tri, lf) * LOG2E
        b_scr[hh] = b2
        k_scr[hh] = k
        b_end_rows = [b_scr[hh, pl.ds(g * ch + ch - 1, 1), :] for g in range(gb)]
        b_end = stack(lambda g: jnp.broadcast_to(b_end_rows[g], (ch, LANES)))

        qe = (q * jnp.exp2(b2)).astype(BF16)
        o = stack(lambda g: _dot_nt(qe[g * ch:(g + 1) * ch], st_ref[g * hb + hh].astype(BF16)))

        scores = jnp.zeros((n_rows, n_rows), F32)
        for lvl, sz in enumerate(sizes):
            pieces = []
            for m in range(n_rows // (2 * sz)):
                r = b_scr[hh, pl.ds(2 * sz * m + sz - 1, 1), :]
                pieces.append(jnp.broadcast_to(r, (2 * sz, LANES)))
            r_all = pieces[0] if len(pieces) == 1 else jnp.concatenate(pieces, axis=0)
            e = jnp.exp2(-jnp.abs(b2 - r_all))
            s_l = _dot_nt((q * e).astype(BF16), (k * e).astype(BF16))
            scores = scores + masks_ref[lvl] * s_l
        blocks = []
        for jb in range(n_rows // SUBLANES):
            bb = b2[SUBLANES * jb:SUBLANES * (jb + 1)]
            qb = q[SUBLANES * jb:SUBLANES * (jb + 1)]
            acc = jnp.zeros((SUBLANES, LANES), F32)
            for sl in range(SUBLANES):
                s = SUBLANES * jb + sl
                bs = b_scr[hh, pl.ds(s, 1), :]
                ks = k_scr[hh, pl.ds(s, 1), :]
                val = (qb * ks) * jnp.exp2(bb - bs)
                red = jnp.sum(val, axis=1, keepdims=True)
                acc = jnp.where(lane8 == s, red, acc)
            blocks.append(jnp.where(lane8 - SUBLANES * jb <= row8, acc, 0.0))
        scores = scores + jnp.concatenate(blocks, axis=0)
        o = o + _dot(scores.astype(BF16), v.astype(BF16))

        dk = k * jnp.exp2(b_end - b2)
        v_t = v.T.astype(BF16)
        for g in range(gb):
            own = (row_id >= g * ch) & (row_id < (g + 1) * ch)
            c = g * hb + hh
            st_ref[c] = (st_ref[c] * jnp.exp2(b_end_rows[g])
                         + _dot(v_t, jnp.where(own, dk, 0.0).astype(BF16)))

        on = o * lax.rsqrt(jnp.mean(o * o, axis=-1, keepdims=True) + EPS) * onorm[:, cols]
        out = on * _silu(stack(lambda g: g_ref[g, rows, cols]))
        for g in range(gb):
            o_ref[g, rows, cols] = out[g * ch:(g + 1) * ch].astype(o_ref.dtype)

    chains = [(g * hb + hh, g, hh) for g in range(gb) for hh in range(hb)]

    @pl.when(tb == 0)
    def _():
        for c, g, hh in chains:
            if has_init:
                st_ref[c] = s0_ref[g, hh].T
            else:
                st_ref[c] = jnp.zeros((LANES, LANES), F32)

    def step(ci, carry):
        for hh in range(hb):
            chunk(hh, ci)
        return carry

    if nc == 1:
        step(0, 0)
    else:
        lax.fori_loop(0, nc, step, 0)

    @pl.when(tb == pl.num_programs(2) - 1)
    def _():
        for c, g, hh in chains:
            sout_ref[g, hh] = st_ref[c].T


def _hgrn_heads(proj3, hgrn_lb, onorm_g, s0, *, ch, valid, gb, hb, tt, layer):
    bsz, t_len, _ = proj3.shape
    n_heads = onorm_g.shape[0] // HGRN_HEAD_DIM
    n_hg = n_heads // hb
    nc = tt // ch
    n_rows = gb * ch
    assert n_rows == LANES
    has_init = s0 is not None
    w = hb * LANES

    def col(sec):
        return lambda b, h, t: (b, t, sec * n_hg + h)

    in_specs = [pl.BlockSpec((gb, tt, w), col(s)) for s in range(4)]
    in_specs += [
        pl.BlockSpec((hgrn_lb.shape[0], w), lambda b, h, t: (0, h)),
        pl.BlockSpec((1, w), lambda b, h, t: (0, h)),
        pl.BlockSpec((n_rows, n_rows), lambda b, h, t: (0, 0)),
        pl.BlockSpec((max(len(_level_sizes(ch)), 1), n_rows, n_rows), lambda b, h, t: (0, 0, 0)),
    ]
    args = [proj3, proj3, proj3, proj3, hgrn_lb, onorm_g.reshape(1, -1),
            jnp.asarray(_block_tri(gb, ch), dtype=BF16), _level_masks(ch, n_rows)]
    if has_init:
        in_specs.append(pl.BlockSpec((gb, hb, LANES, LANES), lambda b, h, t: (b, h, 0, 0)))
        args.append(s0)
    kern = functools.partial(_hgrn_kernel, ch=ch, valid=valid, nc=nc, gb=gb, hb=hb,
                             has_init=has_init, layer=layer)
    n_chain = gb * hb
    return pl.pallas_call(
        kern,
        grid=(bsz // gb, n_hg, t_len // tt),
        in_specs=in_specs,
        out_specs=[
            pl.BlockSpec((gb, tt, w), lambda b, h, t: (b, t, h)),
            pl.BlockSpec((gb, hb, LANES, LANES), lambda b, h, t: (b, h, 0, 0)),
        ],
        out_shape=[
            jax.ShapeDtypeStruct((bsz, t_len, n_heads * HGRN_HEAD_DIM), BF16),
            jax.ShapeDtypeStruct((bsz, n_heads, LANES, LANES), F32),
        ],
        scratch_shapes=[pltpu.VMEM((n_chain, LANES, LANES), F32),
                        pltpu.VMEM((hb, n_rows, LANES), F32),
                        pltpu.VMEM((hb, n_rows, LANES), F32)],
        compiler_params=_cparams(("arbitrary", "arbitrary", "arbitrary")),
        name="hgrn2_heads",
    )(*args)


def _softplus(x):
    return jnp.maximum(x, 0.0) + jnp.log1p(jnp.exp(-jnp.abs(x)))


def _ssd_kernel(*refs, ch, valid, nc, gb, pb, has_init):
    if has_init:
        (z_ref, x_ref, b_ref, c_ref, dt_ref, cwx_ref, cwb_ref, cwc_ref, cbx_ref, cbb_ref, cbc_ref,
         xj_ref, bias_ref, alog_ref, d_ref, ng_ref, tri_ref, tril_ref, h0_ref, cx0_ref, cb0_ref,
         cc0_ref, y_ref, hout_ref, ht_ref, wx_ref, wb_ref, wc_ref) = refs
    else:
        (z_ref, x_ref, b_ref, c_ref, dt_ref, cwx_ref, cwb_ref, cwc_ref, cbx_ref, cbb_ref, cbc_ref,
         xj_ref, bias_ref, alog_ref, d_ref, ng_ref, tri_ref, tril_ref,
         y_ref, hout_ref, ht_ref, wx_ref, wb_ref, wc_ref) = refs
        h0_ref = cx0_ref = cb0_ref = cc0_ref = None

    tri = tri_ref[...]
    tril = tril_ref[...]
    lane = lax.broadcasted_iota(I32, (LANES, LANES), 1)
    first_head = lane < SSD_HEAD_DIM
    tail = CONV_WIDTH - 1
    tb = pl.program_id(1)

    def conv(raw_ref, win_ref, cw_ref, cb_ref, g, t0):
        win_ref[g, pl.ds(SUBLANES, ch), :] = raw_ref[g, pl.ds(t0, ch), :]
        u = cb_ref[...]
        for j in range(CONV_WIDTH):
            u = u + win_ref[g, pl.ds(SUBLANES - tail + j, ch), :] * cw_ref[j:j + 1, :]
        win_ref[g, pl.ds(0, SUBLANES), :] = win_ref[g, pl.ds(ch, SUBLANES), :]
        return _silu(u)

    n_rows = gb * ch
    w = pb * LANES
    n_grp = SSD_GROUPS
    row_id = lax.broadcasted_iota(I32, (n_rows, w), 0)

    def stack(fn):
        parts = [fn(g) for g in range(gb)]
        return parts[0] if gb == 1 else jnp.concatenate(parts, axis=0)

    def group(q, rows, xs, bm, cm, dt_raw, z):
        wcols = slice(q * w, (q + 1) * w)
        dt = _softplus(_dot_exact_rhs(dt_raw, xj_ref[q]) + bias_ref[q])
        if valid < ch:
            dt = jnp.where((row_id & (ch - 1)) < valid, dt, 0.0)
        la_cs = _dot_exact_lhs(tri, dt * (-jnp.exp(alog_ref[q])))
        xdt = xs * dt
        a_end_rows = [la_cs[g * ch + ch - 1:g * ch + ch, :] for g in range(gb)]
        a_end = stack(lambda g: jnp.broadcast_to(a_end_rows[g], (ch, w)))

        cb = _dot_nt(cm.astype(BF16), bm.astype(BF16))
        acs_t = la_cs.T
        y_parts = []
        for p in range(pb):
            cols = slice(p * LANES, (p + 1) * LANES)
            scores = []
            for hl in range(2):
                at = p * LANES + hl * SSD_HEAD_DIM
                lmat = jnp.exp(jnp.minimum(la_cs[:, at:at + 1] - acs_t[at:at + 1, :], 0.0)) * tril
                scores.append((cb * lmat).astype(BF16))
            xp = xdt[:, cols]
            rhs = jnp.concatenate([jnp.where(first_head, xp, 0.0),
                                   jnp.where(first_head, 0.0, xp)], axis=0).astype(BF16)
            y_parts.append(_dot(jnp.concatenate(scores, axis=1), rhs))
        y = y_parts[0] if pb == 1 else jnp.concatenate(y_parts, axis=1)

        y_off = stack(lambda g: _dot(cm[g * ch:(g + 1) * ch].astype(BF16),
                                     ht_ref[g, :, wcols].astype(BF16)))
        y = y + y_off * jnp.exp(la_cs)
        b_t = bm.T.astype(BF16)
        upd = xdt * jnp.exp(a_end - la_cs)
        for g in range(gb):
            own = (row_id >= g * ch) & (row_id < (g + 1) * ch)
            ht_ref[g, :, wcols] = (ht_ref[g, :, wcols] * jnp.exp(a_end_rows[g])
                                   + _dot(b_t, jnp.where(own, upd, 0.0).astype(BF16)))

        y = (y + d_ref[q] * xs) * _silu(z)
        y = y * lax.rsqrt(jnp.mean(y * y, axis=-1, keepdims=True) + EPS) * ng_ref[q]
        for g in range(gb):
            y_ref[g, rows, wcols] = y[g * ch:(g + 1) * ch].astype(y_ref.dtype)

    def step(ci, carry):
        t0 = pl.multiple_of(ci * ch, ch)
        rows = pl.ds(t0, ch)
        xs = stack(lambda g: conv(x_ref, wx_ref, cwx_ref, cbx_ref, g, t0))
        bm = stack(lambda g: conv(b_ref, wb_ref, cwb_ref, cbb_ref, g, t0))
        cm = stack(lambda g: conv(c_ref, wc_ref, cwc_ref, cbc_ref, g, t0))
        dt_raw = stack(lambda g: dt_ref[g, rows, :])
        z = stack(lambda g: z_ref[g, rows, :])
        for q in range(n_grp):
            wcols = slice(q * w, (q + 1) * w)
            ncols = slice(q * LANES, (q + 1) * LANES)
            group(q, rows, xs[:, wcols], bm[:, ncols], cm[:, ncols], dt_raw, z[:, wcols])
        return carry

    @pl.when(tb == 0)
    def _():
        for g in range(gb):
            for win_ref, c0_ref in ((wx_ref, cx0_ref), (wb_ref, cb0_ref), (wc_ref, cc0_ref)):
                win_ref[g, pl.ds(0, SUBLANES), :] = jnp.zeros((SUBLANES, win_ref.shape[2]), F32)
                if has_init:
                    win_ref[g, pl.ds(SUBLANES - tail, tail), :] = c0_ref[g]
            for p in range(n_grp * pb):
                cols = slice(p * LANES, (p + 1) * LANES)
                if has_init:
                    ht_ref[g, :, cols] = h0_ref[g, p].T
                else:
                    ht_ref[g, :, cols] = jnp.zeros((LANES, LANES), F32)

    if nc == 1:
        step(0, 0)
    else:
        lax.fori_loop(0, nc, step, 0)

    @pl.when(tb == pl.num_programs(1) - 1)
    def _():
        for g in range(gb):
            for p in range(n_grp * pb):
                hout_ref[g, p] = ht_ref[g, :, p * LANES:(p + 1) * LANES].T


def _group_rows(v):
    return jnp.repeat(v.astype(F32), SSD_HEAD_DIM).reshape(SSD_GROUPS, 1, -1)


def _head_select(n_heads):
    import numpy as np
    w = n_heads // SSD_GROUPS * SSD_HEAD_DIM
    h = np.arange(LANES)[None, :, None]
    lane = np.arange(w)[None, None, :]
    g = np.arange(SSD_GROUPS)[:, None, None]
    return jnp.asarray((h == g * (n_heads // SSD_GROUPS) + lane // SSD_HEAD_DIM).astype(np.float32),
                       dtype=BF16)


def _block_tri(gb, ch):
    import numpy as np
    t = np.arange(gb * ch)[:, None]
    s = np.arange(gb * ch)[None, :]
    return ((t // ch == s // ch) & (s <= t)).astype(np.float32)


def _ssd_pairs(proj3, dt3, conv_w, conv_b, dt_bias, a_log, d_skip, norm_g, h0, conv0, *, mix_a,
               mix_b, ch, valid, gb, tt):
    bsz, t_len, _ = proj3.shape
    n_pairs = mix_b // LANES
    pb = n_pairs // SSD_GROUPS
    w = pb * LANES
    nc = tt // ch
    n_rows = gb * ch
    has_init = h0 is not None
    wn = SSD_GROUPS * LANES
    assert (4 * mix_a) % mix_b == 0 and (4 * mix_a + 2 * mix_b) % wn == 0 and n_rows == LANES
    z0 = 4 * mix_a // mix_b
    x0 = z0 + 1
    b0 = (4 * mix_a + 2 * mix_b) // wn
    c0 = b0 + 1

    seq = lambda blk: (lambda b, t: (b, t, blk))
    par = lambda blk: (lambda b, t: (0, blk))
    const2 = lambda b, t: (0, 0)
    const3 = lambda b, t: (0, 0, 0)

    in_specs = [
        pl.BlockSpec((gb, tt, mix_b), seq(z0)),
        pl.BlockSpec((gb, tt, mix_b), seq(x0)),
        pl.BlockSpec((gb, tt, wn), seq(b0)),
        pl.BlockSpec((gb, tt, wn), seq(c0)),
        pl.BlockSpec((gb, tt, LANES), seq(0)),
        pl.BlockSpec((CONV_WIDTH, mix_b), par(0)),
        pl.BlockSpec((CONV_WIDTH, wn), par(mix_b // wn)),
        pl.BlockSpec((CONV_WIDTH, wn), par(mix_b // wn + 1)),
        pl.BlockSpec((1, mix_b), par(0)),
        pl.BlockSpec((1, wn), par(mix_b // wn)),
        pl.BlockSpec((1, wn), par(mix_b // wn + 1)),
        pl.BlockSpec((SSD_GROUPS, LANES, w), const3),
        pl.BlockSpec((SSD_GROUPS, 1, w), const3),
        pl.BlockSpec((SSD_GROUPS, 1, w), const3),
        pl.BlockSpec((SSD_GROUPS, 1, w), const3),
        pl.BlockSpec((SSD_GROUPS, 1, w), const3),
        pl.BlockSpec((n_rows, n_rows), const2),
        pl.BlockSpec((n_rows, n_rows), const2),
    ]
    conv_b2 = conv_b.reshape(1, -1)
    block_tri = _block_tri(gb, ch)
    args = [proj3, proj3, proj3, proj3, dt3, conv_w, conv_w, conv_w, conv_b2, conv_b2, conv_b2,
            _head_select(dt_bias.shape[0]), _group_rows(dt_bias), _group_rows(a_log),
            _group_rows(d_skip), norm_g.astype(F32).reshape(SSD_GROUPS, 1, w),
            jnp.asarray(block_tri, dtype=BF16), jnp.asarray(block_tri)]
    if has_init:
        tail = CONV_WIDTH - 1
        in_specs += [
            pl.BlockSpec((gb, n_pairs, LANES, LANES), lambda b, t: (b, 0, 0, 0)),
            pl.BlockSpec((gb, tail, mix_b), lambda b, t: (b, 0, 0)),
            pl.BlockSpec((gb, tail, wn), lambda b, t: (b, 0, mix_b // wn)),
            pl.BlockSpec((gb, tail, wn), lambda b, t: (b, 0, mix_b // wn + 1)),
        ]
        args += [h0, conv0, conv0, conv0]
    kern = functools.partial(_ssd_kernel, ch=ch, valid=valid, nc=nc, gb=gb, pb=pb,
                             has_init=has_init)
    return pl.pallas_call(
        kern,
        grid=(bsz // gb, t_len // tt),
        in_specs=in_specs,
        out_specs=[
            pl.BlockSpec((gb, tt, mix_b), lambda b, t: (b, t, 0)),
            pl.BlockSpec((gb, n_pairs, LANES, LANES), lambda b, t: (b, 0, 0, 0)),
        ],
        out_shape=[
            jax.ShapeDtypeStruct((bsz, t_len, mix_b), BF16),
            jax.ShapeDtypeStruct((bsz, n_pairs, LANES, LANES), F32),
        ],
        scratch_shapes=[pltpu.VMEM((gb, LANES, mix_b), F32),
                        pltpu.VMEM((gb, ch + SUBLANES, mix_b), F32),
                        pltpu.VMEM((gb, ch + SUBLANES, wn), F32),
                        pltpu.VMEM((gb, ch + SUBLANES, wn), F32)],
        compiler_params=_cparams(("arbitrary", "arbitrary")),
        name="ssd_pairs",
    )(*args)


OUT_TM = 512
WEIGHT_STAGE_ROWS = 512
ROUTE_PARTS = 4
GROUP_LANE0 = N_EXPERTS
NEG_BIG = -1e30
NO_LANE = 4 * LANES


def _first_lane_of(mask, lane):
    return jnp.min(jnp.where(mask, lane, float(NO_LANE)), axis=1, keepdims=True)


def _route(logits):
    lane_i = lax.broadcasted_iota(I32, logits.shape, 1)
    lane = lane_i.astype(F32)
    is_grp = (lane_i >= GROUP_LANE0) & (lane_i < GROUP_LANE0 + N_EXPERT_GROUPS)
    lg = jnp.where(is_grp, logits, NEG_BIG)
    g_max = jnp.max(lg, axis=1, keepdims=True)
    g_sum = jnp.sum(jnp.where(is_grp, jnp.exp(lg - g_max), 0.0), axis=1, keepdims=True)
    g_idx = _first_lane_of(lg == g_max, lane) - GROUP_LANE0
    gp_top = 1.0 / g_sum
    lane_grp = jnp.right_shift(lane_i, 3).astype(F32)
    in_grp = (lane_i < N_EXPERTS) & (lane_grp == g_idx)
    le = jnp.where(in_grp, logits, NEG_BIG)
    e_max = jnp.max(le, axis=1, keepdims=True)
    e_exp = jnp.where(in_grp, jnp.exp(le - e_max), 0.0)
    ep = e_exp / jnp.sum(e_exp, axis=1, keepdims=True)
    ep = jnp.where(in_grp, ep, -1.0)
    p1 = jnp.max(ep, axis=1, keepdims=True)
    i1 = _first_lane_of(ep == p1, lane)
    ep2 = jnp.where(lane == i1, -1.0, ep)
    p2 = jnp.max(ep2, axis=1, keepdims=True)
    i2 = _first_lane_of(ep2 == p2, lane)
    den = p1 + p2
    return i1, i2, gp_top * p1 / den, gp_top * p2 / den


def _outproj_kernel(o_ref, y_ref, x_ref, g1_ref, sc_ref, sh_ref, ng_ref, w_hbm_ref, wr_ref, br_ref,
                    tri_ref, cnt0_ref,
                    x1_ref, hn_ref, ids_ref, wts_ref, rank_ref, cnt_ref,
                    wbf_ref, stage_ref, sem, cnts_ref):
    i = pl.program_id(0)
    tm, d = hn_ref.shape
    mix_a = o_ref.shape[1]
    tn = PROJ_TN
    n_col = d // tn
    gshape = (x_ref.shape[0], x_ref.shape[1], tn)

    @pl.when(i == 0)
    def _():
        cnts_ref[...] = cnt0_ref[...]
        rows = stage_ref.shape[0]
        for k in range(d // rows):
            cp = pltpu.make_async_copy(w_hbm_ref.at[pl.ds(k * rows, rows), :], stage_ref, sem)
            cp.start()
            cp.wait()
            wbf_ref[k * rows:(k + 1) * rows, :] = stage_ref[...].astype(BF16)

    o = o_ref[...]
    y = y_ref[...]
    for c in range(n_col):
        cols = slice(c * tn, (c + 1) * tn)
        acc = _dot(o, wbf_ref[0:mix_a, cols]) + _dot(y, wbf_ref[mix_a:d, cols])
        x1_ref[:, :, cols] = x_ref[:, :, cols] + g1_ref[:, :, cols] * acc.reshape(gshape)

    g_blk, r_blk = x_ref.shape[0], x_ref.shape[1]
    pr = tm // ROUTE_PARTS
    lane = lax.broadcasted_iota(I32, (pr, LANES), 1)
    onehots, hits = [], []
    for h in range(ROUTE_PARTS):
        rows = slice(h * pr, (h + 1) * pr)
        if g_blk == 1:
            gsl = (slice(None), rows)
        else:
            gsl = (slice(h * g_blk // ROUTE_PARTS, (h + 1) * g_blk // ROUTE_PARTS), slice(None))
        pshape = (g_blk if g_blk == 1 else g_blk // ROUTE_PARTS,
                  pr if g_blk == 1 else r_blk, tn)
        x1 = [x1_ref[gsl + (slice(c * tn, (c + 1) * tn),)].reshape(pr, tn) for c in range(n_col)]
        ssq = sum(jnp.sum(xc * xc, axis=-1, keepdims=True) for xc in x1)
        rs = lax.rsqrt(ssq / d + EPS)
        logits = jnp.zeros((pr, LANES), F32) + br_ref[...]
        for c in range(n_col):
            cols = slice(c * tn, (c + 1) * tn)
            sc = sc_ref[gsl[0], :, cols] if g_blk > 1 else sc_ref[:, :, cols]
            sh = sh_ref[gsl[0], :, cols] if g_blk > 1 else sh_ref[:, :, cols]
            hn = (x1[c] * rs * ng_ref[:, cols]).reshape(pshape)
            hn = (hn * (1.0 + sc) + sh).reshape(pr, tn)
            hn_ref[rows, cols] = hn
            logits = logits + _dot_hp(hn, wr_ref[cols, :])
        i1, i2, w1, w2 = _route(logits)
        hit1 = lane.astype(F32) == i1
        hit2 = lane.astype(F32) == i2
        onehots.append(jnp.where(hit1 | hit2, 1.0, 0.0).astype(BF16))
        hits.append((hit1, hit2))
        ids_ref[rows, :] = jnp.where(lane == 0, i1, jnp.where(lane == 1, i2, 0.0)).astype(I32)
        wts_ref[rows, :] = jnp.where(lane == 0, w1, jnp.where(lane == 1, w2, 0.0))

    total = cnts_ref[0:1, :]
    for h in range(ROUTE_PARTS):
        rows = slice(h * pr, (h + 1) * pr)
        hit1, hit2 = hits[h]
        before = _dot(tri_ref[...], onehots[h]) + total
        r1 = jnp.sum(jnp.where(hit1, before, 0.0), axis=1, keepdims=True)
        r2 = jnp.sum(jnp.where(hit2, before, 0.0), axis=1, keepdims=True)
        rank_ref[rows, :] = jnp.where(lane == 0, r1, jnp.where(lane == 1, r2, 0.0)).astype(I32)
        total = total + jnp.sum(onehots[h].astype(F32), axis=0, keepdims=True)
    cnts_ref[...] = jnp.broadcast_to(total, cnts_ref.shape)
    cnt_ref[...] = jnp.broadcast_to(total, cnt_ref.shape)


def _strict_lower(n):
    import numpy as np
    t = np.arange(n)[:, None]
    s = np.arange(n)[None, :]
    return jnp.asarray((s < t).astype(np.float32), dtype=BF16)


def _out_projection(o3, y3, x3, mod3, norm_g, w_out, w_router, b_router, cnt0, group_rows):
    bsz, t_len, d = x3.shape
    g_blk, r_blk = group_rows
    tm = g_blk * r_blk
    mix_a = o3.shape[-1]
    mix_b = y3.shape[-1]
    n_tok = bsz * t_len
    tiles_per_seq = max(t_len // r_blk, 1)

    xmap = lambda i: (i // tiles_per_seq, i % tiles_per_seq, 0)
    modmap = lambda s: (lambda i: (i // tiles_per_seq, 0, s))
    row = lambda i: (i, 0)
    const2 = lambda i: (0, 0)
    return pl.pallas_call(
        _outproj_kernel,
        grid=(n_tok // tm,),
        in_specs=[
            pl.BlockSpec((tm, mix_a), row),
            pl.BlockSpec((tm, mix_b), row),
            pl.BlockSpec((g_blk, r_blk, d), xmap),
            pl.BlockSpec((g_blk, 1, d), modmap(2)),
            pl.BlockSpec((g_blk, 1, d), modmap(4)),
            pl.BlockSpec((g_blk, 1, d), modmap(3)),
            pl.BlockSpec((1, d), const2),
            pl.BlockSpec(memory_space=pl.ANY),
            pl.BlockSpec((d, LANES), const2),
            pl.BlockSpec((1, LANES), const2),
            pl.BlockSpec((tm // ROUTE_PARTS, tm // ROUTE_PARTS), const2),
            pl.BlockSpec((SUBLANES, LANES), const2),
        ],
        out_specs=[
            pl.BlockSpec((g_blk, r_blk, d), xmap),
            pl.BlockSpec((tm, d), row),
            pl.BlockSpec((tm, LANES), row),
            pl.BlockSpec((tm, LANES), row),
            pl.BlockSpec((tm, LANES), row),
            pl.BlockSpec((SUBLANES, LANES), const2),
        ],
        out_shape=[
            jax.ShapeDtypeStruct((bsz, t_len, d), F32),
            jax.ShapeDtypeStruct((n_tok, d), F32),
            jax.ShapeDtypeStruct((n_tok, LANES), I32),
            jax.ShapeDtypeStruct((n_tok, LANES), F32),
            jax.ShapeDtypeStruct((n_tok, LANES), I32),
            jax.ShapeDtypeStruct((SUBLANES, LANES), F32),
        ],
        scratch_shapes=[pltpu.VMEM((d, d), BF16), pltpu.VMEM((WEIGHT_STAGE_ROWS, d), F32),
                        pltpu.SemaphoreType.DMA(()), pltpu.VMEM((SUBLANES, LANES), F32)],
        compiler_params=_cparams(("arbitrary",)),
        name="out_proj_router",
    )(o3.reshape(n_tok, mix_a), y3.reshape(n_tok, mix_b), x3, mod3, mod3, mod3,
      norm_g.reshape(1, d), w_out, w_router, b_router, _strict_lower(tm // ROUTE_PARTS), cnt0)


def _positions_kernel(ids_ref, rank_ref, cnt_ref, pos_ref, te_ref, nt_ref, lt_ref, *, block):
    lane8 = lax.broadcasted_iota(I32, (SUBLANES, LANES), 1)
    cnt = cnt_ref[...]
    tiles = jnp.floor((cnt + (MOE_TM - 1)) * (1.0 / MOE_TM))
    tiles = jnp.where(lane8 < N_EXPERTS, tiles, 0.0)
    incl = tiles
    shift = 1
    while shift < N_EXPERTS:
        incl = incl + jnp.where(lane8 >= shift, pltpu.roll(incl, shift, axis=1), 0.0)
        shift *= 2
    offs = ((incl - tiles) * MOE_TM)[0:1, :]
    nt_ref[...] = jnp.sum(tiles, axis=1, keepdims=True).astype(I32) + jnp.zeros(nt_ref.shape, I32)
    lt_ref[...] = jnp.where(tiles > 0.0, incl - 1.0, -1.0).astype(I32)

    incl_col = jnp.broadcast_to(incl[0:1, :], (LANES, LANES)).T
    e_row = lax.broadcasted_iota(I32, (LANES, LANES), 0)
    i_lane = lax.broadcasted_iota(I32, (LANES, LANES), 1).astype(F32)
    done = jnp.where((incl_col <= i_lane) & (e_row < N_EXPERTS), 1.0, 0.0)
    te = jnp.minimum(jnp.sum(done, axis=0, keepdims=True), N_EXPERTS - 1.0)
    te_ref[...] = te.astype(I32) + jnp.zeros(te_ref.shape, I32)

    lane = lax.broadcasted_iota(I32, (block, LANES), 1)

    def body(bi, c):
        r0 = pl.multiple_of(bi * block, block)
        ids = ids_ref[pl.ds(r0, block), :]
        rank = rank_ref[pl.ds(r0, block), :]
        out = jnp.zeros((block, LANES), I32)
        for k in range(TOP_K):
            off = jnp.sum(jnp.where(lane == ids[:, k:k + 1], offs, 0.0), axis=1, keepdims=True)
            out = jnp.where(lane == k, off.astype(I32) + rank[:, k:k + 1], out)
        pos_ref[pl.ds(r0, block), :] = out
        return c

    lax.fori_loop(0, ids_ref.shape[0] // block, body, 0)


def _positions(ids, rank, cnt):
    n_tok = ids.shape[0]
    block = 512
    kern = functools.partial(_positions_kernel, block=block)
    full = lambda s: pl.BlockSpec(s, lambda: tuple(0 for _ in s))
    return pl.pallas_call(
        kern,
        in_specs=[full((n_tok, LANES)), full((n_tok, LANES)), full((SUBLANES, LANES))],
        out_specs=[full((n_tok, LANES))] + [full((SUBLANES, LANES))] * 3,
        out_shape=[jax.ShapeDtypeStruct((n_tok, LANES), I32)]
        + [jax.ShapeDtypeStruct((SUBLANES, LANES), I32)] * 3,
        compiler_params=pltpu.CompilerParams(vmem_limit_bytes=VMEM_LIMIT),
        name="route_positions",
    )(ids, rank, cnt)


def _row_copy(src_ref, src_row, dst_ref, dst_row, sem):
    return pltpu.make_async_copy(src_ref.at[pl.ds(src_row, 1), :], dst_ref.at[pl.ds(dst_row, 1), :],
                                 sem)


def _dispatch_kernel(pos_ref, lt_ref, nt_ref, *refs, steps, n_tiles):
    n_groups = len(steps)
    hn_refs = refs[:n_groups]
    xg_ref, sem, zero_ref, zsem = refs[n_groups:]
    rows = hn_refs[0].shape[0]
    i = pl.program_id(0)

    def zero_copy(tile):
        return pltpu.make_async_copy(zero_ref, xg_ref.at[pl.ds(tile * MOE_TM, MOE_TM), :], zsem)

    @pl.when(i == 0)
    def _():
        zero_ref[...] = jnp.zeros(zero_ref.shape, zero_ref.dtype)
        nt = nt_ref[0]
        for e in range(N_EXPERTS):
            @pl.when(lt_ref[e] >= 0)
            def _():
                zero_copy(jnp.maximum(lt_ref[e], 0)).start()
        lax.fori_loop(nt, n_tiles, lambda t, c: (zero_copy(t).start(), c)[1], 0)
        for e in range(N_EXPERTS):
            @pl.when(lt_ref[e] >= 0)
            def _():
                zero_copy(0).wait()
        lax.fori_loop(nt, n_tiles, lambda t, c: (zero_copy(0).wait(), c)[1], 0)

    first = 0
    for hn_ref, n_steps in zip(hn_refs, steps):
        @pl.when((i >= first) & (i < first + n_steps))
        def _(hn_ref=hn_ref):
            base = i * rows

            def issue(r, c):
                for k in range(TOP_K):
                    _row_copy(hn_ref, r, xg_ref, pos_ref[(base + r) * TOP_K + k], sem).start(
                        priority=k)
                return c

            def drain(r, c):
                for k in range(TOP_K):
                    _row_copy(hn_ref, r, xg_ref, 0, sem).wait()
                return c

            lax.fori_loop(0, rows, issue, 0, unroll=8)
            lax.fori_loop(0, rows, drain, 0, unroll=8)

        first += n_steps


def _dispatch(pos_flat, last_tile, n_live, hns, n_tiles):
    d = hns[0].shape[1]
    steps = tuple(h.shape[0] // ROW_DMA_TILE for h in hns)
    firsts = [sum(steps[:g]) for g in range(len(steps))]

    def group_map(g):
        return lambda i, pos, lt, nt: (jnp.clip(i - firsts[g], 0, steps[g] - 1), 0)

    return pl.pallas_call(
        functools.partial(_dispatch_kernel, steps=steps, n_tiles=n_tiles),
        grid_spec=pltpu.PrefetchScalarGridSpec(
            num_scalar_prefetch=3,
            grid=(sum(steps),),
            in_specs=[pl.BlockSpec((ROW_DMA_TILE, d), group_map(g)) for g in range(len(hns))],
            out_specs=pl.BlockSpec(memory_space=pl.ANY),
            scratch_shapes=[pltpu.SemaphoreType.DMA(()), pltpu.VMEM((MOE_TM, d), F32),
                            pltpu.SemaphoreType.DMA(())],
        ),
        out_shape=jax.ShapeDtypeStruct((n_tiles * MOE_TM, d), F32),
        compiler_params=_cparams(("arbitrary",)),
        name="moe_dispatch",
    )(pos_flat, last_tile, n_live, *hns)


def _weight_copies(w_hbm_refs, expert, wbuf_ref, slot, sem_ref):
    return [pltpu.make_async_copy(w_ref.at[expert], wbuf_ref.at[slot, i], sem_ref.at[slot])
            for i, w_ref in enumerate(w_hbm_refs)]


def _expert_weights_slot(te_ref, nt, t, w_hbm_refs, wbuf_ref, sem_ref, slot_ref):
    expert = te_ref[t]

    @pl.when(t == 0)
    def _():
        slot_ref[0] = 0
        for cp in _weight_copies(w_hbm_refs, expert, wbuf_ref, 0, sem_ref):
            cp.start()

    @pl.when((t == 0) | (expert != te_ref[jnp.maximum(t - 1, 0)]))
    def _():
        @pl.when(t > 0)
        def _():
            slot_ref[0] = 1 - slot_ref[0]

        slot = slot_ref[0]
        for cp in _weight_copies(w_hbm_refs, expert, wbuf_ref, slot, sem_ref):
            cp.wait()
        nxt = lax.while_loop(lambda u: (u < nt) & (te_ref[jnp.minimum(u, nt - 1)] == expert),
                             lambda u: u + 1, t + 1)

        @pl.when(nxt < nt)
        def _():
            for cp in _weight_copies(w_hbm_refs, te_ref[jnp.minimum(nxt, nt - 1)], wbuf_ref,
                                     1 - slot, sem_ref):
                cp.start()

    return slot_ref[0]


def _moe_up_kernel(te_ref, nt_ref, x_ref, w1_ref, w3_ref, act_ref, wbuf_ref, sem_ref, slot_ref):
    t = pl.program_id(0)
    nt = nt_ref[0]

    @pl.when(t < nt)
    def _():
        slot = _expert_weights_slot(te_ref, nt, t, (w1_ref, w3_ref), wbuf_ref, sem_ref, slot_ref)
        xb = x_ref[...].astype(BF16)
        a = _dot(xb, wbuf_ref[slot, 0].astype(BF16))
        b = _dot(xb, wbuf_ref[slot, 1].astype(BF16))
        act_ref[...] = (_silu(a) * b).astype(BF16)

    @pl.when(t >= nt)
    def _():
        act_ref[...] = jnp.zeros(act_ref.shape, act_ref.dtype)


def _moe_down_kernel(te_ref, nt_ref, act_ref, w2_ref, out_ref, wbuf_ref, sem_ref, slot_ref):
    t = pl.program_id(0)
    nt = nt_ref[0]

    @pl.when(t < nt)
    def _():
        slot = _expert_weights_slot(te_ref, nt, t, (w2_ref,), wbuf_ref, sem_ref, slot_ref)
        out_ref[...] = _dot(act_ref[...], wbuf_ref[slot, 0].astype(BF16))

    @pl.when(t >= nt)
    def _():
        out_ref[...] = jnp.zeros(out_ref.shape, out_ref.dtype)


def _moe_experts(te, nt, xg, w1, w3, w2):
    n_rows, d = xg.shape
    d_exp = w1.shape[-1]
    n_tiles = n_rows // MOE_TM
    live_tile = lambda t, te_r, nt_r: (jnp.minimum(t, nt_r[0] - 1), 0)
    any_spec = pl.BlockSpec(memory_space=pl.ANY)

    act = pl.pallas_call(
        _moe_up_kernel,
        grid_spec=pltpu.PrefetchScalarGridSpec(
            num_scalar_prefetch=2,
            grid=(n_tiles,),
            in_specs=[pl.BlockSpec((MOE_TM, d), live_tile), any_spec, any_spec],
            out_specs=pl.BlockSpec((MOE_TM, d_exp), lambda t, te_r, nt_r: (t, 0)),
            scratch_shapes=[pltpu.VMEM((2, 2, d, d_exp), F32), pltpu.SemaphoreType.DMA((2,)),
                            pltpu.SMEM((1,), I32)],
        ),
        out_shape=jax.ShapeDtypeStruct((n_rows, d_exp), BF16),
        compiler_params=_cparams(("arbitrary",)),
        name="moe_up",
    )(te, nt, xg, w1, w3)

    return pl.pallas_call(
        _moe_down_kernel,
        grid_spec=pltpu.PrefetchScalarGridSpec(
            num_scalar_prefetch=2,
            grid=(n_tiles,),
            in_specs=[pl.BlockSpec((MOE_TM, d_exp), live_tile), any_spec],
            out_specs=pl.BlockSpec((MOE_TM, d), lambda t, te_r, nt_r: (t, 0)),
            scratch_shapes=[pltpu.VMEM((2, 1, d_exp, d), F32), pltpu.SemaphoreType.DMA((2,)),
                            pltpu.SMEM((1,), I32)],
        ),
        out_shape=jax.ShapeDtypeStruct((n_rows, d), F32),
        compiler_params=_cparams(("arbitrary",)),
        name="moe_down",
    )(te, nt, act, w2)


def _combine_kernel(pos_ref, x1_ref, g2_ref, wts_ref, fg_ref, eo_ref, y_ref, buf_ref, sem):
    rows = wts_ref.shape[0]
    base = pl.program_id(0) * rows

    def issue(r, c):
        for k in range(TOP_K):
            _row_copy(eo_ref, pos_ref[(base + r) * TOP_K + k], buf_ref.at[k], r, sem).start(
                priority=k)
        return c

    def drain(r, c):
        for k in range(TOP_K):
            _row_copy(eo_ref, 0, buf_ref.at[k], r, sem).wait()
        return c

    lax.fori_loop(0, rows, issue, 0, unroll=8)
    lax.fori_loop(0, rows, drain, 0, unroll=8)
    w = wts_ref[...]
    moe = w[:, 0:1] * buf_ref[0] + w[:, 1:2] * buf_ref[1]
    x2 = x1_ref[...] + g2_ref[...] * moe.reshape(x1_ref.shape)
    ms = jnp.mean(x2 * x2, axis=-1, keepdims=True)
    y_ref[...] = x2 * lax.rsqrt(ms + EPS) * fg_ref[...]


def _combine(pos_flat, x1, mod3, wts, final_g, expert_out, group_rows):
    bsz, t_len, d = x1.shape
    g_blk, r_blk = group_rows
    tm = g_blk * r_blk
    tiles_per_seq = max(t_len // r_blk, 1)

    def xmap(i, pos):
        return (i // tiles_per_seq, i % tiles_per_seq, 0)

    return pl.pallas_call(
        _combine_kernel,
        grid_spec=pltpu.PrefetchScalarGridSpec(
            num_scalar_prefetch=1,
            grid=((bsz * t_len) // tm,),
            in_specs=[
                pl.BlockSpec((g_blk, r_blk, d), xmap),
                pl.BlockSpec((g_blk, 1, d), lambda i, pos: (i // tiles_per_seq, 0, 5)),
                pl.BlockSpec((tm, LANES), lambda i, pos: (i, 0)),
                pl.BlockSpec((1, d), lambda i, pos: (0, 0)),
                pl.BlockSpec(memory_space=pl.ANY),
            ],
            out_specs=pl.BlockSpec((g_blk, r_blk, d), xmap),
            scratch_shapes=[pltpu.VMEM((TOP_K, tm, d), F32), pltpu.SemaphoreType.DMA(())],
        ),
        out_shape=jax.ShapeDtypeStruct((bsz, t_len, d), F32),
        compiler_params=_cparams(("arbitrary",)),
        name="moe_combine",
    )(pos_flat, x1, mod3, wts, final_g.reshape(1, d), expert_out)


def kernel(x_prompt, x_sample, state_hgrn, state_ssm, state_conv, c_prompt, c_sample, ada_w, ada_b,
           norm1_g, norm2_g, w_in, hgrn_lb, hgrn_onorm_g, conv_w, conv_b, dt_bias, a_log, d_skip,
           ssm_norm_g, w_out, w_grp, b_grp, w_rt, b_rt, w1, w3, w2, final_g):
    depth = w_in.shape[0]
    assert depth == 1, "single-layer trunk"
    layer = 0
    bp, t_p, d = x_prompt.shape
    bs, t_s, _ = x_sample.shape
    mix_a = hgrn_onorm_g.shape[1]
    mix_b = ssm_norm_g.shape[1]
    conv_dim = conv_w.shape[2]
    n_main = 4 * mix_a + mix_b + conv_dim
    n_ssd_heads = dt_bias.shape[1]
    n_pairs = mix_b // LANES
    xbc0 = 4 * mix_a + mix_b
    tail = CONV_WIDTH - 1
    assert t_s <= SUBLANES and t_s >= tail and t_p % PROMPT_CHUNK == 0

    xs_pad = jnp.pad(x_sample, ((0, 0), (0, SUBLANES - t_s), (0, 0)))
    n_c = bp + bs
    c_rows = -(-n_c // SUBLANES) * SUBLANES
    c_all = jnp.pad(jnp.concatenate([c_prompt, c_sample], axis=0), ((0, c_rows - n_c), (0, 0)))

    mod = _modulation(c_all, ada_w[layer], ada_b[layer])
    mod_p = mod[:bp].reshape(bp, 1, -1)
    mod_s = mod[bp:n_c].reshape(bs, 1, -1)
    w_in_t = jnp.swapaxes(w_in[layer], 0, 1)
    w_dt_t = jnp.pad(w_in_t[n_main:], ((0, LANES - n_ssd_heads), (0, 0)))
    w_router = jnp.pad(jnp.concatenate([w_rt[layer], w_grp[layer]], axis=1),
                       ((0, 0), (0, LANES - N_EXPERTS - N_EXPERT_GROUPS)))
    b_router = jnp.pad(jnp.concatenate([b_rt[layer], b_grp[layer]]),
                       (0, LANES - N_EXPERTS - N_EXPERT_GROUPS)).reshape(1, LANES)

    groups = (
        (x_prompt, mod_p, None, None, None, t_p, PROMPT_CHUNK, PROMPT_CHUNK, MIXER_ROWS,
         (LANES // PROMPT_CHUNK, 8), LANES // PROMPT_CHUNK, 1, (1, OUT_TM)),
        (xs_pad, mod_s, state_hgrn[layer], state_ssm[layer].reshape(bs, n_pairs, LANES, LANES),
         state_conv[layer], t_s, SUBLANES, t_s, SUBLANES,
         (LANES // SUBLANES, 2), LANES // SUBLANES, PROJ_SEQS, (OUT_TM // SUBLANES, SUBLANES)),
    )

    cnt = jnp.zeros((SUBLANES, LANES), F32)
    per_group = []
    for x3, mod3, s0, h0, c0, t_real, ch, valid, tt, (hg_gb, hg_hb), ssd_gb, proj_gr, out_gr in groups:
        bsz, t_len, _ = x3.shape
        proj, dt = _in_projection(x3, mod3, norm1_g[layer], w_in_t, w_dt_t, n_main, proj_gr)
        proj3 = proj.reshape(bsz, t_len, n_main)
        dt3 = dt.reshape(bsz, t_len, LANES)
        o3, s_new = _hgrn_heads(proj3, hgrn_lb, hgrn_onorm_g[layer], s0, ch=ch, valid=valid,
                                gb=hg_gb, hb=hg_hb, tt=tt, layer=layer)
        y3, h_new = _ssd_pairs(proj3, dt3, conv_w[layer], conv_b[layer], dt_bias[layer],
                               a_log[layer], d_skip[layer], ssm_norm_g[layer], h0, c0, mix_a=mix_a,
                               mix_b=mix_b, ch=ch, valid=valid, gb=ssd_gb, tt=tt)
        conv_new = proj3[:, t_real - tail:t_real, xbc0:xbc0 + conv_dim]
        x1, hn, ids, wts, rank, cnt = _out_projection(
            o3, y3, x3, mod3, norm2_g[layer], w_out[layer], w_router, b_router, cnt, out_gr)
        per_group.append(dict(x1=x1, hn=hn, ids=ids, wts=wts, rank=rank, mod3=mod3, out_gr=out_gr,
                              s_new=s_new, h_new=h_new, conv_new=conv_new))

    ids_all = jnp.concatenate([g["ids"] for g in per_group], axis=0)
    rank_all = jnp.concatenate([g["rank"] for g in per_group], axis=0)
    n_tok = ids_all.shape[0]
    pos, te, nt, lt = _positions(ids_all, rank_all, cnt)
    pos_flat = pos[:, :TOP_K].reshape(-1)
    n_tiles = -(-(n_tok * TOP_K + N_EXPERTS * (MOE_TM - 1)) // MOE_TM)
    assert n_tiles <= LANES
    te_vec = te[0, :n_tiles]
    nt_vec = nt[0, :1]
    lt_vec = lt[0, :N_EXPERTS]

    row0 = 0
    for g in per_group:
        rows = g["hn"].shape[0]
        g["pos"] = pos_flat[row0 * TOP_K:(row0 + rows) * TOP_K]
        row0 += rows
    xg = _dispatch(pos_flat, lt_vec, nt_vec, [g["hn"] for g in per_group], n_tiles)
    expert_out = _moe_experts(te_vec, nt_vec, xg, w1[layer], w3[layer], w2[layer])
    ys = [_combine(g["pos"], g["x1"], g["mod3"], g["wts"], final_g, expert_out, g["out_gr"])
          for g in per_group]

    gp, gs = per_group
    return (
        ys[0],
        ys[1][:, :t_s],
        gp["s_new"][None],
        gp["h_new"].reshape(1, bp, n_ssd_heads, SSD_HEAD_DIM, SSD_STATE),
        gp["conv_new"][None],
        gs["s_new"][None],
        gs["h_new"].reshape(1, bs, n_ssd_heads, SSD_HEAD_DIM, SSD_STATE),
        gs["conv_new"][None],
    )
```

```python
import functools

import jax
import jax.numpy as jnp
from jax import lax
from jax.experimental import pallas as pl
from jax.experimental.pallas import tpu as pltpu

F32 = jnp.float32
BF16 = jnp.bfloat16
I32 = jnp.int32
EPS = 1e-6
LOG2E = 1.4426950408889634

LANES = 128
SUBLANES = 8
VMEM_LIMIT = 56 * 1024 * 1024

HGRN_HEAD_DIM = 128
SSD_HEAD_DIM = 64
SSD_GROUPS = 2
SSD_STATE = 128
CONV_WIDTH = 4
N_EXPERT_GROUPS = 4
EXPERTS_PER_GROUP = 8
N_EXPERTS = N_EXPERT_GROUPS * EXPERTS_PER_GROUP
TOP_K = 2

PROJ_SEQS = 128
NORM_ROWS = 256
PROJ_TN = 512
MOE_TM = 256
PROMPT_CHUNK = 64
MIXER_ROWS = 512
ROW_DMA_TILE = 256


def _cparams(sem):
    return pltpu.CompilerParams(dimension_semantics=sem, vmem_limit_bytes=VMEM_LIMIT)


def _dot(a, b):
    return jnp.dot(a, b, preferred_element_type=F32)


def _dot_nt(a, b):
    return lax.dot_general(a, b, (((1,), (1,)), ((), ())), preferred_element_type=F32)


def _split2(x):
    hi = x.astype(BF16)
    lo = (x - hi.astype(F32)).astype(BF16)
    return hi, lo


def _split3(x):
    hi = x.astype(BF16)
    r = x - hi.astype(F32)
    mid = r.astype(BF16)
    lo = (r - mid.astype(F32)).astype(BF16)
    return hi, mid, lo


def _dot_exact_lhs(m_bf16, x):
    hi, mid, lo = _split3(x)
    return _dot(m_bf16, hi) + _dot(m_bf16, mid) + _dot(m_bf16, lo)


def _dot_exact_rhs(x, m_bf16):
    hi, mid, lo = _split3(x)
    return _dot(hi, m_bf16) + _dot(mid, m_bf16) + _dot(lo, m_bf16)


def _dot_hp(a, w):
    ah, al = _split2(a)
    wh, wl = _split2(w)
    return _dot(ah, wh) + _dot(al, wh) + _dot(ah, wl)


def _silu(x):
    return x * jax.nn.sigmoid(x)


def _mod_kernel(c_ref, w_ref, b_ref, o_ref):
    a = _silu(c_ref[...])
    o_ref[...] = _dot_hp(a, w_ref[...]) + b_ref[...]


def _modulation(c_all, ada_w, ada_b):
    rows, d = c_all.shape
    n_out = ada_w.shape[1]
    tn = 1024
    return pl.pallas_call(
        _mod_kernel,
        grid=(n_out // tn,),
        in_specs=[
            pl.BlockSpec((rows, d), lambda j: (0, 0)),
            pl.BlockSpec((d, tn), lambda j: (0, j)),
            pl.BlockSpec((1, tn), lambda j: (0, j)),
        ],
        out_specs=pl.BlockSpec((rows, tn), lambda j: (0, j)),
        out_shape=jax.ShapeDtypeStruct((rows, n_out), F32),
        compiler_params=_cparams(("arbitrary",)),
        name="adaln_mod",
    )(c_all, ada_w, ada_b.reshape(1, n_out))


def _proj_kernel(x_hbm_ref, sc_ref, sh_ref, g_ref, w_ref, wdt_ref, o_ref, dt_ref, hn_ref, xs_ref,
                 sem):
    i = pl.program_id(0)
    j = pl.program_id(1)
    g_blk, r_blk, d = xs_ref.shape

    def x_copy(tile):
        return pltpu.make_async_copy(x_hbm_ref.at[pl.ds(tile * g_blk, g_blk)], xs_ref, sem)

    @pl.when(j == 0)
    def _():
        @pl.when(i == 0)
        def _():
            x_copy(0).start()

        x_copy(i).wait()
        wh, wl = _split2(wdt_ref[...])
        n_rows = g_blk * r_blk
        step = min(n_rows, NORM_ROWS)
        for c in range(n_rows // step):
            if g_blk == 1:
                x = xs_ref[:, c * step:(c + 1) * step, :]
                sc, sh = sc_ref[...], sh_ref[...]
            else:
                gs = slice(c * step // r_blk, (c + 1) * step // r_blk)
                x, sc, sh = xs_ref[gs], sc_ref[gs], sh_ref[gs]
            ms = jnp.mean(x * x, axis=-1, keepdims=True)
            hn = ((x * lax.rsqrt(ms + EPS)) * g_ref[...] * (1.0 + sc) + sh).reshape(step, d)
            rows = slice(c * step, (c + 1) * step)
            hn_ref[rows, :] = hn.astype(BF16)
            ah, al = _split2(hn)
            dt_ref[rows, :] = _dot_nt(ah, wh) + _dot_nt(al, wh) + _dot_nt(ah, wl)

        @pl.when(i + 1 < pl.num_programs(0))
        def _():
            x_copy(i + 1).start()

    o_ref[...] = _dot_nt(hn_ref[...], w_ref[...].astype(BF16))


def _in_projection(x3, mod3, norm_g, w_t, w_dt_t, n_main, g_blk):
    bsz, t_len, d = x3.shape
    tm = g_blk * t_len
    n_row_tiles = bsz // g_blk

    def mod_map(sec):
        return lambda i, j: (i, 0, sec)

    return pl.pallas_call(
        _proj_kernel,
        grid=(n_row_tiles, n_main // PROJ_TN),
        in_specs=[
            pl.BlockSpec(memory_space=pl.ANY),
            pl.BlockSpec((g_blk, 1, d), mod_map(1)),
            pl.BlockSpec((g_blk, 1, d), mod_map(0)),
            pl.BlockSpec((1, d), lambda i, j: (0, 0)),
            pl.BlockSpec((PROJ_TN, d), lambda i, j: (j, 0)),
            pl.BlockSpec((LANES, d), lambda i, j: (0, 0)),
        ],
        out_specs=[
            pl.BlockSpec((tm, PROJ_TN), lambda i, j: (i, j)),
            pl.BlockSpec((tm, LANES), lambda i, j: (i, 0)),
        ],
        out_shape=[
            jax.ShapeDtypeStruct((bsz * t_len, n_main), F32),
            jax.ShapeDtypeStruct((bsz * t_len, LANES), F32),
        ],
        scratch_shapes=[pltpu.VMEM((tm, d), BF16), pltpu.VMEM((g_blk, t_len, d), F32),
                        pltpu.SemaphoreType.DMA(())],
        compiler_params=_cparams(("arbitrary", "arbitrary")),
        name="in_proj",
    )(x3, mod3, mod3, norm_g.reshape(1, d), w_t, w_dt_t)


def _level_sizes(ch):
    sizes = []
    sz = SUBLANES
    while 2 * sz <= ch:
        sizes.append(sz)
        sz *= 2
    return sizes


def _level_masks(ch, n_rows):
    import numpy as np
    t = np.arange(n_rows)[:, None]
    s = np.arange(n_rows)[None, :]
    out = []
    for sz in _level_sizes(ch):
        m = (t // (2 * sz) == s // (2 * sz)) & (t % (2 * sz) >= sz) & (s % (2 * sz) < sz)
        out.append(m.astype(np.float32))
    if not out:
        out.append(np.zeros((n_rows, n_rows), np.float32))
    return jnp.asarray(np.stack(out))


def _hgrn_kernel(*refs, ch, valid, nc, gb, hb, has_init, layer):
    if has_init:
        (q_ref, f_ref, i_ref, g_ref, lb_ref, on_ref, tri_ref, masks_ref, s0_ref,
         o_ref, sout_ref, st_ref, b_scr, k_scr) = refs
    else:
        (q_ref, f_ref, i_ref, g_ref, lb_ref, on_ref, tri_ref, masks_ref,
         o_ref, sout_ref, st_ref, b_scr, k_scr) = refs
        s0_ref = None

    n_lb = lb_ref.shape[0]
    lb_rows = [lb_ref[i:i + 1, :] for i in range(n_lb)]
    lb_max = functools.reduce(jnp.maximum, lb_rows)
    lb_exp = [jnp.exp(r - lb_max) for r in lb_rows]
    lb = sum(lb_exp[:layer + 1]) / sum(lb_exp)
    onorm = on_ref[...]
    tri = tri_ref[...]
    sizes = _level_sizes(ch)
    lane8 = lax.broadcasted_iota(I32, (SUBLANES, LANES), 1)
    row8 = lax.broadcasted_iota(I32, (SUBLANES, LANES), 0)
    tb = pl.program_id(2)

    n_rows = gb * ch
    row_id = lax.broadcasted_iota(I32, (n_rows, LANES), 0)

    def stack(fn):
        parts = [fn(g) for g in range(gb)]
        return parts[0] if gb == 1 else jnp.concatenate(parts, axis=0)

    def chunk(hh, ci):
        cols = slice(hh * LANES, (hh + 1) * LANES)
        t0 = pl.multiple_of(ci * ch, ch)
        rows = pl.ds(t0, ch)
        lb_h = lb[:, cols]
        q = _silu(stack(lambda g: q_ref[g, rows, cols]))
        fg = lb_h + (1.0 - lb_h) * jax.nn.sigmoid(stack(lambda g: f_ref[g, rows, cols]))
        k = 1.0 - fg
        lf = jnp.log(fg)
        v = stack(lambda g: i_ref[g, rows, cols])
        if valid < ch:
            live = (row_id & (ch - 1)) < valid
            lf = jnp.where(live, lf, 0.0)
            k = jnp.where(live, k, 0.0)
        b2 = _dot_exact_lhs(tri, lf) * LOG2E
        b_scr[hh] = b2
        k_scr[hh] = k
        b_end_rows = [b_scr[hh, pl.ds(g * ch + ch - 1, 1), :] for g in range(gb)]
        b_end = stack(lambda g: jnp.broadcast_to(b_end_rows[g], (ch, LANES)))

        qe = (q * jnp.exp2(b2)).astype(BF16)
        o = stack(lambda g: _dot_nt(qe[g * ch:(g + 1) * ch], st_ref[g * hb + hh].astype(BF16)))

        scores = jnp.zeros((n_rows, n_rows), F32)
        for lvl, sz in enumerate(sizes):
            pieces = []
            for m in range(n_rows // (2 * sz)):
                r = b_scr[hh, pl.ds(2 * sz * m + sz - 1, 1), :]
                pieces.append(jnp.broadcast_to(r, (2 * sz, LANES)))
            r_all = pieces[0] if len(pieces) == 1 else jnp.concatenate(pieces, axis=0)
            e = jnp.exp2(-jnp.abs(b2 - r_all))
            s_l = _dot_nt((q * e).astype(BF16), (k * e).astype(BF16))
            scores = scores + masks_ref[lvl] * s_l
        blocks = []
        for jb in range(n_rows // SUBLANES):
            bb = b2[SUBLANES * jb:SUBLANES * (jb + 1)]
            qb = q[SUBLANES * jb:SUBLANES * (jb + 1)]
            acc = jnp.zeros((SUBLANES, LANES), F32)
            for sl in range(SUBLANES):
                s = SUBLANES * jb + sl
                bs = b_scr[hh, pl.ds(s, 1), :]
                ks = k_scr[hh, pl.ds(s, 1), :]
                val = (qb * ks) * jnp.exp2(bb - bs)
                red = jnp.sum(val, axis=1, keepdims=True)
                acc = jnp.where(lane8 == s, red, acc)
            blocks.append(jnp.where(lane8 - SUBLANES * jb <= row8, acc, 0.0))
        scores = scores + jnp.concatenate(blocks, axis=0)
        o = o + _dot(scores.astype(BF16), v.astype(BF16))

        dk = k * jnp.exp2(b_end - b2)
        v_t = v.T.astype(BF16)
        for g in range(gb):
            own = (row_id >= g * ch) & (row_id < (g + 1) * ch)
            c = g * hb + hh
            st_ref[c] = (st_ref[c] * jnp.exp2(b_end_rows[g])
                         + _dot(v_t, jnp.where(own, dk, 0.0).astype(BF16)))

        on = o * lax.rsqrt(jnp.mean(o * o, axis=-1, keepdims=True) + EPS) * onorm[:, cols]
        out = on * _silu(stack(lambda g: g_ref[g, rows, cols]))
        for g in range(gb):
            o_ref[g, rows, cols] = out[g * ch:(g + 1) * ch].astype(o_ref.dtype)

    chains = [(g * hb + hh, g, hh) for g in range(gb) for hh in range(hb)]

    @pl.when(tb == 0)
    def _():
        for c, g, hh in chains:
            if has_init:
                st_ref[c] = s0_ref[g, hh].T
            else:
                st_ref[c] = jnp.zeros((LANES, LANES), F32)

    def step(ci, carry):
        for hh in range(hb):
            chunk(hh, ci)
        return carry

    if nc == 1:
        step(0, 0)
    else:
        lax.fori_loop(0, nc, step, 0)

    @pl.when(tb == pl.num_programs(2) - 1)
    def _():
        for c, g, hh in chains:
            sout_ref[g, hh] = st_ref[c].T


def _hgrn_heads(proj3, hgrn_lb, onorm_g, s0, *, ch, valid, gb, hb, tt, layer):
    bsz, t_len, _ = proj3.shape
    n_heads = onorm_g.shape[0] // HGRN_HEAD_DIM
    n_hg = n_heads // hb
    nc = tt // ch
    n_rows = gb * ch
    assert n_rows == LANES
    has_init = s0 is not None
    w = hb * LANES

    def col(sec):
        return lambda b, h, t: (b, t, sec * n_hg + h)

    in_specs = [pl.BlockSpec((gb, tt, w), col(s)) for s in range(4)]
    in_specs += [
        pl.BlockSpec((hgrn_lb.shape[0], w), lambda b, h, t: (0, h)),
        pl.BlockSpec((1, w), lambda b, h, t: (0, h)),
        pl.BlockSpec((n_rows, n_rows), lambda b, h, t: (0, 0)),
        pl.BlockSpec((max(len(_level_sizes(ch)), 1), n_rows, n_rows), lambda b, h, t: (0, 0, 0)),
    ]
    args = [proj3, proj3, proj3, proj3, hgrn_lb, onorm_g.reshape(1, -1),
            jnp.asarray(_block_tri(gb, ch), dtype=BF16), _level_masks(ch, n_rows)]
    if has_init:
        in_specs.append(pl.BlockSpec((gb, hb, LANES, LANES), lambda b, h, t: (b, h, 0, 0)))
        args.append(s0)
    kern = functools.partial(_hgrn_kernel, ch=ch, valid=valid, nc=nc, gb=gb, hb=hb,
                             has_init=has_init, layer=layer)
    n_chain = gb * hb
    return pl.pallas_call(
        kern,
        grid=(bsz // gb, n_hg, t_len // tt),
        in_specs=in_specs,
        out_specs=[
            pl.BlockSpec((gb, tt, w), lambda b, h, t: (b, t, h)),
            pl.BlockSpec((gb, hb, LANES, LANES), lambda b, h, t: (b, h, 0, 0)),
        ],
        out_shape=[
            jax.ShapeDtypeStruct((bsz, t_len, n_heads * HGRN_HEAD_DIM), BF16),
            jax.ShapeDtypeStruct((bsz, n_heads, LANES, LANES), F32),
        ],
        scratch_shapes=[pltpu.VMEM((n_chain, LANES, LANES), F32),
                        pltpu.VMEM((hb, n_rows, LANES), F32),
                        pltpu.VMEM((hb, n_rows, LANES), F32)],
        compiler_params=_cparams(("arbitrary", "arbitrary", "arbitrary")),
        name="hgrn2_heads",
    )(*args)


def _softplus(x):
    return jnp.maximum(x, 0.0) + jnp.log1p(jnp.exp(-jnp.abs(x)))


def _ssd_kernel(*refs, ch, valid, nc, gb, pb, has_init):
    if has_init:
        (z_ref, x_ref, b_ref, c_ref, dt_ref, cwx_ref, cwb_ref, cwc_ref, cbx_ref, cbb_ref, cbc_ref,
         xj_ref, bias_ref, alog_ref, d_ref, ng_ref, tri_ref, tril_ref, h0_ref, cx0_ref, cb0_ref,
         cc0_ref, y_ref, hout_ref, ht_ref, wx_ref, wb_ref, wc_ref) = refs
    else:
        (z_ref, x_ref, b_ref, c_ref, dt_ref, cwx_ref, cwb_ref, cwc_ref, cbx_ref, cbb_ref, cbc_ref,
         xj_ref, bias_ref, alog_ref, d_ref, ng_ref, tri_ref, tril_ref,
         y_ref, hout_ref, ht_ref, wx_ref, wb_ref, wc_ref) = refs
        h0_ref = cx0_ref = cb0_ref = cc0_ref = None

    tri = tri_ref[...]
    tril = tril_ref[...]
    lane = lax.broadcasted_iota(I32, (LANES, LANES), 1)
    first_head = lane < SSD_HEAD_DIM
    tail = CONV_WIDTH - 1
    tb = pl.program_id(1)

    def conv(raw_ref, win_ref, cw_ref, cb_ref, g, t0):
        win_ref[g, pl.ds(SUBLANES, ch), :] = raw_ref[g, pl.ds(t0, ch), :]
        u = cb_ref[...]
        for j in range(CONV_WIDTH):
            u = u + win_ref[g, pl.ds(SUBLANES - tail + j, ch), :] * cw_ref[j:j + 1, :]
        win_ref[g, pl.ds(0, SUBLANES), :] = win_ref[g, pl.ds(ch, SUBLANES), :]
        return _silu(u)

    n_rows = gb * ch
    w = pb * LANES
    n_grp = SSD_GROUPS
    row_id = lax.broadcasted_iota(I32, (n_rows, w), 0)

    def stack(fn):
        parts = [fn(g) for g in range(gb)]
        return parts[0] if gb == 1 else jnp.concatenate(parts, axis=0)

    def group(q, rows, xs, bm, cm, dt_raw, z):
        wcols = slice(q * w, (q + 1) * w)
        dt = _softplus(_dot_exact_rhs(dt_raw, xj_ref[q]) + bias_ref[q])
        if valid < ch:
            dt = jnp.where((row_id & (ch - 1)) < valid, dt, 0.0)
        la_cs = _dot_exact_lhs(tri, dt * (-jnp.exp(alog_ref[q])))
        xdt = xs * dt
        a_end_rows = [la_cs[g * ch + ch - 1:g * ch + ch, :] for g in range(gb)]
        a_end = stack(lambda g: jnp.broadcast_to(a_end_rows[g], (ch, w)))

        cb = _dot_nt(cm.astype(BF16), bm.astype(BF16))
        acs_t = la_cs.T
        y_parts = []
        for p in range(pb):
            cols = slice(p * LANES, (p + 1) * LANES)
            scores = []
            for hl in range(2):
                at = p * LANES + hl * SSD_HEAD_DIM
                lmat = jnp.exp(jnp.minimum(la_cs[:, at:at + 1] - acs_t[at:at + 1, :], 0.0)) * tril
                scores.append((cb * lmat).astype(BF16))
            xp = xdt[:, cols]
            rhs = jnp.concatenate([jnp.where(first_head, xp, 0.0),
                                   jnp.where(first_head, 0.0, xp)], axis=0).astype(BF16)
            y_parts.append(_dot(jnp.concatenate(scores, axis=1), rhs))
        y = y_parts[0] if pb == 1 else jnp.concatenate(y_parts, axis=1)

        y_off = stack(lambda g: _dot(cm[g * ch:(g + 1) * ch].astype(BF16),
                                     ht_ref[g, :, wcols].astype(BF16)))
        y = y + y_off * jnp.exp(la_cs)
        b_t = bm.T.astype(BF16)
        upd = xdt * jnp.exp(a_end - la_cs)
        for g in range(gb):
            own = (row_id >= g * ch) & (row_id < (g + 1) * ch)
            ht_ref[g, :, wcols] = (ht_ref[g, :, wcols] * jnp.exp(a_end_rows[g])
                                   + _dot(b_t, jnp.where(own, upd, 0.0).astype(BF16)))

        y = (y + d_ref[q] * xs) * _silu(z)
        y = y * lax.rsqrt(jnp.mean(y * y, axis=-1, keepdims=True) + EPS) * ng_ref[q]
        for g in range(gb):
            y_ref[g, rows, wcols] = y[g * ch:(g + 1) * ch].astype(y_ref.dtype)

    def step(ci, carry):
        t0 = pl.multiple_of(ci * ch, ch)
        rows = pl.ds(t0, ch)
        xs = stack(lambda g: conv(x_ref, wx_ref, cwx_ref, cbx_ref, g, t0))
        bm = stack(lambda g: conv(b_ref, wb_ref, cwb_ref, cbb_ref, g, t0))
        cm = stack(lambda g: conv(c_ref, wc_ref, cwc_ref, cbc_ref, g, t0))
        dt_raw = stack(lambda g: dt_ref[g, rows, :])
        z = stack(lambda g: z_ref[g, rows, :])
        for q in range(n_grp):
            wcols = slice(q * w, (q + 1) * w)
            ncols = slice(q * LANES, (q + 1) * LANES)
            group(q, rows, xs[:, wcols], bm[:, ncols], cm[:, ncols], dt_raw, z[:, wcols])
        return carry

    @pl.when(tb == 0)
    def _():
        for g in range(gb):
            for win_ref, c0_ref in ((wx_ref, cx0_ref), (wb_ref, cb0_ref), (wc_ref, cc0_ref)):
                win_ref[g, pl.ds(0, SUBLANES), :] = jnp.zeros((SUBLANES, win_ref.shape[2]), F32)
                if has_init:
                    win_ref[g, pl.ds(SUBLANES - tail, tail), :] = c0_ref[g]
            for p in range(n_grp * pb):
                cols = slice(p * LANES, (p + 1) * LANES)
                if has_init:
                    ht_ref[g, :, cols] = h0_ref[g, p].T
                else:
                    ht_ref[g, :, cols] = jnp.zeros((LANES, LANES), F32)

    if nc == 1:
        step(0, 0)
    else:
        lax.fori_loop(0, nc, step, 0)

    @pl.when(tb == pl.num_programs(1) - 1)
    def _():
        for g in range(gb):
            for p in range(n_grp * pb):
                hout_ref[g, p] = ht_ref[g, :, p * LANES:(p + 1) * LANES].T


def _group_rows(v):
    return jnp.repeat(v.astype(F32), SSD_HEAD_DIM).reshape(SSD_GROUPS, 1, -1)


def _head_select(n_heads):
    import numpy as np
    w = n_heads // SSD_GROUPS * SSD_HEAD_DIM
    h = np.arange(LANES)[None, :, None]
    lane = np.arange(w)[None, None, :]
    g = np.arange(SSD_GROUPS)[:, None, None]
    return jnp.asarray((h == g * (n_heads // SSD_GROUPS) + lane // SSD_HEAD_DIM).astype(np.float32),
                       dtype=BF16)


def _block_tri(gb, ch):
    import numpy as np
    t = np.arange(gb * ch)[:, None]
    s = np.arange(gb * ch)[None, :]
    return ((t // ch == s // ch) & (s <= t)).astype(np.float32)


def _ssd_pairs(proj3, dt3, conv_w, conv_b, dt_bias, a_log, d_skip, norm_g, h0, conv0, *, mix_a,
               mix_b, ch, valid, gb, tt):
    bsz, t_len, _ = proj3.shape
    n_pairs = mix_b // LANES
    pb = n_pairs // SSD_GROUPS
    w = pb * LANES
    nc = tt // ch
    n_rows = gb * ch
    has_init = h0 is not None
    wn = SSD_GROUPS * LANES
    assert (4 * mix_a) % mix_b == 0 and (4 * mix_a + 2 * mix_b) % wn == 0 and n_rows == LANES
    z0 = 4 * mix_a // mix_b
    x0 = z0 + 1
    b0 = (4 * mix_a + 2 * mix_b) // wn
    c0 = b0 + 1

    seq = lambda blk: (lambda b, t: (b, t, blk))
    par = lambda blk: (lambda b, t: (0, blk))
    const2 = lambda b, t: (0, 0)
    const3 = lambda b, t: (0, 0, 0)

    in_specs = [
        pl.BlockSpec((gb, tt, mix_b), seq(z0)),
        pl.BlockSpec((gb, tt, mix_b), seq(x0)),
        pl.BlockSpec((gb, tt, wn), seq(b0)),
        pl.BlockSpec((gb, tt, wn), seq(c0)),
        pl.BlockSpec((gb, tt, LANES), seq(0)),
        pl.BlockSpec((CONV_WIDTH, mix_b), par(0)),
        pl.BlockSpec((CONV_WIDTH, wn), par(mix_b // wn)),
        pl.BlockSpec((CONV_WIDTH, wn), par(mix_b // wn + 1)),
        pl.BlockSpec((1, mix_b), par(0)),
        pl.BlockSpec((1, wn), par(mix_b // wn)),
        pl.BlockSpec((1, wn), par(mix_b // wn + 1)),
        pl.BlockSpec((SSD_GROUPS, LANES, w), const3),
        pl.BlockSpec((SSD_GROUPS, 1, w), const3),
        pl.BlockSpec((SSD_GROUPS, 1, w), const3),
        pl.BlockSpec((SSD_GROUPS, 1, w), const3),
        pl.BlockSpec((SSD_GROUPS, 1, w), const3),
        pl.BlockSpec((n_rows, n_rows), const2),
        pl.BlockSpec((n_rows, n_rows), const2),
    ]
    conv_b2 = conv_b.reshape(1, -1)
    block_tri = _block_tri(gb, ch)
    args = [proj3, proj3, proj3, proj3, dt3, conv_w, conv_w, conv_w, conv_b2, conv_b2, conv_b2,
            _head_select(dt_bias.shape[0]), _group_rows(dt_bias), _group_rows(a_log),
            _group_rows(d_skip), norm_g.astype(F32).reshape(SSD_GROUPS, 1, w),
            jnp.asarray(block_tri, dtype=BF16), jnp.asarray(block_tri)]
    if has_init:
        tail = CONV_WIDTH - 1
        in_specs += [
            pl.BlockSpec((gb, n_pairs, LANES, LANES), lambda b, t: (b, 0, 0, 0)),
            pl.BlockSpec((gb, tail, mix_b), lambda b, t: (b, 0, 0)),
            pl.BlockSpec((gb, tail, wn), lambda b, t: (b, 0, mix_b // wn)),
            pl.BlockSpec((gb, tail, wn), lambda b, t: (b, 0, mix_b // wn + 1)),
        ]
        args += [h0, conv0, conv0, conv0]
    kern = functools.partial(_ssd_kernel, ch=ch, valid=valid, nc=nc, gb=gb, pb=pb,
                             has_init=has_init)
    return pl.pallas_call(
        kern,
        grid=(bsz // gb, t_len // tt),
        in_specs=in_specs,
        out_specs=[
            pl.BlockSpec((gb, tt, mix_b), lambda b, t: (b, t, 0)),
            pl.BlockSpec((gb, n_pairs, LANES, LANES), lambda b, t: (b, 0, 0, 0)),
        ],
        out_shape=[
            jax.ShapeDtypeStruct((bsz, t_len, mix_b), BF16),
            jax.ShapeDtypeStruct((bsz, n_pairs, LANES, LANES), F32),
        ],
        scratch_shapes=[pltpu.VMEM((gb, LANES, mix_b), F32),
                        pltpu.VMEM((gb, ch + SUBLANES, mix_b), F32),
                        pltpu.VMEM((gb, ch + SUBLANES, wn), F32),
                        pltpu.VMEM((gb, ch + SUBLANES, wn), F32)],
        compiler_params=_cparams(("arbitrary", "arbitrary")),
        name="ssd_pairs",
    )(*args)


OUT_TM = 512
WEIGHT_STAGE_ROWS = 512
ROUTE_PARTS = 4
GROUP_LANE0 = N_EXPERTS
NEG_BIG = -1e30
NO_LANE = 4 * LANES


def _first_lane_of(mask, lane):
    return jnp.min(jnp.where(mask, lane, float(NO_LANE)), axis=1, keepdims=True)


def _route(logits):
    lane_i = lax.broadcasted_iota(I32, logits.shape, 1)
    lane = lane_i.astype(F32)
    is_grp = (lane_i >= GROUP_LANE0) & (lane_i < GROUP_LANE0 + N_EXPERT_GROUPS)
    lg = jnp.where(is_grp, logits, NEG_BIG)
    g_max = jnp.max(lg, axis=1, keepdims=True)
    g_sum = jnp.sum(jnp.where(is_grp, jnp.exp(lg - g_max), 0.0), axis=1, keepdims=True)
    g_idx = _first_lane_of(lg == g_max, lane) - GROUP_LANE0
    gp_top = 1.0 / g_sum
    lane_grp = jnp.right_shift(lane_i, 3).astype(F32)
    in_grp = (lane_i < N_EXPERTS) & (lane_grp == g_idx)
    le = jnp.where(in_grp, logits, NEG_BIG)
    e_max = jnp.max(le, axis=1, keepdims=True)
    e_exp = jnp.where(in_grp, jnp.exp(le - e_max), 0.0)
    ep = e_exp / jnp.sum(e_exp, axis=1, keepdims=True)
    ep = jnp.where(in_grp, ep, -1.0)
    p1 = jnp.max(ep, axis=1, keepdims=True)
    i1 = _first_lane_of(ep == p1, lane)
    ep2 = jnp.where(lane == i1, -1.0, ep)
    p2 = jnp.max(ep2, axis=1, keepdims=True)
    i2 = _first_lane_of(ep2 == p2, lane)
    den = p1 + p2
    return i1, i2, gp_top * p1 / den, gp_top * p2 / den


def _outproj_kernel(o_ref, y_ref, x_ref, g1_ref, sc_ref, sh_ref, ng_ref, w_hbm_ref, wr_ref, br_ref,
                    tri_ref, cnt0_ref,
                    x1_ref, hn_ref, ids_ref, wts_ref, rank_ref, cnt_ref,
                    wbf_ref, stage_ref, sem, cnts_ref):
    i = pl.program_id(0)
    tm, d = hn_ref.shape
    mix_a = o_ref.shape[1]
    tn = PROJ_TN
    n_col = d // tn
    gshape = (x_ref.shape[0], x_ref.shape[1], tn)

    @pl.when(i == 0)
    def _():
        cnts_ref[...] = cnt0_ref[...]
        rows = stage_ref.shape[0]
        for k in range(d // rows):
            cp = pltpu.make_async_copy(w_hbm_ref.at[pl.ds(k * rows, rows), :], stage_ref, sem)
            cp.start()
            cp.wait()
            wbf_ref[k * rows:(k + 1) * rows, :] = stage_ref[...].astype(BF16)

    o = o_ref[...]
    y = y_ref[...]
    for c in range(n_col):
        cols = slice(c * tn, (c + 1) * tn)
        acc = _dot(o, wbf_ref[0:mix_a, cols]) + _dot(y, wbf_ref[mix_a:d, cols])
        x1_ref[:, :, cols] = x_ref[:, :, cols] + g1_ref[:, :, cols] * acc.reshape(gshape)

    g_blk, r_blk = x_ref.shape[0], x_ref.shape[1]
    pr = tm // ROUTE_PARTS
    lane = lax.broadcasted_iota(I32, (pr, LANES), 1)
    onehots, hits = [], []
    for h in range(ROUTE_PARTS):
        rows = slice(h * pr, (h + 1) * pr)
        if g_blk == 1:
            gsl = (slice(None), rows)
        else:
            gsl = (slice(h * g_blk // ROUTE_PARTS, (h + 1) * g_blk // ROUTE_PARTS), slice(None))
        pshape = (g_blk if g_blk == 1 else g_blk // ROUTE_PARTS,
                  pr if g_blk == 1 else r_blk, tn)
        x1 = [x1_ref[gsl + (slice(c * tn, (c + 1) * tn),)].reshape(pr, tn) for c in range(n_col)]
        ssq = sum(jnp.sum(xc * xc, axis=-1, keepdims=True) for xc in x1)
        rs = lax.rsqrt(ssq / d + EPS)
        logits = jnp.zeros((pr, LANES), F32) + br_ref[...]
        for c in range(n_col):
            cols = slice(c * tn, (c + 1) * tn)
            sc = sc_ref[gsl[0], :, cols] if g_blk > 1 else sc_ref[:, :, cols]
            sh = sh_ref[gsl[0], :, cols] if g_blk > 1 else sh_ref[:, :, cols]
            hn = (x1[c] * rs * ng_ref[:, cols]).reshape(pshape)
            hn = (hn * (1.0 + sc) + sh).reshape(pr, tn)
            hn_ref[rows, cols] = hn
            logits = logits + _dot_hp(hn, wr_ref[cols, :])
        i1, i2, w1, w2 = _route(logits)
        hit1 = lane.astype(F32) == i1
        hit2 = lane.astype(F32) == i2
        onehots.append(jnp.where(hit1 | hit2, 1.0, 0.0).astype(BF16))
        hits.append((hit1, hit2))
        ids_ref[rows, :] = jnp.where(lane == 0, i1, jnp.where(lane == 1, i2, 0.0)).astype(I32)
        wts_ref[rows, :] = jnp.where(lane == 0, w1, jnp.where(lane == 1, w2, 0.0))

    total = cnts_ref[0:1, :]
    for h in range(ROUTE_PARTS):
        rows = slice(h * pr, (h + 1) * pr)
        hit1, hit2 = hits[h]
        before = _dot(tri_ref[...], onehots[h]) + total
        r1 = jnp.sum(jnp.where(hit1, before, 0.0), axis=1, keepdims=True)
        r2 = jnp.sum(jnp.where(hit2, before, 0.0), axis=1, keepdims=True)
        rank_ref[rows, :] = jnp.where(lane == 0, r1, jnp.where(lane == 1, r2, 0.0)).astype(I32)
        total = total + jnp.sum(onehots[h].astype(F32), axis=0, keepdims=True)
    cnts_ref[...] = jnp.broadcast_to(total, cnts_ref.shape)
    cnt_ref[...] = jnp.broadcast_to(total, cnt_ref.shape)


def _strict_lower(n):
    import numpy as np
    t = np.arange(n)[:, None]
    s = np.arange(n)[None, :]
    return jnp.asarray((s < t).astype(np.float32), dtype=BF16)


def _out_projection(o3, y3, x3, mod3, norm_g, w_out, w_router, b_router, cnt0, group_rows):
    bsz, t_len, d = x3.shape
    g_blk, r_blk = group_rows
    tm = g_blk * r_blk
    mix_a = o3.shape[-1]
    mix_b = y3.shape[-1]
    n_tok = bsz * t_len
    tiles_per_seq = max(t_len // r_blk, 1)

    xmap = lambda i: (i // tiles_per_seq, i % tiles_per_seq, 0)
    modmap = lambda s: (lambda i: (i // tiles_per_seq, 0, s))
    row = lambda i: (i, 0)
    const2 = lambda i: (0, 0)
    return pl.pallas_call(
        _outproj_kernel,
        grid=(n_tok // tm,),
        in_specs=[
            pl.BlockSpec((tm, mix_a), row),
            pl.BlockSpec((tm, mix_b), row),
            pl.BlockSpec((g_blk, r_blk, d), xmap),
            pl.BlockSpec((g_blk, 1, d), modmap(2)),
            pl.BlockSpec((g_blk, 1, d), modmap(4)),
            pl.BlockSpec((g_blk, 1, d), modmap(3)),
            pl.BlockSpec((1, d), const2),
            pl.BlockSpec(memory_space=pl.ANY),
            pl.BlockSpec((d, LANES), const2),
            pl.BlockSpec((1, LANES), const2),
            pl.BlockSpec((tm // ROUTE_PARTS, tm // ROUTE_PARTS), const2),
            pl.BlockSpec((SUBLANES, LANES), const2),
        ],
        out_specs=[
            pl.BlockSpec((g_blk, r_blk, d), xmap),
            pl.BlockSpec((tm, d), row),
            pl.BlockSpec((tm, LANES), row),
            pl.BlockSpec((tm, LANES), row),
            pl.BlockSpec((tm, LANES), row),
            pl.BlockSpec((SUBLANES, LANES), const2),
        ],
        out_shape=[
            jax.ShapeDtypeStruct((bsz, t_len, d), F32),
            jax.ShapeDtypeStruct((n_tok, d), F32),
            jax.ShapeDtypeStruct((n_tok, LANES), I32),
            jax.ShapeDtypeStruct((n_tok, LANES), F32),
            jax.ShapeDtypeStruct((n_tok, LANES), I32),
            jax.ShapeDtypeStruct((SUBLANES, LANES), F32),
        ],
        scratch_shapes=[pltpu.VMEM((d, d), BF16), pltpu.VMEM((WEIGHT_STAGE_ROWS, d), F32),
                        pltpu.SemaphoreType.DMA(()), pltpu.VMEM((SUBLANES, LANES), F32)],
        compiler_params=_cparams(("arbitrary",)),
        name="out_proj_router",
    )(o3.reshape(n_tok, mix_a), y3.reshape(n_tok, mix_b), x3, mod3, mod3, mod3,
      norm_g.reshape(1, d), w_out, w_router, b_router, _strict_lower(tm // ROUTE_PARTS), cnt0)


def _positions_kernel(ids_ref, rank_ref, cnt_ref, pos_ref, te_ref, nt_ref, lt_ref, *, block):
    lane8 = lax.broadcasted_iota(I32, (SUBLANES, LANES), 1)
    cnt = cnt_ref[...]
    tiles = jnp.floor((cnt + (MOE_TM - 1)) * (1.0 / MOE_TM))
    tiles = jnp.where(lane8 < N_EXPERTS, tiles, 0.0)
    incl = tiles
    shift = 1
    while shift < N_EXPERTS:
        incl = incl + jnp.where(lane8 >= shift, pltpu.roll(incl, shift, axis=1), 0.0)
        shift *= 2
    offs = ((incl - tiles) * MOE_TM)[0:1, :]
    nt_ref[...] = jnp.sum(tiles, axis=1, keepdims=True).astype(I32) + jnp.zeros(nt_ref.shape, I32)
    lt_ref[...] = jnp.where(tiles > 0.0, incl - 1.0, -1.0).astype(I32)

    incl_col = jnp.broadcast_to(incl[0:1, :], (LANES, LANES)).T
    e_row = lax.broadcasted_iota(I32, (LANES, LANES), 0)
    i_lane = lax.broadcasted_iota(I32, (LANES, LANES), 1).astype(F32)
    done = jnp.where((incl_col <= i_lane) & (e_row < N_EXPERTS), 1.0, 0.0)
    te = jnp.minimum(jnp.sum(done, axis=0, keepdims=True), N_EXPERTS - 1.0)
    te_ref[...] = te.astype(I32) + jnp.zeros(te_ref.shape, I32)

    lane = lax.broadcasted_iota(I32, (block, LANES), 1)

    def body(bi, c):
        r0 = pl.multiple_of(bi * block, block)
        ids = ids_ref[pl.ds(r0, block), :]
        rank = rank_ref[pl.ds(r0, block), :]
        out = jnp.zeros((block, LANES), I32)
        for k in range(TOP_K):
            off = jnp.sum(jnp.where(lane == ids[:, k:k + 1], offs, 0.0), axis=1, keepdims=True)
            out = jnp.where(lane == k, off.astype(I32) + rank[:, k:k + 1], out)
        pos_ref[pl.ds(r0, block), :] = out
        return c

    lax.fori_loop(0, ids_ref.shape[0] // block, body, 0)


def _positions(ids, rank, cnt):
    n_tok = ids.shape[0]
    block = 512
    kern = functools.partial(_positions_kernel, block=block)
    full = lambda s: pl.BlockSpec(s, lambda: tuple(0 for _ in s))
    return pl.pallas_call(
        kern,
        in_specs=[full((n_tok, LANES)), full((n_tok, LANES)), full((SUBLANES, LANES))],
        out_specs=[full((n_tok, LANES))] + [full((SUBLANES, LANES))] * 3,
        out_shape=[jax.ShapeDtypeStruct((n_tok, LANES), I32)]
        + [jax.ShapeDtypeStruct((SUBLANES, LANES), I32)] * 3,
        compiler_params=pltpu.CompilerParams(vmem_limit_bytes=VMEM_LIMIT),
        name="route_positions",
    )(ids, rank, cnt)


def _row_copy(src_ref, src_row, dst_ref, dst_row, sem):
    return pltpu.make_async_copy(src_ref.at[pl.ds(src_row, 1), :], dst_ref.at[pl.ds(dst_row, 1), :],
                                 sem)


def _dispatch_kernel(pos_ref, lt_ref, nt_ref, *refs, steps, n_tiles):
    n_groups = len(steps)
    hn_refs = refs[:n_groups]
    xg_ref, sem, zero_ref, zsem = refs[n_groups:]
    rows = hn_refs[0].shape[0]
    i = pl.program_id(0)

    def zero_copy(tile):
        return pltpu.make_async_copy(zero_ref, xg_ref.at[pl.ds(tile * MOE_TM, MOE_TM), :], zsem)

    @pl.when(i == 0)
    def _():
        zero_ref[...] = jnp.zeros(zero_ref.shape, zero_ref.dtype)
        nt = nt_ref[0]
        for e in range(N_EXPERTS):
            @pl.when(lt_ref[e] >= 0)
            def _():
                zero_copy(jnp.maximum(lt_ref[e], 0)).start()
        lax.fori_loop(nt, n_tiles, lambda t, c: (zero_copy(t).start(), c)[1], 0)
        for e in range(N_EXPERTS):
            @pl.when(lt_ref[e] >= 0)
            def _():
                zero_copy(0).wait()
        lax.fori_loop(nt, n_tiles, lambda t, c: (zero_copy(0).wait(), c)[1], 0)

    first = 0
    for hn_ref, n_steps in zip(hn_refs, steps):
        @pl.when((i >= first) & (i < first + n_steps))
        def _(hn_ref=hn_ref):
            base = i * rows

            def issue(r, c):
                for k in range(TOP_K):
                    _row_copy(hn_ref, r, xg_ref, pos_ref[(base + r) * TOP_K + k], sem).start(
                        priority=k)
                return c

            def drain(r, c):
                for k in range(TOP_K):
                    _row_copy(hn_ref, r, xg_ref, 0, sem).wait()
                return c

            lax.fori_loop(0, rows, issue, 0, unroll=8)
            lax.fori_loop(0, rows, drain, 0, unroll=8)

        first += n_steps


def _dispatch(pos_flat, last_tile, n_live, hns, n_tiles):
    d = hns[0].shape[1]
    steps = tuple(h.shape[0] // ROW_DMA_TILE for h in hns)
    firsts = [sum(steps[:g]) for g in range(len(steps))]

    def group_map(g):
        return lambda i, pos, lt, nt: (jnp.clip(i - firsts[g], 0, steps[g] - 1), 0)

    return pl.pallas_call(
        functools.partial(_dispatch_kernel, steps=steps, n_tiles=n_tiles),
        grid_spec=pltpu.PrefetchScalarGridSpec(
            num_scalar_prefetch=3,
            grid=(sum(steps),),
            in_specs=[pl.BlockSpec((ROW_DMA_TILE, d), group_map(g)) for g in range(len(hns))],
            out_specs=pl.BlockSpec(memory_space=pl.ANY),
            scratch_shapes=[pltpu.SemaphoreType.DMA(()), pltpu.VMEM((MOE_TM, d), F32),
                            pltpu.SemaphoreType.DMA(())],
        ),
        out_shape=jax.ShapeDtypeStruct((n_tiles * MOE_TM, d), F32),
        compiler_params=_cparams(("arbitrary",)),
        name="moe_dispatch",
    )(pos_flat, last_tile, n_live, *hns)


def _weight_copies(w_hbm_refs, expert, wbuf_ref, slot, sem_ref):
    copies = []
    for i, w_ref in enumerate(w_hbm_refs):
        half = w_ref.shape[1] // 2
        for part in range(2):
            rows = pl.ds(part * half, half)
            copies.append(pltpu.make_async_copy(w_ref.at[expert, rows], wbuf_ref.at[slot, i, rows],
                                                sem_ref.at[slot]))
    return copies


def _start_weight_copies(copies):
    for k, cp in enumerate(copies):
        cp.start(priority=k % 2)


def _expert_weights_slot(te_ref, nt, t, w_hbm_refs, wbuf_ref, sem_ref, slot_ref):
    expert = te_ref[t]

    @pl.when(t == 0)
    def _():
        slot_ref[0] = 0
        _start_weight_copies(_weight_copies(w_hbm_refs, expert, wbuf_ref, 0, sem_ref))

    @pl.when((t == 0) | (expert != te_ref[jnp.maximum(t - 1, 0)]))
    def _():
        @pl.when(t > 0)
        def _():
            slot_ref[0] = 1 - slot_ref[0]

        slot = slot_ref[0]
        for cp in _weight_copies(w_hbm_refs, expert, wbuf_ref, slot, sem_ref):
            cp.wait()
        nxt = lax.while_loop(lambda u: (u < nt) & (te_ref[jnp.minimum(u, nt - 1)] == expert),
                             lambda u: u + 1, t + 1)

        @pl.when(nxt < nt)
        def _():
            _start_weight_copies(_weight_copies(w_hbm_refs, te_ref[jnp.minimum(nxt, nt - 1)],
                                                wbuf_ref, 1 - slot, sem_ref))

    return slot_ref[0]


def _moe_up_kernel(te_ref, nt_ref, x_ref, w1_ref, w3_ref, act_ref, wbuf_ref, sem_ref, slot_ref):
    t = pl.program_id(0)
    nt = nt_ref[0]

    @pl.when(t < nt)
    def _():
        slot = _expert_weights_slot(te_ref, nt, t, (w1_ref, w3_ref), wbuf_ref, sem_ref, slot_ref)
        xb = x_ref[...].astype(BF16)
        a = _dot(xb, wbuf_ref[slot, 0].astype(BF16))
        b = _dot(xb, wbuf_ref[slot, 1].astype(BF16))
        act_ref[...] = (_silu(a) * b).astype(BF16)

    @pl.when(t >= nt)
    def _():
        act_ref[...] = jnp.zeros(act_ref.shape, act_ref.dtype)


def _moe_down_kernel(te_ref, nt_ref, act_ref, w2_ref, out_ref, wbuf_ref, sem_ref, slot_ref):
    t = pl.program_id(0)
    nt = nt_ref[0]

    @pl.when(t < nt)
    def _():
        slot = _expert_weights_slot(te_ref, nt, t, (w2_ref,), wbuf_ref, sem_ref, slot_ref)
        out_ref[...] = _dot(act_ref[...], wbuf_ref[slot, 0].astype(BF16))

    @pl.when(t >= nt)
    def _():
        out_ref[...] = jnp.zeros(out_ref.shape, out_ref.dtype)


def _moe_experts(te, nt, xg, w1, w3, w2):
    n_rows, d = xg.shape
    d_exp = w1.shape[-1]
    n_tiles = n_rows // MOE_TM
    live_tile = lambda t, te_r, nt_r: (jnp.minimum(t, nt_r[0] - 1), 0)
    any_spec = pl.BlockSpec(memory_space=pl.ANY)

    act = pl.pallas_call(
        _moe_up_kernel,
        grid_spec=pltpu.PrefetchScalarGridSpec(
            num_scalar_prefetch=2,
            grid=(n_tiles,),
            in_specs=[pl.BlockSpec((MOE_TM, d), live_tile), any_spec, any_spec],
            out_specs=pl.BlockSpec((MOE_TM, d_exp), lambda t, te_r, nt_r: (t, 0)),
            scratch_shapes=[pltpu.VMEM((2, 2, d, d_exp), F32), pltpu.SemaphoreType.DMA((2,)),
                            pltpu.SMEM((1,), I32)],
        ),
        out_shape=jax.ShapeDtypeStruct((n_rows, d_exp), BF16),
        compiler_params=_cparams(("arbitrary",)),
        name="moe_up",
    )(te, nt, xg, w1, w3)

    return pl.pallas_call(
        _moe_down_kernel,
        grid_spec=pltpu.PrefetchScalarGridSpec(
            num_scalar_prefetch=2,
            grid=(n_tiles,),
            in_specs=[pl.BlockSpec((MOE_TM, d_exp), live_tile), any_spec],
            out_specs=pl.BlockSpec((MOE_TM, d), lambda t, te_r, nt_r: (t, 0)),
            scratch_shapes=[pltpu.VMEM((2, 1, d_exp, d), F32), pltpu.SemaphoreType.DMA((2,)),
                            pltpu.SMEM((1,), I32)],
        ),
        out_shape=jax.ShapeDtypeStruct((n_rows, d), F32),
        compiler_params=_cparams(("arbitrary",)),
        name="moe_down",
    )(te, nt, act, w2)


def _combine_kernel(pos_ref, x1_ref, g2_ref, wts_ref, fg_ref, eo_ref, y_ref, buf_ref, sem):
    rows = wts_ref.shape[0]
    base = pl.program_id(0) * rows

    def issue(r, c):
        for k in range(TOP_K):
            _row_copy(eo_ref, pos_ref[(base + r) * TOP_K + k], buf_ref.at[k], r, sem).start(
                priority=k)
        return c

    def drain(r, c):
        for k in range(TOP_K):
            _row_copy(eo_ref, 0, buf_ref.at[k], r, sem).wait()
        return c

    lax.fori_loop(0, rows, issue, 0, unroll=8)
    lax.fori_loop(0, rows, drain, 0, unroll=8)
    w = wts_ref[...]
    moe = w[:, 0:1] * buf_ref[0] + w[:, 1:2] * buf_ref[1]
    x2 = x1_ref[...] + g2_ref[...] * moe.reshape(x1_ref.shape)
    ms = jnp.mean(x2 * x2, axis=-1, keepdims=True)
    y_ref[...] = x2 * lax.rsqrt(ms + EPS) * fg_ref[...]


def _combine(pos_flat, x1, mod3, wts, final_g, expert_out, group_rows):
    bsz, t_len, d = x1.shape
    g_blk, r_blk = group_rows
    tm = g_blk * r_blk
    tiles_per_seq = max(t_len // r_blk, 1)

    def xmap(i, pos):
        return (i // tiles_per_seq, i % tiles_per_seq, 0)

    return pl.pallas_call(
        _combine_kernel,
        grid_spec=pltpu.PrefetchScalarGridSpec(
            num_scalar_prefetch=1,
            grid=((bsz * t_len) // tm,),
            in_specs=[
                pl.BlockSpec((g_blk, r_blk, d), xmap),
                pl.BlockSpec((g_blk, 1, d), lambda i, pos: (i // tiles_per_seq, 0, 5)),
                pl.BlockSpec((tm, LANES), lambda i, pos: (i, 0)),
                pl.BlockSpec((1, d), lambda i, pos: (0, 0)),
                pl.BlockSpec(memory_space=pl.ANY),
            ],
            out_specs=pl.BlockSpec((g_blk, r_blk, d), xmap),
            scratch_shapes=[pltpu.VMEM((TOP_K, tm, d), F32), pltpu.SemaphoreType.DMA(())],
        ),
        out_shape=jax.ShapeDtypeStruct((bsz, t_len, d), F32),
        compiler_params=_cparams(("arbitrary",)),
        name="moe_combine",
    )(pos_flat, x1, mod3, wts, final_g.reshape(1, d), expert_out)


def kernel(x_prompt, x_sample, state_hgrn, state_ssm, state_conv, c_prompt, c_sample, ada_w, ada_b,
           norm1_g, norm2_g, w_in, hgrn_lb, hgrn_onorm_g, conv_w, conv_b, dt_bias, a_log, d_skip,
           ssm_norm_g, w_out, w_grp, b_grp, w_rt, b_rt, w1, w3, w2, final_g):
    depth = w_in.shape[0]
    assert depth == 1, "single-layer trunk"
    layer = 0
    bp, t_p, d = x_prompt.shape
    bs, t_s, _ = x_sample.shape
    mix_a = hgrn_onorm_g.shape[1]
    mix_b = ssm_norm_g.shape[1]
    conv_dim = conv_w.shape[2]
    n_main = 4 * mix_a + mix_b + conv_dim
    n_ssd_heads = dt_bias.shape[1]
    n_pairs = mix_b // LANES
    xbc0 = 4 * mix_a + mix_b
    tail = CONV_WIDTH - 1
    assert t_s <= SUBLANES and t_s >= tail and t_p % PROMPT_CHUNK == 0

    xs_pad = jnp.pad(x_sample, ((0, 0), (0, SUBLANES - t_s), (0, 0)))
    n_c = bp + bs
    c_rows = -(-n_c // SUBLANES) * SUBLANES
    c_all = jnp.pad(jnp.concatenate([c_prompt, c_sample], axis=0), ((0, c_rows - n_c), (0, 0)))

    mod = _modulation(c_all, ada_w[layer], ada_b[layer])
    mod_p = mod[:bp].reshape(bp, 1, -1)
    mod_s = mod[bp:n_c].reshape(bs, 1, -1)
    w_in_t = jnp.swapaxes(w_in[layer], 0, 1)
    w_dt_t = jnp.pad(w_in_t[n_main:], ((0, LANES - n_ssd_heads), (0, 0)))
    w_router = jnp.pad(jnp.concatenate([w_rt[layer], w_grp[layer]], axis=1),
                       ((0, 0), (0, LANES - N_EXPERTS - N_EXPERT_GROUPS)))
    b_router = jnp.pad(jnp.concatenate([b_rt[layer], b_grp[layer]]),
                       (0, LANES - N_EXPERTS - N_EXPERT_GROUPS)).reshape(1, LANES)

    groups = (
        (x_prompt, mod_p, None, None, None, t_p, PROMPT_CHUNK, PROMPT_CHUNK, MIXER_ROWS,
         (LANES // PROMPT_CHUNK, 8), LANES // PROMPT_CHUNK, 1, (1, OUT_TM)),
        (xs_pad, mod_s, state_hgrn[layer], state_ssm[layer].reshape(bs, n_pairs, LANES, LANES),
         state_conv[layer], t_s, SUBLANES, t_s, SUBLANES,
         (LANES // SUBLANES, 2), LANES // SUBLANES, PROJ_SEQS, (OUT_TM // SUBLANES, SUBLANES)),
    )

    cnt = jnp.zeros((SUBLANES, LANES), F32)
    per_group = []
    for x3, mod3, s0, h0, c0, t_real, ch, valid, tt, (hg_gb, hg_hb), ssd_gb, proj_gr, out_gr in groups:
        bsz, t_len, _ = x3.shape
        proj, dt = _in_projection(x3, mod3, norm1_g[layer], w_in_t, w_dt_t, n_main, proj_gr)
        proj3 = proj.reshape(bsz, t_len, n_main)
        dt3 = dt.reshape(bsz, t_len, LANES)
        o3, s_new = _hgrn_heads(proj3, hgrn_lb, hgrn_onorm_g[layer], s0, ch=ch, valid=valid,
                                gb=hg_gb, hb=hg_hb, tt=tt, layer=layer)
        y3, h_new = _ssd_pairs(proj3, dt3, conv_w[layer], conv_b[layer], dt_bias[layer],
                               a_log[layer], d_skip[layer], ssm_norm_g[layer], h0, c0, mix_a=mix_a,
                               mix_b=mix_b, ch=ch, valid=valid, gb=ssd_gb, tt=tt)
        conv_new = proj3[:, t_real - tail:t_real, xbc0:xbc0 + conv_dim]
        x1, hn, ids, wts, rank, cnt = _out_projection(
            o3, y3, x3, mod3, norm2_g[layer], w_out[layer], w_router, b_router, cnt, out_gr)
        per_group.append(dict(x1=x1, hn=hn, ids=ids, wts=wts, rank=rank, mod3=mod3, out_gr=out_gr,
                              s_new=s_new, h_new=h_new, conv_new=conv_new))

    ids_all = jnp.concatenate([g["ids"] for g in per_group], axis=0)
    rank_all = jnp.concatenate([g["rank"] for g in per_group], axis=0)
    n_tok = ids_all.shape[0]
    pos, te, nt, lt = _positions(ids_all, rank_all, cnt)
    pos_flat = pos[:, :TOP_K].reshape(-1)
    n_tiles = -(-(n_tok * TOP_K + N_EXPERTS * (MOE_TM - 1)) // MOE_TM)
    assert n_tiles <= LANES
    te_vec = te[0, :n_tiles]
    nt_vec = nt[0, :1]
    lt_vec = lt[0, :N_EXPERTS]

    row0 = 0
    for g in per_group:
        rows = g["hn"].shape[0]
        g["pos"] = pos_flat[row0 * TOP_K:(row0 + rows) * TOP_K]
        row0 += rows
    xg = _dispatch(pos_flat, lt_vec, nt_vec, [g["hn"] for g in per_group], n_tiles)
    expert_out = _moe_experts(te_vec, nt_vec, xg, w1[layer], w3[layer], w2[layer])
    ys = [_combine(g["pos"], g["x1"], g["mod3"], g["wts"], final_g, expert_out, g["out_gr"])
          for g in per_group]

    gp, gs = per_group
    return (
        ys[0],
        ys[1][:, :t_s],
        gp["s_new"][None],
        gp["h_new"].reshape(1, bp, n_ssd_heads, SSD_HEAD_DIM, SSD_STATE),
        gp["conv_new"][None],
        gs["s_new"][None],
        gs["h_new"].reshape(1, bs, n_ssd_heads, SSD_HEAD_DIM, SSD_STATE),
        gs["conv_new"][None],
    )
```

```python
import functools

import jax
import jax.numpy as jnp
from jax import lax
from jax.experimental import pallas as pl
from jax.experimental.pallas import tpu as pltpu

F32 = jnp.float32
BF16 = jnp.bfloat16
I32 = jnp.int32
EPS = 1e-6
LOG2E = 1.4426950408889634

LANES = 128
SUBLANES = 8
VMEM_LIMIT = 56 * 1024 * 1024

HGRN_HEAD_DIM = 128
SSD_HEAD_DIM = 64
SSD_GROUPS = 2
SSD_STATE = 128
CONV_WIDTH = 4
N_EXPERT_GROUPS = 4
EXPERTS_PER_GROUP = 8
N_EXPERTS = N_EXPERT_GROUPS * EXPERTS_PER_GROUP
TOP_K = 2

PROJ_SEQS = 128
NORM_ROWS = 256
PROJ_TN = 512
MOE_TM = 256
PROMPT_CHUNK = 64
MIXER_ROWS = 512
ROW_DMA_TILE = 256


def _cparams(sem):
    return pltpu.CompilerParams(dimension_semantics=sem, vmem_limit_bytes=VMEM_LIMIT)


def _dot(a, b):
    return jnp.dot(a, b, preferred_element_type=F32)


def _dot_nt(a, b):
    return lax.dot_general(a, b, (((1,), (1,)), ((), ())), preferred_element_type=F32)


def _split2(x):
    hi = x.astype(BF16)
    lo = (x - hi.astype(F32)).astype(BF16)
    return hi, lo


def _split3(x):
    hi = x.astype(BF16)
    r = x - hi.astype(F32)
    mid = r.astype(BF16)
    lo = (r - mid.astype(F32)).astype(BF16)
    return hi, mid, lo


def _dot_exact_lhs(m_bf16, x):
    hi, mid, lo = _split3(x)
    return _dot(m_bf16, hi) + _dot(m_bf16, mid) + _dot(m_bf16, lo)


def _dot_exact_rhs(x, m_bf16):
    hi, mid, lo = _split3(x)
    return _dot(hi, m_bf16) + _dot(mid, m_bf16) + _dot(lo, m_bf16)


def _dot_hp(a, w):
    ah, al = _split2(a)
    wh, wl = _split2(w)
    return _dot(ah, wh) + _dot(al, wh) + _dot(ah, wl)


def _silu(x):
    return x * jax.nn.sigmoid(x)


def _mod_kernel(c_ref, w_ref, b_ref, o_ref):
    a = _silu(c_ref[...])
    o_ref[...] = _dot_hp(a, w_ref[...]) + b_ref[...]


def _modulation(c_all, ada_w, ada_b):
    rows, d = c_all.shape
    n_out = ada_w.shape[1]
    tn = 1024
    return pl.pallas_call(
        _mod_kernel,
        grid=(n_out // tn,),
        in_specs=[
            pl.BlockSpec((rows, d), lambda j: (0, 0)),
            pl.BlockSpec((d, tn), lambda j: (0, j)),
            pl.BlockSpec((1, tn), lambda j: (0, j)),
        ],
        out_specs=pl.BlockSpec((rows, tn), lambda j: (0, j)),
        out_shape=jax.ShapeDtypeStruct((rows, n_out), F32),
        compiler_params=_cparams(("arbitrary",)),
        name="adaln_mod",
    )(c_all, ada_w, ada_b.reshape(1, n_out))


def _proj_kernel(x_hbm_ref, sc_ref, sh_ref, g_ref, w_hbm_ref, wdt_ref, o_ref, dt_ref, hn_ref, xs_ref,
                 sem, wbuf_ref, wsem):
    i = pl.program_id(0)
    j = pl.program_id(1)
    n_i = pl.num_programs(0)
    n_j = pl.num_programs(1)
    g_blk, r_blk, d = xs_ref.shape
    tn = wbuf_ref.shape[1]

    def x_copy(tile):
        return pltpu.make_async_copy(x_hbm_ref.at[pl.ds(tile * g_blk, g_blk)], xs_ref, sem)

    def w_copy(col_tile, slot):
        return pltpu.make_async_copy(w_hbm_ref.at[pl.ds(col_tile * tn, tn), :], wbuf_ref.at[slot],
                                     wsem.at[slot])

    slot = (i * n_j + j) % 2

    @pl.when((i == 0) & (j == 0))
    def _():
        w_copy(0, 0).start(priority=1)

    @pl.when((i + 1 < n_i) | (j + 1 < n_j))
    def _():
        w_copy((j + 1) % n_j, 1 - slot).start(priority=1)

    @pl.when(j == 0)
    def _():
        @pl.when(i == 0)
        def _():
            x_copy(0).start()

        x_copy(i).wait()
        wh, wl = _split2(wdt_ref[...])
        n_rows = g_blk * r_blk
        step = min(n_rows, NORM_ROWS)
        for c in range(n_rows // step):
            if g_blk == 1:
                x = xs_ref[:, c * step:(c + 1) * step, :]
                sc, sh = sc_ref[...], sh_ref[...]
            else:
                gs = slice(c * step // r_blk, (c + 1) * step // r_blk)
                x, sc, sh = xs_ref[gs], sc_ref[gs], sh_ref[gs]
            ms = jnp.mean(x * x, axis=-1, keepdims=True)
            hn = ((x * lax.rsqrt(ms + EPS)) * g_ref[...] * (1.0 + sc) + sh).reshape(step, d)
            rows = slice(c * step, (c + 1) * step)
            hn_ref[rows, :] = hn.astype(BF16)
            ah, al = _split2(hn)
            dt_ref[rows, :] = _dot_nt(ah, wh) + _dot_nt(al, wh) + _dot_nt(ah, wl)

        @pl.when(i + 1 < pl.num_programs(0))
        def _():
            x_copy(i + 1).start()

    w_copy(j, slot).wait()
    o_ref[...] = _dot_nt(hn_ref[...], wbuf_ref[slot].astype(BF16))


def _in_projection(x3, mod3, norm_g, w_t, w_dt_t, n_main, g_blk):
    bsz, t_len, d = x3.shape
    tm = g_blk * t_len
    n_row_tiles = bsz // g_blk

    def mod_map(sec):
        return lambda i, j: (i, 0, sec)

    return pl.pallas_call(
        _proj_kernel,
        grid=(n_row_tiles, n_main // PROJ_TN),
        in_specs=[
            pl.BlockSpec(memory_space=pl.ANY),
            pl.BlockSpec((g_blk, 1, d), mod_map(1)),
            pl.BlockSpec((g_blk, 1, d), mod_map(0)),
            pl.BlockSpec((1, d), lambda i, j: (0, 0)),
            pl.BlockSpec(memory_space=pl.ANY),
            pl.BlockSpec((LANES, d), lambda i, j: (0, 0)),
        ],
        out_specs=[
            pl.BlockSpec((tm, PROJ_TN), lambda i, j: (i, j)),
            pl.BlockSpec((tm, LANES), lambda i, j: (i, 0)),
        ],
        out_shape=[
            jax.ShapeDtypeStruct((bsz * t_len, n_main), F32),
            jax.ShapeDtypeStruct((bsz * t_len, LANES), F32),
        ],
        scratch_shapes=[pltpu.VMEM((tm, d), BF16), pltpu.VMEM((g_blk, t_len, d), F32),
                        pltpu.SemaphoreType.DMA(()), pltpu.VMEM((2, PROJ_TN, d), F32),
                        pltpu.SemaphoreType.DMA((2,))],
        compiler_params=_cparams(("arbitrary", "arbitrary")),
        name="in_proj",
    )(x3, mod3, mod3, norm_g.reshape(1, d), w_t, w_dt_t)


def _level_sizes(ch):
    sizes = []
    sz = SUBLANES
    while 2 * sz <= ch:
        sizes.append(sz)
        sz *= 2
    return sizes


def _level_masks(ch, n_rows):
    import numpy as np
    t = np.arange(n_rows)[:, None]
    s = np.arange(n_rows)[None, :]
    out = []
    for sz in _level_sizes(ch):
        m = (t // (2 * sz) == s // (2 * sz)) & (t % (2 * sz) >= sz) & (s % (2 * sz) < sz)
        out.append(m.astype(np.float32))
    if not out:
        out.append(np.zeros((n_rows, n_rows), np.float32))
    return jnp.asarray(np.stack(out))


def _hgrn_kernel(*refs, ch, valid, nc, gb, hb, has_init, layer):
    if has_init:
        (q_ref, f_ref, i_ref, g_ref, lb_ref, on_ref, tri_ref, masks_ref, s0_ref,
         o_ref, sout_ref, st_ref, b_scr, k_scr) = refs
    else:
        (q_ref, f_ref, i_ref, g_ref, lb_ref, on_ref, tri_ref, masks_ref,
         o_ref, sout_ref, st_ref, b_scr, k_scr) = refs
        s0_ref = None

    n_lb = lb_ref.shape[0]
    lb_rows = [lb_ref[i:i + 1, :] for i in range(n_lb)]
    lb_max = functools.reduce(jnp.maximum, lb_rows)
    lb_exp = [jnp.exp(r - lb_max) for r in lb_rows]
    lb = sum(lb_exp[:layer + 1]) / sum(lb_exp)
    onorm = on_ref[...]
    tri = tri_ref[...]
    sizes = _level_sizes(ch)
    lane8 = lax.broadcasted_iota(I32, (SUBLANES, LANES), 1)
    row8 = lax.broadcasted_iota(I32, (SUBLANES, LANES), 0)
    tb = pl.program_id(2)

    n_rows = gb * ch
    row_id = lax.broadcasted_iota(I32, (n_rows, LANES), 0)

    def stack(fn):
        parts = [fn(g) for g in range(gb)]
        return parts[0] if gb == 1 else jnp.concatenate(parts, axis=0)

    def chunk(hh, ci):
        cols = slice(hh * LANES, (hh + 1) * LANES)
        t0 = pl.multiple_of(ci * ch, ch)
        rows = pl.ds(t0, ch)
        lb_h = lb[:, cols]
        q = _silu(stack(lambda g: q_ref[g, rows, cols]))
        fg = lb_h + (1.0 - lb_h) * jax.nn.sigmoid(stack(lambda g: f_ref[g, rows, cols]))
        k = 1.0 - fg
        lf = jnp.log(fg)
        v = stack(lambda g: i_ref[g, rows, cols])
        if valid < ch:
            live = (row_id & (ch - 1)) < valid
            lf = jnp.where(live, lf, 0.0)
            k = jnp.where(live, k, 0.0)
        b2 = _dot_exact_lhs(tri, lf) * LOG2E
        b_scr[hh] = b2
        k_scr[hh] = k
        b_end_rows = [b_scr[hh, pl.ds(g * ch + ch - 1, 1), :] for g in range(gb)]
        b_end = stack(lambda g: jnp.broadcast_to(b_end_rows[g], (ch, LANES)))

        qe = (q * jnp.exp2(b2)).astype(BF16)
        o = stack(lambda g: _dot_nt(qe[g * ch:(g + 1) * ch], st_ref[g * hb + hh].astype(BF16)))

        scores = jnp.zeros((n_rows, n_rows), F32)
        for lvl, sz in enumerate(sizes):
            pieces = []
            for m in range(n_rows // (2 * sz)):
                r = b_scr[hh, pl.ds(2 * sz * m + sz - 1, 1), :]
                pieces.append(jnp.broadcast_to(r, (2 * sz, LANES)))
            r_all = pieces[0] if len(pieces) == 1 else jnp.concatenate(pieces, axis=0)
            e = jnp.exp2(-jnp.abs(b2 - r_all))
            s_l = _dot_nt((q * e).astype(BF16), (k * e).astype(BF16))
            scores = scores + masks_ref[lvl] * s_l
        blocks = []
        for jb in range(n_rows // SUBLANES):
            bb = b2[SUBLANES * jb:SUBLANES * (jb + 1)]
            qb = q[SUBLANES * jb:SUBLANES * (jb + 1)]
            acc = jnp.zeros((SUBLANES, LANES), F32)
            for sl in range(SUBLANES):
                s = SUBLANES * jb + sl
                bs = b_scr[hh, pl.ds(s, 1), :]
                ks = k_scr[hh, pl.ds(s, 1), :]
                val = (qb * ks) * jnp.exp2(bb - bs)
                red = jnp.sum(val, axis=1, keepdims=True)
                acc = jnp.where(lane8 == s, red, acc)
            blocks.append(jnp.where(lane8 - SUBLANES * jb <= row8, acc, 0.0))
        scores = scores + jnp.concatenate(blocks, axis=0)
        o = o + _dot(scores.astype(BF16), v.astype(BF16))

        dk = k * jnp.exp2(b_end - b2)
        v_t = v.T.astype(BF16)
        for g in range(gb):
            own = (row_id >= g * ch) & (row_id < (g + 1) * ch)
            c = g * hb + hh
            st_ref[c] = (st_ref[c] * jnp.exp2(b_end_rows[g])
                         + _dot(v_t, jnp.where(own, dk, 0.0).astype(BF16)))

        on = o * lax.rsqrt(jnp.mean(o * o, axis=-1, keepdims=True) + EPS) * onorm[:, cols]
        out = on * _silu(stack(lambda g: g_ref[g, rows, cols]))
        for g in range(gb):
            o_ref[g, rows, cols] = out[g * ch:(g + 1) * ch].astype(o_ref.dtype)

    chains = [(g * hb + hh, g, hh) for g in range(gb) for hh in range(hb)]

    @pl.when(tb == 0)
    def _():
        for c, g, hh in chains:
            if has_init:
                st_ref[c] = s0_ref[g, hh].T
            else:
                st_ref[c] = jnp.zeros((LANES, LANES), F32)

    def step(ci, carry):
        for hh in range(hb):
            chunk(hh, ci)
        return carry

    if nc == 1:
        step(0, 0)
    else:
        lax.fori_loop(0, nc, step, 0)

    @pl.when(tb == pl.num_programs(2) - 1)
    def _():
        for c, g, hh in chains:
            sout_ref[g, hh] = st_ref[c].T


def _hgrn_heads(proj3, hgrn_lb, onorm_g, s0, *, ch, valid, gb, hb, tt, layer):
    bsz, t_len, _ = proj3.shape
    n_heads = onorm_g.shape[0] // HGRN_HEAD_DIM
    n_hg = n_heads // hb
    nc = tt // ch
    n_rows = gb * ch
    assert n_rows == LANES
    has_init = s0 is not None
    w = hb * LANES

    def col(sec):
        return lambda b, h, t: (b, t, sec * n_hg + h)

    in_specs = [pl.BlockSpec((gb, tt, w), col(s)) for s in range(4)]
    in_specs += [
        pl.BlockSpec((hgrn_lb.shape[0], w), lambda b, h, t: (0, h)),
        pl.BlockSpec((1, w), lambda b, h, t: (0, h)),
        pl.BlockSpec((n_rows, n_rows), lambda b, h, t: (0, 0)),
        pl.BlockSpec((max(len(_level_sizes(ch)), 1), n_rows, n_rows), lambda b, h, t: (0, 0, 0)),
    ]
    args = [proj3, proj3, proj3, proj3, hgrn_lb, onorm_g.reshape(1, -1),
            jnp.asarray(_block_tri(gb, ch), dtype=BF16), _level_masks(ch, n_rows)]
    if has_init:
        in_specs.append(pl.BlockSpec((gb, hb, LANES, LANES), lambda b, h, t: (b, h, 0, 0)))
        args.append(s0)
    kern = functools.partial(_hgrn_kernel, ch=ch, valid=valid, nc=nc, gb=gb, hb=hb,
                             has_init=has_init, layer=layer)
    n_chain = gb * hb
    return pl.pallas_call(
        kern,
        grid=(bsz // gb, n_hg, t_len // tt),
        in_specs=in_specs,
        out_specs=[
            pl.BlockSpec((gb, tt, w), lambda b, h, t: (b, t, h)),
            pl.BlockSpec((gb, hb, LANES, LANES), lambda b, h, t: (b, h, 0, 0)),
        ],
        out_shape=[
            jax.ShapeDtypeStruct((bsz, t_len, n_heads * HGRN_HEAD_DIM), BF16),
            jax.ShapeDtypeStruct((bsz, n_heads, LANES, LANES), F32),
        ],
        scratch_shapes=[pltpu.VMEM((n_chain, LANES, LANES), F32),
                        pltpu.VMEM((hb, n_rows, LANES), F32),
                        pltpu.VMEM((hb, n_rows, LANES), F32)],
        compiler_params=_cparams(("arbitrary", "arbitrary", "arbitrary")),
        name="hgrn2_heads",
    )(*args)


def _softplus(x):
    return jnp.maximum(x, 0.0) + jnp.log1p(jnp.exp(-jnp.abs(x)))


def _ssd_kernel(*refs, ch, valid, nc, gb, pb, has_init):
    if has_init:
        (z_ref, x_ref, b_ref, c_ref, dt_ref, cwx_ref, cwb_ref, cwc_ref, cbx_ref, cbb_ref, cbc_ref,
         xj_ref, bias_ref, alog_ref, d_ref, ng_ref, tri_ref, tril_ref, h0_ref, cx0_ref, cb0_ref,
         cc0_ref, y_ref, hout_ref, ht_ref, wx_ref, wb_ref, wc_ref) = refs
    else:
        (z_ref, x_ref, b_ref, c_ref, dt_ref, cwx_ref, cwb_ref, cwc_ref, cbx_ref, cbb_ref, cbc_ref,
         xj_ref, bias_ref, alog_ref, d_ref, ng_ref, tri_ref, tril_ref,
         y_ref, hout_ref, ht_ref, wx_ref, wb_ref, wc_ref) = refs
        h0_ref = cx0_ref = cb0_ref = cc0_ref = None

    tri = tri_ref[...]
    tril = tril_ref[...]
    lane = lax.broadcasted_iota(I32, (LANES, LANES), 1)
    first_head = lane < SSD_HEAD_DIM
    tail = CONV_WIDTH - 1
    tb = pl.program_id(1)

    def conv(raw_ref, win_ref, cw_ref, cb_ref, g, t0):
        win_ref[g, pl.ds(SUBLANES, ch), :] = raw_ref[g, pl.ds(t0, ch), :]
        u = cb_ref[...]
        for j in range(CONV_WIDTH):
            u = u + win_ref[g, pl.ds(SUBLANES - tail + j, ch), :] * cw_ref[j:j + 1, :]
        win_ref[g, pl.ds(0, SUBLANES), :] = win_ref[g, pl.ds(ch, SUBLANES), :]
        return _silu(u)

    n_rows = gb * ch
    w = pb * LANES
    n_grp = SSD_GROUPS
    row_id = lax.broadcasted_iota(I32, (n_rows, w), 0)

    def stack(fn):
        parts = [fn(g) for g in range(gb)]
        return parts[0] if gb == 1 else jnp.concatenate(parts, axis=0)

    def group(q, rows, xs, bm, cm, dt_raw, z):
        wcols = slice(q * w, (q + 1) * w)
        dt = _softplus(_dot_exact_rhs(dt_raw, xj_ref[q]) + bias_ref[q])
        if valid < ch:
            dt = jnp.where((row_id & (ch - 1)) < valid, dt, 0.0)
        la_cs = _dot_exact_lhs(tri, dt * (-jnp.exp(alog_ref[q])))
        xdt = xs * dt
        a_end_rows = [la_cs[g * ch + ch - 1:g * ch + ch, :] for g in range(gb)]
        a_end = stack(lambda g: jnp.broadcast_to(a_end_rows[g], (ch, w)))

        cb = _dot_nt(cm.astype(BF16), bm.astype(BF16))
        acs_t = la_cs.T
        y_parts = []
        for p in range(pb):
            cols = slice(p * LANES, (p + 1) * LANES)
            scores = []
            for hl in range(2):
                at = p * LANES + hl * SSD_HEAD_DIM
                lmat = jnp.exp(jnp.minimum(la_cs[:, at:at + 1] - acs_t[at:at + 1, :], 0.0)) * tril
                scores.append((cb * lmat).astype(BF16))
            xp = xdt[:, cols]
            rhs = jnp.concatenate([jnp.where(first_head, xp, 0.0),
                                   jnp.where(first_head, 0.0, xp)], axis=0).astype(BF16)
            y_parts.append(_dot(jnp.concatenate(scores, axis=1), rhs))
        y = y_parts[0] if pb == 1 else jnp.concatenate(y_parts, axis=1)

        y_off = stack(lambda g: _dot(cm[g * ch:(g + 1) * ch].astype(BF16),
                                     ht_ref[g, :, wcols].astype(BF16)))
        y = y + y_off * jnp.exp(la_cs)
        b_t = bm.T.astype(BF16)
        upd = xdt * jnp.exp(a_end - la_cs)
        for g in range(gb):
            own = (row_id >= g * ch) & (row_id < (g + 1) * ch)
            ht_ref[g, :, wcols] = (ht_ref[g, :, wcols] * jnp.exp(a_end_rows[g])
                                   + _dot(b_t, jnp.where(own, upd, 0.0).astype(BF16)))

        y = (y + d_ref[q] * xs) * _silu(z)
        y = y * lax.rsqrt(jnp.mean(y * y, axis=-1, keepdims=True) + EPS) * ng_ref[q]
        for g in range(gb):
            y_ref[g, rows, wcols] = y[g * ch:(g + 1) * ch].astype(y_ref.dtype)

    def step(ci, carry):
        t0 = pl.multiple_of(ci * ch, ch)
        rows = pl.ds(t0, ch)
        xs = stack(lambda g: conv(x_ref, wx_ref, cwx_ref, cbx_ref, g, t0))
        bm = stack(lambda g: conv(b_ref, wb_ref, cwb_ref, cbb_ref, g, t0))
        cm = stack(lambda g: conv(c_ref, wc_ref, cwc_ref, cbc_ref, g, t0))
        dt_raw = stack(lambda g: dt_ref[g, rows, :])
        z = stack(lambda g: z_ref[g, rows, :])
        for q in range(n_grp):
            wcols = slice(q * w, (q + 1) * w)
            ncols = slice(q * LANES, (q + 1) * LANES)
            group(q, rows, xs[:, wcols], bm[:, ncols], cm[:, ncols], dt_raw, z[:, wcols])
        return carry

    @pl.when(tb == 0)
    def _():
        for g in range(gb):
            for win_ref, c0_ref in ((wx_ref, cx0_ref), (wb_ref, cb0_ref), (wc_ref, cc0_ref)):
                win_ref[g, pl.ds(0, SUBLANES), :] = jnp.zeros((SUBLANES, win_ref.shape[2]), F32)
                if has_init:
                    win_ref[g, pl.ds(SUBLANES - tail, tail), :] = c0_ref[g]
            for p in range(n_grp * pb):
                cols = slice(p * LANES, (p + 1) * LANES)
                if has_init:
                    ht_ref[g, :, cols] = h0_ref[g, p].T
                else:
                    ht_ref[g, :, cols] = jnp.zeros((LANES, LANES), F32)

    if nc == 1:
        step(0, 0)
    else:
        lax.fori_loop(0, nc, step, 0)

    @pl.when(tb == pl.num_programs(1) - 1)
    def _():
        for g in range(gb):
            for p in range(n_grp * pb):
                hout_ref[g, p] = ht_ref[g, :, p * LANES:(p + 1) * LANES].T


def _group_rows(v):
    return jnp.repeat(v.astype(F32), SSD_HEAD_DIM).reshape(SSD_GROUPS, 1, -1)


def _head_select(n_heads):
    import numpy as np
    w = n_heads // SSD_GROUPS * SSD_HEAD_DIM
    h = np.arange(LANES)[None, :, None]
    lane = np.arange(w)[None, None, :]
    g = np.arange(SSD_GROUPS)[:, None, None]
    return jnp.asarray((h == g * (n_heads // SSD_GROUPS) + lane // SSD_HEAD_DIM).astype(np.float32),
                       dtype=BF16)


def _block_tri(gb, ch):
    import numpy as np
    t = np.arange(gb * ch)[:, None]
    s = np.arange(gb * ch)[None, :]
    return ((t // ch == s // ch) & (s <= t)).astype(np.float32)


def _ssd_pairs(proj3, dt3, conv_w, conv_b, dt_bias, a_log, d_skip, norm_g, h0, conv0, *, mix_a,
               mix_b, ch, valid, gb, tt):
    bsz, t_len, _ = proj3.shape
    n_pairs = mix_b // LANES
    pb = n_pairs // SSD_GROUPS
    w = pb * LANES
    nc = tt // ch
    n_rows = gb * ch
    has_init = h0 is not None
    wn = SSD_GROUPS * LANES
    assert (4 * mix_a) % mix_b == 0 and (4 * mix_a + 2 * mix_b) % wn == 0 and n_rows == LANES
    z0 = 4 * mix_a // mix_b
    x0 = z0 + 1
    b0 = (4 * mix_a + 2 * mix_b) // wn
    c0 = b0 + 1

    seq = lambda blk: (lambda b, t: (b, t, blk))
    par = lambda blk: (lambda b, t: (0, blk))
    const2 = lambda b, t: (0, 0)
    const3 = lambda b, t: (0, 0, 0)

    in_specs = [
        pl.BlockSpec((gb, tt, mix_b), seq(z0)),
        pl.BlockSpec((gb, tt, mix_b), seq(x0)),
        pl.BlockSpec((gb, tt, wn), seq(b0)),
        pl.BlockSpec((gb, tt, wn), seq(c0)),
        pl.BlockSpec((gb, tt, LANES), seq(0)),
        pl.BlockSpec((CONV_WIDTH, mix_b), par(0)),
        pl.BlockSpec((CONV_WIDTH, wn), par(mix_b // wn)),
        pl.BlockSpec((CONV_WIDTH, wn), par(mix_b // wn + 1)),
        pl.BlockSpec((1, mix_b), par(0)),
        pl.BlockSpec((1, wn), par(mix_b // wn)),
        pl.BlockSpec((1, wn), par(mix_b // wn + 1)),
        pl.BlockSpec((SSD_GROUPS, LANES, w), const3),
        pl.BlockSpec((SSD_GROUPS, 1, w), const3),
        pl.BlockSpec((SSD_GROUPS, 1, w), const3),
        pl.BlockSpec((SSD_GROUPS, 1, w), const3),
        pl.BlockSpec((SSD_GROUPS, 1, w), const3),
        pl.BlockSpec((n_rows, n_rows), const2),
        pl.BlockSpec((n_rows, n_rows), const2),
    ]
    conv_b2 = conv_b.reshape(1, -1)
    block_tri = _block_tri(gb, ch)
    args = [proj3, proj3, proj3, proj3, dt3, conv_w, conv_w, conv_w, conv_b2, conv_b2, conv_b2,
            _head_select(dt_bias.shape[0]), _group_rows(dt_bias), _group_rows(a_log),
            _group_rows(d_skip), norm_g.astype(F32).reshape(SSD_GROUPS, 1, w),
            jnp.asarray(block_tri, dtype=BF16), jnp.asarray(block_tri)]
    if has_init:
        tail = CONV_WIDTH - 1
        in_specs += [
            pl.BlockSpec((gb, n_pairs, LANES, LANES), lambda b, t: (b, 0, 0, 0)),
            pl.BlockSpec((gb, tail, mix_b), lambda b, t: (b, 0, 0)),
            pl.BlockSpec((gb, tail, wn), lambda b, t: (b, 0, mix_b // wn)),
            pl.BlockSpec((gb, tail, wn), lambda b, t: (b, 0, mix_b // wn + 1)),
        ]
        args += [h0, conv0, conv0, conv0]
    kern = functools.partial(_ssd_kernel, ch=ch, valid=valid, nc=nc, gb=gb, pb=pb,
                             has_init=has_init)
    return pl.pallas_call(
        kern,
        grid=(bsz // gb, t_len // tt),
        in_specs=in_specs,
        out_specs=[
            pl.BlockSpec((gb, tt, mix_b), lambda b, t: (b, t, 0)),
            pl.BlockSpec((gb, n_pairs, LANES, LANES), lambda b, t: (b, 0, 0, 0)),
        ],
        out_shape=[
            jax.ShapeDtypeStruct((bsz, t_len, mix_b), BF16),
            jax.ShapeDtypeStruct((bsz, n_pairs, LANES, LANES), F32),
        ],
        scratch_shapes=[pltpu.VMEM((gb, LANES, mix_b), F32),
                        pltpu.VMEM((gb, ch + SUBLANES, mix_b), F32),
                        pltpu.VMEM((gb, ch + SUBLANES, wn), F32),
                        pltpu.VMEM((gb, ch + SUBLANES, wn), F32)],
        compiler_params=_cparams(("arbitrary", "arbitrary")),
        name="ssd_pairs",
    )(*args)


OUT_TM = 512
WEIGHT_STAGE_ROWS = 512
ROUTE_PARTS = 4
GROUP_LANE0 = N_EXPERTS
NEG_BIG = -1e30
NO_LANE = 4 * LANES


def _first_lane_of(mask, lane):
    return jnp.min(jnp.where(mask, lane, float(NO_LANE)), axis=1, keepdims=True)


def _route(logits):
    lane_i = lax.broadcasted_iota(I32, logits.shape, 1)
    lane = lane_i.astype(F32)
    is_grp = (lane_i >= GROUP_LANE0) & (lane_i < GROUP_LANE0 + N_EXPERT_GROUPS)
    lg = jnp.where(is_grp, logits, NEG_BIG)
    g_max = jnp.max(lg, axis=1, keepdims=True)
    g_sum = jnp.sum(jnp.where(is_grp, jnp.exp(lg - g_max), 0.0), axis=1, keepdims=True)
    g_idx = _first_lane_of(lg == g_max, lane) - GROUP_LANE0
    gp_top = 1.0 / g_sum
    lane_grp = jnp.right_shift(lane_i, 3).astype(F32)
    in_grp = (lane_i < N_EXPERTS) & (lane_grp == g_idx)
    le = jnp.where(in_grp, logits, NEG_BIG)
    e_max = jnp.max(le, axis=1, keepdims=True)
    e_exp = jnp.where(in_grp, jnp.exp(le - e_max), 0.0)
    ep = e_exp / jnp.sum(e_exp, axis=1, keepdims=True)
    ep = jnp.where(in_grp, ep, -1.0)
    p1 = jnp.max(ep, axis=1, keepdims=True)
    i1 = _first_lane_of(ep == p1, lane)
    ep2 = jnp.where(lane == i1, -1.0, ep)
    p2 = jnp.max(ep2, axis=1, keepdims=True)
    i2 = _first_lane_of(ep2 == p2, lane)
    den = p1 + p2
    return i1, i2, gp_top * p1 / den, gp_top * p2 / den


def _outproj_kernel(o_ref, y_ref, x_ref, g1_ref, sc_ref, sh_ref, ng_ref, w_hbm_ref, wr_ref, br_ref,
                    tri_ref, cnt0_ref,
                    x1_ref, hn_ref, ids_ref, wts_ref, rank_ref, cnt_ref,
                    wbf_ref, stage_ref, sem, cnts_ref):
    i = pl.program_id(0)
    tm, d = hn_ref.shape
    mix_a = o_ref.shape[1]
    tn = PROJ_TN
    n_col = d // tn
    gshape = (x_ref.shape[0], x_ref.shape[1], tn)

    @pl.when(i == 0)
    def _():
        cnts_ref[...] = cnt0_ref[...]
        rows = stage_ref.shape[0]
        for k in range(d // rows):
            cp = pltpu.make_async_copy(w_hbm_ref.at[pl.ds(k * rows, rows), :], stage_ref, sem)
            cp.start()
            cp.wait()
            wbf_ref[k * rows:(k + 1) * rows, :] = stage_ref[...].astype(BF16)

    o = o_ref[...]
    y = y_ref[...]
    for c in range(n_col):
        cols = slice(c * tn, (c + 1) * tn)
        acc = _dot(o, wbf_ref[0:mix_a, cols]) + _dot(y, wbf_ref[mix_a:d, cols])
        x1_ref[:, :, cols] = x_ref[:, :, cols] + g1_ref[:, :, cols] * acc.reshape(gshape)

    g_blk, r_blk = x_ref.shape[0], x_ref.shape[1]
    pr = tm // ROUTE_PARTS
    lane = lax.broadcasted_iota(I32, (pr, LANES), 1)
    onehots, hits = [], []
    for h in range(ROUTE_PARTS):
        rows = slice(h * pr, (h + 1) * pr)
        if g_blk == 1:
            gsl = (slice(None), rows)
        else:
            gsl = (slice(h * g_blk // ROUTE_PARTS, (h + 1) * g_blk // ROUTE_PARTS), slice(None))
        pshape = (g_blk if g_blk == 1 else g_blk // ROUTE_PARTS,
                  pr if g_blk == 1 else r_blk, tn)
        x1 = [x1_ref[gsl + (slice(c * tn, (c + 1) * tn),)].reshape(pr, tn) for c in range(n_col)]
        ssq = sum(jnp.sum(xc * xc, axis=-1, keepdims=True) for xc in x1)
        rs = lax.rsqrt(ssq / d + EPS)
        logits = jnp.zeros((pr, LANES), F32) + br_ref[...]
        for c in range(n_col):
            cols = slice(c * tn, (c + 1) * tn)
            sc = sc_ref[gsl[0], :, cols] if g_blk > 1 else sc_ref[:, :, cols]
            sh = sh_ref[gsl[0], :, cols] if g_blk > 1 else sh_ref[:, :, cols]
            hn = (x1[c] * rs * ng_ref[:, cols]).reshape(pshape)
            hn = (hn * (1.0 + sc) + sh).reshape(pr, tn)
            hn_ref[rows, cols] = hn
            logits = logits + _dot_hp(hn, wr_ref[cols, :])
        i1, i2, w1, w2 = _route(logits)
        hit1 = lane.astype(F32) == i1
        hit2 = lane.astype(F32) == i2
        onehots.append(jnp.where(hit1 | hit2, 1.0, 0.0).astype(BF16))
        hits.append((hit1, hit2))
        ids_ref[rows, :] = jnp.where(lane == 0, i1, jnp.where(lane == 1, i2, 0.0)).astype(I32)
        wts_ref[rows, :] = jnp.where(lane == 0, w1, jnp.where(lane == 1, w2, 0.0))

    total = cnts_ref[0:1, :]
    for h in range(ROUTE_PARTS):
        rows = slice(h * pr, (h + 1) * pr)
        hit1, hit2 = hits[h]
        before = _dot(tri_ref[...], onehots[h]) + total
        r1 = jnp.sum(jnp.where(hit1, before, 0.0), axis=1, keepdims=True)
        r2 = jnp.sum(jnp.where(hit2, before, 0.0), axis=1, keepdims=True)
        rank_ref[rows, :] = jnp.where(lane == 0, r1, jnp.where(lane == 1, r2, 0.0)).astype(I32)
        total = total + jnp.sum(onehots[h].astype(F32), axis=0, keepdims=True)
    cnts_ref[...] = jnp.broadcast_to(total, cnts_ref.shape)
    cnt_ref[...] = jnp.broadcast_to(total, cnt_ref.shape)


def _strict_lower(n):
    import numpy as np
    t = np.arange(n)[:, None]
    s = np.arange(n)[None, :]
    return jnp.asarray((s < t).astype(np.float32), dtype=BF16)


def _out_projection(o3, y3, x3, mod3, norm_g, w_out, w_router, b_router, cnt0, group_rows):
    bsz, t_len, d = x3.shape
    g_blk, r_blk = group_rows
    tm = g_blk * r_blk
    mix_a = o3.shape[-1]
    mix_b = y3.shape[-1]
    n_tok = bsz * t_len
    tiles_per_seq = max(t_len // r_blk, 1)

    xmap = lambda i: (i // tiles_per_seq, i % tiles_per_seq, 0)
    modmap = lambda s: (lambda i: (i // tiles_per_seq, 0, s))
    row = lambda i: (i, 0)
    const2 = lambda i: (0, 0)
    return pl.pallas_call(
        _outproj_kernel,
        grid=(n_tok // tm,),
        in_specs=[
            pl.BlockSpec((tm, mix_a), row),
            pl.BlockSpec((tm, mix_b), row),
            pl.BlockSpec((g_blk, r_blk, d), xmap),
            pl.BlockSpec((g_blk, 1, d), modmap(2)),
            pl.BlockSpec((g_blk, 1, d), modmap(4)),
            pl.BlockSpec((g_blk, 1, d), modmap(3)),
            pl.BlockSpec((1, d), const2),
            pl.BlockSpec(memory_space=pl.ANY),
            pl.BlockSpec((d, LANES), const2),
            pl.BlockSpec((1, LANES), const2),
            pl.BlockSpec((tm // ROUTE_PARTS, tm // ROUTE_PARTS), const2),
            pl.BlockSpec((SUBLANES, LANES), const2),
        ],
        out_specs=[
            pl.BlockSpec((g_blk, r_blk, d), xmap),
            pl.BlockSpec((tm, d), row),
            pl.BlockSpec((tm, LANES), row),
            pl.BlockSpec((tm, LANES), row),
            pl.BlockSpec((tm, LANES), row),
            pl.BlockSpec((SUBLANES, LANES), const2),
        ],
        out_shape=[
            jax.ShapeDtypeStruct((bsz, t_len, d), F32),
            jax.ShapeDtypeStruct((n_tok, d), F32),
            jax.ShapeDtypeStruct((n_tok, LANES), I32),
            jax.ShapeDtypeStruct((n_tok, LANES), F32),
            jax.ShapeDtypeStruct((n_tok, LANES), I32),
            jax.ShapeDtypeStruct((SUBLANES, LANES), F32),
        ],
        scratch_shapes=[pltpu.VMEM((d, d), BF16), pltpu.VMEM((WEIGHT_STAGE_ROWS, d), F32),
                        pltpu.SemaphoreType.DMA(()), pltpu.VMEM((SUBLANES, LANES), F32)],
        compiler_params=_cparams(("arbitrary",)),
        name="out_proj_router",
    )(o3.reshape(n_tok, mix_a), y3.reshape(n_tok, mix_b), x3, mod3, mod3, mod3,
      norm_g.reshape(1, d), w_out, w_router, b_router, _strict_lower(tm // ROUTE_PARTS), cnt0)


def _positions_kernel(ids_ref, rank_ref, cnt_ref, pos_ref, te_ref, nt_ref, lt_ref, *, block):
    lane8 = lax.broadcasted_iota(I32, (SUBLANES, LANES), 1)
    cnt = cnt_ref[...]
    tiles = jnp.floor((cnt + (MOE_TM - 1)) * (1.0 / MOE_TM))
    tiles = jnp.where(lane8 < N_EXPERTS, tiles, 0.0)
    incl = tiles
    shift = 1
    while shift < N_EXPERTS:
        incl = incl + jnp.where(lane8 >= shift, pltpu.roll(incl, shift, axis=1), 0.0)
        shift *= 2
    offs = ((incl - tiles) * MOE_TM)[0:1, :]
    nt_ref[...] = jnp.sum(tiles, axis=1, keepdims=True).astype(I32) + jnp.zeros(nt_ref.shape, I32)
    lt_ref[...] = jnp.where(tiles > 0.0, incl - 1.0, -1.0).astype(I32)

    incl_col = jnp.broadcast_to(incl[0:1, :], (LANES, LANES)).T
    e_row = lax.broadcasted_iota(I32, (LANES, LANES), 0)
    i_lane = lax.broadcasted_iota(I32, (LANES, LANES), 1).astype(F32)
    done = jnp.where((incl_col <= i_lane) & (e_row < N_EXPERTS), 1.0, 0.0)
    te = jnp.minimum(jnp.sum(done, axis=0, keepdims=True), N_EXPERTS - 1.0)
    te_ref[...] = te.astype(I32) + jnp.zeros(te_ref.shape, I32)

    lane = lax.broadcasted_iota(I32, (block, LANES), 1)

    def body(bi, c):
        r0 = pl.multiple_of(bi * block, block)
        ids = ids_ref[pl.ds(r0, block), :]
        rank = rank_ref[pl.ds(r0, block), :]
        out = jnp.zeros((block, LANES), I32)
        for k in range(TOP_K):
            off = jnp.sum(jnp.where(lane == ids[:, k:k + 1], offs, 0.0), axis=1, keepdims=True)
            out = jnp.where(lane == k, off.astype(I32) + rank[:, k:k + 1], out)
        pos_ref[pl.ds(r0, block), :] = out
        return c

    lax.fori_loop(0, ids_ref.shape[0] // block, body, 0)


def _positions(ids, rank, cnt):
    n_tok = ids.shape[0]
    block = 512
    kern = functools.partial(_positions_kernel, block=block)
    full = lambda s: pl.BlockSpec(s, lambda: tuple(0 for _ in s))
    return pl.pallas_call(
        kern,
        in_specs=[full((n_tok, LANES)), full((n_tok, LANES)), full((SUBLANES, LANES))],
        out_specs=[full((n_tok, LANES))] + [full((SUBLANES, LANES))] * 3,
        out_shape=[jax.ShapeDtypeStruct((n_tok, LANES), I32)]
        + [jax.ShapeDtypeStruct((SUBLANES, LANES), I32)] * 3,
        compiler_params=pltpu.CompilerParams(vmem_limit_bytes=VMEM_LIMIT),
        name="route_positions",
    )(ids, rank, cnt)


def _row_copy(src_ref, src_row, dst_ref, dst_row, sem):
    return pltpu.make_async_copy(src_ref.at[pl.ds(src_row, 1), :], dst_ref.at[pl.ds(dst_row, 1), :],
                                 sem)


def _dispatch_kernel(pos_ref, lt_ref, nt_ref, *refs, steps, n_tiles):
    n_groups = len(steps)
    hn_refs = refs[:n_groups]
    xg_ref, sem, zero_ref, zsem = refs[n_groups:]
    rows = hn_refs[0].shape[0]
    i = pl.program_id(0)

    def zero_copy(tile):
        return pltpu.make_async_copy(zero_ref, xg_ref.at[pl.ds(tile * MOE_TM, MOE_TM), :], zsem)

    @pl.when(i == 0)
    def _():
        zero_ref[...] = jnp.zeros(zero_ref.shape, zero_ref.dtype)
        nt = nt_ref[0]
        for e in range(N_EXPERTS):
            @pl.when(lt_ref[e] >= 0)
            def _():
                zero_copy(jnp.maximum(lt_ref[e], 0)).start()
        lax.fori_loop(nt, n_tiles, lambda t, c: (zero_copy(t).start(), c)[1], 0)
        for e in range(N_EXPERTS):
            @pl.when(lt_ref[e] >= 0)
            def _():
                zero_copy(0).wait()
        lax.fori_loop(nt, n_tiles, lambda t, c: (zero_copy(0).wait(), c)[1], 0)

    first = 0
    for hn_ref, n_steps in zip(hn_refs, steps):
        @pl.when((i >= first) & (i < first + n_steps))
        def _(hn_ref=hn_ref):
            base = i * rows

            def issue(r, c):
                for k in range(TOP_K):
                    _row_copy(hn_ref, r, xg_ref, pos_ref[(base + r) * TOP_K + k], sem).start(
                        priority=k)
                return c

            def drain(r, c):
                for k in range(TOP_K):
                    _row_copy(hn_ref, r, xg_ref, 0, sem).wait()
                return c

            lax.fori_loop(0, rows, issue, 0, unroll=8)
            lax.fori_loop(0, rows, drain, 0, unroll=8)

        first += n_steps


def _dispatch(pos_flat, last_tile, n_live, hns, n_tiles):
    d = hns[0].shape[1]
    steps = tuple(h.shape[0] // ROW_DMA_TILE for h in hns)
    firsts = [sum(steps[:g]) for g in range(len(steps))]

    def group_map(g):
        return lambda i, pos, lt, nt: (jnp.clip(i - firsts[g], 0, steps[g] - 1), 0)

    return pl.pallas_call(
        functools.partial(_dispatch_kernel, steps=steps, n_tiles=n_tiles),
        grid_spec=pltpu.PrefetchScalarGridSpec(
            num_scalar_prefetch=3,
            grid=(sum(steps),),
            in_specs=[pl.BlockSpec((ROW_DMA_TILE, d), group_map(g)) for g in range(len(hns))],
            out_specs=pl.BlockSpec(memory_space=pl.ANY),
            scratch_shapes=[pltpu.SemaphoreType.DMA(()), pltpu.VMEM((MOE_TM, d), F32),
                            pltpu.SemaphoreType.DMA(())],
        ),
        out_shape=jax.ShapeDtypeStruct((n_tiles * MOE_TM, d), F32),
        compiler_params=_cparams(("arbitrary",)),
        name="moe_dispatch",
    )(pos_flat, last_tile, n_live, *hns)


WEIGHT_PARTS = 4


def _weight_copies(w_hbm_refs, expert, wbuf_ref, slot, sem_ref):
    copies = []
    for i, w_ref in enumerate(w_hbm_refs):
        part_rows = w_ref.shape[1] // WEIGHT_PARTS
        for part in range(WEIGHT_PARTS):
            rows = pl.ds(part * part_rows, part_rows)
            copies.append(pltpu.make_async_copy(w_ref.at[expert, rows], wbuf_ref.at[slot, i, rows],
                                                sem_ref.at[slot]))
    return copies


def _start_weight_copies(copies, queue0_parts):
    for k, cp in enumerate(copies):
        cp.start(priority=0 if k % WEIGHT_PARTS < queue0_parts else 1)


def _expert_weights_slot(te_ref, nt, t, w_hbm_refs, wbuf_ref, sem_ref, slot_ref, queue0_parts):
    expert = te_ref[t]

    @pl.when(t == 0)
    def _():
        slot_ref[0] = 0
        _start_weight_copies(_weight_copies(w_hbm_refs, expert, wbuf_ref, 0, sem_ref),
                             queue0_parts)

    @pl.when((t == 0) | (expert != te_ref[jnp.maximum(t - 1, 0)]))
    def _():
        @pl.when(t > 0)
        def _():
            slot_ref[0] = 1 - slot_ref[0]

        slot = slot_ref[0]
        for cp in _weight_copies(w_hbm_refs, expert, wbuf_ref, slot, sem_ref):
            cp.wait()
        nxt = lax.while_loop(lambda u: (u < nt) & (te_ref[jnp.minimum(u, nt - 1)] == expert),
                             lambda u: u + 1, t + 1)

        @pl.when(nxt < nt)
        def _():
            _start_weight_copies(_weight_copies(w_hbm_refs, te_ref[jnp.minimum(nxt, nt - 1)],
                                                wbuf_ref, 1 - slot, sem_ref), queue0_parts)

    return slot_ref[0]


def _moe_up_kernel(te_ref, nt_ref, x_ref, w1_ref, w3_ref, act_ref, wbuf_ref, sem_ref, slot_ref):
    t = pl.program_id(0)
    nt = nt_ref[0]

    @pl.when(t < nt)
    def _():
        slot = _expert_weights_slot(te_ref, nt, t, (w1_ref, w3_ref), wbuf_ref, sem_ref, slot_ref,
                                    queue0_parts=1)
        xb = x_ref[...].astype(BF16)
        a = _dot(xb, wbuf_ref[slot, 0].astype(BF16))
        b = _dot(xb, wbuf_ref[slot, 1].astype(BF16))
        act_ref[...] = (_silu(a) * b).astype(BF16)

    @pl.when(t >= nt)
    def _():
        act_ref[...] = jnp.zeros(act_ref.shape, act_ref.dtype)


def _moe_down_kernel(te_ref, nt_ref, act_ref, w2_ref, out_ref, wbuf_ref, sem_ref, slot_ref):
    t = pl.program_id(0)
    nt = nt_ref[0]

    @pl.when(t < nt)
    def _():
        slot = _expert_weights_slot(te_ref, nt, t, (w2_ref,), wbuf_ref, sem_ref, slot_ref,
                                    queue0_parts=0)
        out_ref[...] = _dot(act_ref[...], wbuf_ref[slot, 0].astype(BF16))

    @pl.when(t >= nt)
    def _():
        out_ref[...] = jnp.zeros(out_ref.shape, out_ref.dtype)


def _moe_experts(te, nt, xg, w1, w3, w2):
    n_rows, d = xg.shape
    d_exp = w1.shape[-1]
    n_tiles = n_rows // MOE_TM
    live_tile = lambda t, te_r, nt_r: (jnp.minimum(t, nt_r[0] - 1), 0)
    any_spec = pl.BlockSpec(memory_space=pl.ANY)

    act = pl.pallas_call(
        _moe_up_kernel,
        grid_spec=pltpu.PrefetchScalarGridSpec(
            num_scalar_prefetch=2,
            grid=(n_tiles,),
            in_specs=[pl.BlockSpec((MOE_TM, d), live_tile), any_spec, any_spec],
            out_specs=pl.BlockSpec((MOE_TM, d_exp), lambda t, te_r, nt_r: (t, 0)),
            scratch_shapes=[pltpu.VMEM((2, 2, d, d_exp), F32), pltpu.SemaphoreType.DMA((2,)),
                            pltpu.SMEM((1,), I32)],
        ),
        out_shape=jax.ShapeDtypeStruct((n_rows, d_exp), BF16),
        compiler_params=_cparams(("arbitrary",)),
        name="moe_up",
    )(te, nt, xg, w1, w3)

    return pl.pallas_call(
        _moe_down_kernel,
        grid_spec=pltpu.PrefetchScalarGridSpec(
            num_scalar_prefetch=2,
            grid=(n_tiles,),
            in_specs=[pl.BlockSpec((MOE_TM, d_exp), live_tile), any_spec],
            out_specs=pl.BlockSpec((MOE_TM, d), lambda t, te_r, nt_r: (t, 0)),
            scratch_shapes=[pltpu.VMEM((2, 1, d_exp, d), F32), pltpu.SemaphoreType.DMA((2,)),
                            pltpu.SMEM((1,), I32)],
        ),
        out_shape=jax.ShapeDtypeStruct((n_rows, d), F32),
        compiler_params=_cparams(("arbitrary",)),
        name="moe_down",
    )(te, nt, act, w2)


def _combine_kernel(pos_ref, x1_ref, g2_ref, wts_ref, fg_ref, eo_ref, y_ref, buf_ref, sem):
    rows = wts_ref.shape[0]
    base = pl.program_id(0) * rows

    def issue(r, c):
        for k in range(TOP_K):
            _row_copy(eo_ref, pos_ref[(base + r) * TOP_K + k], buf_ref.at[k], r, sem).start(
                priority=k)
        return c

    def drain(r, c):
        for k in range(TOP_K):
            _row_copy(eo_ref, 0, buf_ref.at[k], r, sem).wait()
        return c

    lax.fori_loop(0, rows, issue, 0, unroll=8)
    lax.fori_loop(0, rows, drain, 0, unroll=8)
    w = wts_ref[...]
    moe = w[:, 0:1] * buf_ref[0] + w[:, 1:2] * buf_ref[1]
    x2 = x1_ref[...] + g2_ref[...] * moe.reshape(x1_ref.shape)
    ms = jnp.mean(x2 * x2, axis=-1, keepdims=True)
    y_ref[...] = x2 * lax.rsqrt(ms + EPS) * fg_ref[...]


def _combine(pos_flat, x1, mod3, wts, final_g, expert_out, group_rows):
    bsz, t_len, d = x1.shape
    g_blk, r_blk = group_rows
    tm = g_blk * r_blk
    tiles_per_seq = max(t_len // r_blk, 1)

    def xmap(i, pos):
        return (i // tiles_per_seq, i % tiles_per_seq, 0)

    return pl.pallas_call(
        _combine_kernel,
        grid_spec=pltpu.PrefetchScalarGridSpec(
            num_scalar_prefetch=1,
            grid=((bsz * t_len) // tm,),
            in_specs=[
                pl.BlockSpec((g_blk, r_blk, d), xmap),
                pl.BlockSpec((g_blk, 1, d), lambda i, pos: (i // tiles_per_seq, 0, 5)),
                pl.BlockSpec((tm, LANES), lambda i, pos: (i, 0)),
                pl.BlockSpec((1, d), lambda i, pos: (0, 0)),
                pl.BlockSpec(memory_space=pl.ANY),
            ],
            out_specs=pl.BlockSpec((g_blk, r_blk, d), xmap),
            scratch_shapes=[pltpu.VMEM((TOP_K, tm, d), F32), pltpu.SemaphoreType.DMA(())],
        ),
        out_shape=jax.ShapeDtypeStruct((bsz, t_len, d), F32),
        compiler_params=_cparams(("arbitrary",)),
        name="moe_combine",
    )(pos_flat, x1, mod3, wts, final_g.reshape(1, d), expert_out)


def kernel(x_prompt, x_sample, state_hgrn, state_ssm, state_conv, c_prompt, c_sample, ada_w, ada_b,
           norm1_g, norm2_g, w_in, hgrn_lb, hgrn_onorm_g, conv_w, conv_b, dt_bias, a_log, d_skip,
           ssm_norm_g, w_out, w_grp, b_grp, w_rt, b_rt, w1, w3, w2, final_g):
    depth = w_in.shape[0]
    assert depth == 1, "single-layer trunk"
    layer = 0
    bp, t_p, d = x_prompt.shape
    bs, t_s, _ = x_sample.shape
    mix_a = hgrn_onorm_g.shape[1]
    mix_b = ssm_norm_g.shape[1]
    conv_dim = conv_w.shape[2]
    n_main = 4 * mix_a + mix_b + conv_dim
    n_ssd_heads = dt_bias.shape[1]
    n_pairs = mix_b // LANES
    xbc0 = 4 * mix_a + mix_b
    tail = CONV_WIDTH - 1
    assert t_s <= SUBLANES and t_s >= tail and t_p % PROMPT_CHUNK == 0

    xs_pad = jnp.pad(x_sample, ((0, 0), (0, SUBLANES - t_s), (0, 0)))
    n_c = bp + bs
    c_rows = -(-n_c // SUBLANES) * SUBLANES
    c_all = jnp.pad(jnp.concatenate([c_prompt, c_sample], axis=0), ((0, c_rows - n_c), (0, 0)))

    mod = _modulation(c_all, ada_w[layer], ada_b[layer])
    mod_p = mod[:bp].reshape(bp, 1, -1)
    mod_s = mod[bp:n_c].reshape(bs, 1, -1)
    w_in_t = jnp.swapaxes(w_in[layer], 0, 1)
    w_dt_t = jnp.pad(w_in_t[n_main:], ((0, LANES - n_ssd_heads), (0, 0)))
    w_router = jnp.pad(jnp.concatenate([w_rt[layer], w_grp[layer]], axis=1),
                       ((0, 0), (0, LANES - N_EXPERTS - N_EXPERT_GROUPS)))
    b_router = jnp.pad(jnp.concatenate([b_rt[layer], b_grp[layer]]),
                       (0, LANES - N_EXPERTS - N_EXPERT_GROUPS)).reshape(1, LANES)

    groups = (
        (x_prompt, mod_p, None, None, None, t_p, PROMPT_CHUNK, PROMPT_CHUNK, MIXER_ROWS,
         (LANES // PROMPT_CHUNK, 8), LANES // PROMPT_CHUNK, 1, (1, OUT_TM)),
        (xs_pad, mod_s, state_hgrn[layer], state_ssm[layer].reshape(bs, n_pairs, LANES, LANES),
         state_conv[layer], t_s, SUBLANES, t_s, SUBLANES,
         (LANES // SUBLANES, 2), LANES // SUBLANES, PROJ_SEQS, (OUT_TM // SUBLANES, SUBLANES)),
    )

    cnt = jnp.zeros((SUBLANES, LANES), F32)
    per_group = []
    for x3, mod3, s0, h0, c0, t_real, ch, valid, tt, (hg_gb, hg_hb), ssd_gb, proj_gr, out_gr in groups:
        bsz, t_len, _ = x3.shape
        proj, dt = _in_projection(x3, mod3, norm1_g[layer], w_in_t, w_dt_t, n_main, proj_gr)
        proj3 = proj.reshape(bsz, t_len, n_main)
        dt3 = dt.reshape(bsz, t_len, LANES)
        o3, s_new = _hgrn_heads(proj3, hgrn_lb, hgrn_onorm_g[layer], s0, ch=ch, valid=valid,
                                gb=hg_gb, hb=hg_hb, tt=tt, layer=layer)
        y3, h_new = _ssd_pairs(proj3, dt3, conv_w[layer], conv_b[layer], dt_bias[layer],
                               a_log[layer], d_skip[layer], ssm_norm_g[layer], h0, c0, mix_a=mix_a,
                               mix_b=mix_b, ch=ch, valid=valid, gb=ssd_gb, tt=tt)
        conv_new = proj3[:, t_real - tail:t_real, xbc0:xbc0 + conv_dim]
        x1, hn, ids, wts, rank, cnt = _out_projection(
            o3, y3, x3, mod3, norm2_g[layer], w_out[layer], w_router, b_router, cnt, out_gr)
        per_group.append(dict(x1=x1, hn=hn, ids=ids, wts=wts, rank=rank, mod3=mod3, out_gr=out_gr,
                              s_new=s_new, h_new=h_new, conv_new=conv_new))

    ids_all = jnp.concatenate([g["ids"] for g in per_group], axis=0)
    rank_all = jnp.concatenate([g["rank"] for g in per_group], axis=0)
    n_tok = ids_all.shape[0]
    pos, te, nt, lt = _positions(ids_all, rank_all, cnt)
    pos_flat = pos[:, :TOP_K].reshape(-1)
    n_tiles = -(-(n_tok * TOP_K + N_EXPERTS * (MOE_TM - 1)) // MOE_TM)
    assert n_tiles <= LANES
    te_vec = te[0, :n_tiles]
    nt_vec = nt[0, :1]
    lt_vec = lt[0, :N_EXPERTS]

    row0 = 0
    for g in per_group:
        rows = g["hn"].shape[0]
        g["pos"] = pos_flat[row0 * TOP_K:(row0 + rows) * TOP_K]
        row0 += rows
    xg = _dispatch(pos_flat, lt_vec, nt_vec, [g["hn"] for g in per_group], n_tiles)
    expert_out = _moe_experts(te_vec, nt_vec, xg, w1[layer], w3[layer], w2[layer])
    ys = [_combine(g["pos"], g["x1"], g["mod3"], g["wts"], final_g, expert_out, g["out_gr"])
          for g in per_group]

    gp, gs = per_group
    return (
        ys[0],
        ys[1][:, :t_s],
        gp["s_new"][None],
        gp["h_new"].reshape(1, bp, n_ssd_heads, SSD_HEAD_DIM, SSD_STATE),
        gp["conv_new"][None],
        gs["s_new"][None],
        gs["h_new"].reshape(1, bs, n_ssd_heads, SSD_HEAD_DIM, SSD_STATE),
        gs["conv_new"][None],
    )
```

```python
import functools

import jax
import jax.numpy as jnp
from jax import lax
from jax.experimental import pallas as pl
from jax.experimental.pallas import tpu as pltpu

F32 = jnp.float32
BF16 = jnp.bfloat16
I32 = jnp.int32
EPS = 1e-6
LOG2E = 1.4426950408889634
MAX_LOG2_DECAY = 12

LANES = 128
SUBLANES = 8
VMEM_LIMIT = 56 * 1024 * 1024

HGRN_HEAD_DIM = 128
SSD_HEAD_DIM = 64
SSD_GROUPS = 2
SSD_STATE = 128
CONV_WIDTH = 4
N_EXPERT_GROUPS = 4
EXPERTS_PER_GROUP = 8
N_EXPERTS = N_EXPERT_GROUPS * EXPERTS_PER_GROUP
TOP_K = 2

PROJ_SEQS = 128
NORM_ROWS = 256
PROJ_TN = 512
MOE_TM = 256
PROMPT_CHUNK = 64
MIXER_ROWS = 512
ROW_DMA_TILE = 256


def _cparams(sem):
    return pltpu.CompilerParams(dimension_semantics=sem, vmem_limit_bytes=VMEM_LIMIT)


def _dot(a, b):
    return jnp.dot(a, b, preferred_element_type=F32)


def _dot_nt(a, b):
    return lax.dot_general(a, b, (((1,), (1,)), ((), ())), preferred_element_type=F32)


def _split2(x):
    hi = x.astype(BF16)
    lo = (x - hi.astype(F32)).astype(BF16)
    return hi, lo


def _split3(x):
    hi = x.astype(BF16)
    r = x - hi.astype(F32)
    mid = r.astype(BF16)
    lo = (r - mid.astype(F32)).astype(BF16)
    return hi, mid, lo


def _dot_exact_lhs(m_bf16, x):
    hi, mid, lo = _split3(x)
    return _dot(m_bf16, hi) + _dot(m_bf16, mid) + _dot(m_bf16, lo)


def _dot_exact_rhs(x, m_bf16):
    hi, mid, lo = _split3(x)
    return _dot(hi, m_bf16) + _dot(mid, m_bf16) + _dot(lo, m_bf16)


def _dot_hp(a, w):
    ah, al = _split2(a)
    wh, wl = _split2(w)
    return _dot(ah, wh) + _dot(al, wh) + _dot(ah, wl)


def _silu(x):
    return x * jax.nn.sigmoid(x)


def _mod_kernel(c_ref, w_ref, b_ref, o_ref):
    a = _silu(c_ref[...])
    o_ref[...] = _dot_hp(a, w_ref[...]) + b_ref[...]


def _modulation(c_all, ada_w, ada_b):
    rows, d = c_all.shape
    n_out = ada_w.shape[1]
    tn = 1024
    return pl.pallas_call(
        _mod_kernel,
        grid=(n_out // tn,),
        in_specs=[
            pl.BlockSpec((rows, d), lambda j: (0, 0)),
            pl.BlockSpec((d, tn), lambda j: (0, j)),
            pl.BlockSpec((1, tn), lambda j: (0, j)),
        ],
        out_specs=pl.BlockSpec((rows, tn), lambda j: (0, j)),
        out_shape=jax.ShapeDtypeStruct((rows, n_out), F32),
        compiler_params=_cparams(("arbitrary",)),
        name="adaln_mod",
    )(c_all, ada_w, ada_b.reshape(1, n_out))


def _proj_kernel(x_hbm_ref, sc_ref, sh_ref, g_ref, w_hbm_ref, wdt_ref, o_ref, dt_ref, hn_ref, xs_ref,
                 sem, wbuf_ref, wsem):
    i = pl.program_id(0)
    j = pl.program_id(1)
    n_i = pl.num_programs(0)
    n_j = pl.num_programs(1)
    g_blk, r_blk, d = xs_ref.shape
    tn = wbuf_ref.shape[1]

    def x_copy(tile):
        return pltpu.make_async_copy(x_hbm_ref.at[pl.ds(tile * g_blk, g_blk)], xs_ref, sem)

    def w_copy(col_tile, slot):
        return pltpu.make_async_copy(w_hbm_ref.at[pl.ds(col_tile * tn, tn), :], wbuf_ref.at[slot],
                                     wsem.at[slot])

    slot = (i * n_j + j) % 2

    @pl.when((i == 0) & (j == 0))
    def _():
        w_copy(0, 0).start(priority=1)

    @pl.when((i + 1 < n_i) | (j + 1 < n_j))
    def _():
        w_copy((j + 1) % n_j, 1 - slot).start(priority=1)

    @pl.when(j == 0)
    def _():
        @pl.when(i == 0)
        def _():
            x_copy(0).start()

        x_copy(i).wait()
        wh, wl = _split2(wdt_ref[...])
        n_rows = g_blk * r_blk
        step = min(n_rows, NORM_ROWS)
        for c in range(n_rows // step):
            if g_blk == 1:
                x = xs_ref[:, c * step:(c + 1) * step, :]
                sc, sh = sc_ref[...], sh_ref[...]
            else:
                gs = slice(c * step // r_blk, (c + 1) * step // r_blk)
                x, sc, sh = xs_ref[gs], sc_ref[gs], sh_ref[gs]
            ms = jnp.mean(x * x, axis=-1, keepdims=True)
            hn = ((x * lax.rsqrt(ms + EPS)) * g_ref[...] * (1.0 + sc) + sh).reshape(step, d)
            rows = slice(c * step, (c + 1) * step)
            hn_ref[rows, :] = hn.astype(BF16)
            ah, al = _split2(hn)
            dt_ref[rows, :] = _dot_nt(ah, wh) + _dot_nt(al, wh) + _dot_nt(ah, wl)

        @pl.when(i + 1 < pl.num_programs(0))
        def _():
            x_copy(i + 1).start()

    w_copy(j, slot).wait()
    o_ref[...] = _dot_nt(hn_ref[...], wbuf_ref[slot].astype(BF16))


def _in_projection(x3, mod3, norm_g, w_t, w_dt_t, n_main, g_blk):
    bsz, t_len, d = x3.shape
    tm = g_blk * t_len
    n_row_tiles = bsz // g_blk

    def mod_map(sec):
        return lambda i, j: (i, 0, sec)

    return pl.pallas_call(
        _proj_kernel,
        grid=(n_row_tiles, n_main // PROJ_TN),
        in_specs=[
            pl.BlockSpec(memory_space=pl.ANY),
            pl.BlockSpec((g_blk, 1, d), mod_map(1)),
            pl.BlockSpec((g_blk, 1, d), mod_map(0)),
            pl.BlockSpec((1, d), lambda i, j: (0, 0)),
            pl.BlockSpec(memory_space=pl.ANY),
            pl.BlockSpec((LANES, d), lambda i, j: (0, 0)),
        ],
        out_specs=[
            pl.BlockSpec((tm, PROJ_TN), lambda i, j: (i, j)),
            pl.BlockSpec((tm, LANES), lambda i, j: (i, 0)),
        ],
        out_shape=[
            jax.ShapeDtypeStruct((bsz * t_len, n_main), F32),
            jax.ShapeDtypeStruct((bsz * t_len, LANES), F32),
        ],
        scratch_shapes=[pltpu.VMEM((tm, d), BF16), pltpu.VMEM((g_blk, t_len, d), F32),
                        pltpu.SemaphoreType.DMA(()), pltpu.VMEM((2, PROJ_TN, d), F32),
                        pltpu.SemaphoreType.DMA((2,))],
        compiler_params=_cparams(("arbitrary", "arbitrary")),
        name="in_proj",
    )(x3, mod3, mod3, norm_g.reshape(1, d), w_t, w_dt_t)


def _level_sizes(ch):
    sizes = []
    sz = SUBLANES
    while 2 * sz <= ch:
        sizes.append(sz)
        sz *= 2
    return sizes


def _level_masks(ch, n_rows):
    import numpy as np
    t = np.arange(n_rows)[:, None]
    s = np.arange(n_rows)[None, :]
    out = []
    for sz in _level_sizes(ch):
        m = (t // (2 * sz) == s // (2 * sz)) & (t % (2 * sz) >= sz) & (s % (2 * sz) < sz)
        out.append(m.astype(np.float32))
    out.append(((t // SUBLANES == s // SUBLANES) & (s <= t)).astype(np.float32))
    return jnp.asarray(np.stack(out))


def _hgrn_kernel(*refs, ch, valid, nc, gb, hb, has_init, layer):
    if has_init:
        (q_ref, f_ref, i_ref, g_ref, lb_ref, on_ref, tri_ref, masks_ref, s0_ref,
         o_ref, sout_ref, st_ref, b_scr, k_scr) = refs
    else:
        (q_ref, f_ref, i_ref, g_ref, lb_ref, on_ref, tri_ref, masks_ref,
         o_ref, sout_ref, st_ref, b_scr, k_scr) = refs
        s0_ref = None

    n_lb = lb_ref.shape[0]
    lb_rows = [lb_ref[i:i + 1, :] for i in range(n_lb)]
    lb_max = functools.reduce(jnp.maximum, lb_rows)
    lb_exp = [jnp.exp(r - lb_max) for r in lb_rows]
    lb = sum(lb_exp[:layer + 1]) / sum(lb_exp)
    onorm = on_ref[...]
    tri = tri_ref[...]
    sizes = _level_sizes(ch)
    lane8 = lax.broadcasted_iota(I32, (SUBLANES, LANES), 1)
    row8 = lax.broadcasted_iota(I32, (SUBLANES, LANES), 0)
    tb = pl.program_id(2)

    n_rows = gb * ch
    row_id = lax.broadcasted_iota(I32, (n_rows, LANES), 0)

    def stack(fn):
        parts = [fn(g) for g in range(gb)]
        return parts[0] if gb == 1 else jnp.concatenate(parts, axis=0)

    def chunk(hh, ci, bounded):
        cols = slice(hh * LANES, (hh + 1) * LANES)
        t0 = pl.multiple_of(ci * ch, ch)
        rows = pl.ds(t0, ch)
        lb_h = lb[:, cols]
        q = _silu(stack(lambda g: q_ref[g, rows, cols]))
        fg = lb_h + (1.0 - lb_h) * jax.nn.sigmoid(stack(lambda g: f_ref[g, rows, cols]))
        k = 1.0 - fg
        lf = jnp.log(fg)
        v = stack(lambda g: i_ref[g, rows, cols])
        if valid < ch:
            live = (row_id & (ch - 1)) < valid
            lf = jnp.where(live, lf, 0.0)
            k = jnp.where(live, k, 0.0)
        b2 = _dot_exact_lhs(tri, lf) * LOG2E
        b_scr[hh] = b2
        k_scr[hh] = k
        b_end_rows = [b_scr[hh, pl.ds(g * ch + ch - 1, 1), :] for g in range(gb)]
        b_end = stack(lambda g: jnp.broadcast_to(b_end_rows[g], (ch, LANES)))

        qe = (q * jnp.exp2(b2)).astype(BF16)
        o = stack(lambda g: _dot_nt(qe[g * ch:(g + 1) * ch], st_ref[g * hb + hh].astype(BF16)))

        scores = jnp.zeros((n_rows, n_rows), F32)
        for lvl, sz in enumerate(sizes):
            pieces = []
            for m in range(n_rows // (2 * sz)):
                r = b_scr[hh, pl.ds(2 * sz * m + sz - 1, 1), :]
                pieces.append(jnp.broadcast_to(r, (2 * sz, LANES)))
            r_all = pieces[0] if len(pieces) == 1 else jnp.concatenate(pieces, axis=0)
            e = jnp.exp2(-jnp.abs(b2 - r_all))
            s_l = _dot_nt((q * e).astype(BF16), (k * e).astype(BF16))
            scores = scores + masks_ref[lvl] * s_l
        if bounded:
            pieces = []
            for jb in range(n_rows // SUBLANES):
                if (jb * SUBLANES) % ch == 0:
                    pieces.append(jnp.zeros((SUBLANES, LANES), F32))
                else:
                    pieces.append(jnp.broadcast_to(b_scr[hh, pl.ds(jb * SUBLANES - 1, 1), :],
                                                   (SUBLANES, LANES)))
            r8 = jnp.concatenate(pieces, axis=0)
            s_d = _dot_nt((q * jnp.exp2(b2 - r8)).astype(BF16),
                          (k * jnp.exp2(r8 - b2)).astype(BF16))
            scores = scores + jnp.where(masks_ref[len(sizes)] > 0.0, s_d, 0.0)
        else:
            blocks = []
            for jb in range(n_rows // SUBLANES):
                bb = b2[SUBLANES * jb:SUBLANES * (jb + 1)]
                qb = q[SUBLANES * jb:SUBLANES * (jb + 1)]
                acc = jnp.zeros((SUBLANES, LANES), F32)
                for sl in range(SUBLANES):
                    s = SUBLANES * jb + sl
                    bs = b_scr[hh, pl.ds(s, 1), :]
                    ks = k_scr[hh, pl.ds(s, 1), :]
                    val = (qb * ks) * jnp.exp2(bb - bs)
                    red = jnp.sum(val, axis=1, keepdims=True)
                    acc = jnp.where(lane8 == s, red, acc)
                blocks.append(jnp.where(lane8 - SUBLANES * jb <= row8, acc, 0.0))
            scores = scores + jnp.concatenate(blocks, axis=0)
        o = o + _dot(scores.astype(BF16), v.astype(BF16))

        dk = k * jnp.exp2(b_end - b2)
        v_t = v.T.astype(BF16)
        for g in range(gb):
            own = (row_id >= g * ch) & (row_id < (g + 1) * ch)
            c = g * hb + hh
            st_ref[c] = (st_ref[c] * jnp.exp2(b_end_rows[g])
                         + _dot(v_t, jnp.where(own, dk, 0.0).astype(BF16)))

        on = o * lax.rsqrt(jnp.mean(o * o, axis=-1, keepdims=True) + EPS) * onorm[:, cols]
        out = on * _silu(stack(lambda g: g_ref[g, rows, cols]))
        for g in range(gb):
            o_ref[g, rows, cols] = out[g * ch:(g + 1) * ch].astype(o_ref.dtype)

    chains = [(g * hb + hh, g, hh) for g in range(gb) for hh in range(hb)]

    @pl.when(tb == 0)
    def _():
        for c, g, hh in chains:
            if has_init:
                st_ref[c] = s0_ref[g, hh].T
            else:
                st_ref[c] = jnp.zeros((LANES, LANES), F32)

    def run(bounded):
        def step(ci, carry):
            for hh in range(hb):
                chunk(hh, ci, bounded)
            return carry

        if nc == 1:
            step(0, 0)
        else:
            lax.fori_loop(0, nc, step, 0, unroll=2)

    if ch > SUBLANES:
        gates_bounded = jnp.min(lb) >= 2.0 ** -MAX_LOG2_DECAY
        lax.cond(gates_bounded, lambda: run(True), lambda: run(False))
    else:
        run(False)

    @pl.when(tb == pl.num_programs(2) - 1)
    def _():
        for c, g, hh in chains:
            sout_ref[g, hh] = st_ref[c].T


def _hgrn_heads(proj3, hgrn_lb, onorm_g, s0, *, ch, valid, gb, hb, tt, layer):
    bsz, t_len, _ = proj3.shape
    n_heads = onorm_g.shape[0] // HGRN_HEAD_DIM
    n_hg = n_heads // hb
    nc = tt // ch
    n_rows = gb * ch
    assert n_rows == LANES
    has_init = s0 is not None
    w = hb * LANES

    def col(sec):
        return lambda b, h, t: (b, t, sec * n_hg + h)

    in_specs = [pl.BlockSpec((gb, tt, w), col(s)) for s in range(4)]
    in_specs += [
        pl.BlockSpec((hgrn_lb.shape[0], w), lambda b, h, t: (0, h)),
        pl.BlockSpec((1, w), lambda b, h, t: (0, h)),
        pl.BlockSpec((n_rows, n_rows), lambda b, h, t: (0, 0)),
        pl.BlockSpec((len(_level_sizes(ch)) + 1, n_rows, n_rows), lambda b, h, t: (0, 0, 0)),
    ]
    args = [proj3, proj3, proj3, proj3, hgrn_lb, onorm_g.reshape(1, -1),
            jnp.asarray(_block_tri(gb, ch), dtype=BF16), _level_masks(ch, n_rows)]
    if has_init:
        in_specs.append(pl.BlockSpec((gb, hb, LANES, LANES), lambda b, h, t: (b, h, 0, 0)))
        args.append(s0)
    kern = functools.partial(_hgrn_kernel, ch=ch, valid=valid, nc=nc, gb=gb, hb=hb,
                             has_init=has_init, layer=layer)
    n_chain = gb * hb
    return pl.pallas_call(
        kern,
        grid=(bsz // gb, n_hg, t_len // tt),
        in_specs=in_specs,
        out_specs=[
            pl.BlockSpec((gb, tt, w), lambda b, h, t: (b, t, h)),
            pl.BlockSpec((gb, hb, LANES, LANES), lambda b, h, t: (b, h, 0, 0)),
        ],
        out_shape=[
            jax.ShapeDtypeStruct((bsz, t_len, n_heads * HGRN_HEAD_DIM), BF16),
            jax.ShapeDtypeStruct((bsz, n_heads, LANES, LANES), F32),
        ],
        scratch_shapes=[pltpu.VMEM((n_chain, LANES, LANES), F32),
                        pltpu.VMEM((hb, n_rows, LANES), F32),
                        pltpu.VMEM((hb, n_rows, LANES), F32)],
        compiler_params=_cparams(("arbitrary", "arbitrary", "arbitrary")),
        name="hgrn2_heads",
    )(*args)


def _softplus(x):
    return jnp.maximum(x, 0.0) + jnp.log1p(jnp.exp(-jnp.abs(x)))


def _ssd_kernel(*refs, ch, valid, nc, gb, pb, has_init):
    if has_init:
        (z_ref, x_ref, b_ref, c_ref, dt_ref, cwx_ref, cwb_ref, cwc_ref, cbx_ref, cbb_ref, cbc_ref,
         xj_ref, bias_ref, alog_ref, d_ref, ng_ref, tri_ref, tril_ref, h0_ref, cx0_ref, cb0_ref,
         cc0_ref, y_ref, hout_ref, ht_ref, wx_ref, wb_ref, wc_ref) = refs
    else:
        (z_ref, x_ref, b_ref, c_ref, dt_ref, cwx_ref, cwb_ref, cwc_ref, cbx_ref, cbb_ref, cbc_ref,
         xj_ref, bias_ref, alog_ref, d_ref, ng_ref, tri_ref, tril_ref,
         y_ref, hout_ref, ht_ref, wx_ref, wb_ref, wc_ref) = refs
        h0_ref = cx0_ref = cb0_ref = cc0_ref = None

    tri = tri_ref[...]
    tril = tril_ref[...]
    lane = lax.broadcasted_iota(I32, (LANES, LANES), 1)
    first_head = lane < SSD_HEAD_DIM
    tail = CONV_WIDTH - 1
    tb = pl.program_id(1)

    def conv(raw_ref, win_ref, cw_ref, cb_ref, g, t0):
        win_ref[g, pl.ds(SUBLANES, ch), :] = raw_ref[g, pl.ds(t0, ch), :]
        u = cb_ref[...]
        for j in range(CONV_WIDTH):
            u = u + win_ref[g, pl.ds(SUBLANES - tail + j, ch), :] * cw_ref[j:j + 1, :]
        win_ref[g, pl.ds(0, SUBLANES), :] = win_ref[g, pl.ds(ch, SUBLANES), :]
        return _silu(u)

    n_rows = gb * ch
    w = pb * LANES
    n_grp = SSD_GROUPS
    row_id = lax.broadcasted_iota(I32, (n_rows, w), 0)

    def stack(fn):
        parts = [fn(g) for g in range(gb)]
        return parts[0] if gb == 1 else jnp.concatenate(parts, axis=0)

    def group(q, rows, xs, bm, cm, dt_raw, z):
        wcols = slice(q * w, (q + 1) * w)
        dt = _softplus(_dot_exact_rhs(dt_raw, xj_ref[q]) + bias_ref[q])
        if valid < ch:
            dt = jnp.where((row_id & (ch - 1)) < valid, dt, 0.0)
        la_cs = _dot_exact_lhs(tri, dt * (-jnp.exp(alog_ref[q])))
        xdt = xs * dt
        a_end_rows = [la_cs[g * ch + ch - 1:g * ch + ch, :] for g in range(gb)]
        a_end = stack(lambda g: jnp.broadcast_to(a_end_rows[g], (ch, w)))

        cb = _dot_nt(cm.astype(BF16), bm.astype(BF16))
        acs_t = la_cs.T
        y_parts = []
        for p in range(pb):
            cols = slice(p * LANES, (p + 1) * LANES)
            scores = []
            for hl in range(2):
                at = p * LANES + hl * SSD_HEAD_DIM
                lmat = jnp.exp(jnp.minimum(la_cs[:, at:at + 1] - acs_t[at:at + 1, :], 0.0)) * tril
                scores.append((cb * lmat).astype(BF16))
            xp = xdt[:, cols]
            rhs = jnp.concatenate([jnp.where(first_head, xp, 0.0),
                                   jnp.where(first_head, 0.0, xp)], axis=0).astype(BF16)
            y_parts.append(_dot(jnp.concatenate(scores, axis=1), rhs))
        y = y_parts[0] if pb == 1 else jnp.concatenate(y_parts, axis=1)

        y_off = stack(lambda g: _dot(cm[g * ch:(g + 1) * ch].astype(BF16),
                                     ht_ref[g, :, wcols].astype(BF16)))
        y = y + y_off * jnp.exp(la_cs)
        b_t = bm.T.astype(BF16)
        upd = xdt * jnp.exp(a_end - la_cs)
        for g in range(gb):
            own = (row_id >= g * ch) & (row_id < (g + 1) * ch)
            ht_ref[g, :, wcols] = (ht_ref[g, :, wcols] * jnp.exp(a_end_rows[g])
                                   + _dot(b_t, jnp.where(own, upd, 0.0).astype(BF16)))

        y = (y + d_ref[q] * xs) * _silu(z)
        y = y * lax.rsqrt(jnp.mean(y * y, axis=-1, keepdims=True) + EPS) * ng_ref[q]
        for g in range(gb):
            y_ref[g, rows, wcols] = y[g * ch:(g + 1) * ch].astype(y_ref.dtype)

    def step(ci, carry):
        t0 = pl.multiple_of(ci * ch, ch)
        rows = pl.ds(t0, ch)
        xs = stack(lambda g: conv(x_ref, wx_ref, cwx_ref, cbx_ref, g, t0))
        bm = stack(lambda g: conv(b_ref, wb_ref, cwb_ref, cbb_ref, g, t0))
        cm = stack(lambda g: conv(c_ref, wc_ref, cwc_ref, cbc_ref, g, t0))
        dt_raw = stack(lambda g: dt_ref[g, rows, :])
        z = stack(lambda g: z_ref[g, rows, :])
        for q in range(n_grp):
            wcols = slice(q * w, (q + 1) * w)
            ncols = slice(q * LANES, (q + 1) * LANES)
            group(q, rows, xs[:, wcols], bm[:, ncols], cm[:, ncols], dt_raw, z[:, wcols])
        return carry

    @pl.when(tb == 0)
    def _():
        for g in range(gb):
            for win_ref, c0_ref in ((wx_ref, cx0_ref), (wb_ref, cb0_ref), (wc_ref, cc0_ref)):
                win_ref[g, pl.ds(0, SUBLANES), :] = jnp.zeros((SUBLANES, win_ref.shape[2]), F32)
                if has_init:
                    win_ref[g, pl.ds(SUBLANES - tail, tail), :] = c0_ref[g]
            for p in range(n_grp * pb):
                cols = slice(p * LANES, (p + 1) * LANES)
                if has_init:
                    ht_ref[g, :, cols] = h0_ref[g, p].T
                else:
                    ht_ref[g, :, cols] = jnp.zeros((LANES, LANES), F32)

    if nc == 1:
        step(0, 0)
    else:
        lax.fori_loop(0, nc, step, 0, unroll=2)

    @pl.when(tb == pl.num_programs(1) - 1)
    def _():
        for g in range(gb):
            for p in range(n_grp * pb):
                hout_ref[g, p] = ht_ref[g, :, p * LANES:(p + 1) * LANES].T


def _group_rows(v):
    return jnp.repeat(v.astype(F32), SSD_HEAD_DIM).reshape(SSD_GROUPS, 1, -1)


def _head_select(n_heads):
    import numpy as np
    w = n_heads // SSD_GROUPS * SSD_HEAD_DIM
    h = np.arange(LANES)[None, :, None]
    lane = np.arange(w)[None, None, :]
    g = np.arange(SSD_GROUPS)[:, None, None]
    return jnp.asarray((h == g * (n_heads // SSD_GROUPS) + lane // SSD_HEAD_DIM).astype(np.float32),
                       dtype=BF16)


def _block_tri(gb, ch):
    import numpy as np
    t = np.arange(gb * ch)[:, None]
    s = np.arange(gb * ch)[None, :]
    return ((t // ch == s // ch) & (s <= t)).astype(np.float32)


def _ssd_pairs(proj3, dt3, conv_w, conv_b, dt_bias, a_log, d_skip, norm_g, h0, conv0, *, mix_a,
               mix_b, ch, valid, gb, tt):
    bsz, t_len, _ = proj3.shape
    n_pairs = mix_b // LANES
    pb = n_pairs // SSD_GROUPS
    w = pb * LANES
    nc = tt // ch
    n_rows = gb * ch
    has_init = h0 is not None
    wn = SSD_GROUPS * LANES
    assert (4 * mix_a) % mix_b == 0 and (4 * mix_a + 2 * mix_b) % wn == 0 and n_rows == LANES
    z0 = 4 * mix_a // mix_b
    x0 = z0 + 1
    b0 = (4 * mix_a + 2 * mix_b) // wn
    c0 = b0 + 1

    seq = lambda blk: (lambda b, t: (b, t, blk))
    par = lambda blk: (lambda b, t: (0, blk))
    const2 = lambda b, t: (0, 0)
    const3 = lambda b, t: (0, 0, 0)

    in_specs = [
        pl.BlockSpec((gb, tt, mix_b), seq(z0)),
        pl.BlockSpec((gb, tt, mix_b), seq(x0)),
        pl.BlockSpec((gb, tt, wn), seq(b0)),
        pl.BlockSpec((gb, tt, wn), seq(c0)),
        pl.BlockSpec((gb, tt, LANES), seq(0)),
        pl.BlockSpec((CONV_WIDTH, mix_b), par(0)),
        pl.BlockSpec((CONV_WIDTH, wn), par(mix_b // wn)),
        pl.BlockSpec((CONV_WIDTH, wn), par(mix_b // wn + 1)),
        pl.BlockSpec((1, mix_b), par(0)),
        pl.BlockSpec((1, wn), par(mix_b // wn)),
        pl.BlockSpec((1, wn), par(mix_b // wn + 1)),
        pl.BlockSpec((SSD_GROUPS, LANES, w), const3),
        pl.BlockSpec((SSD_GROUPS, 1, w), const3),
        pl.BlockSpec((SSD_GROUPS, 1, w), const3),
        pl.BlockSpec((SSD_GROUPS, 1, w), const3),
        pl.BlockSpec((SSD_GROUPS, 1, w), const3),
        pl.BlockSpec((n_rows, n_rows), const2),
        pl.BlockSpec((n_rows, n_rows), const2),
    ]
    conv_b2 = conv_b.reshape(1, -1)
    block_tri = _block_tri(gb, ch)
    args = [proj3, proj3, proj3, proj3, dt3, conv_w, conv_w, conv_w, conv_b2, conv_b2, conv_b2,
            _head_select(dt_bias.shape[0]), _group_rows(dt_bias), _group_rows(a_log),
            _group_rows(d_skip), norm_g.astype(F32).reshape(SSD_GROUPS, 1, w),
            jnp.asarray(block_tri, dtype=BF16), jnp.asarray(block_tri)]
    if has_init:
        tail = CONV_WIDTH - 1
        in_specs += [
            pl.BlockSpec((gb, n_pairs, LANES, LANES), lambda b, t: (b, 0, 0, 0)),
            pl.BlockSpec((gb, tail, mix_b), lambda b, t: (b, 0, 0)),
            pl.BlockSpec((gb, tail, wn), lambda b, t: (b, 0, mix_b // wn)),
            pl.BlockSpec((gb, tail, wn), lambda b, t: (b, 0, mix_b // wn + 1)),
        ]
        args += [h0, conv0, conv0, conv0]
    kern = functools.partial(_ssd_kernel, ch=ch, valid=valid, nc=nc, gb=gb, pb=pb,
                             has_init=has_init)
    return pl.pallas_call(
        kern,
        grid=(bsz // gb, t_len // tt),
        in_specs=in_specs,
        out_specs=[
            pl.BlockSpec((gb, tt, mix_b), lambda b, t: (b, t, 0)),
            pl.BlockSpec((gb, n_pairs, LANES, LANES), lambda b, t: (b, 0, 0, 0)),
        ],
        out_shape=[
            jax.ShapeDtypeStruct((bsz, t_len, mix_b), BF16),
            jax.ShapeDtypeStruct((bsz, n_pairs, LANES, LANES), F32),
        ],
        scratch_shapes=[pltpu.VMEM((gb, LANES, mix_b), F32),
                        pltpu.VMEM((gb, ch + SUBLANES, mix_b), F32),
                        pltpu.VMEM((gb, ch + SUBLANES, wn), F32),
                        pltpu.VMEM((gb, ch + SUBLANES, wn), F32)],
        compiler_params=_cparams(("arbitrary", "arbitrary")),
        name="ssd_pairs",
    )(*args)


OUT_TM = 512
WEIGHT_STAGE_ROWS = 512
ROUTE_PARTS = 4
GROUP_LANE0 = N_EXPERTS
NEG_BIG = -1e30
NO_LANE = 4 * LANES


def _first_lane_of(mask, lane):
    return jnp.min(jnp.where(mask, lane, float(NO_LANE)), axis=1, keepdims=True)


def _route(logits):
    lane_i = lax.broadcasted_iota(I32, logits.shape, 1)
    lane = lane_i.astype(F32)
    is_grp = (lane_i >= GROUP_LANE0) & (lane_i < GROUP_LANE0 + N_EXPERT_GROUPS)
    lg = jnp.where(is_grp, logits, NEG_BIG)
    g_max = jnp.max(lg, axis=1, keepdims=True)
    g_sum = jnp.sum(jnp.where(is_grp, jnp.exp(lg - g_max), 0.0), axis=1, keepdims=True)
    g_idx = _first_lane_of(lg == g_max, lane) - GROUP_LANE0
    gp_top = 1.0 / g_sum
    lane_grp = jnp.right_shift(lane_i, 3).astype(F32)
    in_grp = (lane_i < N_EXPERTS) & (lane_grp == g_idx)
    le = jnp.where(in_grp, logits, NEG_BIG)
    e_max = jnp.max(le, axis=1, keepdims=True)
    e_exp = jnp.where(in_grp, jnp.exp(le - e_max), 0.0)
    ep = e_exp / jnp.sum(e_exp, axis=1, keepdims=True)
    ep = jnp.where(in_grp, ep, -1.0)
    p1 = jnp.max(ep, axis=1, keepdims=True)
    i1 = _first_lane_of(ep == p1, lane)
    ep2 = jnp.where(lane == i1, -1.0, ep)
    p2 = jnp.max(ep2, axis=1, keepdims=True)
    i2 = _first_lane_of(ep2 == p2, lane)
    den = p1 + p2
    return i1, i2, gp_top * p1 / den, gp_top * p2 / den


def _outproj_kernel(o_ref, y_ref, x_ref, g1_ref, sc_ref, sh_ref, ng_ref, w_hbm_ref, wr_ref, br_ref,
                    tri_ref, cnt0_ref,
                    x1_ref, hn_ref, ids_ref, wts_ref, rank_ref, cnt_ref,
                    wbf_ref, stage_ref, sem, cnts_ref):
    i = pl.program_id(0)
    tm, d = hn_ref.shape
    mix_a = o_ref.shape[1]
    tn = PROJ_TN
    n_col = d // tn
    gshape = (x_ref.shape[0], x_ref.shape[1], tn)

    @pl.when(i == 0)
    def _():
        cnts_ref[...] = cnt0_ref[...]
        rows = stage_ref.shape[0]
        for k in range(d // rows):
            cp = pltpu.make_async_copy(w_hbm_ref.at[pl.ds(k * rows, rows), :], stage_ref, sem)
            cp.start()
            cp.wait()
            wbf_ref[k * rows:(k + 1) * rows, :] = stage_ref[...].astype(BF16)

    o = o_ref[...]
    y = y_ref[...]
    for c in range(n_col):
        cols = slice(c * tn, (c + 1) * tn)
        acc = _dot(o, wbf_ref[0:mix_a, cols]) + _dot(y, wbf_ref[mix_a:d, cols])
        x1_ref[:, :, cols] = x_ref[:, :, cols] + g1_ref[:, :, cols] * acc.reshape(gshape)

    g_blk, r_blk = x_ref.shape[0], x_ref.shape[1]
    pr = tm // ROUTE_PARTS
    lane = lax.broadcasted_iota(I32, (pr, LANES), 1)
    onehots, hits = [], []
    for h in range(ROUTE_PARTS):
        rows = slice(h * pr, (h + 1) * pr)
        if g_blk == 1:
            gsl = (slice(None), rows)
        else:
            gsl = (slice(h * g_blk // ROUTE_PARTS, (h + 1) * g_blk // ROUTE_PARTS), slice(None))
        pshape = (g_blk if g_blk == 1 else g_blk // ROUTE_PARTS,
                  pr if g_blk == 1 else r_blk, tn)
        x1 = [x1_ref[gsl + (slice(c * tn, (c + 1) * tn),)].reshape(pr, tn) for c in range(n_col)]
        ssq = sum(jnp.sum(xc * xc, axis=-1, keepdims=True) for xc in x1)
        rs = lax.rsqrt(ssq / d + EPS)
        logits = jnp.zeros((pr, LANES), F32) + br_ref[...]
        for c in range(n_col):
            cols = slice(c * tn, (c + 1) * tn)
            sc = sc_ref[gsl[0], :, cols] if g_blk > 1 else sc_ref[:, :, cols]
            sh = sh_ref[gsl[0], :, cols] if g_blk > 1 else sh_ref[:, :, cols]
            hn = (x1[c] * rs * ng_ref[:, cols]).reshape(pshape)
            hn = (hn * (1.0 + sc) + sh).reshape(pr, tn)
            hn_ref[rows, cols] = hn
            logits = logits + _dot_hp(hn, wr_ref[cols, :])
        i1, i2, w1, w2 = _route(logits)
        hit1 = lane.astype(F32) == i1
        hit2 = lane.astype(F32) == i2
        onehots.append(jnp.where(hit1 | hit2, 1.0, 0.0).astype(BF16))
        hits.append((hit1, hit2))
        ids_ref[rows, :] = jnp.where(lane == 0, i1, jnp.where(lane == 1, i2, 0.0)).astype(I32)
        wts_ref[rows, :] = jnp.where(lane == 0, w1, jnp.where(lane == 1, w2, 0.0))

    total = cnts_ref[0:1, :]
    for h in range(ROUTE_PARTS):
        rows = slice(h * pr, (h + 1) * pr)
        hit1, hit2 = hits[h]
        before = _dot(tri_ref[...], onehots[h]) + total
        r1 = jnp.sum(jnp.where(hit1, before, 0.0), axis=1, keepdims=True)
        r2 = jnp.sum(jnp.where(hit2, before, 0.0), axis=1, keepdims=True)
        rank_ref[rows, :] = jnp.where(lane == 0, r1, jnp.where(lane == 1, r2, 0.0)).astype(I32)
        total = total + jnp.sum(onehots[h].astype(F32), axis=0, keepdims=True)
    cnts_ref[...] = jnp.broadcast_to(total, cnts_ref.shape)
    cnt_ref[...] = jnp.broadcast_to(total, cnt_ref.shape)


def _strict_lower(n):
    import numpy as np
    t = np.arange(n)[:, None]
    s = np.arange(n)[None, :]
    return jnp.asarray((s < t).astype(np.float32), dtype=BF16)


def _out_projection(o3, y3, x3, mod3, norm_g, w_out, w_router, b_router, cnt0, group_rows):
    bsz, t_len, d = x3.shape
    g_blk, r_blk = group_rows
    tm = g_blk * r_blk
    mix_a = o3.shape[-1]
    mix_b = y3.shape[-1]
    n_tok = bsz * t_len
    tiles_per_seq = max(t_len // r_blk, 1)

    xmap = lambda i: (i // tiles_per_seq, i % tiles_per_seq, 0)
    modmap = lambda s: (lambda i: (i // tiles_per_seq, 0, s))
    row = lambda i: (i, 0)
    const2 = lambda i: (0, 0)
    return pl.pallas_call(
        _outproj_kernel,
        grid=(n_tok // tm,),
        in_specs=[
            pl.BlockSpec((tm, mix_a), row),
            pl.BlockSpec((tm, mix_b), row),
            pl.BlockSpec((g_blk, r_blk, d), xmap),
            pl.BlockSpec((g_blk, 1, d), modmap(2)),
            pl.BlockSpec((g_blk, 1, d), modmap(4)),
            pl.BlockSpec((g_blk, 1, d), modmap(3)),
            pl.BlockSpec((1, d), const2),
            pl.BlockSpec(memory_space=pl.ANY),
            pl.BlockSpec((d, LANES), const2),
            pl.BlockSpec((1, LANES), const2),
            pl.BlockSpec((tm // ROUTE_PARTS, tm // ROUTE_PARTS), const2),
            pl.BlockSpec((SUBLANES, LANES), const2),
        ],
        out_specs=[
            pl.BlockSpec((g_blk, r_blk, d), xmap),
            pl.BlockSpec((tm, d), row),
            pl.BlockSpec((tm, LANES), row),
            pl.BlockSpec((tm, LANES), row),
            pl.BlockSpec((tm, LANES), row),
            pl.BlockSpec((SUBLANES, LANES), const2),
        ],
        out_shape=[
            jax.ShapeDtypeStruct((bsz, t_len, d), F32),
            jax.ShapeDtypeStruct((n_tok, d), F32),
            jax.ShapeDtypeStruct((n_tok, LANES), I32),
            jax.ShapeDtypeStruct((n_tok, LANES), F32),
            jax.ShapeDtypeStruct((n_tok, LANES), I32),
            jax.ShapeDtypeStruct((SUBLANES, LANES), F32),
        ],
        scratch_shapes=[pltpu.VMEM((d, d), BF16), pltpu.VMEM((WEIGHT_STAGE_ROWS, d), F32),
                        pltpu.SemaphoreType.DMA(()), pltpu.VMEM((SUBLANES, LANES), F32)],
        compiler_params=_cparams(("arbitrary",)),
        name="out_proj_router",
    )(o3.reshape(n_tok, mix_a), y3.reshape(n_tok, mix_b), x3, mod3, mod3, mod3,
      norm_g.reshape(1, d), w_out, w_router, b_router, _strict_lower(tm // ROUTE_PARTS), cnt0)


def _positions_kernel(ids_ref, rank_ref, cnt_ref, pos_ref, te_ref, nt_ref, lt_ref, *, block):
    lane8 = lax.broadcasted_iota(I32, (SUBLANES, LANES), 1)
    cnt = cnt_ref[...]
    tiles = jnp.floor((cnt + (MOE_TM - 1)) * (1.0 / MOE_TM))
    tiles = jnp.where(lane8 < N_EXPERTS, tiles, 0.0)
    incl = tiles
    shift = 1
    while shift < N_EXPERTS:
        incl = incl + jnp.where(lane8 >= shift, pltpu.roll(incl, shift, axis=1), 0.0)
        shift *= 2
    offs = ((incl - tiles) * MOE_TM)[0:1, :]
    nt_ref[...] = jnp.sum(tiles, axis=1, keepdims=True).astype(I32) + jnp.zeros(nt_ref.shape, I32)
    lt_ref[...] = jnp.where(tiles > 0.0, incl - 1.0, -1.0).astype(I32)

    incl_col = jnp.broadcast_to(incl[0:1, :], (LANES, LANES)).T
    e_row = lax.broadcasted_iota(I32, (LANES, LANES), 0)
    i_lane = lax.broadcasted_iota(I32, (LANES, LANES), 1).astype(F32)
    done = jnp.where((incl_col <= i_lane) & (e_row < N_EXPERTS), 1.0, 0.0)
    te = jnp.minimum(jnp.sum(done, axis=0, keepdims=True), N_EXPERTS - 1.0)
    te_ref[...] = te.astype(I32) + jnp.zeros(te_ref.shape, I32)

    lane = lax.broadcasted_iota(I32, (block, LANES), 1)

    def body(bi, c):
        r0 = pl.multiple_of(bi * block, block)
        ids = ids_ref[pl.ds(r0, block), :]
        rank = rank_ref[pl.ds(r0, block), :]
        out = jnp.zeros((block, LANES), I32)
        for k in range(TOP_K):
            off = jnp.sum(jnp.where(lane == ids[:, k:k + 1], offs, 0.0), axis=1, keepdims=True)
            out = jnp.where(lane == k, off.astype(I32) + rank[:, k:k + 1], out)
        pos_ref[pl.ds(r0, block), :] = out
        return c

    lax.fori_loop(0, ids_ref.shape[0] // block, body, 0)


def _positions(ids, rank, cnt):
    n_tok = ids.shape[0]
    block = 512
    kern = functools.partial(_positions_kernel, block=block)
    full = lambda s: pl.BlockSpec(s, lambda: tuple(0 for _ in s))
    return pl.pallas_call(
        kern,
        in_specs=[full((n_tok, LANES)), full((n_tok, LANES)), full((SUBLANES, LANES))],
        out_specs=[full((n_tok, LANES))] + [full((SUBLANES, LANES))] * 3,
        out_shape=[jax.ShapeDtypeStruct((n_tok, LANES), I32)]
        + [jax.ShapeDtypeStruct((SUBLANES, LANES), I32)] * 3,
        compiler_params=pltpu.CompilerParams(vmem_limit_bytes=VMEM_LIMIT),
        name="route_positions",
    )(ids, rank, cnt)


def _row_copy(src_ref, src_row, dst_ref, dst_row, sem):
    return pltpu.make_async_copy(src_ref.at[pl.ds(src_row, 1), :], dst_ref.at[pl.ds(dst_row, 1), :],
                                 sem)


def _dispatch_kernel(pos_ref, lt_ref, nt_ref, *refs, steps, n_tiles):
    n_groups = len(steps)
    hn_refs = refs[:n_groups]
    xg_ref, sem, zero_ref, zsem = refs[n_groups:]
    rows = hn_refs[0].shape[0]
    i = pl.program_id(0)

    def zero_copy(tile):
        return pltpu.make_async_copy(zero_ref, xg_ref.at[pl.ds(tile * MOE_TM, MOE_TM), :], zsem)

    @pl.when(i == 0)
    def _():
        zero_ref[...] = jnp.zeros(zero_ref.shape, zero_ref.dtype)
        nt = nt_ref[0]
        for e in range(N_EXPERTS):
            @pl.when(lt_ref[e] >= 0)
            def _():
                zero_copy(jnp.maximum(lt_ref[e], 0)).start()
        lax.fori_loop(nt, n_tiles, lambda t, c: (zero_copy(t).start(), c)[1], 0)
        for e in range(N_EXPERTS):
            @pl.when(lt_ref[e] >= 0)
            def _():
                zero_copy(0).wait()
        lax.fori_loop(nt, n_tiles, lambda t, c: (zero_copy(0).wait(), c)[1], 0)

    first = 0
    for hn_ref, n_steps in zip(hn_refs, steps):
        @pl.when((i >= first) & (i < first + n_steps))
        def _(hn_ref=hn_ref):
            base = i * rows

            def issue(r, c):
                for k in range(TOP_K):
                    _row_copy(hn_ref, r, xg_ref, pos_ref[(base + r) * TOP_K + k], sem).start(
                        priority=k)
                return c

            def drain(r, c):
                for k in range(TOP_K):
                    _row_copy(hn_ref, r, xg_ref, 0, sem).wait()
                return c

            lax.fori_loop(0, rows, issue, 0, unroll=8)
            lax.fori_loop(0, rows, drain, 0, unroll=8)

        first += n_steps


def _dispatch(pos_flat, last_tile, n_live, hns, n_tiles):
    d = hns[0].shape[1]
    steps = tuple(h.shape[0] // ROW_DMA_TILE for h in hns)
    firsts = [sum(steps[:g]) for g in range(len(steps))]

    def group_map(g):
        return lambda i, pos, lt, nt: (jnp.clip(i - firsts[g], 0, steps[g] - 1), 0)

    return pl.pallas_call(
        functools.partial(_dispatch_kernel, steps=steps, n_tiles=n_tiles),
        grid_spec=pltpu.PrefetchScalarGridSpec(
            num_scalar_prefetch=3,
            grid=(sum(steps),),
            in_specs=[pl.BlockSpec((ROW_DMA_TILE, d), group_map(g)) for g in range(len(hns))],
            out_specs=pl.BlockSpec(memory_space=pl.ANY),
            scratch_shapes=[pltpu.SemaphoreType.DMA(()), pltpu.VMEM((MOE_TM, d), F32),
                            pltpu.SemaphoreType.DMA(())],
        ),
        out_shape=jax.ShapeDtypeStruct((n_tiles * MOE_TM, d), F32),
        compiler_params=_cparams(("arbitrary",)),
        name="moe_dispatch",
    )(pos_flat, last_tile, n_live, *hns)


WEIGHT_PARTS = 4


def _weight_copies(w_hbm_refs, expert, wbuf_ref, slot, sem_ref):
    copies = []
    for i, w_ref in enumerate(w_hbm_refs):
        part_rows = w_ref.shape[1] // WEIGHT_PARTS
        for part in range(WEIGHT_PARTS):
            rows = pl.ds(part * part_rows, part_rows)
            copies.append(pltpu.make_async_copy(w_ref.at[expert, rows], wbuf_ref.at[slot, i, rows],
                                                sem_ref.at[slot]))
    return copies


def _start_weight_copies(copies, queue0_parts):
    for k, cp in enumerate(copies):
        cp.start(priority=0 if k % WEIGHT_PARTS < queue0_parts else 1)


def _expert_weights_slot(te_ref, nt, t, w_hbm_refs, wbuf_ref, sem_ref, slot_ref, queue0_parts):
    expert = te_ref[t]

    @pl.when(t == 0)
    def _():
        slot_ref[0] = 0
        _start_weight_copies(_weight_copies(w_hbm_refs, expert, wbuf_ref, 0, sem_ref),
                             queue0_parts)

    @pl.when((t == 0) | (expert != te_ref[jnp.maximum(t - 1, 0)]))
    def _():
        @pl.when(t > 0)
        def _():
            slot_ref[0] = 1 - slot_ref[0]

        slot = slot_ref[0]
        for cp in _weight_copies(w_hbm_refs, expert, wbuf_ref, slot, sem_ref):
            cp.wait()
        nxt = lax.while_loop(lambda u: (u < nt) & (te_ref[jnp.minimum(u, nt - 1)] == expert),
                             lambda u: u + 1, t + 1)

        @pl.when(nxt < nt)
        def _():
            _start_weight_copies(_weight_copies(w_hbm_refs, te_ref[jnp.minimum(nxt, nt - 1)],
                                                wbuf_ref, 1 - slot, sem_ref), queue0_parts)

    return slot_ref[0]


def _moe_up_kernel(te_ref, nt_ref, x_ref, w1_ref, w3_ref, act_ref, wbuf_ref, sem_ref, slot_ref):
    t = pl.program_id(0)
    nt = nt_ref[0]

    @pl.when(t < nt)
    def _():
        slot = _expert_weights_slot(te_ref, nt, t, (w1_ref, w3_ref), wbuf_ref, sem_ref, slot_ref,
                                    queue0_parts=1)
        xb = x_ref[...].astype(BF16)
        a = _dot(xb, wbuf_ref[slot, 0].astype(BF16))
        b = _dot(xb, wbuf_ref[slot, 1].astype(BF16))
        act_ref[...] = (_silu(a) * b).astype(BF16)

    @pl.when(t >= nt)
    def _():
        act_ref[...] = jnp.zeros(act_ref.shape, act_ref.dtype)


def _moe_down_kernel(te_ref, nt_ref, act_ref, w2_ref, out_ref, wbuf_ref, sem_ref, slot_ref):
    t = pl.program_id(0)
    nt = nt_ref[0]

    @pl.when(t < nt)
    def _():
        slot = _expert_weights_slot(te_ref, nt, t, (w2_ref,), wbuf_ref, sem_ref, slot_ref,
                                    queue0_parts=0)
        out_ref[...] = _dot(act_ref[...], wbuf_ref[slot, 0].astype(BF16))

    @pl.when(t >= nt)
    def _():
        out_ref[...] = jnp.zeros(out_ref.shape, out_ref.dtype)


def _moe_experts(te, nt, xg, w1, w3, w2):
    n_rows, d = xg.shape
    d_exp = w1.shape[-1]
    n_tiles = n_rows // MOE_TM
    live_tile = lambda t, te_r, nt_r: (jnp.minimum(t, nt_r[0] - 1), 0)
    any_spec = pl.BlockSpec(memory_space=pl.ANY)

    act = pl.pallas_call(
        _moe_up_kernel,
        grid_spec=pltpu.PrefetchScalarGridSpec(
            num_scalar_prefetch=2,
            grid=(n_tiles,),
            in_specs=[pl.BlockSpec((MOE_TM, d), live_tile), any_spec, any_spec],
            out_specs=pl.BlockSpec((MOE_TM, d_exp), lambda t, te_r, nt_r: (t, 0)),
            scratch_shapes=[pltpu.VMEM((2, 2, d, d_exp), F32), pltpu.SemaphoreType.DMA((2,)),
                            pltpu.SMEM((1,), I32)],
        ),
        out_shape=jax.ShapeDtypeStruct((n_rows, d_exp), BF16),
        compiler_params=_cparams(("arbitrary",)),
        name="moe_up",
    )(te, nt, xg, w1, w3)

    return pl.pallas_call(
        _moe_down_kernel,
        grid_spec=pltpu.PrefetchScalarGridSpec(
            num_scalar_prefetch=2,
            grid=(n_tiles,),
            in_specs=[pl.BlockSpec((MOE_TM, d_exp), live_tile), any_spec],
            out_specs=pl.BlockSpec((MOE_TM, d), lambda t, te_r, nt_r: (t, 0)),
            scratch_shapes=[pltpu.VMEM((2, 1, d_exp, d), F32), pltpu.SemaphoreType.DMA((2,)),
                            pltpu.SMEM((1,), I32)],
        ),
        out_shape=jax.ShapeDtypeStruct((n_rows, d), F32),
        compiler_params=_cparams(("arbitrary",)),
        name="moe_down",
    )(te, nt, act, w2)


def _combine_kernel(pos_ref, x1_ref, g2_ref, wts_ref, fg_ref, eo_ref, y_ref, buf_ref, sem):
    rows = wts_ref.shape[0]
    base = pl.program_id(0) * rows

    def issue(r, c):
        for k in range(TOP_K):
            _row_copy(eo_ref, pos_ref[(base + r) * TOP_K + k], buf_ref.at[k], r, sem).start(
                priority=k)
        return c

    def drain(r, c):
        for k in range(TOP_K):
            _row_copy(eo_ref, 0, buf_ref.at[k], r, sem).wait()
        return c

    lax.fori_loop(0, rows, issue, 0, unroll=8)
    lax.fori_loop(0, rows, drain, 0, unroll=8)
    w = wts_ref[...]
    moe = w[:, 0:1] * buf_ref[0] + w[:, 1:2] * buf_ref[1]
    x2 = x1_ref[...] + g2_ref[...] * moe.reshape(x1_ref.shape)
    ms = jnp.mean(x2 * x2, axis=-1, keepdims=True)
    y_ref[...] = x2 * lax.rsqrt(ms + EPS) * fg_ref[...]


def _combine(pos_flat, x1, mod3, wts, final_g, expert_out, group_rows):
    bsz, t_len, d = x1.shape
    g_blk, r_blk = group_rows
    tm = g_blk * r_blk
    tiles_per_seq = max(t_len // r_blk, 1)

    def xmap(i, pos):
        return (i // tiles_per_seq, i % tiles_per_seq, 0)

    return pl.pallas_call(
        _combine_kernel,
        grid_spec=pltpu.PrefetchScalarGridSpec(
            num_scalar_prefetch=1,
            grid=((bsz * t_len) // tm,),
            in_specs=[
                pl.BlockSpec((g_blk, r_blk, d), xmap),
                pl.BlockSpec((g_blk, 1, d), lambda i, pos: (i // tiles_per_seq, 0, 5)),
                pl.BlockSpec((tm, LANES), lambda i, pos: (i, 0)),
                pl.BlockSpec((1, d), lambda i, pos: (0, 0)),
                pl.BlockSpec(memory_space=pl.ANY),
            ],
            out_specs=pl.BlockSpec((g_blk, r_blk, d), xmap),
            scratch_shapes=[pltpu.VMEM((TOP_K, tm, d), F32), pltpu.SemaphoreType.DMA(())],
        ),
        out_shape=jax.ShapeDtypeStruct((bsz, t_len, d), F32),
        compiler_params=_cparams(("arbitrary",)),
        name="moe_combine",
    )(pos_flat, x1, mod3, wts, final_g.reshape(1, d), expert_out)


def kernel(x_prompt, x_sample, state_hgrn, state_ssm, state_conv, c_prompt, c_sample, ada_w, ada_b,
           norm1_g, norm2_g, w_in, hgrn_lb, hgrn_onorm_g, conv_w, conv_b, dt_bias, a_log, d_skip,
           ssm_norm_g, w_out, w_grp, b_grp, w_rt, b_rt, w1, w3, w2, final_g):
    depth = w_in.shape[0]
    assert depth == 1, "single-layer trunk"
    layer = 0
    bp, t_p, d = x_prompt.shape
    bs, t_s, _ = x_sample.shape
    mix_a = hgrn_onorm_g.shape[1]
    mix_b = ssm_norm_g.shape[1]
    conv_dim = conv_w.shape[2]
    n_main = 4 * mix_a + mix_b + conv_dim
    n_ssd_heads = dt_bias.shape[1]
    n_pairs = mix_b // LANES
    xbc0 = 4 * mix_a + mix_b
    tail = CONV_WIDTH - 1
    assert t_s <= SUBLANES and t_s >= tail and t_p % PROMPT_CHUNK == 0

    xs_pad = jnp.pad(x_sample, ((0, 0), (0, SUBLANES - t_s), (0, 0)))
    n_c = bp + bs
    c_rows = -(-n_c // SUBLANES) * SUBLANES
    c_all = jnp.pad(jnp.concatenate([c_prompt, c_sample], axis=0), ((0, c_rows - n_c), (0, 0)))

    mod = _modulation(c_all, ada_w[layer], ada_b[layer])
    mod_p = mod[:bp].reshape(bp, 1, -1)
    mod_s = mod[bp:n_c].reshape(bs, 1, -1)
    w_in_t = jnp.swapaxes(w_in[layer], 0, 1)
    w_dt_t = jnp.pad(w_in_t[n_main:], ((0, LANES - n_ssd_heads), (0, 0)))
    w_router = jnp.pad(jnp.concatenate([w_rt[layer], w_grp[layer]], axis=1),
                       ((0, 0), (0, LANES - N_EXPERTS - N_EXPERT_GROUPS)))
    b_router = jnp.pad(jnp.concatenate([b_rt[layer], b_grp[layer]]),
                       (0, LANES - N_EXPERTS - N_EXPERT_GROUPS)).reshape(1, LANES)

    groups = (
        (x_prompt, mod_p, None, None, None, t_p, PROMPT_CHUNK, PROMPT_CHUNK, MIXER_ROWS,
         (LANES // PROMPT_CHUNK, 8), LANES // PROMPT_CHUNK, 1, (1, OUT_TM)),
        (xs_pad, mod_s, state_hgrn[layer], state_ssm[layer].reshape(bs, n_pairs, LANES, LANES),
         state_conv[layer], t_s, SUBLANES, t_s, SUBLANES,
         (LANES // SUBLANES, 2), LANES // SUBLANES, PROJ_SEQS, (OUT_TM // SUBLANES, SUBLANES)),
    )

    cnt = jnp.zeros((SUBLANES, LANES), F32)
    per_group = []
    for x3, mod3, s0, h0, c0, t_real, ch, valid, tt, (hg_gb, hg_hb), ssd_gb, proj_gr, out_gr in groups:
        bsz, t_len, _ = x3.shape
        proj, dt = _in_projection(x3, mod3, norm1_g[layer], w_in_t, w_dt_t, n_main, proj_gr)
        proj3 = proj.reshape(bsz, t_len, n_main)
        dt3 = dt.reshape(bsz, t_len, LANES)
        o3, s_new = _hgrn_heads(proj3, hgrn_lb, hgrn_onorm_g[layer], s0, ch=ch, valid=valid,
                                gb=hg_gb, hb=hg_hb, tt=tt, layer=layer)
        y3, h_new = _ssd_pairs(proj3, dt3, conv_w[layer], conv_b[layer], dt_bias[layer],
                               a_log[layer], d_skip[layer], ssm_norm_g[layer], h0, c0, mix_a=mix_a,
                               mix_b=mix_b, ch=ch, valid=valid, gb=ssd_gb, tt=tt)
        conv_new = proj3[:, t_real - tail:t_real, xbc0:xbc0 + conv_dim]
        x1, hn, ids, wts, rank, cnt = _out_projection(
            o3, y3, x3, mod3, norm2_g[layer], w_out[layer], w_router, b_router, cnt, out_gr)
        per_group.append(dict(x1=x1, hn=hn, ids=ids, wts=wts, rank=rank, mod3=mod3, out_gr=out_gr,
                              s_new=s_new, h_new=h_new, conv_new=conv_new))

    ids_all = jnp.concatenate([g["ids"] for g in per_group], axis=0)
    rank_all = jnp.concatenate([g["rank"] for g in per_group], axis=0)
    n_tok = ids_all.shape[0]
    pos, te, nt, lt = _positions(ids_all, rank_all, cnt)
    pos_flat = pos[:, :TOP_K].reshape(-1)
    n_tiles = -(-(n_tok * TOP_K + N_EXPERTS * (MOE_TM - 1)) // MOE_TM)
    assert n_tiles <= LANES
    te_vec = te[0, :n_tiles]
    nt_vec = nt[0, :1]
    lt_vec = lt[0, :N_EXPERTS]

    row0 = 0
    for g in per_group:
        rows = g["hn"].shape[0]
        g["pos"] = pos_flat[row0 * TOP_K:(row0 + rows) * TOP_K]
        row0 += rows
    xg = _dispatch(pos_flat, lt_vec, nt_vec, [g["hn"] for g in per_group], n_tiles)
    expert_out = _moe_experts(te_vec, nt_vec, xg, w1[layer], w3[layer], w2[layer])
    ys = [_combine(g["pos"], g["x1"], g["mod3"], g["wts"], final_g, expert_out, g["out_gr"])
          for g in per_group]

    gp, gs = per_group
    return (
        ys[0],
        ys[1][:, :t_s],
        gp["s_new"][None],
        gp["h_new"].reshape(1, bp, n_ssd_heads, SSD_HEAD_DIM, SSD_STATE),
        gp["conv_new"][None],
        gs["s_new"][None],
        gs["h_new"].reshape(1, bs, n_ssd_heads, SSD_HEAD_DIM, SSD_STATE),
        gs["conv_new"][None],
    )
```

```python
import functools

import jax
import jax.numpy as jnp
from jax import lax
from jax.experimental import pallas as pl
from jax.experimental.pallas import tpu as pltpu

F32 = jnp.float32
BF16 = jnp.bfloat16
I32 = jnp.int32
EPS = 1e-6
LOG2E = 1.4426950408889634
MAX_LOG2_DECAY = 12

LANES = 128
SUBLANES = 8
VMEM_LIMIT = 56 * 1024 * 1024

HGRN_HEAD_DIM = 128
SSD_HEAD_DIM = 64
SSD_GROUPS = 2
SSD_STATE = 128
CONV_WIDTH = 4
N_EXPERT_GROUPS = 4
EXPERTS_PER_GROUP = 8
N_EXPERTS = N_EXPERT_GROUPS * EXPERTS_PER_GROUP
TOP_K = 2

PROJ_SEQS = 128
NORM_ROWS = 256
PROJ_TN = 512
MOE_TM = 256
PROMPT_CHUNK = 64
MIXER_ROWS = 512
ROW_DMA_TILE = 256


def _cparams(sem):
    return pltpu.CompilerParams(dimension_semantics=sem, vmem_limit_bytes=VMEM_LIMIT)


def _dot(a, b):
    return jnp.dot(a, b, preferred_element_type=F32)


def _dot_nt(a, b):
    return lax.dot_general(a, b, (((1,), (1,)), ((), ())), preferred_element_type=F32)


def _split2(x):
    hi = x.astype(BF16)
    lo = (x - hi.astype(F32)).astype(BF16)
    return hi, lo


def _split3(x):
    hi = x.astype(BF16)
    r = x - hi.astype(F32)
    mid = r.astype(BF16)
    lo = (r - mid.astype(F32)).astype(BF16)
    return hi, mid, lo


def _dot_exact_lhs(m_bf16, x):
    hi, mid, lo = _split3(x)
    return _dot(m_bf16, hi) + _dot(m_bf16, mid) + _dot(m_bf16, lo)


def _dot_exact_rhs(x, m_bf16):
    hi, mid, lo = _split3(x)
    return _dot(hi, m_bf16) + _dot(mid, m_bf16) + _dot(lo, m_bf16)


def _dot_hp(a, w):
    ah, al = _split2(a)
    wh, wl = _split2(w)
    return _dot(ah, wh) + _dot(al, wh) + _dot(ah, wl)


def _silu(x):
    return x * jax.nn.sigmoid(x)


def _mod_kernel(c_ref, w_hbm_ref, b_ref, o_ref, wbuf_ref, wsem):
    j = pl.program_id(0)
    d, tn = wbuf_ref.shape[1], wbuf_ref.shape[2]
    slot = j % 2

    def copies(col_tile, slot):
        return [pltpu.make_async_copy(
            w_hbm_ref.at[pl.ds(h * (d // 2), d // 2), pl.ds(col_tile * tn, tn)],
            wbuf_ref.at[slot, pl.ds(h * (d // 2), d // 2), :], wsem.at[slot]) for h in range(2)]

    def start(col_tile, slot):
        for h, cp in enumerate(copies(col_tile, slot)):
            cp.start(priority=h)

    @pl.when(j == 0)
    def _():
        start(0, 0)

    @pl.when(j + 1 < pl.num_programs(0))
    def _():
        start(j + 1, 1 - slot)

    for cp in copies(j, slot):
        cp.wait()
    a = _silu(c_ref[...])
    o_ref[...] = _dot_hp(a, wbuf_ref[slot]) + b_ref[...]


def _modulation(c_all, ada_w, ada_b):
    rows, d = c_all.shape
    n_out = ada_w.shape[1]
    tn = 1024
    return pl.pallas_call(
        _mod_kernel,
        grid=(n_out // tn,),
        in_specs=[
            pl.BlockSpec((rows, d), lambda j: (0, 0)),
            pl.BlockSpec(memory_space=pl.ANY),
            pl.BlockSpec((1, tn), lambda j: (0, j)),
        ],
        out_specs=pl.BlockSpec((rows, tn), lambda j: (0, j)),
        out_shape=jax.ShapeDtypeStruct((rows, n_out), F32),
        scratch_shapes=[pltpu.VMEM((2, d, tn), F32), pltpu.SemaphoreType.DMA((2,))],
        compiler_params=_cparams(("arbitrary",)),
        name="adaln_mod",
    )(c_all, ada_w, ada_b.reshape(1, n_out))


def _proj_kernel(x_hbm_ref, sc_ref, sh_ref, g_ref, w_hbm_ref, wdt_ref, o_ref, dt_ref, hn_ref, xs_ref,
                 sem, wbuf_ref, wsem):
    i = pl.program_id(0)
    j = pl.program_id(1)
    n_i = pl.num_programs(0)
    n_j = pl.num_programs(1)
    g_blk, r_blk, d = xs_ref.shape
    tn = wbuf_ref.shape[1]

    def x_copy(tile):
        return pltpu.make_async_copy(x_hbm_ref.at[pl.ds(tile * g_blk, g_blk)], xs_ref, sem)

    def w_copy(col_tile, slot):
        return pltpu.make_async_copy(w_hbm_ref.at[pl.ds(col_tile * tn, tn), :], wbuf_ref.at[slot],
                                     wsem.at[slot])

    slot = (i * n_j + j) % 2

    @pl.when((i == 0) & (j == 0))
    def _():
        w_copy(0, 0).start(priority=1)

    @pl.when((i + 1 < n_i) | (j + 1 < n_j))
    def _():
        w_copy((j + 1) % n_j, 1 - slot).start(priority=1)

    @pl.when(j == 0)
    def _():
        @pl.when(i == 0)
        def _():
            x_copy(0).start()

        x_copy(i).wait()
        wh, wl = _split2(wdt_ref[...])
        n_rows = g_blk * r_blk
        step = min(n_rows, NORM_ROWS)
        for c in range(n_rows // step):
            if g_blk == 1:
                x = xs_ref[:, c * step:(c + 1) * step, :]
                sc, sh = sc_ref[...], sh_ref[...]
            else:
                gs = slice(c * step // r_blk, (c + 1) * step // r_blk)
                x, sc, sh = xs_ref[gs], sc_ref[gs], sh_ref[gs]
            ms = jnp.mean(x * x, axis=-1, keepdims=True)
            hn = ((x * lax.rsqrt(ms + EPS)) * g_ref[...] * (1.0 + sc) + sh).reshape(step, d)
            rows = slice(c * step, (c + 1) * step)
            hn_ref[rows, :] = hn.astype(BF16)
            ah, al = _split2(hn)
            dt_ref[rows, :] = _dot_nt(ah, wh) + _dot_nt(al, wh) + _dot_nt(ah, wl)

        @pl.when(i + 1 < pl.num_programs(0))
        def _():
            x_copy(i + 1).start()

    w_copy(j, slot).wait()
    o_ref[...] = _dot_nt(hn_ref[...], wbuf_ref[slot].astype(BF16))


def _in_projection(x3, mod3, norm_g, w_t, w_dt_t, n_main, g_blk):
    bsz, t_len, d = x3.shape
    tm = g_blk * t_len
    n_row_tiles = bsz // g_blk

    def mod_map(sec):
        return lambda i, j: (i, 0, sec)

    return pl.pallas_call(
        _proj_kernel,
        grid=(n_row_tiles, n_main // PROJ_TN),
        in_specs=[
            pl.BlockSpec(memory_space=pl.ANY),
            pl.BlockSpec((g_blk, 1, d), mod_map(1)),
            pl.BlockSpec((g_blk, 1, d), mod_map(0)),
            pl.BlockSpec((1, d), lambda i, j: (0, 0)),
            pl.BlockSpec(memory_space=pl.ANY),
            pl.BlockSpec((LANES, d), lambda i, j: (0, 0)),
        ],
        out_specs=[
            pl.BlockSpec((tm, PROJ_TN), lambda i, j: (i, j)),
            pl.BlockSpec((tm, LANES), lambda i, j: (i, 0)),
        ],
        out_shape=[
            jax.ShapeDtypeStruct((bsz * t_len, n_main), F32),
            jax.ShapeDtypeStruct((bsz * t_len, LANES), F32),
        ],
        scratch_shapes=[pltpu.VMEM((tm, d), BF16), pltpu.VMEM((g_blk, t_len, d), F32),
                        pltpu.SemaphoreType.DMA(()), pltpu.VMEM((2, PROJ_TN, d), F32),
                        pltpu.SemaphoreType.DMA((2,))],
        compiler_params=_cparams(("arbitrary", "arbitrary")),
        name="in_proj",
    )(x3, mod3, mod3, norm_g.reshape(1, d), w_t, w_dt_t)


def _level_sizes(ch):
    sizes = []
    sz = SUBLANES
    while 2 * sz <= ch:
        sizes.append(sz)
        sz *= 2
    return sizes


def _level_masks(ch, n_rows):
    import numpy as np
    t = np.arange(n_rows)[:, None]
    s = np.arange(n_rows)[None, :]
    out = []
    for sz in _level_sizes(ch):
        m = (t // (2 * sz) == s // (2 * sz)) & (t % (2 * sz) >= sz) & (s % (2 * sz) < sz)
        out.append(m.astype(np.float32))
    out.append(((t // SUBLANES == s // SUBLANES) & (s <= t)).astype(np.float32))
    return jnp.asarray(np.stack(out))


def _hgrn_kernel(*refs, ch, valid, nc, gb, hb, has_init, layer):
    if has_init:
        (q_ref, f_ref, i_ref, g_ref, lb_ref, on_ref, tri_ref, masks_ref, s0_ref,
         o_ref, sout_ref, st_ref, b_scr, k_scr) = refs
    else:
        (q_ref, f_ref, i_ref, g_ref, lb_ref, on_ref, tri_ref, masks_ref,
         o_ref, sout_ref, st_ref, b_scr, k_scr) = refs
        s0_ref = None

    n_lb = lb_ref.shape[0]
    lb_rows = [lb_ref[i:i + 1, :] for i in range(n_lb)]
    lb_max = functools.reduce(jnp.maximum, lb_rows)
    lb_exp = [jnp.exp(r - lb_max) for r in lb_rows]
    lb = sum(lb_exp[:layer + 1]) / sum(lb_exp)
    onorm = on_ref[...]
    tri = tri_ref[...]
    sizes = _level_sizes(ch)
    lane8 = lax.broadcasted_iota(I32, (SUBLANES, LANES), 1)
    row8 = lax.broadcasted_iota(I32, (SUBLANES, LANES), 0)
    tb = pl.program_id(2)

    n_rows = gb * ch
    row_id = lax.broadcasted_iota(I32, (n_rows, LANES), 0)

    def stack(fn):
        parts = [fn(g) for g in range(gb)]
        return parts[0] if gb == 1 else jnp.concatenate(parts, axis=0)

    def chunk(hh, ci, bounded):
        cols = slice(hh * LANES, (hh + 1) * LANES)
        t0 = pl.multiple_of(ci * ch, ch)
        rows = pl.ds(t0, ch)
        lb_h = lb[:, cols]
        q = _silu(stack(lambda g: q_ref[g, rows, cols]))
        fg = lb_h + (1.0 - lb_h) * jax.nn.sigmoid(stack(lambda g: f_ref[g, rows, cols]))
        k = 1.0 - fg
        lf = jnp.log(fg)
        v = stack(lambda g: i_ref[g, rows, cols])
        if valid < ch:
            live = (row_id & (ch - 1)) < valid
            lf = jnp.where(live, lf, 0.0)
            k = jnp.where(live, k, 0.0)
        b2 = _dot_exact_lhs(tri, lf) * LOG2E
        b_scr[hh] = b2
        k_scr[hh] = k
        b_end_rows = [b_scr[hh, pl.ds(g * ch + ch - 1, 1), :] for g in range(gb)]
        b_end = stack(lambda g: jnp.broadcast_to(b_end_rows[g], (ch, LANES)))

        qe = (q * jnp.exp2(b2)).astype(BF16)
        o = stack(lambda g: _dot_nt(qe[g * ch:(g + 1) * ch], st_ref[g * hb + hh].astype(BF16)))

        scores = jnp.zeros((n_rows, n_rows), F32)
        for lvl, sz in enumerate(sizes):
            pieces = []
            for m in range(n_rows // (2 * sz)):
                r = b_scr[hh, pl.ds(2 * sz * m + sz - 1, 1), :]
                pieces.append(jnp.broadcast_to(r, (2 * sz, LANES)))
            r_all = pieces[0] if len(pieces) == 1 else jnp.concatenate(pieces, axis=0)
            e = jnp.exp2(-jnp.abs(b2 - r_all))
            s_l = _dot_nt((q * e).astype(BF16), (k * e).astype(BF16))
            scores = scores + masks_ref[lvl] * s_l
        if bounded:
            pieces = []
            for jb in range(n_rows // SUBLANES):
                if (jb * SUBLANES) % ch == 0:
                    pieces.append(jnp.zeros((SUBLANES, LANES), F32))
                else:
                    pieces.append(jnp.broadcast_to(b_scr[hh, pl.ds(jb * SUBLANES - 1, 1), :],
                                                   (SUBLANES, LANES)))
            r8 = jnp.concatenate(pieces, axis=0)
            s_d = _dot_nt((q * jnp.exp2(b2 - r8)).astype(BF16),
                          (k * jnp.exp2(r8 - b2)).astype(BF16))
            scores = scores + jnp.where(masks_ref[len(sizes)] > 0.0, s_d, 0.0)
        else:
            blocks = []
            for jb in range(n_rows // SUBLANES):
                bb = b2[SUBLANES * jb:SUBLANES * (jb + 1)]
                qb = q[SUBLANES * jb:SUBLANES * (jb + 1)]
                acc = jnp.zeros((SUBLANES, LANES), F32)
                for sl in range(SUBLANES):
                    s = SUBLANES * jb + sl
                    bs = b_scr[hh, pl.ds(s, 1), :]
                    ks = k_scr[hh, pl.ds(s, 1), :]
                    val = (qb * ks) * jnp.exp2(bb - bs)
                    red = jnp.sum(val, axis=1, keepdims=True)
                    acc = jnp.where(lane8 == s, red, acc)
                blocks.append(jnp.where(lane8 - SUBLANES * jb <= row8, acc, 0.0))
            scores = scores + jnp.concatenate(blocks, axis=0)
        o = o + _dot(scores.astype(BF16), v.astype(BF16))

        dk = k * jnp.exp2(b_end - b2)
        v_t = v.T.astype(BF16)
        for g in range(gb):
            own = (row_id >= g * ch) & (row_id < (g + 1) * ch)
            c = g * hb + hh
            st_ref[c] = (st_ref[c] * jnp.exp2(b_end_rows[g])
                         + _dot(v_t, jnp.where(own, dk, 0.0).astype(BF16)))

        on = o * lax.rsqrt(jnp.mean(o * o, axis=-1, keepdims=True) + EPS) * onorm[:, cols]
        out = on * _silu(stack(lambda g: g_ref[g, rows, cols]))
        for g in range(gb):
            o_ref[g, rows, cols] = out[g * ch:(g + 1) * ch].astype(o_ref.dtype)

    chains = [(g * hb + hh, g, hh) for g in range(gb) for hh in range(hb)]

    @pl.when(tb == 0)
    def _():
        for c, g, hh in chains:
            if has_init:
                st_ref[c] = s0_ref[g, hh].T
            else:
                st_ref[c] = jnp.zeros((LANES, LANES), F32)

    def run(bounded):
        def step(ci, carry):
            for hh in range(hb):
                chunk(hh, ci, bounded)
            return carry

        if nc == 1:
            step(0, 0)
        else:
            lax.fori_loop(0, nc, step, 0, unroll=2)

    if ch > SUBLANES:
        gates_bounded = jnp.min(lb) >= 2.0 ** -MAX_LOG2_DECAY
        lax.cond(gates_bounded, lambda: run(True), lambda: run(False))
    else:
        run(False)

    @pl.when(tb == pl.num_programs(2) - 1)
    def _():
        for c, g, hh in chains:
            sout_ref[g, hh] = st_ref[c].T


def _hgrn_heads(proj3, hgrn_lb, onorm_g, s0, *, ch, valid, gb, hb, tt, layer):
    bsz, t_len, _ = proj3.shape
    n_heads = onorm_g.shape[0] // HGRN_HEAD_DIM
    n_hg = n_heads // hb
    nc = tt // ch
    n_rows = gb * ch
    assert n_rows == LANES
    has_init = s0 is not None
    w = hb * LANES

    def col(sec):
        return lambda b, h, t: (b, t, sec * n_hg + h)

    in_specs = [pl.BlockSpec((gb, tt, w), col(s)) for s in range(4)]
    in_specs += [
        pl.BlockSpec((hgrn_lb.shape[0], w), lambda b, h, t: (0, h)),
        pl.BlockSpec((1, w), lambda b, h, t: (0, h)),
        pl.BlockSpec((n_rows, n_rows), lambda b, h, t: (0, 0)),
        pl.BlockSpec((len(_level_sizes(ch)) + 1, n_rows, n_rows), lambda b, h, t: (0, 0, 0)),
    ]
    args = [proj3, proj3, proj3, proj3, hgrn_lb, onorm_g.reshape(1, -1),
            jnp.asarray(_block_tri(gb, ch), dtype=BF16), _level_masks(ch, n_rows)]
    if has_init:
        in_specs.append(pl.BlockSpec((gb, hb, LANES, LANES), lambda b, h, t: (b, h, 0, 0)))
        args.append(s0)
    kern = functools.partial(_hgrn_kernel, ch=ch, valid=valid, nc=nc, gb=gb, hb=hb,
                             has_init=has_init, layer=layer)
    n_chain = gb * hb
    return pl.pallas_call(
        kern,
        grid=(bsz // gb, n_hg, t_len // tt),
        in_specs=in_specs,
        out_specs=[
            pl.BlockSpec((gb, tt, w), lambda b, h, t: (b, t, h)),
            pl.BlockSpec((gb, hb, LANES, LANES), lambda b, h, t: (b, h, 0, 0)),
        ],
        out_shape=[
            jax.ShapeDtypeStruct((bsz, t_len, n_heads * HGRN_HEAD_DIM), BF16),
            jax.ShapeDtypeStruct((bsz, n_heads, LANES, LANES), F32),
        ],
        scratch_shapes=[pltpu.VMEM((n_chain, LANES, LANES), F32),
                        pltpu.VMEM((hb, n_rows, LANES), F32),
                        pltpu.VMEM((hb, n_rows, LANES), F32)],
        compiler_params=_cparams(("arbitrary", "arbitrary", "arbitrary")),
        name="hgrn2_heads",
    )(*args)


def _softplus(x):
    return jnp.maximum(x, 0.0) + jnp.log1p(jnp.exp(-jnp.abs(x)))


def _ssd_kernel(*refs, ch, valid, nc, gb, pb, has_init):
    if has_init:
        (z_ref, x_ref, b_ref, c_ref, dt_ref, cwx_ref, cwb_ref, cwc_ref, cbx_ref, cbb_ref, cbc_ref,
         xj_ref, bias_ref, alog_ref, d_ref, ng_ref, tri_ref, tril_ref, h0_ref, cx0_ref, cb0_ref,
         cc0_ref, y_ref, hout_ref, ht_ref, wx_ref, wb_ref, wc_ref) = refs
    else:
        (z_ref, x_ref, b_ref, c_ref, dt_ref, cwx_ref, cwb_ref, cwc_ref, cbx_ref, cbb_ref, cbc_ref,
         xj_ref, bias_ref, alog_ref, d_ref, ng_ref, tri_ref, tril_ref,
         y_ref, hout_ref, ht_ref, wx_ref, wb_ref, wc_ref) = refs
        h0_ref = cx0_ref = cb0_ref = cc0_ref = None

    tri = tri_ref[...]
    tril = tril_ref[...]
    lane = lax.broadcasted_iota(I32, (LANES, LANES), 1)
    first_head = lane < SSD_HEAD_DIM
    tail = CONV_WIDTH - 1
    tb = pl.program_id(1)

    def conv(raw_ref, win_ref, cw_ref, cb_ref, g, t0):
        win_ref[g, pl.ds(SUBLANES, ch), :] = raw_ref[g, pl.ds(t0, ch), :]
        u = cb_ref[...]
        for j in range(CONV_WIDTH):
            u = u + win_ref[g, pl.ds(SUBLANES - tail + j, ch), :] * cw_ref[j:j + 1, :]
        win_ref[g, pl.ds(0, SUBLANES), :] = win_ref[g, pl.ds(ch, SUBLANES), :]
        return _silu(u)

    n_rows = gb * ch
    w = pb * LANES
    n_grp = SSD_GROUPS
    row_id = lax.broadcasted_iota(I32, (n_rows, w), 0)

    def stack(fn):
        parts = [fn(g) for g in range(gb)]
        return parts[0] if gb == 1 else jnp.concatenate(parts, axis=0)

    def group(q, rows, xs, bm, cm, dt_raw, z):
        wcols = slice(q * w, (q + 1) * w)
        dt = _softplus(_dot_exact_rhs(dt_raw, xj_ref[q]) + bias_ref[q])
        if valid < ch:
            dt = jnp.where((row_id & (ch - 1)) < valid, dt, 0.0)
        la_cs = _dot_exact_lhs(tri, dt * (-jnp.exp(alog_ref[q])))
        xdt = xs * dt
        a_end_rows = [la_cs[g * ch + ch - 1:g * ch + ch, :] for g in range(gb)]
        a_end = stack(lambda g: jnp.broadcast_to(a_end_rows[g], (ch, w)))

        cb = _dot_nt(cm.astype(BF16), bm.astype(BF16))
        acs_t = la_cs.T
        y_parts = []
        for p in range(pb):
            cols = slice(p * LANES, (p + 1) * LANES)
            scores = []
            for hl in range(2):
                at = p * LANES + hl * SSD_HEAD_DIM
                lmat = jnp.exp(jnp.minimum(la_cs[:, at:at + 1] - acs_t[at:at + 1, :], 0.0)) * tril
                scores.append((cb * lmat).astype(BF16))
            xp = xdt[:, cols]
            rhs = jnp.concatenate([jnp.where(first_head, xp, 0.0),
                                   jnp.where(first_head, 0.0, xp)], axis=0).astype(BF16)
            y_parts.append(_dot(jnp.concatenate(scores, axis=1), rhs))
        y = y_parts[0] if pb == 1 else jnp.concatenate(y_parts, axis=1)

        y_off = stack(lambda g: _dot(cm[g * ch:(g + 1) * ch].astype(BF16),
                                     ht_ref[g, :, wcols].astype(BF16)))
        y = y + y_off * jnp.exp(la_cs)
        b_t = bm.T.astype(BF16)
        upd = xdt * jnp.exp(a_end - la_cs)
        for g in range(gb):
            own = (row_id >= g * ch) & (row_id < (g + 1) * ch)
            ht_ref[g, :, wcols] = (ht_ref[g, :, wcols] * jnp.exp(a_end_rows[g])
                                   + _dot(b_t, jnp.where(own, upd, 0.0).astype(BF16)))

        y = (y + d_ref[q] * xs) * _silu(z)
        y = y * lax.rsqrt(jnp.mean(y * y, axis=-1, keepdims=True) + EPS) * ng_ref[q]
        for g in range(gb):
            y_ref[g, rows, wcols] = y[g * ch:(g + 1) * ch].astype(y_ref.dtype)

    def step(ci, carry):
        t0 = pl.multiple_of(ci * ch, ch)
        rows = pl.ds(t0, ch)
        xs = stack(lambda g: conv(x_ref, wx_ref, cwx_ref, cbx_ref, g, t0))
        bm = stack(lambda g: conv(b_ref, wb_ref, cwb_ref, cbb_ref, g, t0))
        cm = stack(lambda g: conv(c_ref, wc_ref, cwc_ref, cbc_ref, g, t0))
        dt_raw = stack(lambda g: dt_ref[g, rows, :])
        z = stack(lambda g: z_ref[g, rows, :])
        for q in range(n_grp):
            wcols = slice(q * w, (q + 1) * w)
            ncols = slice(q * LANES, (q + 1) * LANES)
            group(q, rows, xs[:, wcols], bm[:, ncols], cm[:, ncols], dt_raw, z[:, wcols])
        return carry

    @pl.when(tb == 0)
    def _():
        for g in range(gb):
            for win_ref, c0_ref in ((wx_ref, cx0_ref), (wb_ref, cb0_ref), (wc_ref, cc0_ref)):
                win_ref[g, pl.ds(0, SUBLANES), :] = jnp.zeros((SUBLANES, win_ref.shape[2]), F32)
                if has_init:
                    win_ref[g, pl.ds(SUBLANES - tail, tail), :] = c0_ref[g]
            for p in range(n_grp * pb):
                cols = slice(p * LANES, (p + 1) * LANES)
                if has_init:
                    ht_ref[g, :, cols] = h0_ref[g, p].T
                else:
                    ht_ref[g, :, cols] = jnp.zeros((LANES, LANES), F32)

    if nc == 1:
        step(0, 0)
    else:
        lax.fori_loop(0, nc, step, 0, unroll=2)

    @pl.when(tb == pl.num_programs(1) - 1)
    def _():
        for g in range(gb):
            for p in range(n_grp * pb):
                hout_ref[g, p] = ht_ref[g, :, p * LANES:(p + 1) * LANES].T


def _group_rows(v):
    return jnp.repeat(v.astype(F32), SSD_HEAD_DIM).reshape(SSD_GROUPS, 1, -1)


def _head_select(n_heads):
    import numpy as np
    w = n_heads // SSD_GROUPS * SSD_HEAD_DIM
    h = np.arange(LANES)[None, :, None]
    lane = np.arange(w)[None, None, :]
    g = np.arange(SSD_GROUPS)[:, None, None]
    return jnp.asarray((h == g * (n_heads // SSD_GROUPS) + lane // SSD_HEAD_DIM).astype(np.float32),
                       dtype=BF16)


def _block_tri(gb, ch):
    import numpy as np
    t = np.arange(gb * ch)[:, None]
    s = np.arange(gb * ch)[None, :]
    return ((t // ch == s // ch) & (s <= t)).astype(np.float32)


def _ssd_pairs(proj3, dt3, conv_w, conv_b, dt_bias, a_log, d_skip, norm_g, h0, conv0, *, mix_a,
               mix_b, ch, valid, gb, tt):
    bsz, t_len, _ = proj3.shape
    n_pairs = mix_b // LANES
    pb = n_pairs // SSD_GROUPS
    w = pb * LANES
    nc = tt // ch
    n_rows = gb * ch
    has_init = h0 is not None
    wn = SSD_GROUPS * LANES
    assert (4 * mix_a) % mix_b == 0 and (4 * mix_a + 2 * mix_b) % wn == 0 and n_rows == LANES
    z0 = 4 * mix_a // mix_b
    x0 = z0 + 1
    b0 = (4 * mix_a + 2 * mix_b) // wn
    c0 = b0 + 1

    seq = lambda blk: (lambda b, t: (b, t, blk))
    par = lambda blk: (lambda b, t: (0, blk))
    const2 = lambda b, t: (0, 0)
    const3 = lambda b, t: (0, 0, 0)

    in_specs = [
        pl.BlockSpec((gb, tt, mix_b), seq(z0)),
        pl.BlockSpec((gb, tt, mix_b), seq(x0)),
        pl.BlockSpec((gb, tt, wn), seq(b0)),
        pl.BlockSpec((gb, tt, wn), seq(c0)),
        pl.BlockSpec((gb, tt, LANES), seq(0)),
        pl.BlockSpec((CONV_WIDTH, mix_b), par(0)),
        pl.BlockSpec((CONV_WIDTH, wn), par(mix_b // wn)),
        pl.BlockSpec((CONV_WIDTH, wn), par(mix_b // wn + 1)),
        pl.BlockSpec((1, mix_b), par(0)),
        pl.BlockSpec((1, wn), par(mix_b // wn)),
        pl.BlockSpec((1, wn), par(mix_b // wn + 1)),
        pl.BlockSpec((SSD_GROUPS, LANES, w), const3),
        pl.BlockSpec((SSD_GROUPS, 1, w), const3),
        pl.BlockSpec((SSD_GROUPS, 1, w), const3),
        pl.BlockSpec((SSD_GROUPS, 1, w), const3),
        pl.BlockSpec((SSD_GROUPS, 1, w), const3),
        pl.BlockSpec((n_rows, n_rows), const2),
        pl.BlockSpec((n_rows, n_rows), const2),
    ]
    conv_b2 = conv_b.reshape(1, -1)
    block_tri = _block_tri(gb, ch)
    args = [proj3, proj3, proj3, proj3, dt3, conv_w, conv_w, conv_w, conv_b2, conv_b2, conv_b2,
            _head_select(dt_bias.shape[0]), _group_rows(dt_bias), _group_rows(a_log),
            _group_rows(d_skip), norm_g.astype(F32).reshape(SSD_GROUPS, 1, w),
            jnp.asarray(block_tri, dtype=BF16), jnp.asarray(block_tri)]
    if has_init:
        tail = CONV_WIDTH - 1
        in_specs += [
            pl.BlockSpec((gb, n_pairs, LANES, LANES), lambda b, t: (b, 0, 0, 0)),
            pl.BlockSpec((gb, tail, mix_b), lambda b, t: (b, 0, 0)),
            pl.BlockSpec((gb, tail, wn), lambda b, t: (b, 0, mix_b // wn)),
            pl.BlockSpec((gb, tail, wn), lambda b, t: (b, 0, mix_b // wn + 1)),
        ]
        args += [h0, conv0, conv0, conv0]
    kern = functools.partial(_ssd_kernel, ch=ch, valid=valid, nc=nc, gb=gb, pb=pb,
                             has_init=has_init)
    return pl.pallas_call(
        kern,
        grid=(bsz // gb, t_len // tt),
        in_specs=in_specs,
        out_specs=[
            pl.BlockSpec((gb, tt, mix_b), lambda b, t: (b, t, 0)),
            pl.BlockSpec((gb, n_pairs, LANES, LANES), lambda b, t: (b, 0, 0, 0)),
        ],
        out_shape=[
            jax.ShapeDtypeStruct((bsz, t_len, mix_b), BF16),
            jax.ShapeDtypeStruct((bsz, n_pairs, LANES, LANES), F32),
        ],
        scratch_shapes=[pltpu.VMEM((gb, LANES, mix_b), F32),
                        pltpu.VMEM((gb, ch + SUBLANES, mix_b), F32),
                        pltpu.VMEM((gb, ch + SUBLANES, wn), F32),
                        pltpu.VMEM((gb, ch + SUBLANES, wn), F32)],
        compiler_params=_cparams(("arbitrary", "arbitrary")),
        name="ssd_pairs",
    )(*args)


OUT_TM = 512
WEIGHT_STAGE_ROWS = 512
ROUTE_PARTS = 4
GROUP_LANE0 = N_EXPERTS
NEG_BIG = -1e30
NO_LANE = 4 * LANES


def _first_lane_of(mask, lane):
    return jnp.min(jnp.where(mask, lane, float(NO_LANE)), axis=1, keepdims=True)


def _route(logits):
    lane_i = lax.broadcasted_iota(I32, logits.shape, 1)
    lane = lane_i.astype(F32)
    is_grp = (lane_i >= GROUP_LANE0) & (lane_i < GROUP_LANE0 + N_EXPERT_GROUPS)
    lg = jnp.where(is_grp, logits, NEG_BIG)
    g_max = jnp.max(lg, axis=1, keepdims=True)
    g_sum = jnp.sum(jnp.where(is_grp, jnp.exp(lg - g_max), 0.0), axis=1, keepdims=True)
    g_idx = _first_lane_of(lg == g_max, lane) - GROUP_LANE0
    gp_top = 1.0 / g_sum
    lane_grp = jnp.right_shift(lane_i, 3).astype(F32)
    in_grp = (lane_i < N_EXPERTS) & (lane_grp == g_idx)
    le = jnp.where(in_grp, logits, NEG_BIG)
    e_max = jnp.max(le, axis=1, keepdims=True)
    e_exp = jnp.where(in_grp, jnp.exp(le - e_max), 0.0)
    ep = e_exp / jnp.sum(e_exp, axis=1, keepdims=True)
    ep = jnp.where(in_grp, ep, -1.0)
    p1 = jnp.max(ep, axis=1, keepdims=True)
    i1 = _first_lane_of(ep == p1, lane)
    ep2 = jnp.where(lane == i1, -1.0, ep)
    p2 = jnp.max(ep2, axis=1, keepdims=True)
    i2 = _first_lane_of(ep2 == p2, lane)
    den = p1 + p2
    return i1, i2, gp_top * p1 / den, gp_top * p2 / den


def _outproj_kernel(o_ref, y_ref, x_ref, g1_ref, sc_ref, sh_ref, ng_ref, w_hbm_ref, wr_ref, br_ref,
                    tri_ref, cnt0_ref,
                    x1_ref, hn_ref, ids_ref, wts_ref, rank_ref, cnt_ref,
                    wbf_ref, stage_ref, sem, cnts_ref):
    i = pl.program_id(0)
    tm, d = hn_ref.shape
    mix_a = o_ref.shape[1]
    tn = PROJ_TN
    n_col = d // tn
    gshape = (x_ref.shape[0], x_ref.shape[1], tn)

    @pl.when(i == 0)
    def _():
        cnts_ref[...] = cnt0_ref[...]
        rows = stage_ref.shape[0]
        for k in range(d // rows):
            cp = pltpu.make_async_copy(w_hbm_ref.at[pl.ds(k * rows, rows), :], stage_ref, sem)
            cp.start()
            cp.wait()
            wbf_ref[k * rows:(k + 1) * rows, :] = stage_ref[...].astype(BF16)

    o = o_ref[...]
    y = y_ref[...]
    for c in range(n_col):
        cols = slice(c * tn, (c + 1) * tn)
        acc = _dot(o, wbf_ref[0:mix_a, cols]) + _dot(y, wbf_ref[mix_a:d, cols])
        x1_ref[:, :, cols] = x_ref[:, :, cols] + g1_ref[:, :, cols] * acc.reshape(gshape)

    g_blk, r_blk = x_ref.shape[0], x_ref.shape[1]
    pr = tm // ROUTE_PARTS
    lane = lax.broadcasted_iota(I32, (pr, LANES), 1)
    onehots, hits = [], []
    for h in range(ROUTE_PARTS):
        rows = slice(h * pr, (h + 1) * pr)
        if g_blk == 1:
            gsl = (slice(None), rows)
        else:
            gsl = (slice(h * g_blk // ROUTE_PARTS, (h + 1) * g_blk // ROUTE_PARTS), slice(None))
        pshape = (g_blk if g_blk == 1 else g_blk // ROUTE_PARTS,
                  pr if g_blk == 1 else r_blk, tn)
        x1 = [x1_ref[gsl + (slice(c * tn, (c + 1) * tn),)].reshape(pr, tn) for c in range(n_col)]
        ssq = sum(jnp.sum(xc * xc, axis=-1, keepdims=True) for xc in x1)
        rs = lax.rsqrt(ssq / d + EPS)
        logits = jnp.zeros((pr, LANES), F32) + br_ref[...]
        for c in range(n_col):
            cols = slice(c * tn, (c + 1) * tn)
            sc = sc_ref[gsl[0], :, cols] if g_blk > 1 else sc_ref[:, :, cols]
            sh = sh_ref[gsl[0], :, cols] if g_blk > 1 else sh_ref[:, :, cols]
            hn = (x1[c] * rs * ng_ref[:, cols]).reshape(pshape)
            hn = (hn * (1.0 + sc) + sh).reshape(pr, tn)
            hn_ref[rows, cols] = hn
            logits = logits + _dot_hp(hn, wr_ref[cols, :])
        i1, i2, w1, w2 = _route(logits)
        hit1 = lane.astype(F32) == i1
        hit2 = lane.astype(F32) == i2
        onehots.append(jnp.where(hit1 | hit2, 1.0, 0.0).astype(BF16))
        hits.append((hit1, hit2))
        ids_ref[rows, :] = jnp.where(lane == 0, i1, jnp.where(lane == 1, i2, 0.0)).astype(I32)
        wts_ref[rows, :] = jnp.where(lane == 0, w1, jnp.where(lane == 1, w2, 0.0))

    total = cnts_ref[0:1, :]
    for h in range(ROUTE_PARTS):
        rows = slice(h * pr, (h + 1) * pr)
        hit1, hit2 = hits[h]
        before = _dot(tri_ref[...], onehots[h]) + total
        r1 = jnp.sum(jnp.where(hit1, before, 0.0), axis=1, keepdims=True)
        r2 = jnp.sum(jnp.where(hit2, before, 0.0), axis=1, keepdims=True)
        rank_ref[rows, :] = jnp.where(lane == 0, r1, jnp.where(lane == 1, r2, 0.0)).astype(I32)
        total = total + jnp.sum(onehots[h].astype(F32), axis=0, keepdims=True)
    cnts_ref[...] = jnp.broadcast_to(total, cnts_ref.shape)
    cnt_ref[...] = jnp.broadcast_to(total, cnt_ref.shape)


def _strict_lower(n):
    import numpy as np
    t = np.arange(n)[:, None]
    s = np.arange(n)[None, :]
    return jnp.asarray((s < t).astype(np.float32), dtype=BF16)


def _out_projection(o3, y3, x3, mod3, norm_g, w_out, w_router, b_router, cnt0, group_rows):
    bsz, t_len, d = x3.shape
    g_blk, r_blk = group_rows
    tm = g_blk * r_blk
    mix_a = o3.shape[-1]
    mix_b = y3.shape[-1]
    n_tok = bsz * t_len
    tiles_per_seq = max(t_len // r_blk, 1)

    xmap = lambda i: (i // tiles_per_seq, i % tiles_per_seq, 0)
    modmap = lambda s: (lambda i: (i // tiles_per_seq, 0, s))
    row = lambda i: (i, 0)
    const2 = lambda i: (0, 0)
    return pl.pallas_call(
        _outproj_kernel,
        grid=(n_tok // tm,),
        in_specs=[
            pl.BlockSpec((tm, mix_a), row),
            pl.BlockSpec((tm, mix_b), row),
            pl.BlockSpec((g_blk, r_blk, d), xmap),
            pl.BlockSpec((g_blk, 1, d), modmap(2)),
            pl.BlockSpec((g_blk, 1, d), modmap(4)),
            pl.BlockSpec((g_blk, 1, d), modmap(3)),
            pl.BlockSpec((1, d), const2),
            pl.BlockSpec(memory_space=pl.ANY),
            pl.BlockSpec((d, LANES), const2),
            pl.BlockSpec((1, LANES), const2),
            pl.BlockSpec((tm // ROUTE_PARTS, tm // ROUTE_PARTS), const2),
            pl.BlockSpec((SUBLANES, LANES), const2),
        ],
        out_specs=[
            pl.BlockSpec((g_blk, r_blk, d), xmap),
            pl.BlockSpec((tm, d), row),
            pl.BlockSpec((tm, LANES), row),
            pl.BlockSpec((tm, LANES), row),
            pl.BlockSpec((tm, LANES), row),
            pl.BlockSpec((SUBLANES, LANES), const2),
        ],
        out_shape=[
            jax.ShapeDtypeStruct((bsz, t_len, d), F32),
            jax.ShapeDtypeStruct((n_tok, d), F32),
            jax.ShapeDtypeStruct((n_tok, LANES), I32),
            jax.ShapeDtypeStruct((n_tok, LANES), F32),
            jax.ShapeDtypeStruct((n_tok, LANES), I32),
            jax.ShapeDtypeStruct((SUBLANES, LANES), F32),
        ],
        scratch_shapes=[pltpu.VMEM((d, d), BF16), pltpu.VMEM((WEIGHT_STAGE_ROWS, d), F32),
                        pltpu.SemaphoreType.DMA(()), pltpu.VMEM((SUBLANES, LANES), F32)],
        compiler_params=_cparams(("arbitrary",)),
        name="out_proj_router",
    )(o3.reshape(n_tok, mix_a), y3.reshape(n_tok, mix_b), x3, mod3, mod3, mod3,
      norm_g.reshape(1, d), w_out, w_router, b_router, _strict_lower(tm // ROUTE_PARTS), cnt0)


def _positions_kernel(ids_ref, rank_ref, cnt_ref, pos_ref, te_ref, nt_ref, lt_ref, *, block):
    lane8 = lax.broadcasted_iota(I32, (SUBLANES, LANES), 1)
    cnt = cnt_ref[...]
    tiles = jnp.floor((cnt + (MOE_TM - 1)) * (1.0 / MOE_TM))
    tiles = jnp.where(lane8 < N_EXPERTS, tiles, 0.0)
    incl = tiles
    shift = 1
    while shift < N_EXPERTS:
        incl = incl + jnp.where(lane8 >= shift, pltpu.roll(incl, shift, axis=1), 0.0)
        shift *= 2
    offs = ((incl - tiles) * MOE_TM)[0:1, :]
    nt_ref[...] = jnp.sum(tiles, axis=1, keepdims=True).astype(I32) + jnp.zeros(nt_ref.shape, I32)
    lt_ref[...] = jnp.where(tiles > 0.0, incl - 1.0, -1.0).astype(I32)

    incl_col = jnp.broadcast_to(incl[0:1, :], (LANES, LANES)).T
    e_row = lax.broadcasted_iota(I32, (LANES, LANES), 0)
    i_lane = lax.broadcasted_iota(I32, (LANES, LANES), 1).astype(F32)
    done = jnp.where((incl_col <= i_lane) & (e_row < N_EXPERTS), 1.0, 0.0)
    te = jnp.minimum(jnp.sum(done, axis=0, keepdims=True), N_EXPERTS - 1.0)
    te_ref[...] = te.astype(I32) + jnp.zeros(te_ref.shape, I32)

    lane = lax.broadcasted_iota(I32, (block, LANES), 1)

    def body(bi, c):
        r0 = pl.multiple_of(bi * block, block)
        ids = ids_ref[pl.ds(r0, block), :]
        rank = rank_ref[pl.ds(r0, block), :]
        out = jnp.zeros((block, LANES), I32)
        for k in range(TOP_K):
            off = jnp.sum(jnp.where(lane == ids[:, k:k + 1], offs, 0.0), axis=1, keepdims=True)
            out = jnp.where(lane == k, off.astype(I32) + rank[:, k:k + 1], out)
        pos_ref[pl.ds(r0, block), :] = out
        return c

    lax.fori_loop(0, ids_ref.shape[0] // block, body, 0)


def _positions(ids, rank, cnt):
    n_tok = ids.shape[0]
    block = 512
    kern = functools.partial(_positions_kernel, block=block)
    full = lambda s: pl.BlockSpec(s, lambda: tuple(0 for _ in s))
    return pl.pallas_call(
        kern,
        in_specs=[full((n_tok, LANES)), full((n_tok, LANES)), full((SUBLANES, LANES))],
        out_specs=[full((n_tok, LANES))] + [full((SUBLANES, LANES))] * 3,
        out_shape=[jax.ShapeDtypeStruct((n_tok, LANES), I32)]
        + [jax.ShapeDtypeStruct((SUBLANES, LANES), I32)] * 3,
        compiler_params=pltpu.CompilerParams(vmem_limit_bytes=VMEM_LIMIT),
        name="route_positions",
    )(ids, rank, cnt)


def _row_copy(src_ref, src_row, dst_ref, dst_row, sem):
    return pltpu.make_async_copy(src_ref.at[pl.ds(src_row, 1), :], dst_ref.at[pl.ds(dst_row, 1), :],
                                 sem)


def _dispatch_kernel(pos_ref, lt_ref, nt_ref, *refs, steps, n_tiles):
    n_groups = len(steps)
    hn_refs = refs[:n_groups]
    xg_ref, sem, zero_ref, zsem = refs[n_groups:]
    rows = hn_refs[0].shape[0]
    i = pl.program_id(0)

    def zero_copy(tile):
        return pltpu.make_async_copy(zero_ref, xg_ref.at[pl.ds(tile * MOE_TM, MOE_TM), :], zsem)

    @pl.when(i == 0)
    def _():
        zero_ref[...] = jnp.zeros(zero_ref.shape, zero_ref.dtype)
        nt = nt_ref[0]
        for e in range(N_EXPERTS):
            @pl.when(lt_ref[e] >= 0)
            def _():
                zero_copy(jnp.maximum(lt_ref[e], 0)).start()
        lax.fori_loop(nt, n_tiles, lambda t, c: (zero_copy(t).start(), c)[1], 0)
        for e in range(N_EXPERTS):
            @pl.when(lt_ref[e] >= 0)
            def _():
                zero_copy(0).wait()
        lax.fori_loop(nt, n_tiles, lambda t, c: (zero_copy(0).wait(), c)[1], 0)

    first = 0
    for hn_ref, n_steps in zip(hn_refs, steps):
        @pl.when((i >= first) & (i < first + n_steps))
        def _(hn_ref=hn_ref):
            base = i * rows

            def issue(r, c):
                for k in range(TOP_K):
                    _row_copy(hn_ref, r, xg_ref, pos_ref[(base + r) * TOP_K + k], sem).start(
                        priority=k)
                return c

            def drain(r, c):
                for k in range(TOP_K):
                    _row_copy(hn_ref, r, xg_ref, 0, sem).wait()
                return c

            lax.fori_loop(0, rows, issue, 0, unroll=8)
            lax.fori_loop(0, rows, drain, 0, unroll=8)

        first += n_steps


def _dispatch(pos_flat, last_tile, n_live, hns, n_tiles):
    d = hns[0].shape[1]
    steps = tuple(h.shape[0] // ROW_DMA_TILE for h in hns)
    firsts = [sum(steps[:g]) for g in range(len(steps))]

    def group_map(g):
        return lambda i, pos, lt, nt: (jnp.clip(i - firsts[g], 0, steps[g] - 1), 0)

    return pl.pallas_call(
        functools.partial(_dispatch_kernel, steps=steps, n_tiles=n_tiles),
        grid_spec=pltpu.PrefetchScalarGridSpec(
            num_scalar_prefetch=3,
            grid=(sum(steps),),
            in_specs=[pl.BlockSpec((ROW_DMA_TILE, d), group_map(g)) for g in range(len(hns))],
            out_specs=pl.BlockSpec(memory_space=pl.ANY),
            scratch_shapes=[pltpu.SemaphoreType.DMA(()), pltpu.VMEM((MOE_TM, d), F32),
                            pltpu.SemaphoreType.DMA(())],
        ),
        out_shape=jax.ShapeDtypeStruct((n_tiles * MOE_TM, d), F32),
        compiler_params=_cparams(("arbitrary",)),
        name="moe_dispatch",
    )(pos_flat, last_tile, n_live, *hns)


WEIGHT_PARTS = 4


def _weight_copies(w_hbm_refs, expert, wbuf_ref, slot, sem_ref):
    copies = []
    for i, w_ref in enumerate(w_hbm_refs):
        part_rows = w_ref.shape[1] // WEIGHT_PARTS
        for part in range(WEIGHT_PARTS):
            rows = pl.ds(part * part_rows, part_rows)
            copies.append(pltpu.make_async_copy(w_ref.at[expert, rows], wbuf_ref.at[slot, i, rows],
                                                sem_ref.at[slot]))
    return copies


def _start_weight_copies(copies, queue0_parts):
    for k, cp in enumerate(copies):
        cp.start(priority=0 if k % WEIGHT_PARTS < queue0_parts else 1)


def _expert_weights_slot(te_ref, nt, t, w_hbm_refs, wbuf_ref, sem_ref, slot_ref, queue0_parts):
    expert = te_ref[t]

    @pl.when(t == 0)
    def _():
        slot_ref[0] = 0
        _start_weight_copies(_weight_copies(w_hbm_refs, expert, wbuf_ref, 0, sem_ref),
                             queue0_parts)

    @pl.when((t == 0) | (expert != te_ref[jnp.maximum(t - 1, 0)]))
    def _():
        @pl.when(t > 0)
        def _():
            slot_ref[0] = 1 - slot_ref[0]

        slot = slot_ref[0]
        for cp in _weight_copies(w_hbm_refs, expert, wbuf_ref, slot, sem_ref):
            cp.wait()
        nxt = lax.while_loop(lambda u: (u < nt) & (te_ref[jnp.minimum(u, nt - 1)] == expert),
                             lambda u: u + 1, t + 1)

        @pl.when(nxt < nt)
        def _():
            _start_weight_copies(_weight_copies(w_hbm_refs, te_ref[jnp.minimum(nxt, nt - 1)],
                                                wbuf_ref, 1 - slot, sem_ref), queue0_parts)

    return slot_ref[0]


def _moe_up_kernel(te_ref, nt_ref, x_ref, w1_ref, w3_ref, act_ref, wbuf_ref, sem_ref, slot_ref):
    t = pl.program_id(0)
    nt = nt_ref[0]

    @pl.when(t < nt)
    def _():
        slot = _expert_weights_slot(te_ref, nt, t, (w1_ref, w3_ref), wbuf_ref, sem_ref, slot_ref,
                                    queue0_parts=1)
        xb = x_ref[...].astype(BF16)
        a = _dot(xb, wbuf_ref[slot, 0].astype(BF16))
        b = _dot(xb, wbuf_ref[slot, 1].astype(BF16))
        act_ref[...] = (_silu(a) * b).astype(BF16)

    @pl.when(t >= nt)
    def _():
        act_ref[...] = jnp.zeros(act_ref.shape, act_ref.dtype)


def _moe_down_kernel(te_ref, nt_ref, act_ref, w2_ref, out_ref, wbuf_ref, sem_ref, slot_ref):
    t = pl.program_id(0)
    nt = nt_ref[0]

    @pl.when(t < nt)
    def _():
        slot = _expert_weights_slot(te_ref, nt, t, (w2_ref,), wbuf_ref, sem_ref, slot_ref,
                                    queue0_parts=0)
        out_ref[...] = _dot(act_ref[...], wbuf_ref[slot, 0].astype(BF16))

    @pl.when(t >= nt)
    def _():
        out_ref[...] = jnp.zeros(out_ref.shape, out_ref.dtype)


def _moe_experts(te, nt, xg, w1, w3, w2):
    n_rows, d = xg.shape
    d_exp = w1.shape[-1]
    n_tiles = n_rows // MOE_TM
    live_tile = lambda t, te_r, nt_r: (jnp.minimum(t, nt_r[0] - 1), 0)
    any_spec = pl.BlockSpec(memory_space=pl.ANY)

    act = pl.pallas_call(
        _moe_up_kernel,
        grid_spec=pltpu.PrefetchScalarGridSpec(
            num_scalar_prefetch=2,
            grid=(n_tiles,),
            in_specs=[pl.BlockSpec((MOE_TM, d), live_tile), any_spec, any_spec],
            out_specs=pl.BlockSpec((MOE_TM, d_exp), lambda t, te_r, nt_r: (t, 0)),
            scratch_shapes=[pltpu.VMEM((2, 2, d, d_exp), F32), pltpu.SemaphoreType.DMA((2,)),
                            pltpu.SMEM((1,), I32)],
        ),
        out_shape=jax.ShapeDtypeStruct((n_rows, d_exp), BF16),
        compiler_params=_cparams(("arbitrary",)),
        name="moe_up",
    )(te, nt, xg, w1, w3)

    return pl.pallas_call(
        _moe_down_kernel,
        grid_spec=pltpu.PrefetchScalarGridSpec(
            num_scalar_prefetch=2,
            grid=(n_tiles,),
            in_specs=[pl.BlockSpec((MOE_TM, d_exp), live_tile), any_spec],
            out_specs=pl.BlockSpec((MOE_TM, d), lambda t, te_r, nt_r: (t, 0)),
            scratch_shapes=[pltpu.VMEM((2, 1, d_exp, d), F32), pltpu.SemaphoreType.DMA((2,)),
                            pltpu.SMEM((1,), I32)],
        ),
        out_shape=jax.ShapeDtypeStruct((n_rows, d), F32),
        compiler_params=_cparams(("arbitrary",)),
        name="moe_down",
    )(te, nt, act, w2)


def _combine_kernel(pos_ref, x1_ref, g2_ref, wts_ref, fg_ref, eo_ref, y_ref, buf_ref, sem):
    rows = wts_ref.shape[0]
    i = pl.program_id(0)
    n = pl.num_programs(0)
    slot = i % 2
    g_blk = x1_ref.shape[0]

    def issue(tile, to_slot, r):
        for k in range(TOP_K):
            _row_copy(eo_ref, pos_ref[(tile * rows + r) * TOP_K + k], buf_ref.at[to_slot, k], r,
                      sem.at[to_slot]).start(priority=k)

    def issue_chunk(tile, to_slot, c):
        for u in range(SUBLANES):
            issue(tile, to_slot, c * SUBLANES + u)

    def compute_chunk(c):
        r0 = pl.multiple_of(c * SUBLANES, SUBLANES)
        rs = pl.ds(r0, SUBLANES)
        w = wts_ref[rs, :]
        moe = w[:, 0:1] * buf_ref[slot, 0, rs, :] + w[:, 1:2] * buf_ref[slot, 1, rs, :]
        if g_blk == 1:
            x2 = x1_ref[0, rs, :] + g2_ref[0] * moe
        else:
            x2 = x1_ref[c] + g2_ref[c] * moe
        ms = jnp.mean(x2 * x2, axis=-1, keepdims=True)
        y = x2 * lax.rsqrt(ms + EPS) * fg_ref[...]
        if g_blk == 1:
            y_ref[0, rs, :] = y
        else:
            y_ref[c] = y

    n_chunks = rows // SUBLANES

    @pl.when(i == 0)
    def _():
        lax.fori_loop(0, n_chunks, lambda c, z: (issue_chunk(0, 0, c), z)[1], 0)

    def drain(r, c):
        for k in range(TOP_K):
            _row_copy(eo_ref, 0, buf_ref.at[slot, k], r, sem.at[slot]).wait()
        return c

    lax.fori_loop(0, rows, drain, 0, unroll=8)

    def with_prefetch():
        def body(c, z):
            issue_chunk(i + 1, 1 - slot, c)
            compute_chunk(c)
            return z
        lax.fori_loop(0, n_chunks, body, 0)

    def last_tile():
        lax.fori_loop(0, n_chunks, lambda c, z: (compute_chunk(c), z)[1], 0)

    lax.cond(i + 1 < n, with_prefetch, last_tile)


def _combine(pos_flat, x1, mod3, wts, final_g, expert_out, group_rows):
    bsz, t_len, d = x1.shape
    g_blk, r_blk = group_rows
    tm = g_blk * r_blk
    tiles_per_seq = max(t_len // r_blk, 1)

    def xmap(i, pos):
        return (i // tiles_per_seq, i % tiles_per_seq, 0)

    return pl.pallas_call(
        _combine_kernel,
        grid_spec=pltpu.PrefetchScalarGridSpec(
            num_scalar_prefetch=1,
            grid=((bsz * t_len) // tm,),
            in_specs=[
                pl.BlockSpec((g_blk, r_blk, d), xmap),
                pl.BlockSpec((g_blk, 1, d), lambda i, pos: (i // tiles_per_seq, 0, 5)),
                pl.BlockSpec((tm, LANES), lambda i, pos: (i, 0)),
                pl.BlockSpec((1, d), lambda i, pos: (0, 0)),
                pl.BlockSpec(memory_space=pl.ANY),
            ],
            out_specs=pl.BlockSpec((g_blk, r_blk, d), xmap),
            scratch_shapes=[pltpu.VMEM((2, TOP_K, tm, d), F32), pltpu.SemaphoreType.DMA((2,))],
        ),
        out_shape=jax.ShapeDtypeStruct((bsz, t_len, d), F32),
        compiler_params=_cparams(("arbitrary",)),
        name="moe_combine",
    )(pos_flat, x1, mod3, wts, final_g.reshape(1, d), expert_out)


def kernel(x_prompt, x_sample, state_hgrn, state_ssm, state_conv, c_prompt, c_sample, ada_w, ada_b,
           norm1_g, norm2_g, w_in, hgrn_lb, hgrn_onorm_g, conv_w, conv_b, dt_bias, a_log, d_skip,
           ssm_norm_g, w_out, w_grp, b_grp, w_rt, b_rt, w1, w3, w2, final_g):
    depth = w_in.shape[0]
    assert depth == 1, "single-layer trunk"
    layer = 0
    bp, t_p, d = x_prompt.shape
    bs, t_s, _ = x_sample.shape
    mix_a = hgrn_onorm_g.shape[1]
    mix_b = ssm_norm_g.shape[1]
    conv_dim = conv_w.shape[2]
    n_main = 4 * mix_a + mix_b + conv_dim
    n_ssd_heads = dt_bias.shape[1]
    n_pairs = mix_b // LANES
    xbc0 = 4 * mix_a + mix_b
    tail = CONV_WIDTH - 1
    assert t_s <= SUBLANES and t_s >= tail and t_p % PROMPT_CHUNK == 0

    xs_pad = jnp.pad(x_sample, ((0, 0), (0, SUBLANES - t_s), (0, 0)))
    n_c = bp + bs
    c_rows = -(-n_c // SUBLANES) * SUBLANES
    c_all = jnp.pad(jnp.concatenate([c_prompt, c_sample], axis=0), ((0, c_rows - n_c), (0, 0)))

    mod = _modulation(c_all, ada_w[layer], ada_b[layer])
    mod_p = mod[:bp].reshape(bp, 1, -1)
    mod_s = mod[bp:n_c].reshape(bs, 1, -1)
    w_in_t = jnp.swapaxes(w_in[layer], 0, 1)
    w_dt_t = jnp.pad(w_in_t[n_main:], ((0, LANES - n_ssd_heads), (0, 0)))
    w_router = jnp.pad(jnp.concatenate([w_rt[layer], w_grp[layer]], axis=1),
                       ((0, 0), (0, LANES - N_EXPERTS - N_EXPERT_GROUPS)))
    b_router = jnp.pad(jnp.concatenate([b_rt[layer], b_grp[layer]]),
                       (0, LANES - N_EXPERTS - N_EXPERT_GROUPS)).reshape(1, LANES)

    groups = (
        (x_prompt, mod_p, None, None, None, t_p, PROMPT_CHUNK, PROMPT_CHUNK, MIXER_ROWS,
         (LANES // PROMPT_CHUNK, 8), LANES // PROMPT_CHUNK, 1, (1, OUT_TM)),
        (xs_pad, mod_s, state_hgrn[layer], state_ssm[layer].reshape(bs, n_pairs, LANES, LANES),
         state_conv[layer], t_s, SUBLANES, t_s, SUBLANES,
         (LANES // SUBLANES, 2), LANES // SUBLANES, PROJ_SEQS, (OUT_TM // SUBLANES, SUBLANES)),
    )

    cnt = jnp.zeros((SUBLANES, LANES), F32)
    per_group = []
    for x3, mod3, s0, h0, c0, t_real, ch, valid, tt, (hg_gb, hg_hb), ssd_gb, proj_gr, out_gr in groups:
        bsz, t_len, _ = x3.shape
        proj, dt = _in_projection(x3, mod3, norm1_g[layer], w_in_t, w_dt_t, n_main, proj_gr)
        proj3 = proj.reshape(bsz, t_len, n_main)
        dt3 = dt.reshape(bsz, t_len, LANES)
        o3, s_new = _hgrn_heads(proj3, hgrn_lb, hgrn_onorm_g[layer], s0, ch=ch, valid=valid,
                                gb=hg_gb, hb=hg_hb, tt=tt, layer=layer)
        y3, h_new = _ssd_pairs(proj3, dt3, conv_w[layer], conv_b[layer], dt_bias[layer],
                               a_log[layer], d_skip[layer], ssm_norm_g[layer], h0, c0, mix_a=mix_a,
                               mix_b=mix_b, ch=ch, valid=valid, gb=ssd_gb, tt=tt)
        conv_new = proj3[:, t_real - tail:t_real, xbc0:xbc0 + conv_dim]
        x1, hn, ids, wts, rank, cnt = _out_projection(
            o3, y3, x3, mod3, norm2_g[layer], w_out[layer], w_router, b_router, cnt, out_gr)
        per_group.append(dict(x1=x1, hn=hn, ids=ids, wts=wts, rank=rank, mod3=mod3, out_gr=out_gr,
                              s_new=s_new, h_new=h_new, conv_new=conv_new))

    ids_all = jnp.concatenate([g["ids"] for g in per_group], axis=0)
    rank_all = jnp.concatenate([g["rank"] for g in per_group], axis=0)
    n_tok = ids_all.shape[0]
    pos, te, nt, lt = _positions(ids_all, rank_all, cnt)
    pos_flat = pos[:, :TOP_K].reshape(-1)
    n_tiles = -(-(n_tok * TOP_K + N_EXPERTS * (MOE_TM - 1)) // MOE_TM)
    assert n_tiles <= LANES
    te_vec = te[0, :n_tiles]
    nt_vec = nt[0, :1]
    lt_vec = lt[0, :N_EXPERTS]

    row0 = 0
    for g in per_group:
        rows = g["hn"].shape[0]
        g["pos"] = pos_flat[row0 * TOP_K:(row0 + rows) * TOP_K]
        row0 += rows
    xg = _dispatch(pos_flat, lt_vec, nt_vec, [g["hn"] for g in per_group], n_tiles)
    expert_out = _moe_experts(te_vec, nt_vec, xg, w1[layer], w3[layer], w2[layer])
    ys = [_combine(g["pos"], g["x1"], g["mod3"], g["wts"], final_g, expert_out, g["out_gr"])
          for g in per_group]

    gp, gs = per_group
    return (
        ys[0],
        ys[1][:, :t_s],
        gp["s_new"][None],
        gp["h_new"].reshape(1, bp, n_ssd_heads, SSD_HEAD_DIM, SSD_STATE),
        gp["conv_new"][None],
        gs["s_new"][None],
        gs["h_new"].reshape(1, bs, n_ssd_heads, SSD_HEAD_DIM, SSD_STATE),
        gs["conv_new"][None],
    )
```

```python
import functools

import jax
import jax.numpy as jnp
from jax import lax
from jax.experimental import pallas as pl
from jax.experimental.pallas import tpu as pltpu

F32 = jnp.float32
BF16 = jnp.bfloat16
I32 = jnp.int32
EPS = 1e-6
LOG2E = 1.4426950408889634
MAX_LOG2_DECAY = 12

LANES = 128
SUBLANES = 8
VMEM_LIMIT = 56 * 1024 * 1024

HGRN_HEAD_DIM = 128
SSD_HEAD_DIM = 64
SSD_GROUPS = 2
SSD_STATE = 128
CONV_WIDTH = 4
N_EXPERT_GROUPS = 4
EXPERTS_PER_GROUP = 8
N_EXPERTS = N_EXPERT_GROUPS * EXPERTS_PER_GROUP
TOP_K = 2

PROJ_SEQS = 128
NORM_ROWS = 256
PROJ_TN = 512
MOE_TM = 256
PROMPT_CHUNK = 64
MIXER_ROWS = 512
ROW_DMA_TILE = 256


def _cparams(sem):
    return pltpu.CompilerParams(dimension_semantics=sem, vmem_limit_bytes=VMEM_LIMIT)


def _dot(a, b):
    return jnp.dot(a, b, preferred_element_type=F32)


def _dot_nt(a, b):
    return lax.dot_general(a, b, (((1,), (1,)), ((), ())), preferred_element_type=F32)


def _split2(x):
    hi = x.astype(BF16)
    lo = (x - hi.astype(F32)).astype(BF16)
    return hi, lo


def _split3(x):
    hi = x.astype(BF16)
    r = x - hi.astype(F32)
    mid = r.astype(BF16)
    lo = (r - mid.astype(F32)).astype(BF16)
    return hi, mid, lo


def _dot_exact_lhs(m_bf16, x):
    hi, mid, lo = _split3(x)
    return _dot(m_bf16, hi) + _dot(m_bf16, mid) + _dot(m_bf16, lo)


def _dot_exact_rhs(x, m_bf16):
    hi, mid, lo = _split3(x)
    return _dot(hi, m_bf16) + _dot(mid, m_bf16) + _dot(lo, m_bf16)


def _dot_hp(a, w):
    ah, al = _split2(a)
    wh, wl = _split2(w)
    return _dot(ah, wh) + _dot(al, wh) + _dot(ah, wl)


def _silu(x):
    return x * jax.nn.sigmoid(x)


def _mod_kernel(c_ref, w_ref, b_ref, o_ref):
    a = _silu(c_ref[...])
    o_ref[...] = _dot_hp(a, w_ref[...]) + b_ref[...]


def _modulation(c_all, ada_w, ada_b):
    rows, d = c_all.shape
    n_out = ada_w.shape[1]
    tn = 1024
    return pl.pallas_call(
        _mod_kernel,
        grid=(n_out // tn,),
        in_specs=[
            pl.BlockSpec((rows, d), lambda j: (0, 0)),
            pl.BlockSpec((d, tn), lambda j: (0, j)),
            pl.BlockSpec((1, tn), lambda j: (0, j)),
        ],
        out_specs=pl.BlockSpec((rows, tn), lambda j: (0, j)),
        out_shape=jax.ShapeDtypeStruct((rows, n_out), F32),
        compiler_params=_cparams(("arbitrary",)),
        name="adaln_mod",
    )(c_all, ada_w, ada_b.reshape(1, n_out))


def _proj_kernel(x_hbm_ref, sc_ref, sh_ref, g_ref, w_ref, wdt_ref, o_ref, dt_ref, hn_ref, xs_ref,
                 sem):
    i = pl.program_id(0)
    j = pl.program_id(1)
    g_blk, r_blk, d = xs_ref.shape

    def x_copy(tile):
        return pltpu.make_async_copy(x_hbm_ref.at[pl.ds(tile * g_blk, g_blk)], xs_ref, sem)

    @pl.when(j == 0)
    def _():
        @pl.when(i == 0)
        def _():
            x_copy(0).start()

        x_copy(i).wait()
        wh, wl = _split2(wdt_ref[...])
        n_rows = g_blk * r_blk
        step = min(n_rows, NORM_ROWS)
        for c in range(n_rows // step):
            if g_blk == 1:
                x = xs_ref[:, c * step:(c + 1) * step, :]
                sc, sh = sc_ref[...], sh_ref[...]
            else:
                gs = slice(c * step // r_blk, (c + 1) * step // r_blk)
                x, sc, sh = xs_ref[gs], sc_ref[gs], sh_ref[gs]
            ms = jnp.mean(x * x, axis=-1, keepdims=True)
            hn = ((x * lax.rsqrt(ms + EPS)) * g_ref[...] * (1.0 + sc) + sh).reshape(step, d)
            rows = slice(c * step, (c + 1) * step)
            hn_ref[rows, :] = hn.astype(BF16)
            ah, al = _split2(hn)
            dt_ref[rows, :] = _dot_nt(ah, wh) + _dot_nt(al, wh) + _dot_nt(ah, wl)

        @pl.when(i + 1 < pl.num_programs(0))
        def _():
            x_copy(i + 1).start()

    o_ref[...] = _dot_nt(hn_ref[...], w_ref[...].astype(BF16))


def _in_projection(x3, mod3, norm_g, w_t, w_dt_t, n_main, g_blk):
    bsz, t_len, d = x3.shape
    tm = g_blk * t_len
    n_row_tiles = bsz // g_blk

    def mod_map(sec):
        return lambda i, j: (i, 0, sec)

    return pl.pallas_call(
        _proj_kernel,
        grid=(n_row_tiles, n_main // PROJ_TN),
        in_specs=[
            pl.BlockSpec(memory_space=pl.ANY),
            pl.BlockSpec((g_blk, 1, d), mod_map(1)),
            pl.BlockSpec((g_blk, 1, d), mod_map(0)),
            pl.BlockSpec((1, d), lambda i, j: (0, 0)),
            pl.BlockSpec((PROJ_TN, d), lambda i, j: (j, 0)),
            pl.BlockSpec((LANES, d), lambda i, j: (0, 0)),
        ],
        out_specs=[
            pl.BlockSpec((tm, PROJ_TN), lambda i, j: (i, j)),
            pl.BlockSpec((tm, LANES), lambda i, j: (i, 0)),
        ],
        out_shape=[
            jax.ShapeDtypeStruct((bsz * t_len, n_main), F32),
            jax.ShapeDtypeStruct((bsz * t_len, LANES), F32),
        ],
        scratch_shapes=[pltpu.VMEM((tm, d), BF16), pltpu.VMEM((g_blk, t_len, d), F32),
                        pltpu.SemaphoreType.DMA(())],
        compiler_params=_cparams(("arbitrary", "arbitrary")),
        name="in_proj",
    )(x3, mod3, mod3, norm_g.reshape(1, d), w_t, w_dt_t)


def _level_sizes(ch):
    sizes = []
    sz = SUBLANES
    while 2 * sz <= ch:
        sizes.append(sz)
        sz *= 2
    return sizes


def _level_masks(ch, n_rows):
    import numpy as np
    t = np.arange(n_rows)[:, None]
    s = np.arange(n_rows)[None, :]
    out = []
    for sz in _level_sizes(ch):
        m = (t // (2 * sz) == s // (2 * sz)) & (t % (2 * sz) >= sz) & (s % (2 * sz) < sz)
        out.append(m.astype(np.float32))
    out.append(((t // SUBLANES == s // SUBLANES) & (s <= t)).astype(np.float32))
    return jnp.asarray(np.stack(out))


def _hgrn_kernel(*refs, ch, valid, nc, gb, hb, has_init, layer):
    if has_init:
        (q_ref, f_ref, i_ref, g_ref, lb_ref, on_ref, tri_ref, masks_ref, s0_ref,
         o_ref, sout_ref, st_ref, b_scr, k_scr) = refs
    else:
        (q_ref, f_ref, i_ref, g_ref, lb_ref, on_ref, tri_ref, masks_ref,
         o_ref, sout_ref, st_ref, b_scr, k_scr) = refs
        s0_ref = None

    n_lb = lb_ref.shape[0]
    lb_rows = [lb_ref[i:i + 1, :] for i in range(n_lb)]
    lb_max = functools.reduce(jnp.maximum, lb_rows)
    lb_exp = [jnp.exp(r - lb_max) for r in lb_rows]
    lb = sum(lb_exp[:layer + 1]) / sum(lb_exp)
    onorm = on_ref[...]
    tri = tri_ref[...]
    sizes = _level_sizes(ch)
    lane8 = lax.broadcasted_iota(I32, (SUBLANES, LANES), 1)
    row8 = lax.broadcasted_iota(I32, (SUBLANES, LANES), 0)
    tb = pl.program_id(2)

    n_rows = gb * ch
    row_id = lax.broadcasted_iota(I32, (n_rows, LANES), 0)

    def stack(fn):
        parts = [fn(g) for g in range(gb)]
        return parts[0] if gb == 1 else jnp.concatenate(parts, axis=0)

    def chunk(hh, ci, bounded):
        cols = slice(hh * LANES, (hh + 1) * LANES)
        t0 = pl.multiple_of(ci * ch, ch)
        rows = pl.ds(t0, ch)
        lb_h = lb[:, cols]
        q = _silu(stack(lambda g: q_ref[g, rows, cols]))
        fg = lb_h + (1.0 - lb_h) * jax.nn.sigmoid(stack(lambda g: f_ref[g, rows, cols]))
        k = 1.0 - fg
        lf = jnp.log(fg)
        v = stack(lambda g: i_ref[g, rows, cols])
        if valid < ch:
            live = (row_id & (ch - 1)) < valid
            lf = jnp.where(live, lf, 0.0)
            k = jnp.where(live, k, 0.0)
        b2 = _dot_exact_lhs(tri, lf) * LOG2E
        b_scr[hh] = b2
        k_scr[hh] = k
        b_end_rows = [b_scr[hh, pl.ds(g * ch + ch - 1, 1), :] for g in range(gb)]
        b_end = stack(lambda g: jnp.broadcast_to(b_end_rows[g], (ch, LANES)))

        qe = (q * jnp.exp2(b2)).astype(BF16)
        o = stack(lambda g: _dot_nt(qe[g * ch:(g + 1) * ch], st_ref[g * hb + hh].astype(BF16)))

        scores = jnp.zeros((n_rows, n_rows), F32)
        for lvl, sz in enumerate(sizes):
            pieces = []
            for m in range(n_rows // (2 * sz)):
                r = b_scr[hh, pl.ds(2 * sz * m + sz - 1, 1), :]
                pieces.append(jnp.broadcast_to(r, (2 * sz, LANES)))
            r_all = pieces[0] if len(pieces) == 1 else jnp.concatenate(pieces, axis=0)
            e = jnp.exp2(-jnp.abs(b2 - r_all))
            s_l = _dot_nt((q * e).astype(BF16), (k * e).astype(BF16))
            scores = scores + masks_ref[lvl] * s_l
        if bounded:
            pieces = []
            for jb in range(n_rows // SUBLANES):
                if (jb * SUBLANES) % ch == 0:
                    pieces.append(jnp.zeros((SUBLANES, LANES), F32))
                else:
                    pieces.append(jnp.broadcast_to(b_scr[hh, pl.ds(jb * SUBLANES - 1, 1), :],
                                                   (SUBLANES, LANES)))
            r8 = jnp.concatenate(pieces, axis=0)
            s_d = _dot_nt((q * jnp.exp2(b2 - r8)).astype(BF16),
                          (k * jnp.exp2(r8 - b2)).astype(BF16))
            scores = scores + jnp.where(masks_ref[len(sizes)] > 0.0, s_d, 0.0)
        else:
            blocks = []
            for jb in range(n_rows // SUBLANES):
                bb = b2[SUBLANES * jb:SUBLANES * (jb + 1)]
                qb = q[SUBLANES * jb:SUBLANES * (jb + 1)]
                acc = jnp.zeros((SUBLANES, LANES), F32)
                for sl in range(SUBLANES):
                    s = SUBLANES * jb + sl
                    bs = b_scr[hh, pl.ds(s, 1), :]
                    ks = k_scr[hh, pl.ds(s, 1), :]
                    val = (qb * ks) * jnp.exp2(bb - bs)
                    red = jnp.sum(val, axis=1, keepdims=True)
                    acc = jnp.where(lane8 == s, red, acc)
                blocks.append(jnp.where(lane8 - SUBLANES * jb <= row8, acc, 0.0))
            scores = scores + jnp.concatenate(blocks, axis=0)
        o = o + _dot(scores.astype(BF16), v.astype(BF16))

        dk = k * jnp.exp2(b_end - b2)
        v_t = v.T.astype(BF16)
        for g in range(gb):
            own = (row_id >= g * ch) & (row_id < (g + 1) * ch)
            c = g * hb + hh
            st_ref[c] = (st_ref[c] * jnp.exp2(b_end_rows[g])
                         + _dot(v_t, jnp.where(own, dk, 0.0).astype(BF16)))

        on = o * lax.rsqrt(jnp.mean(o * o, axis=-1, keepdims=True) + EPS) * onorm[:, cols]
        out = on * _silu(stack(lambda g: g_ref[g, rows, cols]))
        for g in range(gb):
            o_ref[g, rows, cols] = out[g * ch:(g + 1) * ch].astype(o_ref.dtype)

    chains = [(g * hb + hh, g, hh) for g in range(gb) for hh in range(hb)]

    @pl.when(tb == 0)
    def _():
        for c, g, hh in chains:
            if has_init:
                st_ref[c] = s0_ref[g, hh].T
            else:
                st_ref[c] = jnp.zeros((LANES, LANES), F32)

    def run(bounded):
        def step(ci, carry):
            for hh in range(hb):
                chunk(hh, ci, bounded)
            return carry

        if nc == 1:
            step(0, 0)
        else:
            lax.fori_loop(0, nc, step, 0, unroll=2)

    if ch > SUBLANES:
        gates_bounded = jnp.min(lb) >= 2.0 ** -MAX_LOG2_DECAY
        lax.cond(gates_bounded, lambda: run(True), lambda: run(False))
    else:
        run(False)

    @pl.when(tb == pl.num_programs(2) - 1)
    def _():
        for c, g, hh in chains:
            sout_ref[g, hh] = st_ref[c].T


def _hgrn_heads(proj3, hgrn_lb, onorm_g, s0, *, ch, valid, gb, hb, tt, layer):
    bsz, t_len, _ = proj3.shape
    n_heads = onorm_g.shape[0] // HGRN_HEAD_DIM
    n_hg = n_heads // hb
    nc = tt // ch
    n_rows = gb * ch
    assert n_rows == LANES
    has_init = s0 is not None
    w = hb * LANES

    def col(sec):
        return lambda b, h, t: (b, t, sec * n_hg + h)

    in_specs = [pl.BlockSpec((gb, tt, w), col(s)) for s in range(4)]
    in_specs += [
        pl.BlockSpec((hgrn_lb.shape[0], w), lambda b, h, t: (0, h)),
        pl.BlockSpec((1, w), lambda b, h, t: (0, h)),
        pl.BlockSpec((n_rows, n_rows), lambda b, h, t: (0, 0)),
        pl.BlockSpec((len(_level_sizes(ch)) + 1, n_rows, n_rows), lambda b, h, t: (0, 0, 0)),
    ]
    args = [proj3, proj3, proj3, proj3, hgrn_lb, onorm_g.reshape(1, -1),
            jnp.asarray(_block_tri(gb, ch), dtype=BF16), _level_masks(ch, n_rows)]
    if has_init:
        in_specs.append(pl.BlockSpec((gb, hb, LANES, LANES), lambda b, h, t: (b, h, 0, 0)))
        args.append(s0)
    kern = functools.partial(_hgrn_kernel, ch=ch, valid=valid, nc=nc, gb=gb, hb=hb,
                             has_init=has_init, layer=layer)
    n_chain = gb * hb
    return pl.pallas_call(
        kern,
        grid=(bsz // gb, n_hg, t_len // tt),
        in_specs=in_specs,
        out_specs=[
            pl.BlockSpec((gb, tt, w), lambda b, h, t: (b, t, h)),
            pl.BlockSpec((gb, hb, LANES, LANES), lambda b, h, t: (b, h, 0, 0)),
        ],
        out_shape=[
            jax.ShapeDtypeStruct((bsz, t_len, n_heads * HGRN_HEAD_DIM), BF16),
            jax.ShapeDtypeStruct((bsz, n_heads, LANES, LANES), F32),
        ],
        scratch_shapes=[pltpu.VMEM((n_chain, LANES, LANES), F32),
                        pltpu.VMEM((hb, n_rows, LANES), F32),
                        pltpu.VMEM((hb, n_rows, LANES), F32)],
        compiler_params=_cparams(("arbitrary", "arbitrary", "arbitrary")),
        name="hgrn2_heads",
    )(*args)


def _softplus(x):
    return jnp.maximum(x, 0.0) + jnp.log1p(jnp.exp(-jnp.abs(x)))


def _ssd_kernel(*refs, ch, valid, nc, gb, pb, has_init):
    if has_init:
        (z_ref, x_ref, b_ref, c_ref, dt_ref, cwx_ref, cwb_ref, cwc_ref, cbx_ref, cbb_ref, cbc_ref,
         xj_ref, bias_ref, alog_ref, d_ref, ng_ref, tri_ref, tril_ref, h0_ref, cx0_ref, cb0_ref,
         cc0_ref, y_ref, hout_ref, ht_ref, wx_ref, wb_ref, wc_ref) = refs
    else:
        (z_ref, x_ref, b_ref, c_ref, dt_ref, cwx_ref, cwb_ref, cwc_ref, cbx_ref, cbb_ref, cbc_ref,
         xj_ref, bias_ref, alog_ref, d_ref, ng_ref, tri_ref, tril_ref,
         y_ref, hout_ref, ht_ref, wx_ref, wb_ref, wc_ref) = refs
        h0_ref = cx0_ref = cb0_ref = cc0_ref = None

    tri = tri_ref[...]
    tril = tril_ref[...]
    lane = lax.broadcasted_iota(I32, (LANES, LANES), 1)
    first_head = lane < SSD_HEAD_DIM
    tail = CONV_WIDTH - 1
    tb = pl.program_id(1)

    def conv(raw_ref, win_ref, cw_ref, cb_ref, g, t0):
        win_ref[g, pl.ds(SUBLANES, ch), :] = raw_ref[g, pl.ds(t0, ch), :]
        u = cb_ref[...]
        for j in range(CONV_WIDTH):
            u = u + win_ref[g, pl.ds(SUBLANES - tail + j, ch), :] * cw_ref[j:j + 1, :]
        win_ref[g, pl.ds(0, SUBLANES), :] = win_ref[g, pl.ds(ch, SUBLANES), :]
        return _silu(u)

    n_rows = gb * ch
    w = pb * LANES
    n_grp = SSD_GROUPS
    row_id = lax.broadcasted_iota(I32, (n_rows, w), 0)

    def stack(fn):
        parts = [fn(g) for g in range(gb)]
        return parts[0] if gb == 1 else jnp.concatenate(parts, axis=0)

    def group(q, rows, xs, bm, cm, dt_heads, z):
        wcols = slice(q * w, (q + 1) * w)
        dt = _dot_exact_rhs(dt_heads, xj_ref[q])
        if valid < ch:
            dt = jnp.where((row_id & (ch - 1)) < valid, dt, 0.0)
        la_cs = _dot_exact_lhs(tri, dt * (-jnp.exp(alog_ref[q])))
        xdt = xs * dt
        a_end_rows = [la_cs[g * ch + ch - 1:g * ch + ch, :] for g in range(gb)]
        a_end = stack(lambda g: jnp.broadcast_to(a_end_rows[g], (ch, w)))

        cb = _dot_nt(cm.astype(BF16), bm.astype(BF16))
        acs_t = la_cs.T
        y_parts = []
        for p in range(pb):
            cols = slice(p * LANES, (p + 1) * LANES)
            scores = []
            for hl in range(2):
                at = p * LANES + hl * SSD_HEAD_DIM
                lmat = jnp.exp(jnp.minimum(la_cs[:, at:at + 1] - acs_t[at:at + 1, :], 0.0)) * tril
                scores.append((cb * lmat).astype(BF16))
            xp = xdt[:, cols]
            rhs = jnp.concatenate([jnp.where(first_head, xp, 0.0),
                                   jnp.where(first_head, 0.0, xp)], axis=0).astype(BF16)
            y_parts.append(_dot(jnp.concatenate(scores, axis=1), rhs))
        y = y_parts[0] if pb == 1 else jnp.concatenate(y_parts, axis=1)

        y_off = stack(lambda g: _dot(cm[g * ch:(g + 1) * ch].astype(BF16),
                                     ht_ref[g, :, wcols].astype(BF16)))
        y = y + y_off * jnp.exp(la_cs)
        b_t = bm.T.astype(BF16)
        upd = xdt * jnp.exp(a_end - la_cs)
        for g in range(gb):
            own = (row_id >= g * ch) & (row_id < (g + 1) * ch)
            ht_ref[g, :, wcols] = (ht_ref[g, :, wcols] * jnp.exp(a_end_rows[g])
                                   + _dot(b_t, jnp.where(own, upd, 0.0).astype(BF16)))

        y = (y + d_ref[q] * xs) * _silu(z)
        y = y * lax.rsqrt(jnp.mean(y * y, axis=-1, keepdims=True) + EPS) * ng_ref[q]
        for g in range(gb):
            y_ref[g, rows, wcols] = y[g * ch:(g + 1) * ch].astype(y_ref.dtype)

    def step(ci, carry):
        t0 = pl.multiple_of(ci * ch, ch)
        rows = pl.ds(t0, ch)
        xs = stack(lambda g: conv(x_ref, wx_ref, cwx_ref, cbx_ref, g, t0))
        bm = stack(lambda g: conv(b_ref, wb_ref, cwb_ref, cbb_ref, g, t0))
        cm = stack(lambda g: conv(c_ref, wc_ref, cwc_ref, cbc_ref, g, t0))
        dt_heads = _softplus(stack(lambda g: dt_ref[g, rows, :]) + bias_ref[...])
        z = stack(lambda g: z_ref[g, rows, :])
        for q in range(n_grp):
            wcols = slice(q * w, (q + 1) * w)
            ncols = slice(q * LANES, (q + 1) * LANES)
            group(q, rows, xs[:, wcols], bm[:, ncols], cm[:, ncols], dt_heads, z[:, wcols])
        return carry

    @pl.when(tb == 0)
    def _():
        for g in range(gb):
            for win_ref, c0_ref in ((wx_ref, cx0_ref), (wb_ref, cb0_ref), (wc_ref, cc0_ref)):
                win_ref[g, pl.ds(0, SUBLANES), :] = jnp.zeros((SUBLANES, win_ref.shape[2]), F32)
                if has_init:
                    win_ref[g, pl.ds(SUBLANES - tail, tail), :] = c0_ref[g]
            for p in range(n_grp * pb):
                cols = slice(p * LANES, (p + 1) * LANES)
                if has_init:
                    ht_ref[g, :, cols] = h0_ref[g, p].T
                else:
                    ht_ref[g, :, cols] = jnp.zeros((LANES, LANES), F32)

    if nc == 1:
        step(0, 0)
    else:
        lax.fori_loop(0, nc, step, 0, unroll=2)

    @pl.when(tb == pl.num_programs(1) - 1)
    def _():
        for g in range(gb):
            for p in range(n_grp * pb):
                hout_ref[g, p] = ht_ref[g, :, p * LANES:(p + 1) * LANES].T


def _group_rows(v):
    return jnp.repeat(v.astype(F32), SSD_HEAD_DIM).reshape(SSD_GROUPS, 1, -1)


def _head_select(n_heads):
    import numpy as np
    w = n_heads // SSD_GROUPS * SSD_HEAD_DIM
    h = np.arange(LANES)[None, :, None]
    lane = np.arange(w)[None, None, :]
    g = np.arange(SSD_GROUPS)[:, None, None]
    return jnp.asarray((h == g * (n_heads // SSD_GROUPS) + lane // SSD_HEAD_DIM).astype(np.float32),
                       dtype=BF16)


def _block_tri(gb, ch):
    import numpy as np
    t = np.arange(gb * ch)[:, None]
    s = np.arange(gb * ch)[None, :]
    return ((t // ch == s // ch) & (s <= t)).astype(np.float32)


def _ssd_pairs(proj3, dt3, conv_w, conv_b, dt_bias, a_log, d_skip, norm_g, h0, conv0, *, mix_a,
               mix_b, ch, valid, gb, tt):
    bsz, t_len, _ = proj3.shape
    n_pairs = mix_b // LANES
    pb = n_pairs // SSD_GROUPS
    w = pb * LANES
    nc = tt // ch
    n_rows = gb * ch
    has_init = h0 is not None
    wn = SSD_GROUPS * LANES
    assert (4 * mix_a) % mix_b == 0 and (4 * mix_a + 2 * mix_b) % wn == 0 and n_rows == LANES
    z0 = 4 * mix_a // mix_b
    x0 = z0 + 1
    b0 = (4 * mix_a + 2 * mix_b) // wn
    c0 = b0 + 1

    seq = lambda blk: (lambda b, t: (b, t, blk))
    par = lambda blk: (lambda b, t: (0, blk))
    const2 = lambda b, t: (0, 0)
    const3 = lambda b, t: (0, 0, 0)

    in_specs = [
        pl.BlockSpec((gb, tt, mix_b), seq(z0)),
        pl.BlockSpec((gb, tt, mix_b), seq(x0)),
        pl.BlockSpec((gb, tt, wn), seq(b0)),
        pl.BlockSpec((gb, tt, wn), seq(c0)),
        pl.BlockSpec((gb, tt, LANES), seq(0)),
        pl.BlockSpec((CONV_WIDTH, mix_b), par(0)),
        pl.BlockSpec((CONV_WIDTH, wn), par(mix_b // wn)),
        pl.BlockSpec((CONV_WIDTH, wn), par(mix_b // wn + 1)),
        pl.BlockSpec((1, mix_b), par(0)),
        pl.BlockSpec((1, wn), par(mix_b // wn)),
        pl.BlockSpec((1, wn), par(mix_b // wn + 1)),
        pl.BlockSpec((SSD_GROUPS, LANES, w), const3),
        pl.BlockSpec((1, LANES), const2),
        pl.BlockSpec((SSD_GROUPS, 1, w), const3),
        pl.BlockSpec((SSD_GROUPS, 1, w), const3),
        pl.BlockSpec((SSD_GROUPS, 1, w), const3),
        pl.BlockSpec((n_rows, n_rows), const2),
        pl.BlockSpec((n_rows, n_rows), const2),
    ]
    conv_b2 = conv_b.reshape(1, -1)
    block_tri = _block_tri(gb, ch)
    args = [proj3, proj3, proj3, proj3, dt3, conv_w, conv_w, conv_w, conv_b2, conv_b2, conv_b2,
            _head_select(dt_bias.shape[0]),
            jnp.pad(dt_bias.astype(F32), (0, LANES - dt_bias.shape[0])).reshape(1, LANES),
            _group_rows(a_log),
            _group_rows(d_skip), norm_g.astype(F32).reshape(SSD_GROUPS, 1, w),
            jnp.asarray(block_tri, dtype=BF16), jnp.asarray(block_tri)]
    if has_init:
        tail = CONV_WIDTH - 1
        in_specs += [
            pl.BlockSpec((gb, n_pairs, LANES, LANES), lambda b, t: (b, 0, 0, 0)),
            pl.BlockSpec((gb, tail, mix_b), lambda b, t: (b, 0, 0)),
            pl.BlockSpec((gb, tail, wn), lambda b, t: (b, 0, mix_b // wn)),
            pl.BlockSpec((gb, tail, wn), lambda b, t: (b, 0, mix_b // wn + 1)),
        ]
        args += [h0, conv0, conv0, conv0]
    kern = functools.partial(_ssd_kernel, ch=ch, valid=valid, nc=nc, gb=gb, pb=pb,
                             has_init=has_init)
    return pl.pallas_call(
        kern,
        grid=(bsz // gb, t_len // tt),
        in_specs=in_specs,
        out_specs=[
            pl.BlockSpec((gb, tt, mix_b), lambda b, t: (b, t, 0)),
            pl.BlockSpec((gb, n_pairs, LANES, LANES), lambda b, t: (b, 0, 0, 0)),
        ],
        out_shape=[
            jax.ShapeDtypeStruct((bsz, t_len, mix_b), BF16),
            jax.ShapeDtypeStruct((bsz, n_pairs, LANES, LANES), F32),
        ],
        scratch_shapes=[pltpu.VMEM((gb, LANES, mix_b), F32),
                        pltpu.VMEM((gb, ch + SUBLANES, mix_b), F32),
                        pltpu.VMEM((gb, ch + SUBLANES, wn), F32),
                        pltpu.VMEM((gb, ch + SUBLANES, wn), F32)],
        compiler_params=_cparams(("arbitrary", "arbitrary")),
        name="ssd_pairs",
    )(*args)


OUT_TM = 512
WEIGHT_STAGE_ROWS = 512
ROUTE_PARTS = 4
GROUP_LANE0 = N_EXPERTS
NEG_BIG = -1e30
NO_LANE = 4 * LANES


def _first_lane_of(mask, lane):
    return jnp.min(jnp.where(mask, lane, float(NO_LANE)), axis=1, keepdims=True)


def _route(logits):
    lane_i = lax.broadcasted_iota(I32, logits.shape, 1)
    lane = lane_i.astype(F32)
    is_grp = (lane_i >= GROUP_LANE0) & (lane_i < GROUP_LANE0 + N_EXPERT_GROUPS)
    lg = jnp.where(is_grp, logits, NEG_BIG)
    g_max = jnp.max(lg, axis=1, keepdims=True)
    g_sum = jnp.sum(jnp.where(is_grp, jnp.exp(lg - g_max), 0.0), axis=1, keepdims=True)
    g_idx = _first_lane_of(lg == g_max, lane) - GROUP_LANE0
    gp_top = 1.0 / g_sum
    lane_grp = jnp.right_shift(lane_i, 3).astype(F32)
    in_grp = (lane_i < N_EXPERTS) & (lane_grp == g_idx)
    le = jnp.where(in_grp, logits, NEG_BIG)
    e_max = jnp.max(le, axis=1, keepdims=True)
    e_exp = jnp.where(in_grp, jnp.exp(le - e_max), 0.0)
    ep = e_exp / jnp.sum(e_exp, axis=1, keepdims=True)
    ep = jnp.where(in_grp, ep, -1.0)
    p1 = jnp.max(ep, axis=1, keepdims=True)
    i1 = _first_lane_of(ep == p1, lane)
    ep2 = jnp.where(lane == i1, -1.0, ep)
    p2 = jnp.max(ep2, axis=1, keepdims=True)
    i2 = _first_lane_of(ep2 == p2, lane)
    den = p1 + p2
    return i1, i2, gp_top * p1 / den, gp_top * p2 / den


def _outproj_kernel(o_ref, y_ref, x_ref, g1_ref, sc_ref, sh_ref, ng_ref, w_hbm_ref, wr_ref, br_ref,
                    tri_ref, cnt0_ref,
                    x1_ref, hn_ref, ids_ref, wts_ref, rank_ref, cnt_ref,
                    wbf_ref, stage_ref, sem, cnts_ref):
    i = pl.program_id(0)
    tm, d = hn_ref.shape
    mix_a = o_ref.shape[1]
    tn = PROJ_TN
    n_col = d // tn
    gshape = (x_ref.shape[0], x_ref.shape[1], tn)

    @pl.when(i == 0)
    def _():
        cnts_ref[...] = cnt0_ref[...]
        rows = stage_ref.shape[0]
        for k in range(d // rows):
            cp = pltpu.make_async_copy(w_hbm_ref.at[pl.ds(k * rows, rows), :], stage_ref, sem)
            cp.start()
            cp.wait()
            wbf_ref[k * rows:(k + 1) * rows, :] = stage_ref[...].astype(BF16)

    o = o_ref[...]
    y = y_ref[...]
    for c in range(n_col):
        cols = slice(c * tn, (c + 1) * tn)
        acc = _dot(o, wbf_ref[0:mix_a, cols]) + _dot(y, wbf_ref[mix_a:d, cols])
        x1_ref[:, :, cols] = x_ref[:, :, cols] + g1_ref[:, :, cols] * acc.reshape(gshape)

    g_blk, r_blk = x_ref.shape[0], x_ref.shape[1]
    pr = tm // ROUTE_PARTS
    lane = lax.broadcasted_iota(I32, (pr, LANES), 1)
    onehots, hits = [], []
    for h in range(ROUTE_PARTS):
        rows = slice(h * pr, (h + 1) * pr)
        if g_blk == 1:
            gsl = (slice(None), rows)
        else:
            gsl = (slice(h * g_blk // ROUTE_PARTS, (h + 1) * g_blk // ROUTE_PARTS), slice(None))
        pshape = (g_blk if g_blk == 1 else g_blk // ROUTE_PARTS,
                  pr if g_blk == 1 else r_blk, tn)
        x1 = [x1_ref[gsl + (slice(c * tn, (c + 1) * tn),)].reshape(pr, tn) for c in range(n_col)]
        ssq = sum(jnp.sum(xc * xc, axis=-1, keepdims=True) for xc in x1)
        rs = lax.rsqrt(ssq / d + EPS)
        logits = jnp.zeros((pr, LANES), F32) + br_ref[...]
        for c in range(n_col):
            cols = slice(c * tn, (c + 1) * tn)
            sc = sc_ref[gsl[0], :, cols] if g_blk > 1 else sc_ref[:, :, cols]
            sh = sh_ref[gsl[0], :, cols] if g_blk > 1 else sh_ref[:, :, cols]
            hn = (x1[c] * rs * ng_ref[:, cols]).reshape(pshape)
            hn = (hn * (1.0 + sc) + sh).reshape(pr, tn)
            hn_ref[rows, cols] = hn
            logits = logits + _dot_hp(hn, wr_ref[cols, :])
        i1, i2, w1, w2 = _route(logits)
        hit1 = lane.astype(F32) == i1
        hit2 = lane.astype(F32) == i2
        onehots.append(jnp.where(hit1 | hit2, 1.0, 0.0).astype(BF16))
        hits.append((hit1, hit2))
        ids_ref[rows, :] = jnp.where(lane == 0, i1, jnp.where(lane == 1, i2, 0.0)).astype(I32)
        wts_ref[rows, :] = jnp.where(lane == 0, w1, jnp.where(lane == 1, w2, 0.0))

    total = cnts_ref[0:1, :]
    for h in range(ROUTE_PARTS):
        rows = slice(h * pr, (h + 1) * pr)
        hit1, hit2 = hits[h]
        before = _dot(tri_ref[...], onehots[h]) + total
        r1 = jnp.sum(jnp.where(hit1, before, 0.0), axis=1, keepdims=True)
        r2 = jnp.sum(jnp.where(hit2, before, 0.0), axis=1, keepdims=True)
        rank_ref[rows, :] = jnp.where(lane == 0, r1, jnp.where(lane == 1, r2, 0.0)).astype(I32)
        total = total + jnp.sum(onehots[h].astype(F32), axis=0, keepdims=True)
    cnts_ref[...] = jnp.broadcast_to(total, cnts_ref.shape)
    cnt_ref[...] = jnp.broadcast_to(total, cnt_ref.shape)


def _strict_lower(n):
    import numpy as np
    t = np.arange(n)[:, None]
    s = np.arange(n)[None, :]
    return jnp.asarray((s < t).astype(np.float32), dtype=BF16)


def _out_projection(o3, y3, x3, mod3, norm_g, w_out, w_router, b_router, cnt0, group_rows):
    bsz, t_len, d = x3.shape
    g_blk, r_blk = group_rows
    tm = g_blk * r_blk
    mix_a = o3.shape[-1]
    mix_b = y3.shape[-1]
    n_tok = bsz * t_len
    tiles_per_seq = max(t_len // r_blk, 1)

    xmap = lambda i: (i // tiles_per_seq, i % tiles_per_seq, 0)
    modmap = lambda s: (lambda i: (i // tiles_per_seq, 0, s))
    row = lambda i: (i, 0)
    const2 = lambda i: (0, 0)
    return pl.pallas_call(
        _outproj_kernel,
        grid=(n_tok // tm,),
        in_specs=[
            pl.BlockSpec((tm, mix_a), row),
            pl.BlockSpec((tm, mix_b), row),
            pl.BlockSpec((g_blk, r_blk, d), xmap),
            pl.BlockSpec((g_blk, 1, d), modmap(2)),
            pl.BlockSpec((g_blk, 1, d), modmap(4)),
            pl.BlockSpec((g_blk, 1, d), modmap(3)),
            pl.BlockSpec((1, d), const2),
            pl.BlockSpec(memory_space=pl.ANY),
            pl.BlockSpec((d, LANES), const2),
            pl.BlockSpec((1, LANES), const2),
            pl.BlockSpec((tm // ROUTE_PARTS, tm // ROUTE_PARTS), const2),
            pl.BlockSpec((SUBLANES, LANES), const2),
        ],
        out_specs=[
            pl.BlockSpec((g_blk, r_blk, d), xmap),
            pl.BlockSpec((tm, d), row),
            pl.BlockSpec((tm, LANES), row),
            pl.BlockSpec((tm, LANES), row),
            pl.BlockSpec((tm, LANES), row),
            pl.BlockSpec((SUBLANES, LANES), const2),
        ],
        out_shape=[
            jax.ShapeDtypeStruct((bsz, t_len, d), F32),
            jax.ShapeDtypeStruct((n_tok, d), F32),
            jax.ShapeDtypeStruct((n_tok, LANES), I32),
            jax.ShapeDtypeStruct((n_tok, LANES), F32),
            jax.ShapeDtypeStruct((n_tok, LANES), I32),
            jax.ShapeDtypeStruct((SUBLANES, LANES), F32),
        ],
        scratch_shapes=[pltpu.VMEM((d, d), BF16), pltpu.VMEM((WEIGHT_STAGE_ROWS, d), F32),
                        pltpu.SemaphoreType.DMA(()), pltpu.VMEM((SUBLANES, LANES), F32)],
        compiler_params=_cparams(("arbitrary",)),
        name="out_proj_router",
    )(o3.reshape(n_tok, mix_a), y3.reshape(n_tok, mix_b), x3, mod3, mod3, mod3,
      norm_g.reshape(1, d), w_out, w_router, b_router, _strict_lower(tm // ROUTE_PARTS), cnt0)


def _positions_kernel(ids_ref, rank_ref, cnt_ref, pos_ref, te_ref, nt_ref, lt_ref, *, block):
    lane8 = lax.broadcasted_iota(I32, (SUBLANES, LANES), 1)
    cnt = cnt_ref[...]
    tiles = jnp.floor((cnt + (MOE_TM - 1)) * (1.0 / MOE_TM))
    tiles = jnp.where(lane8 < N_EXPERTS, tiles, 0.0)
    incl = tiles
    shift = 1
    while shift < N_EXPERTS:
        incl = incl + jnp.where(lane8 >= shift, pltpu.roll(incl, shift, axis=1), 0.0)
        shift *= 2
    offs = ((incl - tiles) * MOE_TM)[0:1, :]
    nt_ref[...] = jnp.sum(tiles, axis=1, keepdims=True).astype(I32) + jnp.zeros(nt_ref.shape, I32)
    lt_ref[...] = jnp.where(tiles > 0.0, incl - 1.0, -1.0).astype(I32)

    incl_col = jnp.broadcast_to(incl[0:1, :], (LANES, LANES)).T
    e_row = lax.broadcasted_iota(I32, (LANES, LANES), 0)
    i_lane = lax.broadcasted_iota(I32, (LANES, LANES), 1).astype(F32)
    done = jnp.where((incl_col <= i_lane) & (e_row < N_EXPERTS), 1.0, 0.0)
    te = jnp.minimum(jnp.sum(done, axis=0, keepdims=True), N_EXPERTS - 1.0)
    te_ref[...] = te.astype(I32) + jnp.zeros(te_ref.shape, I32)

    lane = lax.broadcasted_iota(I32, (block, LANES), 1)

    def body(bi, c):
        r0 = pl.multiple_of(bi * block, block)
        ids = ids_ref[pl.ds(r0, block), :]
        rank = rank_ref[pl.ds(r0, block), :]
        out = jnp.zeros((block, LANES), I32)
        for k in range(TOP_K):
            off = jnp.sum(jnp.where(lane == ids[:, k:k + 1], offs, 0.0), axis=1, keepdims=True)
            out = jnp.where(lane == k, off.astype(I32) + rank[:, k:k + 1], out)
        pos_ref[pl.ds(r0, block), :] = out
        return c

    lax.fori_loop(0, ids_ref.shape[0] // block, body, 0)


def _positions(ids, rank, cnt):
    n_tok = ids.shape[0]
    block = 512
    kern = functools.partial(_positions_kernel, block=block)
    full = lambda s: pl.BlockSpec(s, lambda: tuple(0 for _ in s))
    return pl.pallas_call(
        kern,
        in_specs=[full((n_tok, LANES)), full((n_tok, LANES)), full((SUBLANES, LANES))],
        out_specs=[full((n_tok, LANES))] + [full((SUBLANES, LANES))] * 3,
        out_shape=[jax.ShapeDtypeStruct((n_tok, LANES), I32)]
        + [jax.ShapeDtypeStruct((SUBLANES, LANES), I32)] * 3,
        compiler_params=pltpu.CompilerParams(vmem_limit_bytes=VMEM_LIMIT),
        name="route_positions",
    )(ids, rank, cnt)


def _row_copy(src_ref, src_row, dst_ref, dst_row, sem):
    return pltpu.make_async_copy(src_ref.at[pl.ds(src_row, 1), :], dst_ref.at[pl.ds(dst_row, 1), :],
                                 sem)


def _dispatch_kernel(pos_ref, lt_ref, nt_ref, *refs, steps, n_tiles):
    n_groups = len(steps)
    hn_refs = refs[:n_groups]
    xg_ref, sem, zero_ref, zsem = refs[n_groups:]
    rows = hn_refs[0].shape[0]
    i = pl.program_id(0)

    def zero_copy(tile):
        return pltpu.make_async_copy(zero_ref, xg_ref.at[pl.ds(tile * MOE_TM, MOE_TM), :], zsem)

    @pl.when(i == 0)
    def _():
        zero_ref[...] = jnp.zeros(zero_ref.shape, zero_ref.dtype)
        nt = nt_ref[0]
        for e in range(N_EXPERTS):
            @pl.when(lt_ref[e] >= 0)
            def _():
                zero_copy(jnp.maximum(lt_ref[e], 0)).start()
        lax.fori_loop(nt, n_tiles, lambda t, c: (zero_copy(t).start(), c)[1], 0)
        for e in range(N_EXPERTS):
            @pl.when(lt_ref[e] >= 0)
            def _():
                zero_copy(0).wait()
        lax.fori_loop(nt, n_tiles, lambda t, c: (zero_copy(0).wait(), c)[1], 0)

    first = 0
    for hn_ref, n_steps in zip(hn_refs, steps):
        @pl.when((i >= first) & (i < first + n_steps))
        def _(hn_ref=hn_ref):
            base = i * rows

            def issue(r, c):
                for k in range(TOP_K):
                    _row_copy(hn_ref, r, xg_ref, pos_ref[(base + r) * TOP_K + k], sem).start(
                        priority=k)
                return c

            def drain(r, c):
                for k in range(TOP_K):
                    _row_copy(hn_ref, r, xg_ref, 0, sem).wait()
                return c

            lax.fori_loop(0, rows, issue, 0, unroll=8)
            lax.fori_loop(0, rows, drain, 0, unroll=8)

        first += n_steps


def _dispatch(pos_flat, last_tile, n_live, hns, n_tiles):
    d = hns[0].shape[1]
    steps = tuple(h.shape[0] // ROW_DMA_TILE for h in hns)
    firsts = [sum(steps[:g]) for g in range(len(steps))]

    def group_map(g):
        return lambda i, pos, lt, nt: (jnp.clip(i - firsts[g], 0, steps[g] - 1), 0)

    return pl.pallas_call(
        functools.partial(_dispatch_kernel, steps=steps, n_tiles=n_tiles),
        grid_spec=pltpu.PrefetchScalarGridSpec(
            num_scalar_prefetch=3,
            grid=(sum(steps),),
            in_specs=[pl.BlockSpec((ROW_DMA_TILE, d), group_map(g)) for g in range(len(hns))],
            out_specs=pl.BlockSpec(memory_space=pl.ANY),
            scratch_shapes=[pltpu.SemaphoreType.DMA(()), pltpu.VMEM((MOE_TM, d), F32),
                            pltpu.SemaphoreType.DMA(())],
        ),
        out_shape=jax.ShapeDtypeStruct((n_tiles * MOE_TM, d), F32),
        compiler_params=_cparams(("arbitrary",)),
        name="moe_dispatch",
    )(pos_flat, last_tile, n_live, *hns)


WEIGHT_PARTS = 4


def _weight_copies(w_hbm_refs, expert, wbuf_ref, slot, sem_ref):
    copies = []
    for i, w_ref in enumerate(w_hbm_refs):
        part_rows = w_ref.shape[1] // WEIGHT_PARTS
        for part in range(WEIGHT_PARTS):
            rows = pl.ds(part * part_rows, part_rows)
            copies.append(pltpu.make_async_copy(w_ref.at[expert, rows], wbuf_ref.at[slot, i, rows],
                                                sem_ref.at[slot]))
    return copies


def _start_weight_copies(copies, queue0_parts):
    for k, cp in enumerate(copies):
        cp.start(priority=0 if k % WEIGHT_PARTS < queue0_parts else 1)


def _expert_weights_slot(te_ref, nt, t, w_hbm_refs, wbuf_ref, sem_ref, slot_ref, queue0_parts):
    expert = te_ref[t]

    @pl.when(t == 0)
    def _():
        slot_ref[0] = 0
        _start_weight_copies(_weight_copies(w_hbm_refs, expert, wbuf_ref, 0, sem_ref),
                             queue0_parts)

    @pl.when((t == 0) | (expert != te_ref[jnp.maximum(t - 1, 0)]))
    def _():
        @pl.when(t > 0)
        def _():
            slot_ref[0] = 1 - slot_ref[0]

        slot = slot_ref[0]
        for cp in _weight_copies(w_hbm_refs, expert, wbuf_ref, slot, sem_ref):
            cp.wait()
        nxt = lax.while_loop(lambda u: (u < nt) & (te_ref[jnp.minimum(u, nt - 1)] == expert),
                             lambda u: u + 1, t + 1)

        @pl.when(nxt < nt)
        def _():
            _start_weight_copies(_weight_copies(w_hbm_refs, te_ref[jnp.minimum(nxt, nt - 1)],
                                                wbuf_ref, 1 - slot, sem_ref), queue0_parts)

    return slot_ref[0]


def _moe_up_kernel(te_ref, nt_ref, x_ref, w1_ref, w3_ref, act_ref, wbuf_ref, sem_ref, slot_ref):
    t = pl.program_id(0)
    nt = nt_ref[0]

    @pl.when(t < nt)
    def _():
        slot = _expert_weights_slot(te_ref, nt, t, (w1_ref, w3_ref), wbuf_ref, sem_ref, slot_ref,
                                    queue0_parts=1)
        xb = x_ref[...].astype(BF16)
        a = _dot(xb, wbuf_ref[slot, 0].astype(BF16))
        b = _dot(xb, wbuf_ref[slot, 1].astype(BF16))
        act_ref[...] = (_silu(a) * b).astype(BF16)

    @pl.when(t >= nt)
    def _():
        act_ref[...] = jnp.zeros(act_ref.shape, act_ref.dtype)


def _moe_down_kernel(te_ref, nt_ref, act_ref, w2_ref, out_ref, wbuf_ref, sem_ref, slot_ref):
    t = pl.program_id(0)
    nt = nt_ref[0]

    @pl.when(t < nt)
    def _():
        slot = _expert_weights_slot(te_ref, nt, t, (w2_ref,), wbuf_ref, sem_ref, slot_ref,
                                    queue0_parts=0)
        out_ref[...] = _dot(act_ref[...], wbuf_ref[slot, 0].astype(BF16))

    @pl.when(t >= nt)
    def _():
        out_ref[...] = jnp.zeros(out_ref.shape, out_ref.dtype)


def _moe_experts(te, nt, xg, w1, w3, w2):
    n_rows, d = xg.shape
    d_exp = w1.shape[-1]
    n_tiles = n_rows // MOE_TM
    live_tile = lambda t, te_r, nt_r: (jnp.minimum(t, nt_r[0] - 1), 0)
    any_spec = pl.BlockSpec(memory_space=pl.ANY)

    act = pl.pallas_call(
        _moe_up_kernel,
        grid_spec=pltpu.PrefetchScalarGridSpec(
            num_scalar_prefetch=2,
            grid=(n_tiles,),
            in_specs=[pl.BlockSpec((MOE_TM, d), live_tile), any_spec, any_spec],
            out_specs=pl.BlockSpec((MOE_TM, d_exp), lambda t, te_r, nt_r: (t, 0)),
            scratch_shapes=[pltpu.VMEM((2, 2, d, d_exp), F32), pltpu.SemaphoreType.DMA((2,)),
                            pltpu.SMEM((1,), I32)],
        ),
        out_shape=jax.ShapeDtypeStruct((n_rows, d_exp), BF16),
        compiler_params=_cparams(("arbitrary",)),
        name="moe_up",
    )(te, nt, xg, w1, w3)

    return pl.pallas_call(
        _moe_down_kernel,
        grid_spec=pltpu.PrefetchScalarGridSpec(
            num_scalar_prefetch=2,
            grid=(n_tiles,),
            in_specs=[pl.BlockSpec((MOE_TM, d_exp), live_tile), any_spec],
            out_specs=pl.BlockSpec((MOE_TM, d), lambda t, te_r, nt_r: (t, 0)),
            scratch_shapes=[pltpu.VMEM((2, 1, d_exp, d), F32), pltpu.SemaphoreType.DMA((2,)),
                            pltpu.SMEM((1,), I32)],
        ),
        out_shape=jax.ShapeDtypeStruct((n_rows, d), F32),
        compiler_params=_cparams(("arbitrary",)),
        name="moe_down",
    )(te, nt, act, w2)


def _combine_kernel(pos_ref, x1_ref, g2_ref, wts_ref, fg_ref, eo_ref, y_ref, buf_ref, sem):
    rows = wts_ref.shape[0]
    base = pl.program_id(0) * rows

    def issue(r, c):
        for k in range(TOP_K):
            _row_copy(eo_ref, pos_ref[(base + r) * TOP_K + k], buf_ref.at[k], r, sem).start(
                priority=k)
        return c

    def drain(r, c):
        for k in range(TOP_K):
            _row_copy(eo_ref, 0, buf_ref.at[k], r, sem).wait()
        return c

    lax.fori_loop(0, rows, issue, 0, unroll=8)
    lax.fori_loop(0, rows, drain, 0, unroll=8)
    w = wts_ref[...]
    moe = w[:, 0:1] * buf_ref[0] + w[:, 1:2] * buf_ref[1]
    x2 = x1_ref[...] + g2_ref[...] * moe.reshape(x1_ref.shape)
    ms = jnp.mean(x2 * x2, axis=-1, keepdims=True)
    y_ref[...] = x2 * lax.rsqrt(ms + EPS) * fg_ref[...]


def _combine(pos_flat, x1, mod3, wts, final_g, expert_out, group_rows):
    bsz, t_len, d = x1.shape
    g_blk, r_blk = group_rows
    tm = g_blk * r_blk
    tiles_per_seq = max(t_len // r_blk, 1)

    def xmap(i, pos):
        return (i // tiles_per_seq, i % tiles_per_seq, 0)

    return pl.pallas_call(
        _combine_kernel,
        grid_spec=pltpu.PrefetchScalarGridSpec(
            num_scalar_prefetch=1,
            grid=((bsz * t_len) // tm,),
            in_specs=[
                pl.BlockSpec((g_blk, r_blk, d), xmap),
                pl.BlockSpec((g_blk, 1, d), lambda i, pos: (i // tiles_per_seq, 0, 5)),
                pl.BlockSpec((tm, LANES), lambda i, pos: (i, 0)),
                pl.BlockSpec((1, d), lambda i, pos: (0, 0)),
                pl.BlockSpec(memory_space=pl.ANY),
            ],
            out_specs=pl.BlockSpec((g_blk, r_blk, d), xmap),
            scratch_shapes=[pltpu.VMEM((TOP_K, tm, d), F32), pltpu.SemaphoreType.DMA(())],
        ),
        out_shape=jax.ShapeDtypeStruct((bsz, t_len, d), F32),
        compiler_params=_cparams(("arbitrary",)),
        name="moe_combine",
    )(pos_flat, x1, mod3, wts, final_g.reshape(1, d), expert_out)


def kernel(x_prompt, x_sample, state_hgrn, state_ssm, state_conv, c_prompt, c_sample, ada_w, ada_b,
           norm1_g, norm2_g, w_in, hgrn_lb, hgrn_onorm_g, conv_w, conv_b, dt_bias, a_log, d_skip,
           ssm_norm_g, w_out, w_grp, b_grp, w_rt, b_rt, w1, w3, w2, final_g):
    depth = w_in.shape[0]
    assert depth == 1, "single-layer trunk"
    layer = 0
    bp, t_p, d = x_prompt.shape
    bs, t_s, _ = x_sample.shape
    mix_a = hgrn_onorm_g.shape[1]
    mix_b = ssm_norm_g.shape[1]
    conv_dim = conv_w.shape[2]
    n_main = 4 * mix_a + mix_b + conv_dim
    n_ssd_heads = dt_bias.shape[1]
    n_pairs = mix_b // LANES
    xbc0 = 4 * mix_a + mix_b
    tail = CONV_WIDTH - 1
    assert t_s <= SUBLANES and t_s >= tail and t_p % PROMPT_CHUNK == 0

    xs_pad = jnp.pad(x_sample, ((0, 0), (0, SUBLANES - t_s), (0, 0)))
    n_c = bp + bs
    c_rows = -(-n_c // SUBLANES) * SUBLANES
    c_all = jnp.pad(jnp.concatenate([c_prompt, c_sample], axis=0), ((0, c_rows - n_c), (0, 0)))

    mod = _modulation(c_all, ada_w[layer], ada_b[layer])
    mod_p = mod[:bp].reshape(bp, 1, -1)
    mod_s = mod[bp:n_c].reshape(bs, 1, -1)
    w_in_t = jnp.swapaxes(w_in[layer], 0, 1)
    w_dt_t = jnp.pad(w_in_t[n_main:], ((0, LANES - n_ssd_heads), (0, 0)))
    w_router = jnp.pad(jnp.concatenate([w_rt[layer], w_grp[layer]], axis=1),
                       ((0, 0), (0, LANES - N_EXPERTS - N_EXPERT_GROUPS)))
    b_router = jnp.pad(jnp.concatenate([b_rt[layer], b_grp[layer]]),
                       (0, LANES - N_EXPERTS - N_EXPERT_GROUPS)).reshape(1, LANES)

    groups = (
        (x_prompt, mod_p, None, None, None, t_p, PROMPT_CHUNK, PROMPT_CHUNK, MIXER_ROWS,
         (LANES // PROMPT_CHUNK, 8), LANES // PROMPT_CHUNK, 1, (1, OUT_TM)),
        (xs_pad, mod_s, state_hgrn[layer], state_ssm[layer].reshape(bs, n_pairs, LANES, LANES),
         state_conv[layer], t_s, SUBLANES, t_s, SUBLANES,
         (LANES // SUBLANES, 2), LANES // SUBLANES, PROJ_SEQS, (OUT_TM // SUBLANES, SUBLANES)),
    )

    cnt = jnp.zeros((SUBLANES, LANES), F32)
    per_group = []
    for x3, mod3, s0, h0, c0, t_real, ch, valid, tt, (hg_gb, hg_hb), ssd_gb, proj_gr, out_gr in groups:
        bsz, t_len, _ = x3.shape
        proj, dt = _in_projection(x3, mod3, norm1_g[layer], w_in_t, w_dt_t, n_main, proj_gr)
        proj3 = proj.reshape(bsz, t_len, n_main)
        dt3 = dt.reshape(bsz, t_len, LANES)
        o3, s_new = _hgrn_heads(proj3, hgrn_lb, hgrn_onorm_g[layer], s0, ch=ch, valid=valid,
                                gb=hg_gb, hb=hg_hb, tt=tt, layer=layer)
        y3, h_new = _ssd_pairs(proj3, dt3, conv_w[layer], conv_b[layer], dt_bias[layer],
                               a_log[layer], d_skip[layer], ssm_norm_g[layer], h0, c0, mix_a=mix_a,
                               mix_b=mix_b, ch=ch, valid=valid, gb=ssd_gb, tt=tt)
        conv_new = proj3[:, t_real - tail:t_real, xbc0:xbc0 + conv_dim]
        x1, hn, ids, wts, rank, cnt = _out_projection(
            o3, y3, x3, mod3, norm2_g[layer], w_out[layer], w_router, b_router, cnt, out_gr)
        per_group.append(dict(x1=x1, hn=hn, ids=ids, wts=wts, rank=rank, mod3=mod3, out_gr=out_gr,
                              s_new=s_new, h_new=h_new, conv_new=conv_new))

    ids_all = jnp.concatenate([g["ids"] for g in per_group], axis=0)
    rank_all = jnp.concatenate([g["rank"] for g in per_group], axis=0)
    n_tok = ids_all.shape[0]
    pos, te, nt, lt = _positions(ids_all, rank_all, cnt)
    pos_flat = pos[:, :TOP_K].reshape(-1)
    n_tiles = -(-(n_tok * TOP_K + N_EXPERTS * (MOE_TM - 1)) // MOE_TM)
    assert n_tiles <= LANES
    te_vec = te[0, :n_tiles]
    nt_vec = nt[0, :1]
    lt_vec = lt[0, :N_EXPERTS]

    row0 = 0
    for g in per_group:
        rows = g["hn"].shape[0]
        g["pos"] = pos_flat[row0 * TOP_K:(row0 + rows) * TOP_K]
        row0 += rows
    xg = _dispatch(pos_flat, lt_vec, nt_vec, [g["hn"] for g in per_group], n_tiles)
    expert_out = _moe_experts(te_vec, nt_vec, xg, w1[layer], w3[layer], w2[layer])
    ys = [_combine(g["pos"], g["x1"], g["mod3"], g["wts"], final_g, expert_out, g["out_gr"])
          for g in per_group]

    gp, gs = per_group
    return (
        ys[0],
        ys[1][:, :t_s],
        gp["s_new"][None],
        gp["h_new"].reshape(1, bp, n_ssd_heads, SSD_HEAD_DIM, SSD_STATE),
        gp["conv_new"][None],
        gs["s_new"][None],
        gs["h_new"].reshape(1, bs, n_ssd_heads, SSD_HEAD_DIM, SSD_STATE),
        gs["conv_new"][None],
    )
```

```python
import functools

import jax
import jax.numpy as jnp
from jax import lax
from jax.experimental import pallas as pl
from jax.experimental.pallas import tpu as pltpu

F32 = jnp.float32
BF16 = jnp.bfloat16
I32 = jnp.int32
EPS = 1e-6
LOG2E = 1.4426950408889634
MAX_LOG2_DECAY = 12

LANES = 128
SUBLANES = 8
VMEM_LIMIT = 56 * 1024 * 1024

HGRN_HEAD_DIM = 128
SSD_HEAD_DIM = 64
SSD_GROUPS = 2
SSD_STATE = 128
CONV_WIDTH = 4
N_EXPERT_GROUPS = 4
EXPERTS_PER_GROUP = 8
N_EXPERTS = N_EXPERT_GROUPS * EXPERTS_PER_GROUP
TOP_K = 2

PROJ_SEQS = 128
NORM_ROWS = 256
PROJ_TN = 512
MOE_TM = 256
PROMPT_CHUNK = 64
MIXER_ROWS = 512
ROW_DMA_TILE = 256


def _cparams(sem):
    return pltpu.CompilerParams(dimension_semantics=sem, vmem_limit_bytes=VMEM_LIMIT)


def _dot(a, b):
    return jnp.dot(a, b, preferred_element_type=F32)


def _dot_nt(a, b):
    return lax.dot_general(a, b, (((1,), (1,)), ((), ())), preferred_element_type=F32)


def _split2(x):
    hi = x.astype(BF16)
    lo = (x - hi.astype(F32)).astype(BF16)
    return hi, lo


def _split3(x):
    hi = x.astype(BF16)
    r = x - hi.astype(F32)
    mid = r.astype(BF16)
    lo = (r - mid.astype(F32)).astype(BF16)
    return hi, mid, lo


def _dot_exact_lhs(m_bf16, x):
    hi, mid, lo = _split3(x)
    return _dot(m_bf16, hi) + _dot(m_bf16, mid) + _dot(m_bf16, lo)


def _dot_exact_rhs(x, m_bf16):
    hi, mid, lo = _split3(x)
    return _dot(hi, m_bf16) + _dot(mid, m_bf16) + _dot(lo, m_bf16)


def _dot_hp(a, w):
    ah, al = _split2(a)
    wh, wl = _split2(w)
    return _dot(ah, wh) + _dot(al, wh) + _dot(ah, wl)


def _silu(x):
    return x * jax.nn.sigmoid(x)


def _mod_kernel(c_ref, w_ref, b_ref, o_ref):
    a = _silu(c_ref[...])
    o_ref[...] = _dot_hp(a, w_ref[...]) + b_ref[...]


def _modulation(c_all, ada_w, ada_b):
    rows, d = c_all.shape
    n_out = ada_w.shape[1]
    tn = 1024
    return pl.pallas_call(
        _mod_kernel,
        grid=(n_out // tn,),
        in_specs=[
            pl.BlockSpec((rows, d), lambda j: (0, 0)),
            pl.BlockSpec((d, tn), lambda j: (0, j)),
            pl.BlockSpec((1, tn), lambda j: (0, j)),
        ],
        out_specs=pl.BlockSpec((rows, tn), lambda j: (0, j)),
        out_shape=jax.ShapeDtypeStruct((rows, n_out), F32),
        compiler_params=_cparams(("arbitrary",)),
        name="adaln_mod",
    )(c_all, ada_w, ada_b.reshape(1, n_out))


def _proj_kernel(x_hbm_ref, sc_ref, sh_ref, g_ref, w_ref, wdt_ref, o_ref, dt_ref, hn_ref, xs_ref,
                 sem):
    i = pl.program_id(0)
    j = pl.program_id(1)
    g_blk, r_blk, d = xs_ref.shape

    def x_copy(tile):
        return pltpu.make_async_copy(x_hbm_ref.at[pl.ds(tile * g_blk, g_blk)], xs_ref, sem)

    @pl.when(j == 0)
    def _():
        @pl.when(i == 0)
        def _():
            x_copy(0).start()

        x_copy(i).wait()
        wh, wl = _split2(wdt_ref[...])
        n_rows = g_blk * r_blk
        step = min(n_rows, NORM_ROWS)
        for c in range(n_rows // step):
            if g_blk == 1:
                x = xs_ref[:, c * step:(c + 1) * step, :]
                sc, sh = sc_ref[...], sh_ref[...]
            else:
                gs = slice(c * step // r_blk, (c + 1) * step // r_blk)
                x, sc, sh = xs_ref[gs], sc_ref[gs], sh_ref[gs]
            ms = jnp.mean(x * x, axis=-1, keepdims=True)
            hn = ((x * lax.rsqrt(ms + EPS)) * g_ref[...] * (1.0 + sc) + sh).reshape(step, d)
            rows = slice(c * step, (c + 1) * step)
            hn_ref[rows, :] = hn.astype(BF16)
            ah, al = _split2(hn)
            dt_ref[rows, :] = _dot_nt(ah, wh) + _dot_nt(al, wh) + _dot_nt(ah, wl)

        @pl.when(i + 1 < pl.num_programs(0))
        def _():
            x_copy(i + 1).start()

    o_ref[...] = _dot_nt(hn_ref[...], w_ref[...].astype(BF16))


def _in_projection(x3, mod3, norm_g, w_t, w_dt_t, n_main, g_blk):
    bsz, t_len, d = x3.shape
    tm = g_blk * t_len
    n_row_tiles = bsz // g_blk

    def mod_map(sec):
        return lambda i, j: (i, 0, sec)

    return pl.pallas_call(
        _proj_kernel,
        grid=(n_row_tiles, n_main // PROJ_TN),
        in_specs=[
            pl.BlockSpec(memory_space=pl.ANY),
            pl.BlockSpec((g_blk, 1, d), mod_map(1)),
            pl.BlockSpec((g_blk, 1, d), mod_map(0)),
            pl.BlockSpec((1, d), lambda i, j: (0, 0)),
            pl.BlockSpec((PROJ_TN, d), lambda i, j: (j, 0)),
            pl.BlockSpec((LANES, d), lambda i, j: (0, 0)),
        ],
        out_specs=[
            pl.BlockSpec((tm, PROJ_TN), lambda i, j: (i, j)),
            pl.BlockSpec((tm, LANES), lambda i, j: (i, 0)),
        ],
        out_shape=[
            jax.ShapeDtypeStruct((bsz * t_len, n_main), F32),
            jax.ShapeDtypeStruct((bsz * t_len, LANES), F32),
        ],
        scratch_shapes=[pltpu.VMEM((tm, d), BF16), pltpu.VMEM((g_blk, t_len, d), F32),
                        pltpu.SemaphoreType.DMA(())],
        compiler_params=_cparams(("arbitrary", "arbitrary")),
        name="in_proj",
    )(x3, mod3, mod3, norm_g.reshape(1, d), w_t, w_dt_t)


def _level_sizes(ch):
    sizes = []
    sz = SUBLANES
    while 2 * sz <= ch:
        sizes.append(sz)
        sz *= 2
    return sizes


def _level_masks(ch, n_rows):
    import numpy as np
    t = np.arange(n_rows)[:, None]
    s = np.arange(n_rows)[None, :]
    out = []
    for sz in _level_sizes(ch):
        m = (t // (2 * sz) == s // (2 * sz)) & (t % (2 * sz) >= sz) & (s % (2 * sz) < sz)
        out.append(m.astype(np.float32))
    out.append(((t // SUBLANES == s // SUBLANES) & (s <= t)).astype(np.float32))
    return jnp.asarray(np.stack(out))


def _hgrn_kernel(*refs, ch, valid, nc, gb, hb, has_init, layer):
    if has_init:
        (q_ref, f_ref, i_ref, g_ref, lb_ref, on_ref, tri_ref, masks_ref, s0_ref,
         o_ref, sout_ref, st_ref, b_scr, k_scr) = refs
    else:
        (q_ref, f_ref, i_ref, g_ref, lb_ref, on_ref, tri_ref, masks_ref,
         o_ref, sout_ref, st_ref, b_scr, k_scr) = refs
        s0_ref = None

    n_lb = lb_ref.shape[0]
    lb_rows = [lb_ref[i:i + 1, :] for i in range(n_lb)]
    lb_max = functools.reduce(jnp.maximum, lb_rows)
    lb_exp = [jnp.exp(r - lb_max) for r in lb_rows]
    lb = sum(lb_exp[:layer + 1]) / sum(lb_exp)
    onorm = on_ref[...]
    tri = tri_ref[...]
    sizes = _level_sizes(ch)
    lane8 = lax.broadcasted_iota(I32, (SUBLANES, LANES), 1)
    row8 = lax.broadcasted_iota(I32, (SUBLANES, LANES), 0)
    tb = pl.program_id(2)

    n_rows = gb * ch
    row_id = lax.broadcasted_iota(I32, (n_rows, LANES), 0)

    def stack(fn):
        parts = [fn(g) for g in range(gb)]
        return parts[0] if gb == 1 else jnp.concatenate(parts, axis=0)

    def chunk(hh, ci, bounded):
        cols = slice(hh * LANES, (hh + 1) * LANES)
        t0 = pl.multiple_of(ci * ch, ch)
        rows = pl.ds(t0, ch)
        lb_h = lb[:, cols]
        q = _silu(stack(lambda g: q_ref[g, rows, cols]))
        fg = lb_h + (1.0 - lb_h) * jax.nn.sigmoid(stack(lambda g: f_ref[g, rows, cols]))
        k = 1.0 - fg
        lf = jnp.log(fg)
        v = stack(lambda g: i_ref[g, rows, cols])
        if valid < ch:
            live = (row_id & (ch - 1)) < valid
            lf = jnp.where(live, lf, 0.0)
            k = jnp.where(live, k, 0.0)
        b2 = _dot_exact_lhs(tri, lf) * LOG2E
        b_scr[hh] = b2
        k_scr[hh] = k
        b_end_rows = [b_scr[hh, pl.ds(g * ch + ch - 1, 1), :] for g in range(gb)]
        b_end = stack(lambda g: jnp.broadcast_to(b_end_rows[g], (ch, LANES)))

        qe = (q * jnp.exp2(b2)).astype(BF16)
        o = stack(lambda g: _dot_nt(qe[g * ch:(g + 1) * ch], st_ref[g * hb + hh].astype(BF16)))

        scores = jnp.zeros((n_rows, n_rows), F32)
        for lvl, sz in enumerate(sizes):
            pieces = []
            for m in range(n_rows // (2 * sz)):
                r = b_scr[hh, pl.ds(2 * sz * m + sz - 1, 1), :]
                pieces.append(jnp.broadcast_to(r, (2 * sz, LANES)))
            r_all = pieces[0] if len(pieces) == 1 else jnp.concatenate(pieces, axis=0)
            e = jnp.exp2(-jnp.abs(b2 - r_all))
            s_l = _dot_nt((q * e).astype(BF16), (k * e).astype(BF16))
            scores = scores + masks_ref[lvl] * s_l
        if bounded:
            pieces = []
            for jb in range(n_rows // SUBLANES):
                if (jb * SUBLANES) % ch == 0:
                    pieces.append(jnp.zeros((SUBLANES, LANES), F32))
                else:
                    pieces.append(jnp.broadcast_to(b_scr[hh, pl.ds(jb * SUBLANES - 1, 1), :],
                                                   (SUBLANES, LANES)))
            r8 = jnp.concatenate(pieces, axis=0)
            s_d = _dot_nt((q * jnp.exp2(b2 - r8)).astype(BF16),
                          (k * jnp.exp2(r8 - b2)).astype(BF16))
            scores = scores + jnp.where(masks_ref[len(sizes)] > 0.0, s_d, 0.0)
        else:
            blocks = []
            for jb in range(n_rows // SUBLANES):
                bb = b2[SUBLANES * jb:SUBLANES * (jb + 1)]
                qb = q[SUBLANES * jb:SUBLANES * (jb + 1)]
                acc = jnp.zeros((SUBLANES, LANES), F32)
                for sl in range(SUBLANES):
                    s = SUBLANES * jb + sl
                    bs = b_scr[hh, pl.ds(s, 1), :]
                    ks = k_scr[hh, pl.ds(s, 1), :]
                    val = (qb * ks) * jnp.exp2(bb - bs)
                    red = jnp.sum(val, axis=1, keepdims=True)
                    acc = jnp.where(lane8 == s, red, acc)
                blocks.append(jnp.where(lane8 - SUBLANES * jb <= row8, acc, 0.0))
            scores = scores + jnp.concatenate(blocks, axis=0)
        o = o + _dot(scores.astype(BF16), v.astype(BF16))

        dk = k * jnp.exp2(b_end - b2)
        v_t = v.T.astype(BF16)
        for g in range(gb):
            own = (row_id >= g * ch) & (row_id < (g + 1) * ch)
            c = g * hb + hh
            st_ref[c] = (st_ref[c] * jnp.exp2(b_end_rows[g])
                         + _dot(v_t, jnp.where(own, dk, 0.0).astype(BF16)))

        on = o * lax.rsqrt(jnp.mean(o * o, axis=-1, keepdims=True) + EPS) * onorm[:, cols]
        out = on * _silu(stack(lambda g: g_ref[g, rows, cols]))
        for g in range(gb):
            o_ref[g, rows, cols] = out[g * ch:(g + 1) * ch].astype(o_ref.dtype)

    chains = [(g * hb + hh, g, hh) for g in range(gb) for hh in range(hb)]

    @pl.when(tb == 0)
    def _():
        for c, g, hh in chains:
            if has_init:
                st_ref[c] = s0_ref[g, hh].T
            else:
                st_ref[c] = jnp.zeros((LANES, LANES), F32)

    def run(bounded):
        def step(ci, carry):
            for hh in range(hb):
                chunk(hh, ci, bounded)
            return carry

        if nc == 1:
            step(0, 0)
        else:
            lax.fori_loop(0, nc, step, 0, unroll=2)

    if ch > SUBLANES:
        gates_bounded = jnp.min(lb) >= 2.0 ** -MAX_LOG2_DECAY
        lax.cond(gates_bounded, lambda: run(True), lambda: run(False))
    else:
        run(False)

    @pl.when(tb == pl.num_programs(2) - 1)
    def _():
        for c, g, hh in chains:
            sout_ref[g, hh] = st_ref[c].T


def _hgrn_heads(proj3, hgrn_lb, onorm_g, s0, *, ch, valid, gb, hb, tt, layer):
    bsz, t_len, _ = proj3.shape
    n_heads = onorm_g.shape[0] // HGRN_HEAD_DIM
    n_hg = n_heads // hb
    nc = tt // ch
    n_rows = gb * ch
    assert n_rows == LANES
    has_init = s0 is not None
    w = hb * LANES

    def col(sec):
        return lambda b, h, t: (b, t, sec * n_hg + h)

    in_specs = [pl.BlockSpec((gb, tt, w), col(s)) for s in range(4)]
    in_specs += [
        pl.BlockSpec((hgrn_lb.shape[0], w), lambda b, h, t: (0, h)),
        pl.BlockSpec((1, w), lambda b, h, t: (0, h)),
        pl.BlockSpec((n_rows, n_rows), lambda b, h, t: (0, 0)),
        pl.BlockSpec((len(_level_sizes(ch)) + 1, n_rows, n_rows), lambda b, h, t: (0, 0, 0)),
    ]
    args = [proj3, proj3, proj3, proj3, hgrn_lb, onorm_g.reshape(1, -1),
            jnp.asarray(_block_tri(gb, ch), dtype=BF16), _level_masks(ch, n_rows)]
    if has_init:
        in_specs.append(pl.BlockSpec((gb, hb, LANES, LANES), lambda b, h, t: (b, h, 0, 0)))
        args.append(s0)
    kern = functools.partial(_hgrn_kernel, ch=ch, valid=valid, nc=nc, gb=gb, hb=hb,
                             has_init=has_init, layer=layer)
    n_chain = gb * hb
    return pl.pallas_call(
        kern,
        grid=(bsz // gb, n_hg, t_len // tt),
        in_specs=in_specs,
        out_specs=[
            pl.BlockSpec((gb, tt, w), lambda b, h, t: (b, t, h)),
            pl.BlockSpec((gb, hb, LANES, LANES), lambda b, h, t: (b, h, 0, 0)),
        ],
        out_shape=[
            jax.ShapeDtypeStruct((bsz, t_len, n_heads * HGRN_HEAD_DIM), BF16),
            jax.ShapeDtypeStruct((bsz, n_heads, LANES, LANES), F32),
        ],
        scratch_shapes=[pltpu.VMEM((n_chain, LANES, LANES), F32),
                        pltpu.VMEM((hb, n_rows, LANES), F32),
                        pltpu.VMEM((hb, n_rows, LANES), F32)],
        compiler_params=_cparams(("arbitrary", "arbitrary", "arbitrary")),
        name="hgrn2_heads",
    )(*args)


def _softplus(x):
    return jnp.maximum(x, 0.0) + jnp.log1p(jnp.exp(-jnp.abs(x)))


def _ssd_kernel(*refs, ch, valid, nc, gb, pb, has_init):
    if has_init:
        (z_ref, x_ref, b_ref, c_ref, dt_ref, cwx_ref, cwb_ref, cwc_ref, cbx_ref, cbb_ref, cbc_ref,
         xj_ref, bias_ref, alog_ref, d_ref, ng_ref, tri_ref, tril_ref, h0_ref, cx0_ref, cb0_ref,
         cc0_ref, y_ref, hout_ref, ht_ref, wx_ref, wb_ref, wc_ref) = refs
    else:
        (z_ref, x_ref, b_ref, c_ref, dt_ref, cwx_ref, cwb_ref, cwc_ref, cbx_ref, cbb_ref, cbc_ref,
         xj_ref, bias_ref, alog_ref, d_ref, ng_ref, tri_ref, tril_ref,
         y_ref, hout_ref, ht_ref, wx_ref, wb_ref, wc_ref) = refs
        h0_ref = cx0_ref = cb0_ref = cc0_ref = None

    tri = tri_ref[...]
    tril = tril_ref[...]
    lane = lax.broadcasted_iota(I32, (LANES, LANES), 1)
    first_head = lane < SSD_HEAD_DIM
    tail = CONV_WIDTH - 1
    tb = pl.program_id(1)

    def conv(raw_ref, win_ref, cw_ref, cb_ref, g, t0):
        raw = raw_ref[g, pl.ds(t0, ch), :]
        win_ref[g, pl.ds(SUBLANES, ch), :] = raw
        ext = win_ref[g]
        u = cb_ref[...] + raw * cw_ref[tail:tail + 1, :]
        for j in range(tail):
            shifted = pltpu.roll(ext, tail - j, axis=0)[SUBLANES:SUBLANES + ch]
            u = u + shifted * cw_ref[j:j + 1, :]
        win_ref[g, pl.ds(0, SUBLANES), :] = win_ref[g, pl.ds(ch, SUBLANES), :]
        return _silu(u)

    n_rows = gb * ch
    w = pb * LANES
    n_grp = SSD_GROUPS
    row_id = lax.broadcasted_iota(I32, (n_rows, w), 0)

    def stack(fn):
        parts = [fn(g) for g in range(gb)]
        return parts[0] if gb == 1 else jnp.concatenate(parts, axis=0)

    def group(q, rows, xs, bm, cm, dt_heads, z):
        wcols = slice(q * w, (q + 1) * w)
        dt = _dot_exact_rhs(dt_heads, xj_ref[q])
        if valid < ch:
            dt = jnp.where((row_id & (ch - 1)) < valid, dt, 0.0)
        la_cs = _dot_exact_lhs(tri, dt * (-jnp.exp(alog_ref[q])))
        xdt = xs * dt
        a_end_rows = [la_cs[g * ch + ch - 1:g * ch + ch, :] for g in range(gb)]
        a_end = stack(lambda g: jnp.broadcast_to(a_end_rows[g], (ch, w)))

        cb = _dot_nt(cm.astype(BF16), bm.astype(BF16))
        acs_t = la_cs.T
        y_parts = []
        for p in range(pb):
            cols = slice(p * LANES, (p + 1) * LANES)
            scores = []
            for hl in range(2):
                at = p * LANES + hl * SSD_HEAD_DIM
                lmat = jnp.exp(jnp.minimum(la_cs[:, at:at + 1] - acs_t[at:at + 1, :], 0.0)) * tril
                scores.append((cb * lmat).astype(BF16))
            xp = xdt[:, cols]
            rhs = jnp.concatenate([jnp.where(first_head, xp, 0.0),
                                   jnp.where(first_head, 0.0, xp)], axis=0).astype(BF16)
            y_parts.append(_dot(jnp.concatenate(scores, axis=1), rhs))
        y = y_parts[0] if pb == 1 else jnp.concatenate(y_parts, axis=1)

        y_off = stack(lambda g: _dot(cm[g * ch:(g + 1) * ch].astype(BF16),
                                     ht_ref[g, :, wcols].astype(BF16)))
        y = y + y_off * jnp.exp(la_cs)
        b_t = bm.T.astype(BF16)
        upd = xdt * jnp.exp(a_end - la_cs)
        for g in range(gb):
            own = (row_id >= g * ch) & (row_id < (g + 1) * ch)
            ht_ref[g, :, wcols] = (ht_ref[g, :, wcols] * jnp.exp(a_end_rows[g])
                                   + _dot(b_t, jnp.where(own, upd, 0.0).astype(BF16)))

        y = (y + d_ref[q] * xs) * _silu(z)
        y = y * lax.rsqrt(jnp.mean(y * y, axis=-1, keepdims=True) + EPS) * ng_ref[q]
        for g in range(gb):
            y_ref[g, rows, wcols] = y[g * ch:(g + 1) * ch].astype(y_ref.dtype)

    def step(ci, carry):
        t0 = pl.multiple_of(ci * ch, ch)
        rows = pl.ds(t0, ch)
        xs = stack(lambda g: conv(x_ref, wx_ref, cwx_ref, cbx_ref, g, t0))
        bm = stack(lambda g: conv(b_ref, wb_ref, cwb_ref, cbb_ref, g, t0))
        cm = stack(lambda g: conv(c_ref, wc_ref, cwc_ref, cbc_ref, g, t0))
        dt_heads = _softplus(stack(lambda g: dt_ref[g, rows, :]) + bias_ref[...])
        z = stack(lambda g: z_ref[g, rows, :])
        for q in range(n_grp):
            wcols = slice(q * w, (q + 1) * w)
            ncols = slice(q * LANES, (q + 1) * LANES)
            group(q, rows, xs[:, wcols], bm[:, ncols], cm[:, ncols], dt_heads, z[:, wcols])
        return carry

    @pl.when(tb == 0)
    def _():
        for g in range(gb):
            for win_ref, c0_ref in ((wx_ref, cx0_ref), (wb_ref, cb0_ref), (wc_ref, cc0_ref)):
                win_ref[g, pl.ds(0, SUBLANES), :] = jnp.zeros((SUBLANES, win_ref.shape[2]), F32)
                if has_init:
                    win_ref[g, pl.ds(SUBLANES - tail, tail), :] = c0_ref[g]
            for p in range(n_grp * pb):
                cols = slice(p * LANES, (p + 1) * LANES)
                if has_init:
                    ht_ref[g, :, cols] = h0_ref[g, p].T
                else:
                    ht_ref[g, :, cols] = jnp.zeros((LANES, LANES), F32)

    if nc == 1:
        step(0, 0)
    else:
        lax.fori_loop(0, nc, step, 0, unroll=2)

    @pl.when(tb == pl.num_programs(1) - 1)
    def _():
        for g in range(gb):
            for p in range(n_grp * pb):
                hout_ref[g, p] = ht_ref[g, :, p * LANES:(p + 1) * LANES].T


def _group_rows(v):
    return jnp.repeat(v.astype(F32), SSD_HEAD_DIM).reshape(SSD_GROUPS, 1, -1)


def _head_select(n_heads):
    import numpy as np
    w = n_heads // SSD_GROUPS * SSD_HEAD_DIM
    h = np.arange(LANES)[None, :, None]
    lane = np.arange(w)[None, None, :]
    g = np.arange(SSD_GROUPS)[:, None, None]
    return jnp.asarray((h == g * (n_heads // SSD_GROUPS) + lane // SSD_HEAD_DIM).astype(np.float32),
                       dtype=BF16)


def _block_tri(gb, ch):
    import numpy as np
    t = np.arange(gb * ch)[:, None]
    s = np.arange(gb * ch)[None, :]
    return ((t // ch == s // ch) & (s <= t)).astype(np.float32)


def _ssd_pairs(proj3, dt3, conv_w, conv_b, dt_bias, a_log, d_skip, norm_g, h0, conv0, *, mix_a,
               mix_b, ch, valid, gb, tt):
    bsz, t_len, _ = proj3.shape
    n_pairs = mix_b // LANES
    pb = n_pairs // SSD_GROUPS
    w = pb * LANES
    nc = tt // ch
    n_rows = gb * ch
    has_init = h0 is not None
    wn = SSD_GROUPS * LANES
    assert (4 * mix_a) % mix_b == 0 and (4 * mix_a + 2 * mix_b) % wn == 0 and n_rows == LANES
    z0 = 4 * mix_a // mix_b
    x0 = z0 + 1
    b0 = (4 * mix_a + 2 * mix_b) // wn
    c0 = b0 + 1

    seq = lambda blk: (lambda b, t: (b, t, blk))
    par = lambda blk: (lambda b, t: (0, blk))
    const2 = lambda b, t: (0, 0)
    const3 = lambda b, t: (0, 0, 0)

    in_specs = [
        pl.BlockSpec((gb, tt, mix_b), seq(z0)),
        pl.BlockSpec((gb, tt, mix_b), seq(x0)),
        pl.BlockSpec((gb, tt, wn), seq(b0)),
        pl.BlockSpec((gb, tt, wn), seq(c0)),
        pl.BlockSpec((gb, tt, LANES), seq(0)),
        pl.BlockSpec((CONV_WIDTH, mix_b), par(0)),
        pl.BlockSpec((CONV_WIDTH, wn), par(mix_b // wn)),
        pl.BlockSpec((CONV_WIDTH, wn), par(mix_b // wn + 1)),
        pl.BlockSpec((1, mix_b), par(0)),
        pl.BlockSpec((1, wn), par(mix_b // wn)),
        pl.BlockSpec((1, wn), par(mix_b // wn + 1)),
        pl.BlockSpec((SSD_GROUPS, LANES, w), const3),
        pl.BlockSpec((1, LANES), const2),
        pl.BlockSpec((SSD_GROUPS, 1, w), const3),
        pl.BlockSpec((SSD_GROUPS, 1, w), const3),
        pl.BlockSpec((SSD_GROUPS, 1, w), const3),
        pl.BlockSpec((n_rows, n_rows), const2),
        pl.BlockSpec((n_rows, n_rows), const2),
    ]
    conv_b2 = conv_b.reshape(1, -1)
    block_tri = _block_tri(gb, ch)
    args = [proj3, proj3, proj3, proj3, dt3, conv_w, conv_w, conv_w, conv_b2, conv_b2, conv_b2,
            _head_select(dt_bias.shape[0]),
            jnp.pad(dt_bias.astype(F32), (0, LANES - dt_bias.shape[0])).reshape(1, LANES),
            _group_rows(a_log),
            _group_rows(d_skip), norm_g.astype(F32).reshape(SSD_GROUPS, 1, w),
            jnp.asarray(block_tri, dtype=BF16), jnp.asarray(block_tri)]
    if has_init:
        tail = CONV_WIDTH - 1
        in_specs += [
            pl.BlockSpec((gb, n_pairs, LANES, LANES), lambda b, t: (b, 0, 0, 0)),
            pl.BlockSpec((gb, tail, mix_b), lambda b, t: (b, 0, 0)),
            pl.BlockSpec((gb, tail, wn), lambda b, t: (b, 0, mix_b // wn)),
            pl.BlockSpec((gb, tail, wn), lambda b, t: (b, 0, mix_b // wn + 1)),
        ]
        args += [h0, conv0, conv0, conv0]
    kern = functools.partial(_ssd_kernel, ch=ch, valid=valid, nc=nc, gb=gb, pb=pb,
                             has_init=has_init)
    return pl.pallas_call(
        kern,
        grid=(bsz // gb, t_len // tt),
        in_specs=in_specs,
        out_specs=[
            pl.BlockSpec((gb, tt, mix_b), lambda b, t: (b, t, 0)),
            pl.BlockSpec((gb, n_pairs, LANES, LANES), lambda b, t: (b, 0, 0, 0)),
        ],
        out_shape=[
            jax.ShapeDtypeStruct((bsz, t_len, mix_b), BF16),
            jax.ShapeDtypeStruct((bsz, n_pairs, LANES, LANES), F32),
        ],
        scratch_shapes=[pltpu.VMEM((gb, LANES, mix_b), F32),
                        pltpu.VMEM((gb, ch + SUBLANES, mix_b), F32),
                        pltpu.VMEM((gb, ch + SUBLANES, wn), F32),
                        pltpu.VMEM((gb, ch + SUBLANES, wn), F32)],
        compiler_params=_cparams(("arbitrary", "arbitrary")),
        name="ssd_pairs",
    )(*args)


OUT_TM = 512
WEIGHT_STAGE_ROWS = 512
ROUTE_PARTS = 4
GROUP_LANE0 = N_EXPERTS
NEG_BIG = -1e30
NO_LANE = 4 * LANES


def _first_lane_of(mask, lane):
    return jnp.min(jnp.where(mask, lane, float(NO_LANE)), axis=1, keepdims=True)


def _route(logits):
    lane_i = lax.broadcasted_iota(I32, logits.shape, 1)
    lane = lane_i.astype(F32)
    is_grp = (lane_i >= GROUP_LANE0) & (lane_i < GROUP_LANE0 + N_EXPERT_GROUPS)
    lg = jnp.where(is_grp, logits, NEG_BIG)
    g_max = jnp.max(lg, axis=1, keepdims=True)
    g_sum = jnp.sum(jnp.where(is_grp, jnp.exp(lg - g_max), 0.0), axis=1, keepdims=True)
    g_idx = _first_lane_of(lg == g_max, lane) - GROUP_LANE0
    gp_top = 1.0 / g_sum
    lane_grp = jnp.right_shift(lane_i, 3).astype(F32)
    in_grp = (lane_i < N_EXPERTS) & (lane_grp == g_idx)
    le = jnp.where(in_grp, logits, NEG_BIG)
    e_max = jnp.max(le, axis=1, keepdims=True)
    e_exp = jnp.where(in_grp, jnp.exp(le - e_max), 0.0)
    ep = e_exp / jnp.sum(e_exp, axis=1, keepdims=True)
    ep = jnp.where(in_grp, ep, -1.0)
    p1 = jnp.max(ep, axis=1, keepdims=True)
    i1 = _first_lane_of(ep == p1, lane)
    ep2 = jnp.where(lane == i1, -1.0, ep)
    p2 = jnp.max(ep2, axis=1, keepdims=True)
    i2 = _first_lane_of(ep2 == p2, lane)
    den = p1 + p2
    return i1, i2, gp_top * p1 / den, gp_top * p2 / den


def _outproj_kernel(o_ref, y_ref, x_ref, g1_ref, sc_ref, sh_ref, ng_ref, w_hbm_ref, wr_ref, br_ref,
                    tri_ref, cnt0_ref,
                    x1_ref, hn_ref, ids_ref, wts_ref, rank_ref, cnt_ref,
                    wbf_ref, stage_ref, sem, cnts_ref):
    i = pl.program_id(0)
    tm, d = hn_ref.shape
    mix_a = o_ref.shape[1]
    tn = PROJ_TN
    n_col = d // tn
    gshape = (x_ref.shape[0], x_ref.shape[1], tn)

    @pl.when(i == 0)
    def _():
        cnts_ref[...] = cnt0_ref[...]
        rows = stage_ref.shape[0]
        for k in range(d // rows):
            cp = pltpu.make_async_copy(w_hbm_ref.at[pl.ds(k * rows, rows), :], stage_ref, sem)
            cp.start()
            cp.wait()
            wbf_ref[k * rows:(k + 1) * rows, :] = stage_ref[...].astype(BF16)

    o = o_ref[...]
    y = y_ref[...]
    for c in range(n_col):
        cols = slice(c * tn, (c + 1) * tn)
        acc = _dot(o, wbf_ref[0:mix_a, cols]) + _dot(y, wbf_ref[mix_a:d, cols])
        x1_ref[:, :, cols] = x_ref[:, :, cols] + g1_ref[:, :, cols] * acc.reshape(gshape)

    g_blk, r_blk = x_ref.shape[0], x_ref.shape[1]
    pr = tm // ROUTE_PARTS
    lane = lax.broadcasted_iota(I32, (pr, LANES), 1)
    onehots, hits = [], []
    for h in range(ROUTE_PARTS):
        rows = slice(h * pr, (h + 1) * pr)
        if g_blk == 1:
            gsl = (slice(None), rows)
        else:
            gsl = (slice(h * g_blk // ROUTE_PARTS, (h + 1) * g_blk // ROUTE_PARTS), slice(None))
        pshape = (g_blk if g_blk == 1 else g_blk // ROUTE_PARTS,
                  pr if g_blk == 1 else r_blk, tn)
        x1 = [x1_ref[gsl + (slice(c * tn, (c + 1) * tn),)].reshape(pr, tn) for c in range(n_col)]
        ssq = sum(jnp.sum(xc * xc, axis=-1, keepdims=True) for xc in x1)
        rs = lax.rsqrt(ssq / d + EPS)
        logits = jnp.zeros((pr, LANES), F32) + br_ref[...]
        for c in range(n_col):
            cols = slice(c * tn, (c + 1) * tn)
            sc = sc_ref[gsl[0], :, cols] if g_blk > 1 else sc_ref[:, :, cols]
            sh = sh_ref[gsl[0], :, cols] if g_blk > 1 else sh_ref[:, :, cols]
            hn = (x1[c] * rs * ng_ref[:, cols]).reshape(pshape)
            hn = (hn * (1.0 + sc) + sh).reshape(pr, tn)
            hn_ref[rows, cols] = hn
            logits = logits + _dot_hp(hn, wr_ref[cols, :])
        i1, i2, w1, w2 = _route(logits)
        hit1 = lane.astype(F32) == i1
        hit2 = lane.astype(F32) == i2
        onehots.append(jnp.where(hit1 | hit2, 1.0, 0.0).astype(BF16))
        hits.append((hit1, hit2))
        ids_ref[rows, :] = jnp.where(lane == 0, i1, jnp.where(lane == 1, i2, 0.0)).astype(I32)
        wts_ref[rows, :] = jnp.where(lane == 0, w1, jnp.where(lane == 1, w2, 0.0))

    total = cnts_ref[0:1, :]
    for h in range(ROUTE_PARTS):
        rows = slice(h * pr, (h + 1) * pr)
        hit1, hit2 = hits[h]
        before = _dot(tri_ref[...], onehots[h]) + total
        r1 = jnp.sum(jnp.where(hit1, before, 0.0), axis=1, keepdims=True)
        r2 = jnp.sum(jnp.where(hit2, before, 0.0), axis=1, keepdims=True)
        rank_ref[rows, :] = jnp.where(lane == 0, r1, jnp.where(lane == 1, r2, 0.0)).astype(I32)
        total = total + jnp.sum(onehots[h].astype(F32), axis=0, keepdims=True)
    cnts_ref[...] = jnp.broadcast_to(total, cnts_ref.shape)
    cnt_ref[...] = jnp.broadcast_to(total, cnt_ref.shape)


def _strict_lower(n):
    import numpy as np
    t = np.arange(n)[:, None]
    s = np.arange(n)[None, :]
    return jnp.asarray((s < t).astype(np.float32), dtype=BF16)


def _out_projection(o3, y3, x3, mod3, norm_g, w_out, w_router, b_router, cnt0, group_rows):
    bsz, t_len, d = x3.shape
    g_blk, r_blk = group_rows
    tm = g_blk * r_blk
    mix_a = o3.shape[-1]
    mix_b = y3.shape[-1]
    n_tok = bsz * t_len
    tiles_per_seq = max(t_len // r_blk, 1)

    xmap = lambda i: (i // tiles_per_seq, i % tiles_per_seq, 0)
    modmap = lambda s: (lambda i: (i // tiles_per_seq, 0, s))
    row = lambda i: (i, 0)
    const2 = lambda i: (0, 0)
    return pl.pallas_call(
        _outproj_kernel,
        grid=(n_tok // tm,),
        in_specs=[
            pl.BlockSpec((tm, mix_a), row),
            pl.BlockSpec((tm, mix_b), row),
            pl.BlockSpec((g_blk, r_blk, d), xmap),
            pl.BlockSpec((g_blk, 1, d), modmap(2)),
            pl.BlockSpec((g_blk, 1, d), modmap(4)),
            pl.BlockSpec((g_blk, 1, d), modmap(3)),
            pl.BlockSpec((1, d), const2),
            pl.BlockSpec(memory_space=pl.ANY),
            pl.BlockSpec((d, LANES), const2),
            pl.BlockSpec((1, LANES), const2),
            pl.BlockSpec((tm // ROUTE_PARTS, tm // ROUTE_PARTS), const2),
            pl.BlockSpec((SUBLANES, LANES), const2),
        ],
        out_specs=[
            pl.BlockSpec((g_blk, r_blk, d), xmap),
            pl.BlockSpec((tm, d), row),
            pl.BlockSpec((tm, LANES), row),
            pl.BlockSpec((tm, LANES), row),
            pl.BlockSpec((tm, LANES), row),
            pl.BlockSpec((SUBLANES, LANES), const2),
        ],
        out_shape=[
            jax.ShapeDtypeStruct((bsz, t_len, d), F32),
            jax.ShapeDtypeStruct((n_tok, d), F32),
            jax.ShapeDtypeStruct((n_tok, LANES), I32),
            jax.ShapeDtypeStruct((n_tok, LANES), F32),
            jax.ShapeDtypeStruct((n_tok, LANES), I32),
            jax.ShapeDtypeStruct((SUBLANES, LANES), F32),
        ],
        scratch_shapes=[pltpu.VMEM((d, d), BF16), pltpu.VMEM((WEIGHT_STAGE_ROWS, d), F32),
                        pltpu.SemaphoreType.DMA(()), pltpu.VMEM((SUBLANES, LANES), F32)],
        compiler_params=_cparams(("arbitrary",)),
        name="out_proj_router",
    )(o3.reshape(n_tok, mix_a), y3.reshape(n_tok, mix_b), x3, mod3, mod3, mod3,
      norm_g.reshape(1, d), w_out, w_router, b_router, _strict_lower(tm // ROUTE_PARTS), cnt0)


def _positions_kernel(ids_ref, rank_ref, cnt_ref, pos_ref, te_ref, nt_ref, lt_ref, *, block):
    lane8 = lax.broadcasted_iota(I32, (SUBLANES, LANES), 1)
    cnt = cnt_ref[...]
    tiles = jnp.floor((cnt + (MOE_TM - 1)) * (1.0 / MOE_TM))
    tiles = jnp.where(lane8 < N_EXPERTS, tiles, 0.0)
    incl = tiles
    shift = 1
    while shift < N_EXPERTS:
        incl = incl + jnp.where(lane8 >= shift, pltpu.roll(incl, shift, axis=1), 0.0)
        shift *= 2
    offs = ((incl - tiles) * MOE_TM)[0:1, :]
    nt_ref[...] = jnp.sum(tiles, axis=1, keepdims=True).astype(I32) + jnp.zeros(nt_ref.shape, I32)
    lt_ref[...] = jnp.where(tiles > 0.0, incl - 1.0, -1.0).astype(I32)

    incl_col = jnp.broadcast_to(incl[0:1, :], (LANES, LANES)).T
    e_row = lax.broadcasted_iota(I32, (LANES, LANES), 0)
    i_lane = lax.broadcasted_iota(I32, (LANES, LANES), 1).astype(F32)
    done = jnp.where((incl_col <= i_lane) & (e_row < N_EXPERTS), 1.0, 0.0)
    te = jnp.minimum(jnp.sum(done, axis=0, keepdims=True), N_EXPERTS - 1.0)
    te_ref[...] = te.astype(I32) + jnp.zeros(te_ref.shape, I32)

    lane = lax.broadcasted_iota(I32, (block, LANES), 1)

    def body(bi, c):
        r0 = pl.multiple_of(bi * block, block)
        ids = ids_ref[pl.ds(r0, block), :]
        rank = rank_ref[pl.ds(r0, block), :]
        out = jnp.zeros((block, LANES), I32)
        for k in range(TOP_K):
            off = jnp.sum(jnp.where(lane == ids[:, k:k + 1], offs, 0.0), axis=1, keepdims=True)
            out = jnp.where(lane == k, off.astype(I32) + rank[:, k:k + 1], out)
        pos_ref[pl.ds(r0, block), :] = out
        return c

    lax.fori_loop(0, ids_ref.shape[0] // block, body, 0)


def _positions(ids, rank, cnt):
    n_tok = ids.shape[0]
    block = 512
    kern = functools.partial(_positions_kernel, block=block)
    full = lambda s: pl.BlockSpec(s, lambda: tuple(0 for _ in s))
    return pl.pallas_call(
        kern,
        in_specs=[full((n_tok, LANES)), full((n_tok, LANES)), full((SUBLANES, LANES))],
        out_specs=[full((n_tok, LANES))] + [full((SUBLANES, LANES))] * 3,
        out_shape=[jax.ShapeDtypeStruct((n_tok, LANES), I32)]
        + [jax.ShapeDtypeStruct((SUBLANES, LANES), I32)] * 3,
        compiler_params=pltpu.CompilerParams(vmem_limit_bytes=VMEM_LIMIT),
        name="route_positions",
    )(ids, rank, cnt)


def _row_copy(src_ref, src_row, dst_ref, dst_row, sem):
    return pltpu.make_async_copy(src_ref.at[pl.ds(src_row, 1), :], dst_ref.at[pl.ds(dst_row, 1), :],
                                 sem)


def _dispatch_kernel(pos_ref, lt_ref, nt_ref, *refs, steps, n_tiles):
    n_groups = len(steps)
    hn_refs = refs[:n_groups]
    xg_ref, sem, zero_ref, zsem = refs[n_groups:]
    rows = hn_refs[0].shape[0]
    i = pl.program_id(0)

    def zero_copy(tile):
        return pltpu.make_async_copy(zero_ref, xg_ref.at[pl.ds(tile * MOE_TM, MOE_TM), :], zsem)

    @pl.when(i == 0)
    def _():
        zero_ref[...] = jnp.zeros(zero_ref.shape, zero_ref.dtype)
        nt = nt_ref[0]
        for e in range(N_EXPERTS):
            @pl.when(lt_ref[e] >= 0)
            def _():
                zero_copy(jnp.maximum(lt_ref[e], 0)).start(priority=e % 2)
        lax.fori_loop(nt, n_tiles, lambda t, c: (zero_copy(t).start(priority=1), c)[1], 0)
        for e in range(N_EXPERTS):
            @pl.when(lt_ref[e] >= 0)
            def _():
                zero_copy(0).wait()
        lax.fori_loop(nt, n_tiles, lambda t, c: (zero_copy(0).wait(), c)[1], 0)

    first = 0
    for hn_ref, n_steps in zip(hn_refs, steps):
        @pl.when((i >= first) & (i < first + n_steps))
        def _(hn_ref=hn_ref):
            base = i * rows

            def issue(r, c):
                for k in range(TOP_K):
                    _row_copy(hn_ref, r, xg_ref, pos_ref[(base + r) * TOP_K + k], sem).start(
                        priority=k)
                return c

            def drain(r, c):
                for k in range(TOP_K):
                    _row_copy(hn_ref, r, xg_ref, 0, sem).wait()
                return c

            lax.fori_loop(0, rows, issue, 0, unroll=8)
            lax.fori_loop(0, rows, drain, 0, unroll=8)

        first += n_steps


def _dispatch(pos_flat, last_tile, n_live, hns, n_tiles):
    d = hns[0].shape[1]
    steps = tuple(h.shape[0] // ROW_DMA_TILE for h in hns)
    firsts = [sum(steps[:g]) for g in range(len(steps))]

    def group_map(g):
        return lambda i, pos, lt, nt: (jnp.clip(i - firsts[g], 0, steps[g] - 1), 0)

    return pl.pallas_call(
        functools.partial(_dispatch_kernel, steps=steps, n_tiles=n_tiles),
        grid_spec=pltpu.PrefetchScalarGridSpec(
            num_scalar_prefetch=3,
            grid=(sum(steps),),
            in_specs=[pl.BlockSpec((ROW_DMA_TILE, d), group_map(g)) for g in range(len(hns))],
            out_specs=pl.BlockSpec(memory_space=pl.ANY),
            scratch_shapes=[pltpu.SemaphoreType.DMA(()), pltpu.VMEM((MOE_TM, d), F32),
                            pltpu.SemaphoreType.DMA(())],
        ),
        out_shape=jax.ShapeDtypeStruct((n_tiles * MOE_TM, d), F32),
        compiler_params=_cparams(("arbitrary",)),
        name="moe_dispatch",
    )(pos_flat, last_tile, n_live, *hns)


WEIGHT_PARTS = 4


def _weight_copies(w_hbm_refs, expert, wbuf_ref, slot, sem_ref):
    copies = []
    for i, w_ref in enumerate(w_hbm_refs):
        part_rows = w_ref.shape[1] // WEIGHT_PARTS
        for part in range(WEIGHT_PARTS):
            rows = pl.ds(part * part_rows, part_rows)
            copies.append(pltpu.make_async_copy(w_ref.at[expert, rows], wbuf_ref.at[slot, i, rows],
                                                sem_ref.at[slot]))
    return copies


def _start_weight_copies(copies, queue0_parts):
    for k, cp in enumerate(copies):
        cp.start(priority=0 if k % WEIGHT_PARTS < queue0_parts else 1)


def _expert_weights_slot(te_ref, nt, t, w_hbm_refs, wbuf_ref, sem_ref, slot_ref, queue0_parts):
    expert = te_ref[t]

    @pl.when(t == 0)
    def _():
        slot_ref[0] = 0
        _start_weight_copies(_weight_copies(w_hbm_refs, expert, wbuf_ref, 0, sem_ref),
                             queue0_parts)

    @pl.when((t == 0) | (expert != te_ref[jnp.maximum(t - 1, 0)]))
    def _():
        @pl.when(t > 0)
        def _():
            slot_ref[0] = 1 - slot_ref[0]

        slot = slot_ref[0]
        for cp in _weight_copies(w_hbm_refs, expert, wbuf_ref, slot, sem_ref):
            cp.wait()
        nxt = lax.while_loop(lambda u: (u < nt) & (te_ref[jnp.minimum(u, nt - 1)] == expert),
                             lambda u: u + 1, t + 1)

        @pl.when(nxt < nt)
        def _():
            _start_weight_copies(_weight_copies(w_hbm_refs, te_ref[jnp.minimum(nxt, nt - 1)],
                                                wbuf_ref, 1 - slot, sem_ref), queue0_parts)

    return slot_ref[0]


def _moe_up_kernel(te_ref, nt_ref, x_ref, w1_ref, w3_ref, act_ref, wbuf_ref, sem_ref, slot_ref):
    t = pl.program_id(0)
    nt = nt_ref[0]

    @pl.when(t < nt)
    def _():
        slot = _expert_weights_slot(te_ref, nt, t, (w1_ref, w3_ref), wbuf_ref, sem_ref, slot_ref,
                                    queue0_parts=1)
        xb = x_ref[...].astype(BF16)
        a = _dot(xb, wbuf_ref[slot, 0].astype(BF16))
        b = _dot(xb, wbuf_ref[slot, 1].astype(BF16))
        act_ref[...] = (_silu(a) * b).astype(BF16)

    @pl.when(t >= nt)
    def _():
        act_ref[...] = jnp.zeros(act_ref.shape, act_ref.dtype)


def _moe_down_kernel(te_ref, nt_ref, act_ref, w2_ref, out_ref, wbuf_ref, sem_ref, slot_ref):
    t = pl.program_id(0)
    nt = nt_ref[0]

    @pl.when(t < nt)
    def _():
        slot = _expert_weights_slot(te_ref, nt, t, (w2_ref,), wbuf_ref, sem_ref, slot_ref,
                                    queue0_parts=0)
        out_ref[...] = _dot(act_ref[...], wbuf_ref[slot, 0].astype(BF16))

    @pl.when(t >= nt)
    def _():
        out_ref[...] = jnp.zeros(out_ref.shape, out_ref.dtype)


def _moe_experts(te, nt, xg, w1, w3, w2):
    n_rows, d = xg.shape
    d_exp = w1.shape[-1]
    n_tiles = n_rows // MOE_TM
    live_tile = lambda t, te_r, nt_r: (jnp.minimum(t, nt_r[0] - 1), 0)
    any_spec = pl.BlockSpec(memory_space=pl.ANY)

    act = pl.pallas_call(
        _moe_up_kernel,
        grid_spec=pltpu.PrefetchScalarGridSpec(
            num_scalar_prefetch=2,
            grid=(n_tiles,),
            in_specs=[pl.BlockSpec((MOE_TM, d), live_tile), any_spec, any_spec],
            out_specs=pl.BlockSpec((MOE_TM, d_exp), lambda t, te_r, nt_r: (t, 0)),
            scratch_shapes=[pltpu.VMEM((2, 2, d, d_exp), F32), pltpu.SemaphoreType.DMA((2,)),
                            pltpu.SMEM((1,), I32)],
        ),
        out_shape=jax.ShapeDtypeStruct((n_rows, d_exp), BF16),
        compiler_params=_cparams(("arbitrary",)),
        name="moe_up",
    )(te, nt, xg, w1, w3)

    return pl.pallas_call(
        _moe_down_kernel,
        grid_spec=pltpu.PrefetchScalarGridSpec(
            num_scalar_prefetch=2,
            grid=(n_tiles,),
            in_specs=[pl.BlockSpec((MOE_TM, d_exp), live_tile), any_spec],
            out_specs=pl.BlockSpec((MOE_TM, d), lambda t, te_r, nt_r: (t, 0)),
            scratch_shapes=[pltpu.VMEM((2, 1, d_exp, d), F32), pltpu.SemaphoreType.DMA((2,)),
                            pltpu.SMEM((1,), I32)],
        ),
        out_shape=jax.ShapeDtypeStruct((n_rows, d), F32),
        compiler_params=_cparams(("arbitrary",)),
        name="moe_down",
    )(te, nt, act, w2)


def _combine_kernel(pos_ref, x1_ref, g2_ref, wts_ref, fg_ref, eo_ref, y_ref, buf_ref, sem):
    rows = wts_ref.shape[0]
    base = pl.program_id(0) * rows

    def issue(r, c):
        for k in range(TOP_K):
            _row_copy(eo_ref, pos_ref[(base + r) * TOP_K + k], buf_ref.at[k], r, sem).start(
                priority=k)
        return c

    def drain(r, c):
        for k in range(TOP_K):
            _row_copy(eo_ref, 0, buf_ref.at[k], r, sem).wait()
        return c

    lax.fori_loop(0, rows, issue, 0, unroll=8)
    lax.fori_loop(0, rows, drain, 0, unroll=8)
    w = wts_ref[...]
    moe = w[:, 0:1] * buf_ref[0] + w[:, 1:2] * buf_ref[1]
    x2 = x1_ref[...] + g2_ref[...] * moe.reshape(x1_ref.shape)
    ms = jnp.mean(x2 * x2, axis=-1, keepdims=True)
    y_ref[...] = x2 * lax.rsqrt(ms + EPS) * fg_ref[...]


def _combine(pos_flat, x1, mod3, wts, final_g, expert_out, group_rows):
    bsz, t_len, d = x1.shape
    g_blk, r_blk = group_rows
    tm = g_blk * r_blk
    tiles_per_seq = max(t_len // r_blk, 1)

    def xmap(i, pos):
        return (i // tiles_per_seq, i % tiles_per_seq, 0)

    return pl.pallas_call(
        _combine_kernel,
        grid_spec=pltpu.PrefetchScalarGridSpec(
            num_scalar_prefetch=1,
            grid=((bsz * t_len) // tm,),
            in_specs=[
                pl.BlockSpec((g_blk, r_blk, d), xmap),
                pl.BlockSpec((g_blk, 1, d), lambda i, pos: (i // tiles_per_seq, 0, 5)),
                pl.BlockSpec((tm, LANES), lambda i, pos: (i, 0)),
                pl.BlockSpec((1, d), lambda i, pos: (0, 0)),
                pl.BlockSpec(memory_space=pl.ANY),
            ],
            out_specs=pl.BlockSpec((g_blk, r_blk, d), xmap),
            scratch_shapes=[pltpu.VMEM((TOP_K, tm, d), F32), pltpu.SemaphoreType.DMA(())],
        ),
        out_shape=jax.ShapeDtypeStruct((bsz, t_len, d), F32),
        compiler_params=_cparams(("arbitrary",)),
        name="moe_combine",
    )(pos_flat, x1, mod3, wts, final_g.reshape(1, d), expert_out)


def kernel(x_prompt, x_sample, state_hgrn, state_ssm, state_conv, c_prompt, c_sample, ada_w, ada_b,
           norm1_g, norm2_g, w_in, hgrn_lb, hgrn_onorm_g, conv_w, conv_b, dt_bias, a_log, d_skip,
           ssm_norm_g, w_out, w_grp, b_grp, w_rt, b_rt, w1, w3, w2, final_g):
    depth = w_in.shape[0]
    assert depth == 1, "single-layer trunk"
    layer = 0
    bp, t_p, d = x_prompt.shape
    bs, t_s, _ = x_sample.shape
    mix_a = hgrn_onorm_g.shape[1]
    mix_b = ssm_norm_g.shape[1]
    conv_dim = conv_w.shape[2]
    n_main = 4 * mix_a + mix_b + conv_dim
    n_ssd_heads = dt_bias.shape[1]
    n_pairs = mix_b // LANES
    xbc0 = 4 * mix_a + mix_b
    tail = CONV_WIDTH - 1
    assert t_s <= SUBLANES and t_s >= tail and t_p % PROMPT_CHUNK == 0

    xs_pad = jnp.pad(x_sample, ((0, 0), (0, SUBLANES - t_s), (0, 0)))
    n_c = bp + bs
    c_rows = -(-n_c // SUBLANES) * SUBLANES
    c_all = jnp.pad(jnp.concatenate([c_prompt, c_sample], axis=0), ((0, c_rows - n_c), (0, 0)))

    mod = _modulation(c_all, ada_w[layer], ada_b[layer])
    mod_p = mod[:bp].reshape(bp, 1, -1)
    mod_s = mod[bp:n_c].reshape(bs, 1, -1)
    w_in_t = jnp.swapaxes(w_in[layer], 0, 1)
    w_dt_t = jnp.pad(w_in_t[n_main:], ((0, LANES - n_ssd_heads), (0, 0)))
    w_router = jnp.pad(jnp.concatenate([w_rt[layer], w_grp[layer]], axis=1),
                       ((0, 0), (0, LANES - N_EXPERTS - N_EXPERT_GROUPS)))
    b_router = jnp.pad(jnp.concatenate([b_rt[layer], b_grp[layer]]),
                       (0, LANES - N_EXPERTS - N_EXPERT_GROUPS)).reshape(1, LANES)

    groups = (
        (x_prompt, mod_p, None, None, None, t_p, PROMPT_CHUNK, PROMPT_CHUNK, MIXER_ROWS,
         (LANES // PROMPT_CHUNK, 8), LANES // PROMPT_CHUNK, 1, (1, OUT_TM)),
        (xs_pad, mod_s, state_hgrn[layer], state_ssm[layer].reshape(bs, n_pairs, LANES, LANES),
         state_conv[layer], t_s, SUBLANES, t_s, SUBLANES,
         (LANES // SUBLANES, 4), LANES // SUBLANES, PROJ_SEQS, (OUT_TM // SUBLANES, SUBLANES)),
    )

    cnt = jnp.zeros((SUBLANES, LANES), F32)
    per_group = []
    for x3, mod3, s0, h0, c0, t_real, ch, valid, tt, (hg_gb, hg_hb), ssd_gb, proj_gr, out_gr in groups:
        bsz, t_len, _ = x3.shape
        proj, dt = _in_projection(x3, mod3, norm1_g[layer], w_in_t, w_dt_t, n_main, proj_gr)
        proj3 = proj.reshape(bsz, t_len, n_main)
        dt3 = dt.reshape(bsz, t_len, LANES)
        o3, s_new = _hgrn_heads(proj3, hgrn_lb, hgrn_onorm_g[layer], s0, ch=ch, valid=valid,
                                gb=hg_gb, hb=hg_hb, tt=tt, layer=layer)
        y3, h_new = _ssd_pairs(proj3, dt3, conv_w[layer], conv_b[layer], dt_bias[layer],
                               a_log[layer], d_skip[layer], ssm_norm_g[layer], h0, c0, mix_a=mix_a,
                               mix_b=mix_b, ch=ch, valid=valid, gb=ssd_gb, tt=tt)
        conv_new = proj3[:, t_real - tail:t_real, xbc0:xbc0 + conv_dim]
        x1, hn, ids, wts, rank, cnt = _out_projection(
            o3, y3, x3, mod3, norm2_g[layer], w_out[layer], w_router, b_router, cnt, out_gr)
        per_group.append(dict(x1=x1, hn=hn, ids=ids, wts=wts, rank=rank, mod3=mod3, out_gr=out_gr,
                              s_new=s_new, h_new=h_new, conv_new=conv_new))

    ids_all = jnp.concatenate([g["ids"] for g in per_group], axis=0)
    rank_all = jnp.concatenate([g["rank"] for g in per_group], axis=0)
    n_tok = ids_all.shape[0]
    pos, te, nt, lt = _positions(ids_all, rank_all, cnt)
    pos_flat = pos[:, :TOP_K].reshape(-1)
    n_tiles = -(-(n_tok * TOP_K + N_EXPERTS * (MOE_TM - 1)) // MOE_TM)
    assert n_tiles <= LANES
    te_vec = te[0, :n_tiles]
    nt_vec = nt[0, :1]
    lt_vec = lt[0, :N_EXPERTS]

    row0 = 0
    for g in per_group:
        rows = g["hn"].shape[0]
        g["pos"] = pos_flat[row0 * TOP_K:(row0 + rows) * TOP_K]
        row0 += rows
    xg = _dispatch(pos_flat, lt_vec, nt_vec, [g["hn"] for g in per_group], n_tiles)
    expert_out = _moe_experts(te_vec, nt_vec, xg, w1[layer], w3[layer], w2[layer])
    ys = [_combine(g["pos"], g["x1"], g["mod3"], g["wts"], final_g, expert_out, g["out_gr"])
          for g in per_group]

    gp, gs = per_group
    return (
        ys[0],
        ys[1][:, :t_s],
        gp["s_new"][None],
        gp["h_new"].reshape(1, bp, n_ssd_heads, SSD_HEAD_DIM, SSD_STATE),
        gp["conv_new"][None],
        gs["s_new"][None],
        gs["h_new"].reshape(1, bs, n_ssd_heads, SSD_HEAD_DIM, SSD_STATE),
        gs["conv_new"][None],
    )
```

```python
import functools

import jax
import jax.numpy as jnp
from jax import lax
from jax.experimental import pallas as pl
from jax.experimental.pallas import tpu as pltpu

F32 = jnp.float32
BF16 = jnp.bfloat16
I32 = jnp.int32
EPS = 1e-6
LOG2E = 1.4426950408889634
MAX_LOG2_DECAY = 12

LANES = 128
SUBLANES = 8
VMEM_LIMIT = 56 * 1024 * 1024

HGRN_HEAD_DIM = 128
SSD_HEAD_DIM = 64
SSD_GROUPS = 2
SSD_STATE = 128
CONV_WIDTH = 4
N_EXPERT_GROUPS = 4
EXPERTS_PER_GROUP = 8
N_EXPERTS = N_EXPERT_GROUPS * EXPERTS_PER_GROUP
TOP_K = 2

PROJ_SEQS = 128
NORM_ROWS = 256
PROJ_TN = 512
MOE_TM = 256
PROMPT_CHUNK = 64
MIXER_ROWS = 512
ROW_DMA_TILE = 256


def _cparams(sem):
    return pltpu.CompilerParams(dimension_semantics=sem, vmem_limit_bytes=VMEM_LIMIT)


def _dot(a, b):
    return jnp.dot(a, b, preferred_element_type=F32)


def _dot_nt(a, b):
    return lax.dot_general(a, b, (((1,), (1,)), ((), ())), preferred_element_type=F32)


def _split2(x):
    hi = x.astype(BF16)
    lo = (x - hi.astype(F32)).astype(BF16)
    return hi, lo


def _split3(x):
    hi = x.astype(BF16)
    r = x - hi.astype(F32)
    mid = r.astype(BF16)
    lo = (r - mid.astype(F32)).astype(BF16)
    return hi, mid, lo


def _dot_exact_lhs(m_bf16, x):
    hi, mid, lo = _split3(x)
    return _dot(m_bf16, hi) + _dot(m_bf16, mid) + _dot(m_bf16, lo)


def _dot_exact_rhs(x, m_bf16):
    hi, mid, lo = _split3(x)
    return _dot(hi, m_bf16) + _dot(mid, m_bf16) + _dot(lo, m_bf16)


def _dot_hp(a, w):
    ah, al = _split2(a)
    wh, wl = _split2(w)
    return _dot(ah, wh) + _dot(al, wh) + _dot(ah, wl)


def _silu(x):
    return x * jax.nn.sigmoid(x)


def _mod_kernel(c_ref, w_ref, b_ref, o_ref):
    a = _silu(c_ref[...])
    res = _dot_hp(a, w_ref[...]) + b_ref[...]
    o_ref[...] = res.reshape(o_ref.shape)


def _modulation(c_all, ada_w, ada_b):
    rows, d = c_all.shape
    n_out = ada_w.shape[1]
    tn = 1024
    return pl.pallas_call(
        _mod_kernel,
        grid=(n_out // tn,),
        in_specs=[
            pl.BlockSpec((rows, d), lambda j: (0, 0)),
            pl.BlockSpec((d, tn), lambda j: (0, j)),
            pl.BlockSpec((1, tn), lambda j: (0, j)),
        ],
        out_specs=pl.BlockSpec((rows, 1, tn), lambda j: (0, 0, j)),
        out_shape=jax.ShapeDtypeStruct((rows, 1, n_out), F32),
        compiler_params=_cparams(("arbitrary",)),
        name="adaln_mod",
    )(c_all, ada_w, ada_b.reshape(1, n_out))


def _proj_kernel(x_hbm_ref, sc_ref, sh_ref, g_ref, w_ref, wdt_ref, o_ref, dt_ref, hn_ref, xs_ref,
                 sem):
    i = pl.program_id(0)
    j = pl.program_id(1)
    g_blk, r_blk, d = xs_ref.shape

    def x_copy(tile):
        return pltpu.make_async_copy(x_hbm_ref.at[pl.ds(tile * g_blk, g_blk)], xs_ref, sem)

    @pl.when(j == 0)
    def _():
        @pl.when(i == 0)
        def _():
            x_copy(0).start()

        x_copy(i).wait()
        wh, wl = _split2(wdt_ref[...])
        n_rows = g_blk * r_blk
        step = min(n_rows, NORM_ROWS)
        for c in range(n_rows // step):
            if g_blk == 1:
                x = xs_ref[:, c * step:(c + 1) * step, :]
                sc, sh = sc_ref[...], sh_ref[...]
            else:
                gs = slice(c * step // r_blk, (c + 1) * step // r_blk)
                x, sc, sh = xs_ref[gs], sc_ref[gs], sh_ref[gs]
            ms = jnp.mean(x * x, axis=-1, keepdims=True)
            hn = ((x * lax.rsqrt(ms + EPS)) * g_ref[...] * (1.0 + sc) + sh).reshape(step, d)
            rows = slice(c * step, (c + 1) * step)
            hn_ref[rows, :] = hn.astype(BF16)
            ah, al = _split2(hn)
            dt_ref[rows, :] = _dot_nt(ah, wh) + _dot_nt(al, wh) + _dot_nt(ah, wl)

        @pl.when(i + 1 < pl.num_programs(0))
        def _():
            x_copy(i + 1).start()

    o_ref[...] = _dot_nt(hn_ref[...], w_ref[...].astype(BF16))


def _in_projection(x3, mod3, norm_g, w_t, w_dt_t, n_main, g_blk):
    bsz, t_len, d = x3.shape
    tm = g_blk * t_len
    n_row_tiles = bsz // g_blk

    def mod_map(sec):
        return lambda i, j: (i, 0, sec)

    return pl.pallas_call(
        _proj_kernel,
        grid=(n_row_tiles, n_main // PROJ_TN),
        in_specs=[
            pl.BlockSpec(memory_space=pl.ANY),
            pl.BlockSpec((g_blk, 1, d), mod_map(1)),
            pl.BlockSpec((g_blk, 1, d), mod_map(0)),
            pl.BlockSpec((1, d), lambda i, j: (0, 0)),
            pl.BlockSpec((PROJ_TN, d), lambda i, j: (j, 0)),
            pl.BlockSpec((LANES, d), lambda i, j: (0, 0)),
        ],
        out_specs=[
            pl.BlockSpec((tm, PROJ_TN), lambda i, j: (i, j)),
            pl.BlockSpec((tm, LANES), lambda i, j: (i, 0)),
        ],
        out_shape=[
            jax.ShapeDtypeStruct((bsz * t_len, n_main), F32),
            jax.ShapeDtypeStruct((bsz * t_len, LANES), F32),
        ],
        scratch_shapes=[pltpu.VMEM((tm, d), BF16), pltpu.VMEM((g_blk, t_len, d), F32),
                        pltpu.SemaphoreType.DMA(())],
        compiler_params=_cparams(("arbitrary", "arbitrary")),
        name="in_proj",
    )(x3, mod3, mod3, norm_g.reshape(1, d), w_t, w_dt_t)


def _level_sizes(ch):
    sizes = []
    sz = SUBLANES
    while 2 * sz <= ch:
        sizes.append(sz)
        sz *= 2
    return sizes


def _level_masks(ch, n_rows):
    import numpy as np
    t = np.arange(n_rows)[:, None]
    s = np.arange(n_rows)[None, :]
    out = []
    for sz in _level_sizes(ch):
        m = (t // (2 * sz) == s // (2 * sz)) & (t % (2 * sz) >= sz) & (s % (2 * sz) < sz)
        out.append(m.astype(np.float32))
    out.append(((t // SUBLANES == s // SUBLANES) & (s <= t)).astype(np.float32))
    return jnp.asarray(np.stack(out))


def _hgrn_kernel(*refs, ch, valid, nc, gb, hb, has_init, layer):
    if has_init:
        (q_ref, f_ref, i_ref, g_ref, lb_ref, on_ref, tri_ref, masks_ref, s0_ref,
         o_ref, sout_ref, st_ref, b_scr, k_scr) = refs
    else:
        (q_ref, f_ref, i_ref, g_ref, lb_ref, on_ref, tri_ref, masks_ref,
         o_ref, sout_ref, st_ref, b_scr, k_scr) = refs
        s0_ref = None

    n_lb = lb_ref.shape[0]
    lb_rows = [lb_ref[i:i + 1, :] for i in range(n_lb)]
    lb_max = functools.reduce(jnp.maximum, lb_rows)
    lb_exp = [jnp.exp(r - lb_max) for r in lb_rows]
    lb = sum(lb_exp[:layer + 1]) / sum(lb_exp)
    onorm = on_ref[...]
    tri = tri_ref[...]
    sizes = _level_sizes(ch)
    lane8 = lax.broadcasted_iota(I32, (SUBLANES, LANES), 1)
    row8 = lax.broadcasted_iota(I32, (SUBLANES, LANES), 0)
    tb = pl.program_id(2)

    n_rows = gb * ch
    row_id = lax.broadcasted_iota(I32, (n_rows, LANES), 0)

    def stack(fn):
        parts = [fn(g) for g in range(gb)]
        return parts[0] if gb == 1 else jnp.concatenate(parts, axis=0)

    def chunk(hh, ci, bounded):
        cols = slice(hh * LANES, (hh + 1) * LANES)
        t0 = pl.multiple_of(ci * ch, ch)
        rows = pl.ds(t0, ch)
        lb_h = lb[:, cols]
        q = _silu(stack(lambda g: q_ref[g, rows, cols]))
        fg = lb_h + (1.0 - lb_h) * jax.nn.sigmoid(stack(lambda g: f_ref[g, rows, cols]))
        k = 1.0 - fg
        lf = jnp.log(fg)
        v = stack(lambda g: i_ref[g, rows, cols])
        if valid < ch:
            live = (row_id & (ch - 1)) < valid
            lf = jnp.where(live, lf, 0.0)
            k = jnp.where(live, k, 0.0)
        b2 = _dot_exact_lhs(tri, lf) * LOG2E
        b_scr[hh] = b2
        k_scr[hh] = k
        b_end_rows = [b_scr[hh, pl.ds(g * ch + ch - 1, 1), :] for g in range(gb)]
        b_end = stack(lambda g: jnp.broadcast_to(b_end_rows[g], (ch, LANES)))

        qe = (q * jnp.exp2(b2)).astype(BF16)
        o = stack(lambda g: _dot_nt(qe[g * ch:(g + 1) * ch], st_ref[g * hb + hh].astype(BF16)))

        scores = jnp.zeros((n_rows, n_rows), F32)
        for lvl, sz in enumerate(sizes):
            pieces = []
            for m in range(n_rows // (2 * sz)):
                r = b_scr[hh, pl.ds(2 * sz * m + sz - 1, 1), :]
                pieces.append(jnp.broadcast_to(r, (2 * sz, LANES)))
            r_all = pieces[0] if len(pieces) == 1 else jnp.concatenate(pieces, axis=0)
            e = jnp.exp2(-jnp.abs(b2 - r_all))
            s_l = _dot_nt((q * e).astype(BF16), (k * e).astype(BF16))
            scores = scores + masks_ref[lvl] * s_l
        if bounded:
            pieces = []
            for jb in range(n_rows // SUBLANES):
                if (jb * SUBLANES) % ch == 0:
                    pieces.append(jnp.zeros((SUBLANES, LANES), F32))
                else:
                    pieces.append(jnp.broadcast_to(b_scr[hh, pl.ds(jb * SUBLANES - 1, 1), :],
                                                   (SUBLANES, LANES)))
            r8 = jnp.concatenate(pieces, axis=0)
            s_d = _dot_nt((q * jnp.exp2(b2 - r8)).astype(BF16),
                          (k * jnp.exp2(r8 - b2)).astype(BF16))
            scores = scores + jnp.where(masks_ref[len(sizes)] > 0.0, s_d, 0.0)
        else:
            blocks = []
            for jb in range(n_rows // SUBLANES):
                bb = b2[SUBLANES * jb:SUBLANES * (jb + 1)]
                qb = q[SUBLANES * jb:SUBLANES * (jb + 1)]
                acc = jnp.zeros((SUBLANES, LANES), F32)
                for sl in range(SUBLANES):
                    s = SUBLANES * jb + sl
                    bs = b_scr[hh, pl.ds(s, 1), :]
                    ks = k_scr[hh, pl.ds(s, 1), :]
                    val = (qb * ks) * jnp.exp2(bb - bs)
                    red = jnp.sum(val, axis=1, keepdims=True)
                    acc = jnp.where(lane8 == s, red, acc)
                blocks.append(jnp.where(lane8 - SUBLANES * jb <= row8, acc, 0.0))
            scores = scores + jnp.concatenate(blocks, axis=0)
        o = o + _dot(scores.astype(BF16), v.astype(BF16))

        dk = k * jnp.exp2(b_end - b2)
        v_t = v.T.astype(BF16)
        for g in range(gb):
            own = (row_id >= g * ch) & (row_id < (g + 1) * ch)
            c = g * hb + hh
            st_ref[c] = (st_ref[c] * jnp.exp2(b_end_rows[g])
                         + _dot(v_t, jnp.where(own, dk, 0.0).astype(BF16)))

        on = o * lax.rsqrt(jnp.mean(o * o, axis=-1, keepdims=True) + EPS) * onorm[:, cols]
        out = on * _silu(stack(lambda g: g_ref[g, rows, cols]))
        for g in range(gb):
            o_ref[g, rows, cols] = out[g * ch:(g + 1) * ch].astype(o_ref.dtype)

    chains = [(g * hb + hh, g, hh) for g in range(gb) for hh in range(hb)]

    @pl.when(tb == 0)
    def _():
        for c, g, hh in chains:
            if has_init:
                st_ref[c] = s0_ref[g, hh].T
            else:
                st_ref[c] = jnp.zeros((LANES, LANES), F32)

    def run(bounded):
        def step(ci, carry):
            for hh in range(hb):
                chunk(hh, ci, bounded)
            return carry

        if nc == 1:
            step(0, 0)
        else:
            lax.fori_loop(0, nc, step, 0, unroll=2)

    if ch > SUBLANES:
        gates_bounded = jnp.min(lb) >= 2.0 ** -MAX_LOG2_DECAY
        lax.cond(gates_bounded, lambda: run(True), lambda: run(False))
    else:
        run(False)

    @pl.when(tb == pl.num_programs(2) - 1)
    def _():
        for c, g, hh in chains:
            sout_ref[g, hh] = st_ref[c].T


def _hgrn_heads(proj3, hgrn_lb, onorm_g, s0, *, ch, valid, gb, hb, tt, layer):
    bsz, t_len, _ = proj3.shape
    n_heads = onorm_g.shape[0] // HGRN_HEAD_DIM
    n_hg = n_heads // hb
    nc = tt // ch
    n_rows = gb * ch
    assert n_rows == LANES
    has_init = s0 is not None
    w = hb * LANES

    def col(sec):
        return lambda b, h, t: (b, t, sec * n_hg + h)

    in_specs = [pl.BlockSpec((gb, tt, w), col(s)) for s in range(4)]
    in_specs += [
        pl.BlockSpec((hgrn_lb.shape[0], w), lambda b, h, t: (0, h)),
        pl.BlockSpec((1, w), lambda b, h, t: (0, h)),
        pl.BlockSpec((n_rows, n_rows), lambda b, h, t: (0, 0)),
        pl.BlockSpec((len(_level_sizes(ch)) + 1, n_rows, n_rows), lambda b, h, t: (0, 0, 0)),
    ]
    args = [proj3, proj3, proj3, proj3, hgrn_lb, onorm_g.reshape(1, -1),
            jnp.asarray(_block_tri(gb, ch), dtype=BF16), _level_masks(ch, n_rows)]
    if has_init:
        in_specs.append(pl.BlockSpec((gb, hb, LANES, LANES), lambda b, h, t: (b, h, 0, 0)))
        args.append(s0)
    kern = functools.partial(_hgrn_kernel, ch=ch, valid=valid, nc=nc, gb=gb, hb=hb,
                             has_init=has_init, layer=layer)
    n_chain = gb * hb
    return pl.pallas_call(
        kern,
        grid=(bsz // gb, n_hg, t_len // tt),
        in_specs=in_specs,
        out_specs=[
            pl.BlockSpec((gb, tt, w), lambda b, h, t: (b, t, h)),
            pl.BlockSpec((gb, hb, LANES, LANES), lambda b, h, t: (b, h, 0, 0)),
        ],
        out_shape=[
            jax.ShapeDtypeStruct((bsz, t_len, n_heads * HGRN_HEAD_DIM), BF16),
            jax.ShapeDtypeStruct((bsz, n_heads, LANES, LANES), F32),
        ],
        scratch_shapes=[pltpu.VMEM((n_chain, LANES, LANES), F32),
                        pltpu.VMEM((hb, n_rows, LANES), F32),
                        pltpu.VMEM((hb, n_rows, LANES), F32)],
        compiler_params=_cparams(("arbitrary", "arbitrary", "arbitrary")),
        name="hgrn2_heads",
    )(*args)


def _softplus(x):
    return jnp.maximum(x, 0.0) + jnp.log1p(jnp.exp(-jnp.abs(x)))


def _ssd_kernel(*refs, ch, valid, nc, gb, pb, has_init):
    if has_init:
        (z_ref, x_ref, b_ref, c_ref, dt_ref, cwx_ref, cwb_ref, cwc_ref, cbx_ref, cbb_ref, cbc_ref,
         xj_ref, bias_ref, alog_ref, d_ref, ng_ref, tri_ref, tril_ref, h0_ref, cx0_ref, cb0_ref,
         cc0_ref, y_ref, hout_ref, ht_ref, wx_ref, wb_ref, wc_ref) = refs
    else:
        (z_ref, x_ref, b_ref, c_ref, dt_ref, cwx_ref, cwb_ref, cwc_ref, cbx_ref, cbb_ref, cbc_ref,
         xj_ref, bias_ref, alog_ref, d_ref, ng_ref, tri_ref, tril_ref,
         y_ref, hout_ref, ht_ref, wx_ref, wb_ref, wc_ref) = refs
        h0_ref = cx0_ref = cb0_ref = cc0_ref = None

    tri = tri_ref[...]
    tril = tril_ref[...]
    lane = lax.broadcasted_iota(I32, (LANES, LANES), 1)
    first_head = lane < SSD_HEAD_DIM
    tail = CONV_WIDTH - 1
    tb = pl.program_id(1)

    def conv(raw_ref, win_ref, cw_ref, cb_ref, g, t0):
        raw = raw_ref[g, pl.ds(t0, ch), :]
        win_ref[g, pl.ds(SUBLANES, ch), :] = raw
        ext = win_ref[g]
        u = cb_ref[...] + raw * cw_ref[tail:tail + 1, :]
        for j in range(tail):
            shifted = pltpu.roll(ext, tail - j, axis=0)[SUBLANES:SUBLANES + ch]
            u = u + shifted * cw_ref[j:j + 1, :]
        win_ref[g, pl.ds(0, SUBLANES), :] = win_ref[g, pl.ds(ch, SUBLANES), :]
        return _silu(u)

    n_rows = gb * ch
    w = pb * LANES
    n_grp = SSD_GROUPS
    row_id = lax.broadcasted_iota(I32, (n_rows, w), 0)

    def stack(fn):
        parts = [fn(g) for g in range(gb)]
        return parts[0] if gb == 1 else jnp.concatenate(parts, axis=0)

    def group(q, rows, xs, bm, cm, dt_heads, z):
        wcols = slice(q * w, (q + 1) * w)
        dt = _dot_exact_rhs(dt_heads, xj_ref[q])
        if valid < ch:
            dt = jnp.where((row_id & (ch - 1)) < valid, dt, 0.0)
        la_cs = _dot_exact_lhs(tri, dt * (-jnp.exp(alog_ref[q])))
        xdt = xs * dt
        a_end_rows = [la_cs[g * ch + ch - 1:g * ch + ch, :] for g in range(gb)]
        a_end = stack(lambda g: jnp.broadcast_to(a_end_rows[g], (ch, w)))

        cb = _dot_nt(cm.astype(BF16), bm.astype(BF16))
        acs_t = la_cs.T
        y_parts = []
        for p in range(pb):
            cols = slice(p * LANES, (p + 1) * LANES)
            scores = []
            for hl in range(2):
                at = p * LANES + hl * SSD_HEAD_DIM
                lmat = jnp.exp(jnp.minimum(la_cs[:, at:at + 1] - acs_t[at:at + 1, :], 0.0)) * tril
                scores.append((cb * lmat).astype(BF16))
            xp = xdt[:, cols]
            rhs = jnp.concatenate([jnp.where(first_head, xp, 0.0),
                                   jnp.where(first_head, 0.0, xp)], axis=0).astype(BF16)
            y_parts.append(_dot(jnp.concatenate(scores, axis=1), rhs))
        y = y_parts[0] if pb == 1 else jnp.concatenate(y_parts, axis=1)

        y_off = stack(lambda g: _dot(cm[g * ch:(g + 1) * ch].astype(BF16),
                                     ht_ref[g, :, wcols].astype(BF16)))
        y = y + y_off * jnp.exp(la_cs)
        b_t = bm.T.astype(BF16)
        upd = xdt * jnp.exp(a_end - la_cs)
        for g in range(gb):
            own = (row_id >= g * ch) & (row_id < (g + 1) * ch)
            ht_ref[g, :, wcols] = (ht_ref[g, :, wcols] * jnp.exp(a_end_rows[g])
                                   + _dot(b_t, jnp.where(own, upd, 0.0).astype(BF16)))

        y = (y + d_ref[q] * xs) * _silu(z)
        y = y * lax.rsqrt(jnp.mean(y * y, axis=-1, keepdims=True) + EPS) * ng_ref[q]
        for g in range(gb):
            y_ref[g, rows, wcols] = y[g * ch:(g + 1) * ch].astype(y_ref.dtype)

    def step(ci, carry):
        t0 = pl.multiple_of(ci * ch, ch)
        rows = pl.ds(t0, ch)
        xs = stack(lambda g: conv(x_ref, wx_ref, cwx_ref, cbx_ref, g, t0))
        bm = stack(lambda g: conv(b_ref, wb_ref, cwb_ref, cbb_ref, g, t0))
        cm = stack(lambda g: conv(c_ref, wc_ref, cwc_ref, cbc_ref, g, t0))
        dt_heads = _softplus(stack(lambda g: dt_ref[g, rows, :]) + bias_ref[...])
        z = stack(lambda g: z_ref[g, rows, :])
        for q in range(n_grp):
            wcols = slice(q * w, (q + 1) * w)
            ncols = slice(q * LANES, (q + 1) * LANES)
            group(q, rows, xs[:, wcols], bm[:, ncols], cm[:, ncols], dt_heads, z[:, wcols])
        return carry

    @pl.when(tb == 0)
    def _():
        for g in range(gb):
            for win_ref, c0_ref in ((wx_ref, cx0_ref), (wb_ref, cb0_ref), (wc_ref, cc0_ref)):
                win_ref[g, pl.ds(0, SUBLANES), :] = jnp.zeros((SUBLANES, win_ref.shape[2]), F32)
                if has_init:
                    win_ref[g, pl.ds(SUBLANES - tail, tail), :] = c0_ref[g]
            for p in range(n_grp * pb):
                cols = slice(p * LANES, (p + 1) * LANES)
                if has_init:
                    ht_ref[g, :, cols] = h0_ref[g, p].T
                else:
                    ht_ref[g, :, cols] = jnp.zeros((LANES, LANES), F32)

    if nc == 1:
        step(0, 0)
    else:
        lax.fori_loop(0, nc, step, 0, unroll=2)

    @pl.when(tb == pl.num_programs(1) - 1)
    def _():
        for g in range(gb):
            for p in range(n_grp * pb):
                hout_ref[g, p] = ht_ref[g, :, p * LANES:(p + 1) * LANES].T


def _group_rows(v):
    return jnp.repeat(v.astype(F32), SSD_HEAD_DIM).reshape(SSD_GROUPS, 1, -1)


def _head_select(n_heads):
    import numpy as np
    w = n_heads // SSD_GROUPS * SSD_HEAD_DIM
    h = np.arange(LANES)[None, :, None]
    lane = np.arange(w)[None, None, :]
    g = np.arange(SSD_GROUPS)[:, None, None]
    return jnp.asarray((h == g * (n_heads // SSD_GROUPS) + lane // SSD_HEAD_DIM).astype(np.float32),
                       dtype=BF16)


def _block_tri(gb, ch):
    import numpy as np
    t = np.arange(gb * ch)[:, None]
    s = np.arange(gb * ch)[None, :]
    return ((t // ch == s // ch) & (s <= t)).astype(np.float32)


def _ssd_pairs(proj3, dt3, conv_w, conv_b, dt_bias, a_log, d_skip, norm_g, h0, conv0, *, mix_a,
               mix_b, ch, valid, gb, tt):
    bsz, t_len, _ = proj3.shape
    n_pairs = mix_b // LANES
    pb = n_pairs // SSD_GROUPS
    w = pb * LANES
    nc = tt // ch
    n_rows = gb * ch
    has_init = h0 is not None
    wn = SSD_GROUPS * LANES
    assert (4 * mix_a) % mix_b == 0 and (4 * mix_a + 2 * mix_b) % wn == 0 and n_rows == LANES
    z0 = 4 * mix_a // mix_b
    x0 = z0 + 1
    b0 = (4 * mix_a + 2 * mix_b) // wn
    c0 = b0 + 1

    seq = lambda blk: (lambda b, t: (b, t, blk))
    par = lambda blk: (lambda b, t: (0, blk))
    const2 = lambda b, t: (0, 0)
    const3 = lambda b, t: (0, 0, 0)

    in_specs = [
        pl.BlockSpec((gb, tt, mix_b), seq(z0)),
        pl.BlockSpec((gb, tt, mix_b), seq(x0)),
        pl.BlockSpec((gb, tt, wn), seq(b0)),
        pl.BlockSpec((gb, tt, wn), seq(c0)),
        pl.BlockSpec((gb, tt, LANES), seq(0)),
        pl.BlockSpec((CONV_WIDTH, mix_b), par(0)),
        pl.BlockSpec((CONV_WIDTH, wn), par(mix_b // wn)),
        pl.BlockSpec((CONV_WIDTH, wn), par(mix_b // wn + 1)),
        pl.BlockSpec((1, mix_b), par(0)),
        pl.BlockSpec((1, wn), par(mix_b // wn)),
        pl.BlockSpec((1, wn), par(mix_b // wn + 1)),
        pl.BlockSpec((SSD_GROUPS, LANES, w), const3),
        pl.BlockSpec((1, LANES), const2),
        pl.BlockSpec((SSD_GROUPS, 1, w), const3),
        pl.BlockSpec((SSD_GROUPS, 1, w), const3),
        pl.BlockSpec((SSD_GROUPS, 1, w), const3),
        pl.BlockSpec((n_rows, n_rows), const2),
        pl.BlockSpec((n_rows, n_rows), const2),
    ]
    conv_b2 = conv_b.reshape(1, -1)
    block_tri = _block_tri(gb, ch)
    args = [proj3, proj3, proj3, proj3, dt3, conv_w, conv_w, conv_w, conv_b2, conv_b2, conv_b2,
            _head_select(dt_bias.shape[0]),
            jnp.pad(dt_bias.astype(F32), (0, LANES - dt_bias.shape[0])).reshape(1, LANES),
            _group_rows(a_log),
            _group_rows(d_skip), norm_g.astype(F32).reshape(SSD_GROUPS, 1, w),
            jnp.asarray(block_tri, dtype=BF16), jnp.asarray(block_tri)]
    if has_init:
        tail = CONV_WIDTH - 1
        in_specs += [
            pl.BlockSpec((gb, n_pairs, LANES, LANES), lambda b, t: (b, 0, 0, 0)),
            pl.BlockSpec((gb, tail, mix_b), lambda b, t: (b, 0, 0)),
            pl.BlockSpec((gb, tail, wn), lambda b, t: (b, 0, mix_b // wn)),
            pl.BlockSpec((gb, tail, wn), lambda b, t: (b, 0, mix_b // wn + 1)),
        ]
        args += [h0, conv0, conv0, conv0]
    kern = functools.partial(_ssd_kernel, ch=ch, valid=valid, nc=nc, gb=gb, pb=pb,
                             has_init=has_init)
    return pl.pallas_call(
        kern,
        grid=(bsz // gb, t_len // tt),
        in_specs=in_specs,
        out_specs=[
            pl.BlockSpec((gb, tt, mix_b), lambda b, t: (b, t, 0)),
            pl.BlockSpec((gb, n_pairs, LANES, LANES), lambda b, t: (b, 0, 0, 0)),
        ],
        out_shape=[
            jax.ShapeDtypeStruct((bsz, t_len, mix_b), BF16),
            jax.ShapeDtypeStruct((bsz, n_pairs, LANES, LANES), F32),
        ],
        scratch_shapes=[pltpu.VMEM((gb, LANES, mix_b), F32),
                        pltpu.VMEM((gb, ch + SUBLANES, mix_b), F32),
                        pltpu.VMEM((gb, ch + SUBLANES, wn), F32),
                        pltpu.VMEM((gb, ch + SUBLANES, wn), F32)],
        compiler_params=_cparams(("arbitrary", "arbitrary")),
        name="ssd_pairs",
    )(*args)


OUT_TM = 512
WEIGHT_STAGE_ROWS = 512
ROUTE_PARTS = 4
GROUP_LANE0 = N_EXPERTS
NEG_BIG = -1e30
NO_LANE = 4 * LANES


def _first_lane_of(mask, lane):
    return jnp.min(jnp.where(mask, lane, float(NO_LANE)), axis=1, keepdims=True)


def _route(logits):
    lane_i = lax.broadcasted_iota(I32, logits.shape, 1)
    lane = lane_i.astype(F32)
    is_grp = (lane_i >= GROUP_LANE0) & (lane_i < GROUP_LANE0 + N_EXPERT_GROUPS)
    lg = jnp.where(is_grp, logits, NEG_BIG)
    g_max = jnp.max(lg, axis=1, keepdims=True)
    g_sum = jnp.sum(jnp.where(is_grp, jnp.exp(lg - g_max), 0.0), axis=1, keepdims=True)
    g_idx = _first_lane_of(lg == g_max, lane) - GROUP_LANE0
    gp_top = 1.0 / g_sum
    lane_grp = jnp.right_shift(lane_i, 3).astype(F32)
    in_grp = (lane_i < N_EXPERTS) & (lane_grp == g_idx)
    le = jnp.where(in_grp, logits, NEG_BIG)
    e_max = jnp.max(le, axis=1, keepdims=True)
    e_exp = jnp.where(in_grp, jnp.exp(le - e_max), 0.0)
    ep = e_exp / jnp.sum(e_exp, axis=1, keepdims=True)
    ep = jnp.where(in_grp, ep, -1.0)
    p1 = jnp.max(ep, axis=1, keepdims=True)
    i1 = _first_lane_of(ep == p1, lane)
    ep2 = jnp.where(lane == i1, -1.0, ep)
    p2 = jnp.max(ep2, axis=1, keepdims=True)
    i2 = _first_lane_of(ep2 == p2, lane)
    den = p1 + p2
    return i1, i2, gp_top * p1 / den, gp_top * p2 / den


def _outproj_kernel(o_ref, y_ref, x_ref, g1_ref, sc_ref, sh_ref, ng_ref, w_hbm_ref, wr_ref, br_ref,
                    tri_ref, cnt0_ref,
                    x1_ref, hn_ref, ids_ref, wts_ref, rank_ref, cnt_ref,
                    wbf_ref, stage_ref, sem, cnts_ref):
    i = pl.program_id(0)
    tm, d = hn_ref.shape
    mix_a = o_ref.shape[1]
    tn = PROJ_TN
    n_col = d // tn
    gshape = (x_ref.shape[0], x_ref.shape[1], tn)

    @pl.when(i == 0)
    def _():
        cnts_ref[...] = cnt0_ref[...]
        rows = stage_ref.shape[0]
        for k in range(d // rows):
            cp = pltpu.make_async_copy(w_hbm_ref.at[pl.ds(k * rows, rows), :], stage_ref, sem)
            cp.start()
            cp.wait()
            wbf_ref[k * rows:(k + 1) * rows, :] = stage_ref[...].astype(BF16)

    o = o_ref[...]
    y = y_ref[...]
    for c in range(n_col):
        cols = slice(c * tn, (c + 1) * tn)
        acc = _dot(o, wbf_ref[0:mix_a, cols]) + _dot(y, wbf_ref[mix_a:d, cols])
        x1_ref[:, :, cols] = x_ref[:, :, cols] + g1_ref[:, :, cols] * acc.reshape(gshape)

    g_blk, r_blk = x_ref.shape[0], x_ref.shape[1]
    pr = tm // ROUTE_PARTS
    lane = lax.broadcasted_iota(I32, (pr, LANES), 1)
    onehots, hits = [], []
    for h in range(ROUTE_PARTS):
        rows = slice(h * pr, (h + 1) * pr)
        if g_blk == 1:
            gsl = (slice(None), rows)
        else:
            gsl = (slice(h * g_blk // ROUTE_PARTS, (h + 1) * g_blk // ROUTE_PARTS), slice(None))
        pshape = (g_blk if g_blk == 1 else g_blk // ROUTE_PARTS,
                  pr if g_blk == 1 else r_blk, tn)
        x1 = [x1_ref[gsl + (slice(c * tn, (c + 1) * tn),)].reshape(pr, tn) for c in range(n_col)]
        ssq = sum(jnp.sum(xc * xc, axis=-1, keepdims=True) for xc in x1)
        rs = lax.rsqrt(ssq / d + EPS)
        logits = jnp.zeros((pr, LANES), F32) + br_ref[...]
        for c in range(n_col):
            cols = slice(c * tn, (c + 1) * tn)
            sc = sc_ref[gsl[0], :, cols] if g_blk > 1 else sc_ref[:, :, cols]
            sh = sh_ref[gsl[0], :, cols] if g_blk > 1 else sh_ref[:, :, cols]
            hn = (x1[c] * rs * ng_ref[:, cols]).reshape(pshape)
            hn = (hn * (1.0 + sc) + sh).reshape(pr, tn)
            hn_ref[rows, cols] = hn
            logits = logits + _dot_hp(hn, wr_ref[cols, :])
        i1, i2, w1, w2 = _route(logits)
        hit1 = lane.astype(F32) == i1
        hit2 = lane.astype(F32) == i2
        onehots.append(jnp.where(hit1 | hit2, 1.0, 0.0).astype(BF16))
        hits.append((hit1, hit2))
        ids_ref[rows, :] = jnp.where(lane == 0, i1, jnp.where(lane == 1, i2, 0.0)).astype(I32)
        wts_ref[rows, :] = jnp.where(lane == 0, w1, jnp.where(lane == 1, w2, 0.0))

    total = cnts_ref[0:1, :]
    for h in range(ROUTE_PARTS):
        rows = slice(h * pr, (h + 1) * pr)
        hit1, hit2 = hits[h]
        before = _dot(tri_ref[...], onehots[h]) + total
        r1 = jnp.sum(jnp.where(hit1, before, 0.0), axis=1, keepdims=True)
        r2 = jnp.sum(jnp.where(hit2, before, 0.0), axis=1, keepdims=True)
        rank_ref[rows, :] = jnp.where(lane == 0, r1, jnp.where(lane == 1, r2, 0.0)).astype(I32)
        total = total + jnp.sum(onehots[h].astype(F32), axis=0, keepdims=True)
    cnts_ref[...] = jnp.broadcast_to(total, cnts_ref.shape)
    cnt_ref[...] = jnp.broadcast_to(total, cnt_ref.shape)


def _strict_lower(n):
    import numpy as np
    t = np.arange(n)[:, None]
    s = np.arange(n)[None, :]
    return jnp.asarray((s < t).astype(np.float32), dtype=BF16)


def _out_projection(o3, y3, x3, mod3, norm_g, w_out, w_router, b_router, cnt0, group_rows):
    bsz, t_len, d = x3.shape
    g_blk, r_blk = group_rows
    tm = g_blk * r_blk
    mix_a = o3.shape[-1]
    mix_b = y3.shape[-1]
    n_tok = bsz * t_len
    tiles_per_seq = max(t_len // r_blk, 1)

    xmap = lambda i: (i // tiles_per_seq, i % tiles_per_seq, 0)
    modmap = lambda s: (lambda i: (i // tiles_per_seq, 0, s))
    row = lambda i: (i, 0)
    const2 = lambda i: (0, 0)
    return pl.pallas_call(
        _outproj_kernel,
        grid=(n_tok // tm,),
        in_specs=[
            pl.BlockSpec((tm, mix_a), row),
            pl.BlockSpec((tm, mix_b), row),
            pl.BlockSpec((g_blk, r_blk, d), xmap),
            pl.BlockSpec((g_blk, 1, d), modmap(2)),
            pl.BlockSpec((g_blk, 1, d), modmap(4)),
            pl.BlockSpec((g_blk, 1, d), modmap(3)),
            pl.BlockSpec((1, d), const2),
            pl.BlockSpec(memory_space=pl.ANY),
            pl.BlockSpec((d, LANES), const2),
            pl.BlockSpec((1, LANES), const2),
            pl.BlockSpec((tm // ROUTE_PARTS, tm // ROUTE_PARTS), const2),
            pl.BlockSpec((SUBLANES, LANES), const2),
        ],
        out_specs=[
            pl.BlockSpec((g_blk, r_blk, d), xmap),
            pl.BlockSpec((tm, d), row),
            pl.BlockSpec((tm, LANES), row),
            pl.BlockSpec((tm, LANES), row),
            pl.BlockSpec((tm, LANES), row),
            pl.BlockSpec((SUBLANES, LANES), const2),
        ],
        out_shape=[
            jax.ShapeDtypeStruct((bsz, t_len, d), F32),
            jax.ShapeDtypeStruct((n_tok, d), F32),
            jax.ShapeDtypeStruct((n_tok, LANES), I32),
            jax.ShapeDtypeStruct((n_tok, LANES), F32),
            jax.ShapeDtypeStruct((n_tok, LANES), I32),
            jax.ShapeDtypeStruct((SUBLANES, LANES), F32),
        ],
        scratch_shapes=[pltpu.VMEM((d, d), BF16), pltpu.VMEM((WEIGHT_STAGE_ROWS, d), F32),
                        pltpu.SemaphoreType.DMA(()), pltpu.VMEM((SUBLANES, LANES), F32)],
        compiler_params=_cparams(("arbitrary",)),
        name="out_proj_router",
    )(o3.reshape(n_tok, mix_a), y3.reshape(n_tok, mix_b), x3, mod3, mod3, mod3,
      norm_g.reshape(1, d), w_out, w_router, b_router, _strict_lower(tm // ROUTE_PARTS), cnt0)


def _positions_kernel(ids_ref, rank_ref, cnt_ref, pos_ref, te_ref, nt_ref, lt_ref, *, block):
    lane8 = lax.broadcasted_iota(I32, (SUBLANES, LANES), 1)
    cnt = cnt_ref[...]
    tiles = jnp.floor((cnt + (MOE_TM - 1)) * (1.0 / MOE_TM))
    tiles = jnp.where(lane8 < N_EXPERTS, tiles, 0.0)
    incl = tiles
    shift = 1
    while shift < N_EXPERTS:
        incl = incl + jnp.where(lane8 >= shift, pltpu.roll(incl, shift, axis=1), 0.0)
        shift *= 2
    offs = ((incl - tiles) * MOE_TM)[0:1, :]
    nt_ref[...] = jnp.sum(tiles, axis=1, keepdims=True).astype(I32) + jnp.zeros(nt_ref.shape, I32)
    lt_ref[...] = jnp.where(tiles > 0.0, incl - 1.0, -1.0).astype(I32)

    incl_col = jnp.broadcast_to(incl[0:1, :], (LANES, LANES)).T
    e_row = lax.broadcasted_iota(I32, (LANES, LANES), 0)
    i_lane = lax.broadcasted_iota(I32, (LANES, LANES), 1).astype(F32)
    done = jnp.where((incl_col <= i_lane) & (e_row < N_EXPERTS), 1.0, 0.0)
    te = jnp.minimum(jnp.sum(done, axis=0, keepdims=True), N_EXPERTS - 1.0)
    te_ref[...] = te.astype(I32) + jnp.zeros(te_ref.shape, I32)

    lane = lax.broadcasted_iota(I32, (block, LANES), 1)

    def body(bi, c):
        r0 = pl.multiple_of(bi * block, block)
        ids = ids_ref[pl.ds(r0, block), :]
        rank = rank_ref[pl.ds(r0, block), :]
        out = jnp.zeros((block, LANES), I32)
        for k in range(TOP_K):
            off = jnp.sum(jnp.where(lane == ids[:, k:k + 1], offs, 0.0), axis=1, keepdims=True)
            out = jnp.where(lane == k, off.astype(I32) + rank[:, k:k + 1], out)
        pos_ref[pl.ds(r0, block), :] = out
        return c

    lax.fori_loop(0, ids_ref.shape[0] // block, body, 0)


def _positions(ids, rank, cnt):
    n_tok = ids.shape[0]
    block = 512
    kern = functools.partial(_positions_kernel, block=block)
    full = lambda s: pl.BlockSpec(s, lambda: tuple(0 for _ in s))
    return pl.pallas_call(
        kern,
        in_specs=[full((n_tok, LANES)), full((n_tok, LANES)), full((SUBLANES, LANES))],
        out_specs=[full((n_tok, LANES))] + [full((SUBLANES, LANES))] * 3,
        out_shape=[jax.ShapeDtypeStruct((n_tok, LANES), I32)]
        + [jax.ShapeDtypeStruct((SUBLANES, LANES), I32)] * 3,
        compiler_params=pltpu.CompilerParams(vmem_limit_bytes=VMEM_LIMIT),
        name="route_positions",
    )(ids, rank, cnt)


def _row_copy(src_ref, src_row, dst_ref, dst_row, sem):
    return pltpu.make_async_copy(src_ref.at[pl.ds(src_row, 1), :], dst_ref.at[pl.ds(dst_row, 1), :],
                                 sem)


def _dispatch_kernel(pos_ref, lt_ref, nt_ref, *refs, steps, n_tiles):
    n_groups = len(steps)
    hn_refs = refs[:n_groups]
    xg_ref, sem, zero_ref, zsem = refs[n_groups:]
    rows = hn_refs[0].shape[0]
    i = pl.program_id(0)

    def zero_copy(tile):
        return pltpu.make_async_copy(zero_ref, xg_ref.at[pl.ds(tile * MOE_TM, MOE_TM), :], zsem)

    @pl.when(i == 0)
    def _():
        zero_ref[...] = jnp.zeros(zero_ref.shape, zero_ref.dtype)
        nt = nt_ref[0]
        for e in range(N_EXPERTS):
            @pl.when(lt_ref[e] >= 0)
            def _():
                zero_copy(jnp.maximum(lt_ref[e], 0)).start(priority=e % 2)
        lax.fori_loop(nt, n_tiles, lambda t, c: (zero_copy(t).start(priority=1), c)[1], 0)
        for e in range(N_EXPERTS):
            @pl.when(lt_ref[e] >= 0)
            def _():
                zero_copy(0).wait()
        lax.fori_loop(nt, n_tiles, lambda t, c: (zero_copy(0).wait(), c)[1], 0)

    first = 0
    for hn_ref, n_steps in zip(hn_refs, steps):
        @pl.when((i >= first) & (i < first + n_steps))
        def _(hn_ref=hn_ref):
            base = i * rows

            def issue(r, c):
                for k in range(TOP_K):
                    _row_copy(hn_ref, r, xg_ref, pos_ref[(base + r) * TOP_K + k], sem).start(
                        priority=k)
                return c

            def drain(r, c):
                for k in range(TOP_K):
                    _row_copy(hn_ref, r, xg_ref, 0, sem).wait()
                return c

            lax.fori_loop(0, rows, issue, 0, unroll=8)
            lax.fori_loop(0, rows, drain, 0, unroll=8)

        first += n_steps


def _dispatch(pos_flat, last_tile, n_live, hns, n_tiles):
    d = hns[0].shape[1]
    steps = tuple(h.shape[0] // ROW_DMA_TILE for h in hns)
    firsts = [sum(steps[:g]) for g in range(len(steps))]

    def group_map(g):
        return lambda i, pos, lt, nt: (jnp.clip(i - firsts[g], 0, steps[g] - 1), 0)

    return pl.pallas_call(
        functools.partial(_dispatch_kernel, steps=steps, n_tiles=n_tiles),
        grid_spec=pltpu.PrefetchScalarGridSpec(
            num_scalar_prefetch=3,
            grid=(sum(steps),),
            in_specs=[pl.BlockSpec((ROW_DMA_TILE, d), group_map(g)) for g in range(len(hns))],
            out_specs=pl.BlockSpec(memory_space=pl.ANY),
            scratch_shapes=[pltpu.SemaphoreType.DMA(()), pltpu.VMEM((MOE_TM, d), F32),
                            pltpu.SemaphoreType.DMA(())],
        ),
        out_shape=jax.ShapeDtypeStruct((n_tiles * MOE_TM, d), F32),
        compiler_params=_cparams(("arbitrary",)),
        name="moe_dispatch",
    )(pos_flat, last_tile, n_live, *hns)


WEIGHT_PARTS = 4


def _weight_copies(w_hbm_refs, expert, wbuf_ref, slot, sem_ref):
    copies = []
    for i, w_ref in enumerate(w_hbm_refs):
        part_rows = w_ref.shape[1] // WEIGHT_PARTS
        for part in range(WEIGHT_PARTS):
            rows = pl.ds(part * part_rows, part_rows)
            copies.append(pltpu.make_async_copy(w_ref.at[expert, rows], wbuf_ref.at[slot, i, rows],
                                                sem_ref.at[slot]))
    return copies


def _start_weight_copies(copies, queue0_parts):
    for k, cp in enumerate(copies):
        cp.start(priority=0 if k % WEIGHT_PARTS < queue0_parts else 1)


def _expert_weights_slot(te_ref, nt, t, w_hbm_refs, wbuf_ref, sem_ref, slot_ref, queue0_parts):
    expert = te_ref[t]

    @pl.when(t == 0)
    def _():
        slot_ref[0] = 0
        _start_weight_copies(_weight_copies(w_hbm_refs, expert, wbuf_ref, 0, sem_ref),
                             queue0_parts)

    @pl.when((t == 0) | (expert != te_ref[jnp.maximum(t - 1, 0)]))
    def _():
        @pl.when(t > 0)
        def _():
            slot_ref[0] = 1 - slot_ref[0]

        slot = slot_ref[0]
        for cp in _weight_copies(w_hbm_refs, expert, wbuf_ref, slot, sem_ref):
            cp.wait()
        nxt = lax.while_loop(lambda u: (u < nt) & (te_ref[jnp.minimum(u, nt - 1)] == expert),
                             lambda u: u + 1, t + 1)

        @pl.when(nxt < nt)
        def _():
            _start_weight_copies(_weight_copies(w_hbm_refs, te_ref[jnp.minimum(nxt, nt - 1)],
                                                wbuf_ref, 1 - slot, sem_ref), queue0_parts)

    return slot_ref[0]


def _moe_up_kernel(te_ref, nt_ref, x_ref, w1_ref, w3_ref, act_ref, wbuf_ref, sem_ref, slot_ref):
    t = pl.program_id(0)
    nt = nt_ref[0]

    @pl.when(t < nt)
    def _():
        slot = _expert_weights_slot(te_ref, nt, t, (w1_ref, w3_ref), wbuf_ref, sem_ref, slot_ref,
                                    queue0_parts=1)
        xb = x_ref[...].astype(BF16)
        a = _dot(xb, wbuf_ref[slot, 0].astype(BF16))
        b = _dot(xb, wbuf_ref[slot, 1].astype(BF16))
        act_ref[...] = (_silu(a) * b).astype(BF16)

    @pl.when(t >= nt)
    def _():
        act_ref[...] = jnp.zeros(act_ref.shape, act_ref.dtype)


def _moe_down_kernel(te_ref, nt_ref, act_ref, w2_ref, out_ref, wbuf_ref, sem_ref, slot_ref):
    t = pl.program_id(0)
    nt = nt_ref[0]

    @pl.when(t < nt)
    def _():
        slot = _expert_weights_slot(te_ref, nt, t, (w2_ref,), wbuf_ref, sem_ref, slot_ref,
                                    queue0_parts=0)
        out_ref[...] = _dot(act_ref[...], wbuf_ref[slot, 0].astype(BF16))

    @pl.when(t >= nt)
    def _():
        out_ref[...] = jnp.zeros(out_ref.shape, out_ref.dtype)


def _moe_experts(te, nt, xg, w1, w3, w2):
    n_rows, d = xg.shape
    d_exp = w1.shape[-1]
    n_tiles = n_rows // MOE_TM
    live_tile = lambda t, te_r, nt_r: (jnp.minimum(t, nt_r[0] - 1), 0)
    any_spec = pl.BlockSpec(memory_space=pl.ANY)

    act = pl.pallas_call(
        _moe_up_kernel,
        grid_spec=pltpu.PrefetchScalarGridSpec(
            num_scalar_prefetch=2,
            grid=(n_tiles,),
            in_specs=[pl.BlockSpec((MOE_TM, d), live_tile), any_spec, any_spec],
            out_specs=pl.BlockSpec((MOE_TM, d_exp), lambda t, te_r, nt_r: (t, 0)),
            scratch_shapes=[pltpu.VMEM((2, 2, d, d_exp), F32), pltpu.SemaphoreType.DMA((2,)),
                            pltpu.SMEM((1,), I32)],
        ),
        out_shape=jax.ShapeDtypeStruct((n_rows, d_exp), BF16),
        compiler_params=_cparams(("arbitrary",)),
        name="moe_up",
    )(te, nt, xg, w1, w3)

    return pl.pallas_call(
        _moe_down_kernel,
        grid_spec=pltpu.PrefetchScalarGridSpec(
            num_scalar_prefetch=2,
            grid=(n_tiles,),
            in_specs=[pl.BlockSpec((MOE_TM, d_exp), live_tile), any_spec],
            out_specs=pl.BlockSpec((MOE_TM, d), lambda t, te_r, nt_r: (t, 0)),
            scratch_shapes=[pltpu.VMEM((2, 1, d_exp, d), F32), pltpu.SemaphoreType.DMA((2,)),
                            pltpu.SMEM((1,), I32)],
        ),
        out_shape=jax.ShapeDtypeStruct((n_rows, d), F32),
        compiler_params=_cparams(("arbitrary",)),
        name="moe_down",
    )(te, nt, act, w2)


def _combine_kernel(pos_ref, x1_ref, g2_ref, wts_ref, fg_ref, eo_ref, y_ref, buf_ref, sem):
    rows = wts_ref.shape[0]
    base = pl.program_id(0) * rows

    def issue(r, c):
        for k in range(TOP_K):
            _row_copy(eo_ref, pos_ref[(base + r) * TOP_K + k], buf_ref.at[k], r, sem).start(
                priority=k)
        return c

    def drain(r, c):
        for k in range(TOP_K):
            _row_copy(eo_ref, 0, buf_ref.at[k], r, sem).wait()
        return c

    lax.fori_loop(0, rows, issue, 0, unroll=8)
    lax.fori_loop(0, rows, drain, 0, unroll=8)
    w = wts_ref[...]
    moe = w[:, 0:1] * buf_ref[0] + w[:, 1:2] * buf_ref[1]
    x2 = x1_ref[...] + g2_ref[...] * moe.reshape(x1_ref.shape)
    ms = jnp.mean(x2 * x2, axis=-1, keepdims=True)
    y_ref[...] = x2 * lax.rsqrt(ms + EPS) * fg_ref[...]


def _combine(pos_flat, x1, mod3, wts, final_g, expert_out, group_rows):
    bsz, t_len, d = x1.shape
    g_blk, r_blk = group_rows
    tm = g_blk * r_blk
    tiles_per_seq = max(t_len // r_blk, 1)

    def xmap(i, pos):
        return (i // tiles_per_seq, i % tiles_per_seq, 0)

    return pl.pallas_call(
        _combine_kernel,
        grid_spec=pltpu.PrefetchScalarGridSpec(
            num_scalar_prefetch=1,
            grid=((bsz * t_len) // tm,),
            in_specs=[
                pl.BlockSpec((g_blk, r_blk, d), xmap),
                pl.BlockSpec((g_blk, 1, d), lambda i, pos: (i // tiles_per_seq, 0, 5)),
                pl.BlockSpec((tm, LANES), lambda i, pos: (i, 0)),
                pl.BlockSpec((1, d), lambda i, pos: (0, 0)),
                pl.BlockSpec(memory_space=pl.ANY),
            ],
            out_specs=pl.BlockSpec((g_blk, r_blk, d), xmap),
            scratch_shapes=[pltpu.VMEM((TOP_K, tm, d), F32), pltpu.SemaphoreType.DMA(())],
        ),
        out_shape=jax.ShapeDtypeStruct((bsz, t_len, d), F32),
        compiler_params=_cparams(("arbitrary",)),
        name="moe_combine",
    )(pos_flat, x1, mod3, wts, final_g.reshape(1, d), expert_out)


def kernel(x_prompt, x_sample, state_hgrn, state_ssm, state_conv, c_prompt, c_sample, ada_w, ada_b,
           norm1_g, norm2_g, w_in, hgrn_lb, hgrn_onorm_g, conv_w, conv_b, dt_bias, a_log, d_skip,
           ssm_norm_g, w_out, w_grp, b_grp, w_rt, b_rt, w1, w3, w2, final_g):
    depth = w_in.shape[0]
    assert depth == 1, "single-layer trunk"
    layer = 0
    bp, t_p, d = x_prompt.shape
    bs, t_s, _ = x_sample.shape
    mix_a = hgrn_onorm_g.shape[1]
    mix_b = ssm_norm_g.shape[1]
    conv_dim = conv_w.shape[2]
    n_main = 4 * mix_a + mix_b + conv_dim
    n_ssd_heads = dt_bias.shape[1]
    n_pairs = mix_b // LANES
    xbc0 = 4 * mix_a + mix_b
    tail = CONV_WIDTH - 1
    assert t_s <= SUBLANES and t_s >= tail and t_p % PROMPT_CHUNK == 0

    xs_pad = jnp.pad(x_sample, ((0, 0), (0, SUBLANES - t_s), (0, 0)))
    n_c = bp + bs
    c_rows = -(-n_c // SUBLANES) * SUBLANES
    c_all = jnp.pad(jnp.concatenate([c_prompt, c_sample], axis=0), ((0, c_rows - n_c), (0, 0)))

    mod = _modulation(c_all, ada_w[layer], ada_b[layer])
    mod_p = mod[:bp]
    mod_s = mod[bp:n_c]
    w_in_t = jnp.swapaxes(w_in[layer], 0, 1)
    w_dt_t = jnp.pad(w_in_t[n_main:], ((0, LANES - n_ssd_heads), (0, 0)))
    w_router = jnp.pad(jnp.concatenate([w_rt[layer], w_grp[layer]], axis=1),
                       ((0, 0), (0, LANES - N_EXPERTS - N_EXPERT_GROUPS)))
    b_router = jnp.pad(jnp.concatenate([b_rt[layer], b_grp[layer]]),
                       (0, LANES - N_EXPERTS - N_EXPERT_GROUPS)).reshape(1, LANES)

    groups = (
        (x_prompt, mod_p, None, None, None, t_p, PROMPT_CHUNK, PROMPT_CHUNK, MIXER_ROWS,
         (LANES // PROMPT_CHUNK, 8), LANES // PROMPT_CHUNK, 1, (1, OUT_TM)),
        (xs_pad, mod_s, state_hgrn[layer], state_ssm[layer].reshape(bs, n_pairs, LANES, LANES),
         state_conv[layer], t_s, SUBLANES, t_s, SUBLANES,
         (LANES // SUBLANES, 4), LANES // SUBLANES, PROJ_SEQS, (OUT_TM // SUBLANES, SUBLANES)),
    )

    cnt = jnp.zeros((SUBLANES, LANES), F32)
    per_group = []
    for x3, mod3, s0, h0, c0, t_real, ch, valid, tt, (hg_gb, hg_hb), ssd_gb, proj_gr, out_gr in groups:
        bsz, t_len, _ = x3.shape
        proj, dt = _in_projection(x3, mod3, norm1_g[layer], w_in_t, w_dt_t, n_main, proj_gr)
        proj3 = proj.reshape(bsz, t_len, n_main)
        dt3 = dt.reshape(bsz, t_len, LANES)
        o3, s_new = _hgrn_heads(proj3, hgrn_lb, hgrn_onorm_g[layer], s0, ch=ch, valid=valid,
                                gb=hg_gb, hb=hg_hb, tt=tt, layer=layer)
        y3, h_new = _ssd_pairs(proj3, dt3, conv_w[layer], conv_b[layer], dt_bias[layer],
                               a_log[layer], d_skip[layer], ssm_norm_g[layer], h0, c0, mix_a=mix_a,
                               mix_b=mix_b, ch=ch, valid=valid, gb=ssd_gb, tt=tt)
        conv_new = proj3[:, t_real - tail:t_real, xbc0:xbc0 + conv_dim]
        x1, hn, ids, wts, rank, cnt = _out_projection(
            o3, y3, x3, mod3, norm2_g[layer], w_out[layer], w_router, b_router, cnt, out_gr)
        per_group.append(dict(x1=x1, hn=hn, ids=ids, wts=wts, rank=rank, mod3=mod3, out_gr=out_gr,
                              s_new=s_new, h_new=h_new, conv_new=conv_new))

    ids_all = jnp.concatenate([g["ids"] for g in per_group], axis=0)
    rank_all = jnp.concatenate([g["rank"] for g in per_group], axis=0)
    n_tok = ids_all.shape[0]
    pos, te, nt, lt = _positions(ids_all, rank_all, cnt)
    pos_flat = pos[:, :TOP_K].reshape(-1)
    n_tiles = -(-(n_tok * TOP_K + N_EXPERTS * (MOE_TM - 1)) // MOE_TM)
    assert n_tiles <= LANES
    te_vec = te[0, :n_tiles]
    nt_vec = nt[0, :1]
    lt_vec = lt[0, :N_EXPERTS]

    row0 = 0
    for g in per_group:
        rows = g["hn"].shape[0]
        g["pos"] = pos_flat[row0 * TOP_K:(row0 + rows) * TOP_K]
        row0 += rows
    xg = _dispatch(pos_flat, lt_vec, nt_vec, [g["hn"] for g in per_group], n_tiles)
    expert_out = _moe_experts(te_vec, nt_vec, xg, w1[layer], w3[layer], w2[layer])
    ys = [_combine(g["pos"], g["x1"], g["mod3"], g["wts"], final_g, expert_out, g["out_gr"])
          for g in per_group]

    gp, gs = per_group
    return (
        ys[0],
        ys[1][:, :t_s],
        gp["s_new"][None],
        gp["h_new"].reshape(1, bp, n_ssd_heads, SSD_HEAD_DIM, SSD_STATE),
        gp["conv_new"][None],
        gs["s_new"][None],
        gs["h_new"].reshape(1, bs, n_ssd_heads, SSD_HEAD_DIM, SSD_STATE),
        gs["conv_new"][None],
    )
```
